```python
import jax
import jax.numpy as jnp
from jax import lax
import numpy as np

D_MODEL = 1024
BATCH = 16
SEQ = 256
DEPTH = 2
DEC_BATCH = 4
DEC_SEQ = 2048
PAST_LEN = 512

GRID_W = 64
N_EVEN = (DEPTH + 1) // 2
N_ODD = DEPTH // 2
EPS = 1e-6

H_A = 8
Q_LORA = 512
KV_LORA = 256
NOPE = 64
ROPE = 32
QK_DIM = NOPE + ROPE
V_A = 64
ROPE_THETA = 10000.0
Q_BLOCK = 128
G_B = 8
C_B = 64
W_B = G_B * C_B
CHUNK_B = 128
IN_A = Q_LORA + KV_LORA + ROPE + 2 * W_B
MIX_A = H_A * V_A + W_B
H_C = 4
DK_C = 128
DV_C = 256
GATE_RANK = 16
GATE_TAU = 16.0
GLA_CHUNK = 64
IN_C = 2 * H_C * DK_C + 2 * H_C * DV_C + 2 * GATE_RANK
MIX_C = H_C * DV_C
N_EXPERTS = 32
TOP_K = 4
D_FF = 1024
SWIGLU_LIMIT = 7.0
SWIGLU_ALPHA = 1.702

kernel_name = "hybrid_diffusion_mla_cmlp_gla_moe_step"


def rmsnorm(x, g):
    xf = x.astype(jnp.float32)
    y = xf * lax.rsqrt(jnp.mean(xf * xf, axis=-1, keepdims=True) + EPS)
    return (y * g.astype(jnp.float32)).astype(x.dtype)


def adaln(cond, w, b):
    m = jax.nn.silu(cond) @ w + b
    return jnp.split(m[:, None, :], 6, axis=-1)


def axial_rope(x):
    n = x.shape[-2]
    rows = n // GRID_W
    row = jnp.repeat(jnp.arange(rows, dtype=jnp.float32), GRID_W)
    col = jnp.tile(jnp.arange(GRID_W, dtype=jnp.float32), rows)
    half = ROPE // 2
    inv_freq = jnp.power(ROPE_THETA, -jnp.arange(0, half, 2, dtype=jnp.float32) / half)
    x_nope, x_row, x_col = jnp.split(x, [NOPE, NOPE + half], axis=-1)

    def rotate(z, pos):
        ang = pos[:, None] * inv_freq[None, :]
        cos, sin = jnp.cos(ang), jnp.sin(ang)
        z1, z2 = jnp.split(z.astype(jnp.float32), 2, axis=-1)
        return jnp.concatenate([z1 * cos - z2 * sin, z1 * sin + z2 * cos], axis=-1).astype(z.dtype)

    return jnp.concatenate([x_nope, rotate(x_row, row), rotate(x_col, col)], axis=-1)


def block_attention(q, k, v):
    b, h, n, dq = q.shape
    nb = n // Q_BLOCK
    qb = q.reshape(b, h, nb, Q_BLOCK, dq).transpose(2, 0, 1, 3, 4)
    scale = dq ** -0.5

    def one_block(qi):
        s = jnp.einsum('bhqd,bhkd->bhqk', qi, k, preferred_element_type=jnp.float32) * scale
        p = jax.nn.softmax(s, axis=-1)
        return jnp.einsum('bhqk,bhkv->bhqv', p.astype(v.dtype), v)

    o = lax.map(one_block, qb)
    return o.transpose(1, 2, 0, 3, 4).reshape(b, h, n, v.shape[-1])


def mla_keys_values(ckv, kpe, w_ukv, k_norm):
    b, m, _ = ckv.shape
    kv = (ckv @ w_ukv).reshape(b, m, H_A, NOPE + V_A).transpose(0, 2, 1, 3)
    k_nope, v = jnp.split(kv, [NOPE], axis=-1)
    k_pe = jnp.broadcast_to(kpe[:, None], (b, H_A, m, ROPE))
    k = rmsnorm(jnp.concatenate([k_nope, k_pe], axis=-1), k_norm)
    return k, v


def chunk_mlp(u, v, v_norm, w_s, b_s):
    b, n, _ = v.shape
    nc = n // CHUNK_B
    u = jax.nn.gelu(u)
    v = rmsnorm(jax.nn.gelu(v), v_norm).reshape(b, nc, CHUNK_B, G_B, C_B)
    mixed = jnp.einsum('gts,bcsgd->bctgd', w_s, v) + b_s.T[None, None, :, :, None]
    return u * mixed.reshape(b, n, W_B)


def even_mixer(h, w_in, q_a_norm, w_uq, q_norm, kv_a_norm, w_ukv, k_norm, v_norm, w_s, b_s, w_out, ctx):
    b, n, _ = h.shape
    q_c, ckv, kpe, u, v = jnp.split(
        h @ w_in, [Q_LORA, Q_LORA + KV_LORA, Q_LORA + KV_LORA + ROPE, Q_LORA + KV_LORA + ROPE + W_B], axis=-1)
    ckv = rmsnorm(ckv, kv_a_norm)
    q = (rmsnorm(q_c, q_a_norm) @ w_uq).reshape(b, n, H_A, QK_DIM).transpose(0, 2, 1, 3)
    q = rmsnorm(q, q_norm)
    k, val = mla_keys_values(ckv, kpe, w_ukv, k_norm)
    if ctx is not None:
        q, k = axial_rope(q), axial_rope(k)
        k_ctx, v_ctx = mla_keys_values(ctx[0], ctx[1], w_ukv, k_norm)
        k = jnp.concatenate([k, k_ctx], axis=2)
        val = jnp.concatenate([val, v_ctx], axis=2)
    o_a = block_attention(q, k, val).transpose(0, 2, 1, 3).reshape(b, n, H_A * V_A)
    o_b = chunk_mlp(u, v, v_norm, w_s, b_s)
    return jnp.concatenate([o_a, o_b], axis=-1) @ w_out, ckv, kpe


def gla_chunked(q, k, v, log_a, s0):
    b, n, h, dk = q.shape
    dv = v.shape[-1]
    nc = n // GLA_CHUNK

    def blocks(x):
        return x.astype(jnp.float32).reshape(b, nc, GLA_CHUNK, h, x.shape[-1]).transpose(1, 0, 3, 2, 4)

    lower = jnp.tril(jnp.ones((GLA_CHUNK, GLA_CHUNK), dtype=bool))

    def step(s, inp):
        qc, kc, vc, gc = inp
        cum = jnp.cumsum(gc, axis=2)
        cum_last = cum[:, :, -1:, :]
        o_inter = jnp.einsum('bhtk,bhkv->bhtv', qc * jnp.exp(cum), s)
        diff = jnp.where(lower[:, :, None], cum[:, :, :, None, :] - cum[:, :, None, :, :], -jnp.inf)
        att = jnp.einsum('bhtk,bhsk,bhtsk->bhts', qc, kc, jnp.exp(diff))
        o = o_inter + jnp.einsum('bhts,bhsv->bhtv', att, vc)
        s = jnp.exp(cum_last[:, :, 0, :])[..., None] * s + jnp.einsum(
            'bhsk,bhsv->bhkv', kc * jnp.exp(cum_last - cum), vc)
        return s, o

    s_fin, o = lax.scan(step, s0.astype(jnp.float32), (blocks(q), blocks(k), blocks(v), blocks(log_a)))
    o = o.transpose(1, 0, 3, 2, 4).reshape(b, n, h, dv)
    return o.astype(v.dtype), s_fin.astype(v.dtype)


def odd_mixer(h, w_in, w_gk_fwd, b_gk_fwd, w_gk_bwd, b_gk_bwd, o_norm, w_out, ctx):
    b, n, _ = h.shape
    hk, hv = H_C * DK_C, H_C * DV_C
    q, k, v, g, lr_f, lr_b = jnp.split(
        h @ w_in, [hk, 2 * hk, 2 * hk + hv, 2 * hk + 2 * hv, 2 * hk + 2 * hv + GATE_RANK], axis=-1)
    q = q.reshape(b, n, H_C, DK_C) * (DK_C ** -0.5)
    k = k.reshape(b, n, H_C, DK_C)
    v = v.reshape(b, n, H_C, DV_C)

    def log_decay(lr, w, bias):
        z = (lr @ w + bias).astype(jnp.float32).reshape(b, n, H_C, DK_C)
        return jax.nn.log_sigmoid(z) / GATE_TAU

    la_f = log_decay(lr_f, w_gk_fwd, b_gk_fwd)
    la_b = log_decay(lr_b, w_gk_bwd, b_gk_bwd)
    if ctx is None:
        s0_f = jnp.zeros((b, H_C, DK_C, DV_C), jnp.float32)
        s0_b = jnp.zeros((b, H_C, DK_C, DV_C), jnp.float32)
    else:
        s0_f, s0_b = ctx

    def rev(t):
        return jnp.flip(t, axis=1)

    o_f, s_f = gla_chunked(q, k, v, la_f, s0_f)
    o_b, s_b = gla_chunked(rev(q), rev(k), rev(v), rev(la_b), s0_b)
    o = rmsnorm(o_f + rev(o_b), o_norm) * jax.nn.silu(g.reshape(b, n, H_C, DV_C))
    return o.reshape(b, n, MIX_C) @ w_out, s_f, s_b


def moe(h, w_router, b_router, w_gu, b_gu, w_down, b_down):
    b, n, d = h.shape
    x = h.reshape(b * n, d)
    logits = (x @ w_router + b_router).astype(jnp.float32)
    top_val, top_idx = lax.top_k(logits, TOP_K)
    top_w = jax.nn.softmax(top_val, axis=-1)
    gates = jnp.sum(jax.nn.one_hot(top_idx, N_EXPERTS, dtype=jnp.float32) * top_w[..., None], axis=1)

    def expert(acc, p):
        wgu, bgu, wd, bd, gate = p
        x_glu, x_lin = jnp.split(x @ wgu + bgu, 2, axis=-1)
        x_glu = jnp.minimum(x_glu, SWIGLU_LIMIT)
        x_lin = jnp.clip(x_lin, -SWIGLU_LIMIT, SWIGLU_LIMIT)
        y = (x_glu * jax.nn.sigmoid(SWIGLU_ALPHA * x_glu)) * (x_lin + 1.0)
        return acc + gate[:, None].astype(x.dtype) * (y @ wd + bd), None

    out, _ = lax.scan(expert, jnp.zeros_like(x), (w_gu, b_gu, w_down, b_down, gates.T))
    return out.reshape(b, n, d)


def setup_inputs(seed: int = 0) -> dict:
    key = jax.random.key(seed)
    ks = iter(jax.random.split(key, 64))

    def nrm(shape, scale):
        return jax.random.normal(next(ks), shape, jnp.float32) * scale

    def gain(shape):
        return 1.0 + nrm(shape, 0.02)

    D = D_MODEL
    return {
        "x_prompt": nrm((BATCH, SEQ, D), 1.0),
        "x_sample": nrm((DEC_BATCH, DEC_SEQ, D), 1.0),
        "cache_mla_ckv": nrm((DEC_BATCH, N_EVEN, PAST_LEN, KV_LORA), 1.0),
        "cache_mla_kpe": nrm((DEC_BATCH, N_EVEN, PAST_LEN, ROPE), 1.0),
        "state_gla_fwd": nrm((DEC_BATCH, N_ODD, H_C, DK_C, DV_C), 0.5),
        "state_gla_bwd": nrm((DEC_BATCH, N_ODD, H_C, DK_C, DV_C), 0.5),
        "c": nrm((DEC_BATCH, D), 1.0),
        "c_ctx": nrm((D,), 1.0),
        "ada_w": nrm((DEPTH, D, 6 * D), 0.5 * D ** -0.5),
        "ada_b": nrm((DEPTH, 6 * D), 0.02),
        "norm_mix": gain((DEPTH, D)),
        "norm_ffn": gain((DEPTH, D)),
        "even_w_in": nrm((N_EVEN, D, IN_A), D ** -0.5),
        "mla_q_a_norm": gain((N_EVEN, Q_LORA)),
        "mla_w_uq": nrm((N_EVEN, Q_LORA, H_A * QK_DIM), Q_LORA ** -0.5),
        "mla_q_norm": gain((N_EVEN, QK_DIM)),
        "mla_kv_a_norm": gain((N_EVEN, KV_LORA)),
        "mla_w_ukv": nrm((N_EVEN, KV_LORA, H_A * (NOPE + V_A)), KV_LORA ** -0.5),
        "mla_k_norm": gain((N_EVEN, QK_DIM)),
        "cmlp_v_norm": gain((N_EVEN, W_B)),
        "cmlp_w_s": nrm((N_EVEN, G_B, CHUNK_B, CHUNK_B), CHUNK_B ** -0.5),
        "cmlp_b_s": gain((N_EVEN, G_B, CHUNK_B)),
        "even_w_out": nrm((N_EVEN, MIX_A, D), MIX_A ** -0.5),
        "odd_w_in": nrm((N_ODD, D, IN_C), D ** -0.5),
        "gla_w_gk_fwd": nrm((N_ODD, GATE_RANK, H_C * DK_C), GATE_RANK ** -0.5),
        "gla_b_gk_fwd": nrm((N_ODD, H_C * DK_C), 0.02),
        "gla_w_gk_bwd": nrm((N_ODD, GATE_RANK, H_C * DK_C), GATE_RANK ** -0.5),
        "gla_b_gk_bwd": nrm((N_ODD, H_C * DK_C), 0.02),
        "gla_o_norm": gain((N_ODD, DV_C)),
        "odd_w_out": nrm((N_ODD, MIX_C, D), MIX_C ** -0.5),
        "moe_w_router": nrm((DEPTH, D, N_EXPERTS), D ** -0.5),
        "moe_b_router": nrm((DEPTH, N_EXPERTS), 0.01),
        "moe_w_gu": nrm((DEPTH, N_EXPERTS, D, 2 * D_FF), D ** -0.5),
        "moe_b_gu": nrm((DEPTH, N_EXPERTS, 2 * D_FF), 0.02),
        "moe_w_down": nrm((DEPTH, N_EXPERTS, D_FF, D), D_FF ** -0.5),
        "moe_b_down": nrm((DEPTH, N_EXPERTS, D), 0.02),
    }


def reference(x_prompt, x_sample, cache_mla_ckv, cache_mla_kpe, state_gla_fwd, state_gla_bwd, c, c_ctx,
              ada_w, ada_b, norm_mix, norm_ffn, even_w_in, mla_q_a_norm, mla_w_uq, mla_q_norm, mla_kv_a_norm,
              mla_w_ukv, mla_k_norm, cmlp_v_norm, cmlp_w_s, cmlp_b_s, even_w_out, odd_w_in, gla_w_gk_fwd,
              gla_b_gk_fwd, gla_w_gk_bwd, gla_b_gk_bwd, gla_o_norm, odd_w_out, moe_w_router, moe_b_router,
              moe_w_gu, moe_b_gu, moe_w_down, moe_b_down):

    def trunk_layer(x, l, cond, ctx):
        sh1, sc1, g1, sh2, sc2, g2 = adaln(cond, ada_w[l], ada_b[l])
        h = rmsnorm(x, norm_mix[l]) * (1.0 + sc1) + sh1
        i = l // 2
        if l % 2 == 0:
            out, t1, t2 = even_mixer(h, even_w_in[i], mla_q_a_norm[i], mla_w_uq[i], mla_q_norm[i],
                                     mla_kv_a_norm[i], mla_w_ukv[i], mla_k_norm[i], cmlp_v_norm[i],
                                     cmlp_w_s[i], cmlp_b_s[i], even_w_out[i], ctx)
        else:
            out, t1, t2 = odd_mixer(h, odd_w_in[i], gla_w_gk_fwd[i], gla_b_gk_fwd[i], gla_w_gk_bwd[i],
                                    gla_b_gk_bwd[i], gla_o_norm[i], odd_w_out[i], ctx)
        x = x + g1 * out
        h = rmsnorm(x, norm_ffn[l]) * (1.0 + sc2) + sh2
        x = x + g2 * moe(h, moe_w_router[l], moe_b_router[l], moe_w_gu[l], moe_b_gu[l],
                         moe_w_down[l], moe_b_down[l])
        return x, t1, t2

    ckvs, kpes, s_fwds, s_bwds = [], [], [], []
    y_prompt = x_prompt
    cond_ctx = c_ctx[None, :]
    for l in range(DEPTH):
        y_prompt, t1, t2 = trunk_layer(y_prompt, l, cond_ctx, None)
        if l % 2 == 0:
            ckvs.append(t1)
            kpes.append(t2)
        else:
            s_fwds.append(t1)
            s_bwds.append(t2)

    y_sample = x_sample
    for l in range(DEPTH):
        i = l // 2
        if l % 2 == 0:
            ctx = (cache_mla_ckv[:, i], cache_mla_kpe[:, i])
        else:
            ctx = (state_gla_fwd[:, i], state_gla_bwd[:, i])
        y_sample, _, _ = trunk_layer(y_sample, l, c, ctx)

    new_mla_ckv = jnp.stack(ckvs, axis=1)
    new_mla_kpe = jnp.stack(kpes, axis=1)
    new_gla_fwd = jnp.stack(s_fwds, axis=1)
    new_gla_bwd = jnp.stack(s_bwds, axis=1)
    return (y_prompt, y_sample, new_mla_ckv, new_mla_kpe, new_gla_fwd, new_gla_bwd)
```

```python
import functools

import numpy as np
import jax
import jax.numpy as jnp
from jax import lax
from jax.experimental import pallas as pl
from jax.experimental.pallas import tpu as pltpu

F32 = jnp.float32
BF16 = jnp.bfloat16

D = 1024
N_PROMPT_SEQ, PROMPT_LEN = 16, 256
N_SAMPLE_SEQ, SAMPLE_LEN = 4, 2048
PAST_LEN = 512
N_PROMPT = N_PROMPT_SEQ * PROMPT_LEN
N_SAMPLE = N_SAMPLE_SEQ * SAMPLE_LEN
T = N_PROMPT + N_SAMPLE
EPS = 1e-6
GRID_W = 64
H_A, Q_LORA, KV_LORA, NOPE, ROPE, V_A = 8, 512, 256, 64, 32, 64
QK_DIM = NOPE + ROPE
G_B, C_B, W_B, CHUNK_B = 8, 64, 512, 128
H_C, DK_C, DV_C, GATE_RANK, GATE_TAU, GLA_CHUNK = 4, 128, 256, 16, 16.0, 64
N_EXPERTS, TOP_K, D_FF = 32, 4, 1024
SWIGLU_LIMIT, SWIGLU_ALPHA = 7.0, 1.702
ROPE_THETA = 10000.0

LANES = 128
SUBLANES = 8
VMEM_LIMIT = 56 * 1024 * 1024

TM = 256
NBLK = T // TM
PROMPT_BLKS = N_PROMPT // TM
BLKS_PER_SAMPLE_SEQ = SAMPLE_LEN // TM
ETILE = 256
N_PAIRS = T * TOP_K
N_ETILES = N_PAIRS // ETILE + N_EXPERTS
P_ROWS = N_ETILES * ETILE
HP = LANES


def _cond_row(i):
    return jnp.where(i < PROMPT_BLKS, 0, 1 + (i - PROMPT_BLKS) // BLKS_PER_SAMPLE_SEQ)


def _rope_blk(i):
    return jnp.where(i < PROMPT_BLKS, 0, 1 + (i - PROMPT_BLKS) % BLKS_PER_SAMPLE_SEQ)


def _rms(x):
    return x * lax.rsqrt(jnp.mean(x * x, axis=-1, keepdims=True) + EPS)


def _gelu(x):
    return 0.5 * x * (1.0 + jnp.tanh(0.7978845608028654 * (x + 0.044715 * (x * x * x))))


def _silu(x):
    return x * jax.nn.sigmoid(x)


def _dot(a, b):
    return jnp.dot(a, b, preferred_element_type=F32)


def _dot_nt(a, b):
    return lax.dot_general(a, b, (((1,), (1,)), ((), ())), preferred_element_type=F32)


def _dot_tn(a, b):
    return lax.dot_general(a, b, (((0,), (0,)), ((), ())), preferred_element_type=F32)


def _params(sem, vmem=VMEM_LIMIT):
    return pltpu.CompilerParams(dimension_semantics=sem, vmem_limit_bytes=vmem)


def _full(shape):
    nd = len(shape)
    return pl.BlockSpec(shape, lambda *_: (0,) * nd)


ADA_TN = 1536


def _adaln_kernel(c_ref, w_ref, b_ref, o_ref):
    s = _silu(c_ref[...]).astype(BF16)
    o_ref[0] = _dot(s, w_ref[0].astype(BF16)) + b_ref[0]


def _adaln(cond8, ada_w, ada_b):
    depth = ada_w.shape[0]
    n = ada_w.shape[2]
    out = pl.pallas_call(
        _adaln_kernel,
        grid=(depth, n // ADA_TN),
        in_specs=[
            pl.BlockSpec((SUBLANES, D), lambda l, j: (0, 0)),
            pl.BlockSpec((1, D, ADA_TN), lambda l, j: (l, 0, j)),
            pl.BlockSpec((1, 1, ADA_TN), lambda l, j: (l, 0, j)),
        ],
        out_specs=pl.BlockSpec((1, SUBLANES, ADA_TN), lambda l, j: (l, 0, j)),
        out_shape=jax.ShapeDtypeStruct((depth, SUBLANES, n), F32),
        compiler_params=_params(("arbitrary", "arbitrary")),
        name="adaln",
    )(cond8, ada_w, ada_b.reshape(depth, 1, n))
    return out.reshape(depth, SUBLANES, 6, D)


_QC0, _CKV0, _U0, _V0, _KPE0, _WIN_N = 0, 512, 768, 1280, 1792, 1920


def _rope(y, c, s1, s2):
    return y * c + pltpu.roll(y, LANES - 8, 1) * s1 + pltpu.roll(y, 8, 1) * s2


def _k_heads(k_raw, kpe128, kp_rot, kgain, k_ref):
    sskpe = jnp.sum(kpe128 * kpe128, axis=-1, keepdims=True)
    for h in range(H_A):
        kb = k_raw[:, h * HP:(h + 1) * HP]
        r = lax.rsqrt((jnp.sum(kb * kb, axis=-1, keepdims=True) + sskpe) * (1.0 / QK_DIM) + EPS)
        k_ref[:, h * HP:(h + 1) * HP] = ((kb * kgain + kp_rot) * r).astype(BF16)


def _even_in_kernel(x_ref, mod_ref, nmix_ref, win_ref, qan_ref, wuq_ref, qgain_ref, kvan_ref, wukv_ref,
                    kgain_ref, vnorm_ref, ws_ref, bs_ref, rc_ref, rs1_ref, rs2_ref,
                    q_ref, k_ref, v_ref, ob_ref, ckv_ref, kpe_ref):
    m = mod_ref[0]
    h = _rms(x_ref[...]) * nmix_ref[...] * (1.0 + m[1:2]) + m[0:1]
    a = _dot(h.astype(BF16), win_ref[...])
    qc = a[:, _QC0:_CKV0]
    ckv = a[:, _CKV0:_U0]
    u = a[:, _U0:_V0]
    vv = a[:, _V0:_KPE0]
    kpe128 = a[:, _KPE0:_WIN_N]

    ckv_n = _rms(ckv) * kvan_ref[...]
    ckv_ref[...] = ckv_n
    kpe_ref[...] = kpe128[:, NOPE:QK_DIM]

    rc, rs1, rs2 = rc_ref[...], rs1_ref[...], rs2_ref[...]
    qn = (_rms(qc) * qan_ref[...]).astype(BF16)
    qr = _dot(qn, wuq_ref[...])
    qgain = qgain_ref[...]
    for hh in range(H_A):
        blk = qr[:, hh * HP:(hh + 1) * HP]
        r = lax.rsqrt(jnp.sum(blk * blk, axis=-1, keepdims=True) * (1.0 / QK_DIM) + EPS)
        q_ref[:, hh * HP:(hh + 1) * HP] = _rope(blk * r * qgain, rc, rs1, rs2).astype(BF16)

    kv = _dot(ckv_n.astype(BF16), wukv_ref[...])
    v_ref[...] = kv[:, H_A * HP:].astype(BF16)
    kgain = kgain_ref[...]
    kp_rot = _rope(kpe128 * kgain, rc, rs1, rs2)
    _k_heads(kv[:, :H_A * HP], kpe128, kp_rot, kgain, k_ref)

    ug = _gelu(u)
    vn = (_rms(_gelu(vv)) * vnorm_ref[...]).astype(BF16)
    low = lax.broadcasted_iota(jnp.int32, (CHUNK_B, LANES), 1) < C_B
    for c in range(TM // CHUNK_B):
        rows = slice(c * CHUNK_B, (c + 1) * CHUNK_B)
        for p in range(G_B // 2):
            cols = slice(p * LANES, (p + 1) * LANES)
            blk = vn[rows, cols]
            mixed = jnp.where(low, _dot(ws_ref[2 * p], blk), _dot(ws_ref[2 * p + 1], blk)) + bs_ref[p]
            ob_ref[rows, cols] = (ug[rows, cols] * mixed).astype(BF16)


def _ctx_kv_kernel(ckv_ref, kpe_ref, wukv_ref, kgain_ref, k_ref, v_ref):
    kv = _dot(ckv_ref[...].astype(BF16), wukv_ref[...])
    v_ref[...] = kv[:, H_A * HP:].astype(BF16)
    kgain = kgain_ref[...]
    kpe128 = kpe_ref[...]
    _k_heads(kv[:, :H_A * HP], kpe128, kpe128 * kgain, kgain, k_ref)


def _attn_self_kernel(q_ref, k_ref, v_ref, o_ref):
    for h in range(H_A):
        cols = slice(h * HP, (h + 1) * HP)
        s = _dot_nt(q_ref[:, cols], k_ref[:, cols])
        p = jnp.exp(s - jnp.max(s, axis=-1, keepdims=True))
        inv = 1.0 / jnp.sum(p, axis=-1, keepdims=True)
        o_ref[:, cols] = (_dot(p.astype(BF16), v_ref[:, cols]) * inv).astype(BF16)


def _attn_ctx_kernel(q_ref, k_ref, v_ref, kc_ref, vc_ref, prev_ref, o_ref):
    del prev_ref
    for h in range(H_A):
        cols = slice(h * HP, (h + 1) * HP)
        q = q_ref[:, cols]
        s1 = _dot_nt(q, k_ref[:, cols])
        s2 = _dot_nt(q, kc_ref[:, cols])
        mx = jnp.maximum(jnp.max(s1, axis=-1, keepdims=True), jnp.max(s2, axis=-1, keepdims=True))
        p1 = jnp.exp(s1 - mx)
        p2 = jnp.exp(s2 - mx)
        inv = 1.0 / (jnp.sum(p1, axis=-1, keepdims=True) + jnp.sum(p2, axis=-1, keepdims=True))
        o = _dot(p1.astype(BF16), v_ref[:, cols]) + _dot(p2.astype(BF16), vc_ref[:, cols])
        o_ref[:, cols] = (o * inv).astype(BF16)


def _attention(q, k, v, k_ctx, v_ctx):
    width = H_A * HP
    o = pl.pallas_call(
        _attn_self_kernel,
        grid=(N_PROMPT_SEQ,),
        in_specs=[pl.BlockSpec((PROMPT_LEN, width), lambda i: (i, 0))] * 3,
        out_specs=pl.BlockSpec((PROMPT_LEN, width), lambda i: (i, 0)),
        out_shape=jax.ShapeDtypeStruct((T, width), BF16),
        compiler_params=_params(("arbitrary",)),
        name="attn_prompt",
    )(q, k, v)
    first = N_PROMPT // SAMPLE_LEN
    qblk = lambda b, j: (PROMPT_BLKS + b * BLKS_PER_SAMPLE_SEQ + j, 0)
    return pl.pallas_call(
        _attn_ctx_kernel,
        grid=(N_SAMPLE_SEQ, BLKS_PER_SAMPLE_SEQ),
        in_specs=[
            pl.BlockSpec((TM, width), qblk),
            pl.BlockSpec((SAMPLE_LEN, width), lambda b, j: (first + b, 0)),
            pl.BlockSpec((SAMPLE_LEN, width), lambda b, j: (first + b, 0)),
            pl.BlockSpec((PAST_LEN, width), lambda b, j: (b, 0)),
            pl.BlockSpec((PAST_LEN, width), lambda b, j: (b, 0)),
            pl.BlockSpec(memory_space=pl.ANY),
        ],
        out_specs=pl.BlockSpec((TM, width), qblk),
        out_shape=jax.ShapeDtypeStruct((T, width), BF16),
        input_output_aliases={5: 0},
        compiler_params=_params(("arbitrary", "arbitrary")),
        name="attn_sample",
    )(q, k, v, k_ctx, v_ctx, o)


_META_IDX, _META_RANK, _META_W = 0, TOP_K, 2 * TOP_K


def _moe_prologue(i, x1, m, nffn_ref, wr_ref, br_ref, carry_ref, x1_ref, h2_ref, meta_ref, cnt_ref):
    @pl.when(i == 0)
    def _():
        carry_ref[...] = jnp.zeros_like(carry_ref)

    x1_ref[...] = x1
    h2 = _rms(x1) * nffn_ref[...] * (1.0 + m[4:5]) + m[3:4]
    h2_ref[...] = h2
    lane = lax.broadcasted_iota(jnp.int32, (TM, LANES), 1)
    lanef = lane.astype(F32)
    logits = jnp.dot(h2, wr_ref[...], precision=lax.Precision.HIGHEST, preferred_element_type=F32) + br_ref[...]
    work = jnp.where(lane < N_EXPERTS, logits, -jnp.inf)
    hots, vals = [], []
    for _ in range(TOP_K):
        mx = jnp.max(work, axis=-1, keepdims=True)
        idx = jnp.min(jnp.where(work == mx, lanef, float(LANES)), axis=-1, keepdims=True)
        hot = lanef == idx
        work = jnp.where(hot, -jnp.inf, work)
        hots.append((hot, idx))
        vals.append(mx)
    es = [jnp.exp(v - vals[0]) for v in vals]
    inv = 1.0 / (es[0] + es[1] + es[2] + es[3])
    sel = jnp.zeros((TM, LANES), F32)
    for hot, _ in hots:
        sel = jnp.where(hot, 1.0, sel)
    row = lax.broadcasted_iota(jnp.int32, (TM, TM), 0)
    col = lax.broadcasted_iota(jnp.int32, (TM, TM), 1)
    strict = jnp.where(row > col, 1.0, 0.0).astype(BF16)
    carry = carry_ref[0:1, :]
    before = _dot(strict, sel.astype(BF16)) + carry
    meta = jnp.zeros((TM, LANES), F32)
    for kk, (hot, idx) in enumerate(hots):
        rank = jnp.sum(jnp.where(hot, before, 0.0), axis=-1, keepdims=True)
        meta = jnp.where(lane == _META_IDX + kk, idx, meta)
        meta = jnp.where(lane == _META_RANK + kk, rank, meta)
        meta = jnp.where(lane == _META_W + kk, es[kk] * inv, meta)
    meta_ref[...] = meta
    total = carry + jnp.sum(sel, axis=0, keepdims=True)
    carry_ref[...] = jnp.broadcast_to(total, carry_ref.shape)
    cnt_ref[...] = jnp.broadcast_to(total, cnt_ref.shape)


def _even_out_kernel(oa_ref, ob_ref, x_ref, mod_ref, woa_ref, wob_ref, nffn_ref, wr_ref, br_ref,
                     x1_ref, h2_ref, meta_ref, cnt_ref, carry_ref):
    m = mod_ref[0]
    out = _dot(oa_ref[...], woa_ref[...]) + _dot(ob_ref[...], wob_ref[...])
    x1 = x_ref[...] + m[2:3] * out
    _moe_prologue(pl.program_id(0), x1, m, nffn_ref, wr_ref, br_ref, carry_ref, x1_ref, h2_ref, meta_ref, cnt_ref)


def _odd_out_kernel(of_ref, ob_ref, g_ref, x_ref, mod_ref, onorm_ref, wo_ref, nffn_ref, wr_ref, br_ref,
                    x1_ref, h2_ref, meta_ref, cnt_ref, carry_ref):
    m = mod_ref[0]
    onorm = onorm_ref[...]
    parts = []
    for h in range(H_C):
        cols = slice(h * DV_C, (h + 1) * DV_C)
        o = of_ref[:, cols] + ob_ref[:, cols]
        parts.append((_rms(o) * onorm * _silu(g_ref[:, cols])).astype(BF16))
    out = _dot(jnp.concatenate(parts, axis=-1), wo_ref[...])
    x1 = x_ref[...] + m[2:3] * out
    _moe_prologue(pl.program_id(0), x1, m, nffn_ref, wr_ref, br_ref, carry_ref, x1_ref, h2_ref, meta_ref, cnt_ref)


_TOK = lambda w: pl.BlockSpec((TM, w), lambda i: (i, 0))
_MOD = pl.BlockSpec((1, 6, D), lambda i: (_cond_row(i), 0, 0))

_PROLOGUE_OUT_SPECS = [_TOK(D), _TOK(D), _TOK(LANES), pl.BlockSpec((SUBLANES, LANES), lambda i: (0, 0))]
_PROLOGUE_OUT_SHAPE = [
    jax.ShapeDtypeStruct((T, D), F32),
    jax.ShapeDtypeStruct((T, D), F32),
    jax.ShapeDtypeStruct((T, LANES), F32),
    jax.ShapeDtypeStruct((SUBLANES, LANES), F32),
]


def _dispatch_kernel(pos_ref, h_hbm, zero_hbm, xs_hbm, sem):
    del zero_hbm
    base = pl.program_id(0) * TM

    def copy(j, k):
        return pltpu.make_async_copy(h_hbm.at[pl.ds(base + j, 1)], xs_hbm.at[pl.ds(pos_ref[0, k, j], 1)], sem)

    def issue(j, carry):
        for k in range(TOP_K):
            copy(j, k).start()
        return carry

    def drain(j, carry):
        for k in range(TOP_K):
            copy(j, k).wait()
        return carry

    lax.fori_loop(0, TM, issue, 0)
    lax.fori_loop(0, TM, drain, 0)


def _dispatch(pos_blocks, h2):
    zeros = jnp.zeros((P_ROWS, D), F32)
    return pl.pallas_call(
        _dispatch_kernel,
        grid=(NBLK,),
        in_specs=[
            pl.BlockSpec((1, TOP_K, TM), lambda i: (i, 0, 0), memory_space=pltpu.SMEM),
            pl.BlockSpec(memory_space=pl.ANY),
            pl.BlockSpec(memory_space=pl.ANY),
        ],
        out_specs=pl.BlockSpec(memory_space=pl.ANY),
        out_shape=jax.ShapeDtypeStruct((P_ROWS, D), F32),
        scratch_shapes=[pltpu.SemaphoreType.DMA],
        input_output_aliases={2: 0},
        compiler_params=_params(("arbitrary",)),
        name="moe_dispatch",
    )(pos_blocks, h2, zeros)


def _expert_kernel(te_ref, tfirst_ref, tvalid_ref, x_ref, wgu_ref, bgu_ref, wd_ref, bd_ref, y_ref,
                   wgu_bf, wd_bf):
    i = pl.program_id(0)

    @pl.when(tfirst_ref[i] == 1)
    def _():
        wgu_bf[...] = wgu_ref[0].astype(BF16)
        wd_bf[...] = wd_ref[0].astype(BF16)

    @pl.when(tvalid_ref[i] == 1)
    def _():
        a = _dot(x_ref[...].astype(BF16), wgu_bf[...]) + bgu_ref[0]
        glu = jnp.minimum(a[:, :D_FF], SWIGLU_LIMIT)
        lin = jnp.clip(a[:, D_FF:], -SWIGLU_LIMIT, SWIGLU_LIMIT)
        act = (glu * jax.nn.sigmoid(SWIGLU_ALPHA * glu)) * (lin + 1.0)
        y_ref[...] = _dot(act.astype(BF16), wd_bf[...]) + bd_ref[0]

    @pl.when(tvalid_ref[i] == 0)
    def _():
        y_ref[...] = jnp.zeros_like(y_ref)


def _experts(tile_expert, tile_first, tile_valid, xs, w_gu, b_gu, w_down, b_down):
    e_of = lambda i, te, tf, tv: (te[i], 0, 0)
    grid_spec = pltpu.PrefetchScalarGridSpec(
        num_scalar_prefetch=3,
        grid=(N_ETILES,),
        in_specs=[
            pl.BlockSpec((ETILE, D), lambda i, te, tf, tv: (i, 0)),
            pl.BlockSpec((1, D, 2 * D_FF), e_of),
            pl.BlockSpec((1, 1, 2 * D_FF), e_of),
            pl.BlockSpec((1, D_FF, D), e_of),
            pl.BlockSpec((1, 1, D), e_of),
        ],
        out_specs=pl.BlockSpec((ETILE, D), lambda i, te, tf, tv: (i, 0)),
        scratch_shapes=[pltpu.VMEM((D, 2 * D_FF), BF16), pltpu.VMEM((D_FF, D), BF16)],
    )
    return pl.pallas_call(
        _expert_kernel,
        grid_spec=grid_spec,
        out_shape=jax.ShapeDtypeStruct((P_ROWS, D), F32),
        compiler_params=_params(("arbitrary",)),
        name="moe_experts",
    )(tile_expert, tile_first, tile_valid, xs, w_gu, b_gu.reshape(N_EXPERTS, 1, 2 * D_FF), w_down,
      b_down.reshape(N_EXPERTS, 1, D))


def _combine_kernel(pos_ref, y_hbm, x1_ref, meta_ref, mod_ref, o_ref, buf, sem):
    def copy(j, k):
        return pltpu.make_async_copy(y_hbm.at[pl.ds(pos_ref[0, k, j], 1)], buf.at[k, pl.ds(j, 1)], sem)

    def issue(j, carry):
        for k in range(TOP_K):
            copy(j, k).start()
        return carry

    def drain(j, carry):
        for k in range(TOP_K):
            copy(j, k).wait()
        return carry

    lax.fori_loop(0, TM, issue, 0)
    lax.fori_loop(0, TM, drain, 0)
    meta = meta_ref[...]
    acc = jnp.zeros((TM, D), F32)
    for k in range(TOP_K):
        acc = acc + meta[:, _META_W + k:_META_W + k + 1] * buf[k]
    o_ref[...] = x1_ref[...] + mod_ref[0][5:6] * acc


def _combine(pos_blocks, y, x1, meta, mod):
    return pl.pallas_call(
        _combine_kernel,
        grid=(NBLK,),
        in_specs=[
            pl.BlockSpec((1, TOP_K, TM), lambda i: (i, 0, 0), memory_space=pltpu.SMEM),
            pl.BlockSpec(memory_space=pl.ANY),
            _TOK(D),
            _TOK(LANES),
            _MOD,
        ],
        out_specs=_TOK(D),
        out_shape=jax.ShapeDtypeStruct((T, D), F32),
        scratch_shapes=[pltpu.VMEM((TOP_K, TM, D), F32), pltpu.SemaphoreType.DMA],
        compiler_params=_params(("arbitrary",)),
        name="moe_combine",
    )(pos_blocks, y, x1, meta, mod)


def _moe(x1, h2, meta, cnt, mod, w_gu, b_gu, w_down, b_down):
    counts = cnt[0, :N_EXPERTS].astype(jnp.int32)
    tiles = (counts + ETILE - 1) // ETILE
    tile_end = jnp.cumsum(tiles)
    offs = (tile_end - tiles) * ETILE
    tid = jnp.arange(N_ETILES, dtype=jnp.int32)
    te = jnp.searchsorted(tile_end, tid, side="right").astype(jnp.int32)
    tvalid = (te < N_EXPERTS).astype(jnp.int32)
    last = jnp.max(jnp.where(tiles > 0, jnp.arange(N_EXPERTS, dtype=jnp.int32), 0))
    te = jnp.where(tvalid == 1, te, last)
    tfirst = jnp.concatenate([jnp.ones((1,), jnp.int32), (te[1:] != te[:-1]).astype(jnp.int32)])
    idx = meta[:, _META_IDX:_META_IDX + TOP_K].astype(jnp.int32)
    rank = meta[:, _META_RANK:_META_RANK + TOP_K].astype(jnp.int32)
    pos = offs[idx] + rank
    pos_blocks = pos.reshape(NBLK, TM, TOP_K).transpose(0, 2, 1)
    xs = _dispatch(pos_blocks, h2)
    y = _experts(te, tfirst, tvalid, xs, w_gu, b_gu, w_down, b_down)
    return _combine(pos_blocks, y, x1, meta, mod)


_HK, _HV = H_C * DK_C, H_C * DV_C
_ODD_MAIN = 2 * _HK + 2 * _HV


def _odd_in_kernel(x_ref, mod_ref, nmix_ref, win_ref, wgk_ref, bgk_ref, q_ref, k_ref, v_ref, g_ref, la_ref):
    m = mod_ref[0]
    h = _rms(x_ref[...]) * nmix_ref[...] * (1.0 + m[1:2]) + m[0:1]
    a = _dot(h.astype(BF16), win_ref[...])
    q_ref[...] = a[:, :_HK] * (DK_C ** -0.5)
    k_ref[...] = a[:, _HK:2 * _HK]
    v_ref[...] = a[:, 2 * _HK:2 * _HK + _HV]
    g_ref[...] = a[:, 2 * _HK + _HV:_ODD_MAIN]
    z = _dot(a[:, _ODD_MAIN:].astype(BF16), wgk_ref[...]) + bgk_ref[...]
    la_ref[...] = (jnp.minimum(z, 0.0) - jnp.log(1.0 + jnp.exp(-jnp.abs(z)))) * (1.0 / GATE_TAU)


def _gla_kernel(has_init, nchunk, *refs):
    if has_init:
        (qf, kf, vf, laf, qb, kb, vb, lab, s0f, s0b, _, _, of_ref, ob_ref, st) = refs
    else:
        (qf, kf, vf, laf, qb, kb, vb, lab, of_ref, ob_ref, sf_ref, sb_ref, st) = refs
    j = pl.program_id(1)

    @pl.when(j == 0)
    def _():
        if has_init:
            st[0] = s0f[0]
            st[1] = s0b[0]
        else:
            st[...] = jnp.zeros_like(st)

    row = lax.broadcasted_iota(jnp.int32, (GLA_CHUNK, GLA_CHUNK), 0)
    col = lax.broadcasted_iota(jnp.int32, (GLA_CHUNK, GLA_CHUNK), 1)
    for d, (q_r, k_r, v_r, la_r, o_r) in enumerate(((qf, kf, vf, laf, of_ref), (qb, kb, vb, lab, ob_ref))):
        keep = (col <= row) if d == 0 else (col >= row)
        tri = jnp.where(keep, 1.0, 0.0)
        for h in range(H_C):
            kc = slice(h * DK_C, (h + 1) * DK_C)
            vc = slice(h * DV_C, (h + 1) * DV_C)
            g = la_r[:, kc]
            c = jnp.dot(tri, g, precision=lax.Precision.HIGHEST, preferred_element_type=F32)
            tot = jnp.sum(g, axis=0, keepdims=True)
            q, k, v = q_r[:, kc], k_r[:, kc], v_r[:, vc].astype(BF16)
            qe = (q * jnp.exp(c)).astype(BF16)
            kd = (k * jnp.exp(-c)).astype(BF16)
            att = jnp.where(keep, _dot_nt(qe, kd), 0.0)
            s_t = st[d, h]
            o_r[:, vc] = _dot_nt(qe, s_t.astype(BF16)) + _dot(att.astype(BF16), v)
            k2 = (k * jnp.exp(tot - c)).astype(BF16)
            st[d, h] = s_t * jnp.exp(tot) + _dot_tn(v, k2)

    if not has_init:
        @pl.when(j == nchunk - 1)
        def _():
            sf_ref[0] = st[0]
            sb_ref[0] = st[1]


def _gla_call(has_init, nseq, seqlen, row0, q, k, v, la, prev_f=None, prev_b=None, s0f=None, s0b=None):
    nchunk = seqlen // GLA_CHUNK
    blk0 = row0 // GLA_CHUNK
    fwd = lambda b, j: (blk0 + b * nchunk + j, 0)
    bwd = lambda b, j: (blk0 + b * nchunk + nchunk - 1 - j, 0)
    bwd_la = lambda b, j: (blk0 + b * nchunk + nchunk - 1 - j, 1)
    state_spec = pl.BlockSpec((1, H_C, DV_C, DK_C), lambda b, j: (b, 0, 0, 0))
    in_specs = [
        pl.BlockSpec((GLA_CHUNK, _HK), fwd), pl.BlockSpec((GLA_CHUNK, _HK), fwd),
        pl.BlockSpec((GLA_CHUNK, _HV), fwd), pl.BlockSpec((GLA_CHUNK, _HK), fwd),
        pl.BlockSpec((GLA_CHUNK, _HK), bwd), pl.BlockSpec((GLA_CHUNK, _HK), bwd),
        pl.BlockSpec((GLA_CHUNK, _HV), bwd), pl.BlockSpec((GLA_CHUNK, _HK), bwd_la),
    ]
    args = [q, k, v, la, q, k, v, la]
    aliases = {}
    if has_init:
        in_specs += [state_spec, state_spec] + [pl.BlockSpec(memory_space=pl.ANY)] * 2
        args += [s0f, s0b, prev_f, prev_b]
        aliases = {len(args) - 2: 0, len(args) - 1: 1}
    out_specs = [pl.BlockSpec((GLA_CHUNK, _HV), fwd), pl.BlockSpec((GLA_CHUNK, _HV), bwd)]
    out_shape = [jax.ShapeDtypeStruct((T, _HV), F32)] * 2
    if not has_init:
        out_specs += [state_spec, state_spec]
        out_shape += [jax.ShapeDtypeStruct((nseq, H_C, DV_C, DK_C), F32)] * 2
    return pl.pallas_call(
        functools.partial(_gla_kernel, has_init, nchunk),
        grid=(nseq, nchunk),
        in_specs=in_specs,
        out_specs=out_specs,
        out_shape=out_shape,
        scratch_shapes=[pltpu.VMEM((2, H_C, DV_C, DK_C), F32)],
        input_output_aliases=aliases,
        compiler_params=_params(("arbitrary", "arbitrary")),
        name="gla_sample" if has_init else "gla_prompt",
    )(*args)


def _rope_tables():
    half = ROPE // 2
    inv_freq = np.power(np.float32(ROPE_THETA), -np.arange(0, half, 2, dtype=np.float32) / np.float32(half))
    n = np.arange(SAMPLE_LEN)
    row = (n // GRID_W).astype(np.float32)
    col = (n % GRID_W).astype(np.float32)
    ang_r = (row[:, None] * inv_freq[None, :]).astype(np.float32)
    ang_c = (col[:, None] * inv_freq[None, :]).astype(np.float32)
    nf = half // 2
    c = np.ones((TM + SAMPLE_LEN, LANES), np.float32)
    s1 = np.zeros((TM + SAMPLE_LEN, LANES), np.float32)
    s2 = np.zeros((TM + SAMPLE_LEN, LANES), np.float32)
    for base, ang in ((NOPE, ang_r), (NOPE + half, ang_c)):
        c[TM:, base:base + nf] = np.cos(ang)
        c[TM:, base + nf:base + half] = np.cos(ang)
        s1[TM:, base:base + nf] = -np.sin(ang)
        s2[TM:, base + nf:base + half] = np.sin(ang)
    return jnp.asarray(c), jnp.asarray(s1), jnp.asarray(s2)


def _pad_heads(w, nheads, width, lo=0):
    k = w.shape[0]
    w = w.reshape(k, nheads, width)
    w = jnp.pad(w, ((0, 0), (0, 0), (lo, HP - lo - width)))
    return w.reshape(k, nheads * HP)


def _row128(v, lo=0):
    return jnp.pad(v, (lo, LANES - lo - v.shape[0])).reshape(1, LANES)


def _even_layer(x, mod, nmix, w_in, q_a_norm, w_uq, q_norm, kv_a_norm, w_ukv, k_norm, v_norm, w_s, b_s, w_out,
                cache_ckv, cache_kpe):
    s = np.cumsum([Q_LORA, KV_LORA, ROPE, W_B])
    w_q, w_ckv, w_kpe, w_u, w_v = (w_in[:, :s[0]], w_in[:, s[0]:s[1]], w_in[:, s[1]:s[2]], w_in[:, s[2]:s[3]],
                                   w_in[:, s[3]:])
    w_kpe = jnp.pad(w_kpe, ((0, 0), (NOPE, LANES - QK_DIM)))
    win = jnp.concatenate([w_q, w_ckv, w_u, w_v, w_kpe], axis=1).astype(BF16)
    wuq = _pad_heads(w_uq, H_A, QK_DIM).astype(BF16)
    ukv = w_ukv.reshape(KV_LORA, H_A, NOPE + V_A)
    wuk = _pad_heads(ukv[:, :, :NOPE].reshape(KV_LORA, H_A * NOPE), H_A, NOPE)
    wuv = _pad_heads(ukv[:, :, NOPE:].reshape(KV_LORA, H_A * V_A), H_A, V_A)
    wukv = jnp.concatenate([wuk, wuv], axis=1).astype(BF16)
    qgain = _row128(q_norm * (QK_DIM ** -0.5))
    kgain = _row128(k_norm)
    bias = b_s.reshape(G_B // 2, 2, CHUNK_B)
    bias = jnp.concatenate([jnp.broadcast_to(bias[:, 0, :, None], (G_B // 2, CHUNK_B, C_B)),
                            jnp.broadcast_to(bias[:, 1, :, None], (G_B // 2, CHUNK_B, C_B))], axis=-1)
    rc, rs1, rs2 = _rope_tables()
    rope_spec = pl.BlockSpec((TM, LANES), lambda i: (_rope_blk(i), 0))
    width = H_A * HP
    q, k, v, ob, ckv, kpe = pl.pallas_call(
        _even_in_kernel,
        grid=(NBLK,),
        in_specs=[
            _TOK(D), _MOD, _full((1, D)), _full((D, _WIN_N)), _full((1, Q_LORA)), _full((Q_LORA, width)),
            _full((1, LANES)), _full((1, KV_LORA)), _full((KV_LORA, 2 * width)), _full((1, LANES)),
            _full((1, W_B)), _full((G_B, CHUNK_B, CHUNK_B)), _full((G_B // 2, CHUNK_B, LANES)),
            rope_spec, rope_spec, rope_spec,
        ],
        out_specs=[_TOK(width), _TOK(width), _TOK(width), _TOK(W_B), _TOK(KV_LORA), _TOK(ROPE)],
        out_shape=[
            jax.ShapeDtypeStruct((T, width), BF16), jax.ShapeDtypeStruct((T, width), BF16),
            jax.ShapeDtypeStruct((T, width), BF16), jax.ShapeDtypeStruct((T, W_B), BF16),
            jax.ShapeDtypeStruct((T, KV_LORA), F32), jax.ShapeDtypeStruct((T, ROPE), F32),
        ],
        compiler_params=_params(("arbitrary",)),
        name="even_in",
    )(x, mod, nmix.reshape(1, D), win, q_a_norm.reshape(1, Q_LORA), wuq, qgain, kv_a_norm.reshape(1, KV_LORA),
      wukv, kgain, v_norm.reshape(1, W_B), w_s.astype(BF16), bias, rc, rs1, rs2)

    n_ctx = N_SAMPLE_SEQ * PAST_LEN
    kpe_ctx = jnp.pad(cache_kpe.reshape(n_ctx, ROPE), ((0, 0), (NOPE, LANES - QK_DIM)))
    k_ctx, v_ctx = pl.pallas_call(
        _ctx_kv_kernel,
        grid=(n_ctx // TM,),
        in_specs=[_TOK(KV_LORA), _TOK(LANES), _full((KV_LORA, 2 * width)), _full((1, LANES))],
        out_specs=[_TOK(width), _TOK(width)],
        out_shape=[jax.ShapeDtypeStruct((n_ctx, width), BF16)] * 2,
        compiler_params=_params(("arbitrary",)),
        name="ctx_kv",
    )(cache_ckv.reshape(n_ctx, KV_LORA), kpe_ctx, wukv, kgain)

    oa = _attention(q, k, v, k_ctx, v_ctx)
    woa = jnp.pad(w_out[:H_A * V_A].reshape(H_A, V_A, D), ((0, 0), (0, HP - V_A), (0, 0))).reshape(width, D)
    return oa, ob, woa.astype(BF16), w_out[H_A * V_A:].astype(BF16), ckv, kpe


def kernel(x_prompt, x_sample, cache_mla_ckv, cache_mla_kpe, state_gla_fwd, state_gla_bwd, c, c_ctx, ada_w, ada_b,
           norm_mix, norm_ffn, even_w_in, mla_q_a_norm, mla_w_uq, mla_q_norm, mla_kv_a_norm, mla_w_ukv, mla_k_norm,
           cmlp_v_norm, cmlp_w_s, cmlp_b_s, even_w_out, odd_w_in, gla_w_gk_fwd, gla_b_gk_fwd, gla_w_gk_bwd,
           gla_b_gk_bwd, gla_o_norm, odd_w_out, moe_w_router, moe_b_router, moe_w_gu, moe_b_gu, moe_w_down,
           moe_b_down):
    x0 = jnp.concatenate([x_prompt.reshape(N_PROMPT, D), x_sample.reshape(N_SAMPLE, D)], axis=0)
    cond8 = jnp.concatenate([c_ctx[None, :], c, jnp.zeros((SUBLANES - 1 - N_SAMPLE_SEQ, D), F32)], axis=0)
    mods = _adaln(cond8, ada_w, ada_b)
    wr = jnp.pad(moe_w_router, ((0, 0), (0, 0), (0, LANES - N_EXPERTS)))
    br = jnp.pad(moe_b_router, ((0, 0), (0, LANES - N_EXPERTS))).reshape(2, 1, LANES)

    oa, ob, woa, wob, ckv, kpe = _even_layer(
        x0, mods[0], norm_mix[0], even_w_in[0], mla_q_a_norm[0], mla_w_uq[0], mla_q_norm[0], mla_kv_a_norm[0],
        mla_w_ukv[0], mla_k_norm[0], cmlp_v_norm[0], cmlp_w_s[0], cmlp_b_s[0], even_w_out[0],
        cache_mla_ckv[:, 0], cache_mla_kpe[:, 0])
    width = H_A * HP
    x1, h2, meta, cnt = pl.pallas_call(
        _even_out_kernel,
        grid=(NBLK,),
        in_specs=[_TOK(width), _TOK(W_B), _TOK(D), _MOD, _full((width, D)), _full((W_B, D)), _full((1, D)),
                  _full((D, LANES)), _full((1, LANES))],
        out_specs=_PROLOGUE_OUT_SPECS,
        out_shape=_PROLOGUE_OUT_SHAPE,
        scratch_shapes=[pltpu.VMEM((SUBLANES, LANES), F32)],
        compiler_params=_params(("arbitrary",)),
        name="even_out",
    )(oa, ob, x0, mods[0], woa, wob, norm_ffn[0].reshape(1, D), wr[0], br[0])
    x2 = _moe(x1, h2, meta, cnt, mods[0], moe_w_gu[0], moe_b_gu[0], moe_w_down[0], moe_b_down[0])

    w_in = odd_w_in[0]
    win = jnp.concatenate([w_in, jnp.zeros((D, LANES - 2 * GATE_RANK), F32)], axis=1).astype(BF16)
    wgk = jnp.zeros((LANES, 2 * _HK), F32)
    wgk = wgk.at[:GATE_RANK, :_HK].set(gla_w_gk_fwd[0]).at[GATE_RANK:2 * GATE_RANK, _HK:].set(gla_w_gk_bwd[0])
    bgk = jnp.concatenate([gla_b_gk_fwd[0], gla_b_gk_bwd[0]]).reshape(1, 2 * _HK)
    q, k, v, g, la = pl.pallas_call(
        _odd_in_kernel,
        grid=(NBLK,),
        in_specs=[_TOK(D), _MOD, _full((1, D)), _full((D, _ODD_MAIN + LANES)), _full((LANES, 2 * _HK)),
                  _full((1, 2 * _HK))],
        out_specs=[_TOK(_HK), _TOK(_HK), _TOK(_HV), _TOK(_HV), _TOK(2 * _HK)],
        out_shape=[jax.ShapeDtypeStruct((T, _HK), F32), jax.ShapeDtypeStruct((T, _HK), F32),
                   jax.ShapeDtypeStruct((T, _HV), F32), jax.ShapeDtypeStruct((T, _HV), F32),
                   jax.ShapeDtypeStruct((T, 2 * _HK), F32)],
        compiler_params=_params(("arbitrary",)),
        name="odd_in",
    )(x2, mods[1], norm_mix[1].reshape(1, D), win, wgk.astype(BF16), bgk)

    of, obk, st_f, st_b = _gla_call(False, N_PROMPT_SEQ, PROMPT_LEN, 0, q, k, v, la)
    s0f = state_gla_fwd[:, 0].transpose(0, 1, 3, 2)
    s0b = state_gla_bwd[:, 0].transpose(0, 1, 3, 2)
    of, obk = _gla_call(True, N_SAMPLE_SEQ, SAMPLE_LEN, N_PROMPT, q, k, v, la, of, obk, s0f, s0b)

    x3, h2, meta, cnt = pl.pallas_call(
        _odd_out_kernel,
        grid=(NBLK,),
        in_specs=[_TOK(_HV), _TOK(_HV), _TOK(_HV), _TOK(D), _MOD, _full((1, DV_C)), _full((_HV, D)),
                  _full((1, D)), _full((D, LANES)), _full((1, LANES))],
        out_specs=_PROLOGUE_OUT_SPECS,
        out_shape=_PROLOGUE_OUT_SHAPE,
        scratch_shapes=[pltpu.VMEM((SUBLANES, LANES), F32)],
        compiler_params=_params(("arbitrary",)),
        name="odd_out",
    )(of, obk, g, x2, mods[1], gla_o_norm[0].reshape(1, DV_C), odd_w_out[0].astype(BF16),
      norm_ffn[1].reshape(1, D), wr[1], br[1])
    x4 = _moe(x3, h2, meta, cnt, mods[1], moe_w_gu[1], moe_b_gu[1], moe_w_down[1], moe_b_down[1])

    y_prompt = x4[:N_PROMPT].reshape(N_PROMPT_SEQ, PROMPT_LEN, D)
    y_sample = x4[N_PROMPT:].reshape(N_SAMPLE_SEQ, SAMPLE_LEN, D)
    new_ckv = ckv[:N_PROMPT].reshape(N_PROMPT_SEQ, 1, PROMPT_LEN, KV_LORA)
    new_kpe = kpe[:N_PROMPT].reshape(N_PROMPT_SEQ, 1, PROMPT_LEN, ROPE)
    new_fwd = st_f.transpose(0, 1, 3, 2)[:, None]
    new_bwd = st_b.transpose(0, 1, 3, 2)[:, None]
    return (y_prompt, y_sample, new_ckv, new_kpe, new_fwd, new_bwd)
```

```python
import functools

import numpy as np
import jax
import jax.numpy as jnp
from jax import lax
from jax.experimental import pallas as pl
from jax.experimental.pallas import tpu as pltpu

F32 = jnp.float32
BF16 = jnp.bfloat16

D = 1024
N_PROMPT_SEQ, PROMPT_LEN = 16, 256
N_SAMPLE_SEQ, SAMPLE_LEN = 4, 2048
PAST_LEN = 512
N_PROMPT = N_PROMPT_SEQ * PROMPT_LEN
N_SAMPLE = N_SAMPLE_SEQ * SAMPLE_LEN
T = N_PROMPT + N_SAMPLE
EPS = 1e-6
GRID_W = 64
H_A, Q_LORA, KV_LORA, NOPE, ROPE, V_A = 8, 512, 256, 64, 32, 64
QK_DIM = NOPE + ROPE
G_B, C_B, W_B, CHUNK_B = 8, 64, 512, 128
H_C, DK_C, DV_C, GATE_RANK, GATE_TAU, GLA_CHUNK = 4, 128, 256, 16, 16.0, 64
N_EXPERTS, TOP_K, D_FF = 32, 4, 1024
SWIGLU_LIMIT, SWIGLU_ALPHA = 7.0, 1.702
ROPE_THETA = 10000.0

LANES = 128
SUBLANES = 8
VMEM_LIMIT = 56 * 1024 * 1024

TM = 256
NBLK = T // TM
PROMPT_BLKS = N_PROMPT // TM
BLKS_PER_SAMPLE_SEQ = SAMPLE_LEN // TM
ETILE = 256
N_PAIRS = T * TOP_K
N_ETILES = N_PAIRS // ETILE + N_EXPERTS
P_ROWS = N_ETILES * ETILE
HP = LANES


def _cond_row(i):
    return jnp.where(i < PROMPT_BLKS, 0, 1 + (i - PROMPT_BLKS) // BLKS_PER_SAMPLE_SEQ)


def _rope_blk(i):
    return jnp.where(i < PROMPT_BLKS, 0, 1 + (i - PROMPT_BLKS) % BLKS_PER_SAMPLE_SEQ)


def _rms(x):
    return x * lax.rsqrt(jnp.mean(x * x, axis=-1, keepdims=True) + EPS)


def _gelu(x):
    return 0.5 * x * (1.0 + jnp.tanh(0.7978845608028654 * (x + 0.044715 * (x * x * x))))


def _silu(x):
    return x * jax.nn.sigmoid(x)


def _dot(a, b):
    return jnp.dot(a, b, preferred_element_type=F32)


def _dot_nt(a, b):
    return lax.dot_general(a, b, (((1,), (1,)), ((), ())), preferred_element_type=F32)


def _dot_tn(a, b):
    return lax.dot_general(a, b, (((0,), (0,)), ((), ())), preferred_element_type=F32)


def _params(sem, vmem=VMEM_LIMIT):
    return pltpu.CompilerParams(dimension_semantics=sem, vmem_limit_bytes=vmem)


def _full(shape):
    nd = len(shape)
    return pl.BlockSpec(shape, lambda *_: (0,) * nd)


ADA_TN = 1536


def _adaln_kernel(c_ref, w_ref, b_ref, o_ref):
    s = _silu(c_ref[...]).astype(BF16)
    o_ref[0] = _dot(s, w_ref[0].astype(BF16)) + b_ref[0]


def _adaln(cond8, ada_w, ada_b):
    depth = ada_w.shape[0]
    n = ada_w.shape[2]
    out = pl.pallas_call(
        _adaln_kernel,
        grid=(depth, n // ADA_TN),
        in_specs=[
            pl.BlockSpec((SUBLANES, D), lambda l, j: (0, 0)),
            pl.BlockSpec((1, D, ADA_TN), lambda l, j: (l, 0, j)),
            pl.BlockSpec((1, 1, ADA_TN), lambda l, j: (l, 0, j)),
        ],
        out_specs=pl.BlockSpec((1, SUBLANES, ADA_TN), lambda l, j: (l, 0, j)),
        out_shape=jax.ShapeDtypeStruct((depth, SUBLANES, n), F32),
        compiler_params=_params(("arbitrary", "arbitrary")),
        name="adaln",
    )(cond8, ada_w, ada_b.reshape(depth, 1, n))
    return out.reshape(depth, SUBLANES, 6, D)


_QC0, _CKV0, _U0, _V0, _KPE0, _WIN_N = 0, 512, 768, 1280, 1792, 1920


def _rope(y, c, s1, s2):
    return y * c + pltpu.roll(y, LANES - 8, 1) * s1 + pltpu.roll(y, 8, 1) * s2


def _k_heads(k_raw, kpe128, kp_rot, kgain, k_ref):
    sskpe = jnp.sum(kpe128 * kpe128, axis=-1, keepdims=True)
    for h in range(H_A):
        kb = k_raw[:, h * HP:(h + 1) * HP]
        r = lax.rsqrt((jnp.sum(kb * kb, axis=-1, keepdims=True) + sskpe) * (1.0 / QK_DIM) + EPS)
        k_ref[:, h * HP:(h + 1) * HP] = ((kb * kgain + kp_rot) * r).astype(BF16)


def _even_in_kernel(x_ref, mod_ref, nmix_ref, win_ref, qan_ref, wuq_ref, qgain_ref, kvan_ref, wukv_ref,
                    kgain_ref, vnorm_ref, ws_ref, bs_ref, rc_ref, rs1_ref, rs2_ref,
                    q_ref, k_ref, v_ref, ob_ref, ckv_ref, kpe_ref):
    m = mod_ref[0]
    h = _rms(x_ref[...]) * nmix_ref[...] * (1.0 + m[1:2]) + m[0:1]
    a = _dot(h.astype(BF16), win_ref[...])
    qc = a[:, _QC0:_CKV0]
    ckv = a[:, _CKV0:_U0]
    u = a[:, _U0:_V0]
    vv = a[:, _V0:_KPE0]
    kpe128 = a[:, _KPE0:_WIN_N]

    ckv_n = _rms(ckv) * kvan_ref[...]
    ckv_ref[...] = ckv_n
    kpe_ref[...] = kpe128[:, NOPE:QK_DIM]

    rc, rs1, rs2 = rc_ref[...], rs1_ref[...], rs2_ref[...]
    qn = (_rms(qc) * qan_ref[...]).astype(BF16)
    qr = _dot(qn, wuq_ref[...])
    qgain = qgain_ref[...]
    for hh in range(H_A):
        blk = qr[:, hh * HP:(hh + 1) * HP]
        r = lax.rsqrt(jnp.sum(blk * blk, axis=-1, keepdims=True) * (1.0 / QK_DIM) + EPS)
        q_ref[:, hh * HP:(hh + 1) * HP] = _rope(blk * r * qgain, rc, rs1, rs2).astype(BF16)

    kv = _dot(ckv_n.astype(BF16), wukv_ref[...])
    v_ref[...] = kv[:, H_A * HP:].astype(BF16)
    kgain = kgain_ref[...]
    kp_rot = _rope(kpe128 * kgain, rc, rs1, rs2)
    _k_heads(kv[:, :H_A * HP], kpe128, kp_rot, kgain, k_ref)

    ug = _gelu(u)
    vn = (_rms(_gelu(vv)) * vnorm_ref[...]).astype(BF16)
    low = lax.broadcasted_iota(jnp.int32, (CHUNK_B, LANES), 1) < C_B
    for c in range(TM // CHUNK_B):
        rows = slice(c * CHUNK_B, (c + 1) * CHUNK_B)
        for p in range(G_B // 2):
            cols = slice(p * LANES, (p + 1) * LANES)
            blk = vn[rows, cols]
            mixed = jnp.where(low, _dot(ws_ref[2 * p], blk), _dot(ws_ref[2 * p + 1], blk)) + bs_ref[p]
            ob_ref[rows, cols] = (ug[rows, cols] * mixed).astype(BF16)


def _ctx_kv_kernel(ckv_ref, kpe_ref, wukv_ref, kgain_ref, k_ref, v_ref):
    kv = _dot(ckv_ref[...].astype(BF16), wukv_ref[...])
    v_ref[...] = kv[:, H_A * HP:].astype(BF16)
    kgain = kgain_ref[...]
    kpe128 = kpe_ref[...]
    _k_heads(kv[:, :H_A * HP], kpe128, kpe128 * kgain, kgain, k_ref)


def _attn_self_kernel(q_ref, k_ref, v_ref, o_ref):
    for h in range(H_A):
        cols = slice(h * HP, (h + 1) * HP)
        s = _dot_nt(q_ref[:, cols], k_ref[:, cols])
        p = jnp.exp(s - jnp.max(s, axis=-1, keepdims=True))
        inv = 1.0 / jnp.sum(p, axis=-1, keepdims=True)
        o_ref[:, cols] = (_dot(p.astype(BF16), v_ref[:, cols]) * inv).astype(BF16)


def _attn_ctx_kernel(q_ref, k_ref, v_ref, kc_ref, vc_ref, prev_ref, o_ref):
    del prev_ref
    for h in range(H_A):
        cols = slice(h * HP, (h + 1) * HP)
        q = q_ref[:, cols]
        s1 = _dot_nt(q, k_ref[:, cols])
        s2 = _dot_nt(q, kc_ref[:, cols])
        mx = jnp.maximum(jnp.max(s1, axis=-1, keepdims=True), jnp.max(s2, axis=-1, keepdims=True))
        p1 = jnp.exp(s1 - mx)
        p2 = jnp.exp(s2 - mx)
        inv = 1.0 / (jnp.sum(p1, axis=-1, keepdims=True) + jnp.sum(p2, axis=-1, keepdims=True))
        o = _dot(p1.astype(BF16), v_ref[:, cols]) + _dot(p2.astype(BF16), vc_ref[:, cols])
        o_ref[:, cols] = (o * inv).astype(BF16)


def _attention(q, k, v, k_ctx, v_ctx):
    width = H_A * HP
    o = pl.pallas_call(
        _attn_self_kernel,
        grid=(N_PROMPT_SEQ,),
        in_specs=[pl.BlockSpec((PROMPT_LEN, width), lambda i: (i, 0))] * 3,
        out_specs=pl.BlockSpec((PROMPT_LEN, width), lambda i: (i, 0)),
        out_shape=jax.ShapeDtypeStruct((T, width), BF16),
        compiler_params=_params(("arbitrary",)),
        name="attn_prompt",
    )(q, k, v)
    first = N_PROMPT // SAMPLE_LEN
    qblk = lambda b, j: (PROMPT_BLKS + b * BLKS_PER_SAMPLE_SEQ + j, 0)
    return pl.pallas_call(
        _attn_ctx_kernel,
        grid=(N_SAMPLE_SEQ, BLKS_PER_SAMPLE_SEQ),
        in_specs=[
            pl.BlockSpec((TM, width), qblk),
            pl.BlockSpec((SAMPLE_LEN, width), lambda b, j: (first + b, 0)),
            pl.BlockSpec((SAMPLE_LEN, width), lambda b, j: (first + b, 0)),
            pl.BlockSpec((PAST_LEN, width), lambda b, j: (b, 0)),
            pl.BlockSpec((PAST_LEN, width), lambda b, j: (b, 0)),
            pl.BlockSpec(memory_space=pl.ANY),
        ],
        out_specs=pl.BlockSpec((TM, width), qblk),
        out_shape=jax.ShapeDtypeStruct((T, width), BF16),
        input_output_aliases={5: 0},
        compiler_params=_params(("arbitrary", "arbitrary")),
        name="attn_sample",
    )(q, k, v, k_ctx, v_ctx, o)


_META_IDX, _META_RANK, _META_W = 0, TOP_K, 2 * TOP_K


def _moe_prologue(i, x1, m, nffn_ref, wr_ref, br_ref, carry_ref, x1_ref, h2_ref, meta_ref, cnt_ref):
    @pl.when(i == 0)
    def _():
        carry_ref[...] = jnp.zeros_like(carry_ref)

    x1_ref[...] = x1
    h2 = _rms(x1) * nffn_ref[...] * (1.0 + m[4:5]) + m[3:4]
    h2_ref[...] = h2
    lane = lax.broadcasted_iota(jnp.int32, (TM, LANES), 1)
    lanef = lane.astype(F32)
    logits = jnp.dot(h2, wr_ref[...], precision=lax.Precision.HIGHEST, preferred_element_type=F32) + br_ref[...]
    work = jnp.where(lane < N_EXPERTS, logits, -jnp.inf)
    hots, vals = [], []
    for _ in range(TOP_K):
        mx = jnp.max(work, axis=-1, keepdims=True)
        idx = jnp.min(jnp.where(work == mx, lanef, float(LANES)), axis=-1, keepdims=True)
        hot = lanef == idx
        work = jnp.where(hot, -jnp.inf, work)
        hots.append((hot, idx))
        vals.append(mx)
    es = [jnp.exp(v - vals[0]) for v in vals]
    inv = 1.0 / (es[0] + es[1] + es[2] + es[3])
    sel = jnp.zeros((TM, LANES), F32)
    for hot, _ in hots:
        sel = jnp.where(hot, 1.0, sel)
    row = lax.broadcasted_iota(jnp.int32, (TM, TM), 0)
    col = lax.broadcasted_iota(jnp.int32, (TM, TM), 1)
    strict = jnp.where(row > col, 1.0, 0.0).astype(BF16)
    carry = carry_ref[0:1, :]
    before = _dot(strict, sel.astype(BF16)) + carry
    meta = jnp.zeros((TM, LANES), F32)
    for kk, (hot, idx) in enumerate(hots):
        rank = jnp.sum(jnp.where(hot, before, 0.0), axis=-1, keepdims=True)
        meta = jnp.where(lane == _META_IDX + kk, idx, meta)
        meta = jnp.where(lane == _META_RANK + kk, rank, meta)
        meta = jnp.where(lane == _META_W + kk, es[kk] * inv, meta)
    meta_ref[...] = meta
    total = carry + jnp.sum(sel, axis=0, keepdims=True)
    carry_ref[...] = jnp.broadcast_to(total, carry_ref.shape)
    cnt_ref[...] = jnp.broadcast_to(total, cnt_ref.shape)


def _even_out_kernel(oa_ref, ob_ref, x_ref, mod_ref, woa_ref, wob_ref, nffn_ref, wr_ref, br_ref,
                     x1_ref, h2_ref, meta_ref, cnt_ref, carry_ref):
    m = mod_ref[0]
    out = _dot(oa_ref[...], woa_ref[...]) + _dot(ob_ref[...], wob_ref[...])
    x1 = x_ref[...] + m[2:3] * out
    _moe_prologue(pl.program_id(0), x1, m, nffn_ref, wr_ref, br_ref, carry_ref, x1_ref, h2_ref, meta_ref, cnt_ref)


def _odd_out_kernel(of_ref, ob_ref, g_ref, x_ref, mod_ref, onorm_ref, wo_ref, nffn_ref, wr_ref, br_ref,
                    x1_ref, h2_ref, meta_ref, cnt_ref, carry_ref):
    m = mod_ref[0]
    onorm = onorm_ref[...]
    parts = []
    for h in range(H_C):
        cols = slice(h * DV_C, (h + 1) * DV_C)
        o = of_ref[:, cols] + ob_ref[:, cols]
        parts.append((_rms(o) * onorm * _silu(g_ref[:, cols])).astype(BF16))
    out = _dot(jnp.concatenate(parts, axis=-1), wo_ref[...])
    x1 = x_ref[...] + m[2:3] * out
    _moe_prologue(pl.program_id(0), x1, m, nffn_ref, wr_ref, br_ref, carry_ref, x1_ref, h2_ref, meta_ref, cnt_ref)


_TOK = lambda w: pl.BlockSpec((TM, w), lambda i: (i, 0))
_MOD = pl.BlockSpec((1, 6, D), lambda i: (_cond_row(i), 0, 0))

_PROLOGUE_OUT_SPECS = [_TOK(D), _TOK(D), _TOK(LANES), pl.BlockSpec((SUBLANES, LANES), lambda i: (0, 0))]
_PROLOGUE_OUT_SHAPE = [
    jax.ShapeDtypeStruct((T, D), F32),
    jax.ShapeDtypeStruct((T, D), F32),
    jax.ShapeDtypeStruct((T, LANES), F32),
    jax.ShapeDtypeStruct((SUBLANES, LANES), F32),
]


def _dispatch_kernel(pos_ref, h_ref, zero_hbm, xs_hbm, sem):
    del zero_hbm

    def copy(j, k):
        return pltpu.make_async_copy(h_ref.at[pl.ds(j, 1)], xs_hbm.at[pl.ds(pos_ref[0, k, j], 1)], sem)

    def issue(j, carry):
        for k in range(TOP_K):
            copy(j, k).start()
        return carry

    def drain(j, carry):
        for k in range(TOP_K):
            copy(j, k).wait()
        return carry

    lax.fori_loop(0, TM, issue, 0)
    lax.fori_loop(0, TM, drain, 0)


def _dispatch(pos_blocks, h2):
    zeros = jnp.zeros((P_ROWS, D), F32)
    return pl.pallas_call(
        _dispatch_kernel,
        grid=(NBLK,),
        in_specs=[
            pl.BlockSpec((1, TOP_K, TM), lambda i: (i, 0, 0), memory_space=pltpu.SMEM),
            _TOK(D),
            pl.BlockSpec(memory_space=pl.ANY),
        ],
        out_specs=pl.BlockSpec(memory_space=pl.ANY),
        out_shape=jax.ShapeDtypeStruct((P_ROWS, D), F32),
        scratch_shapes=[pltpu.SemaphoreType.DMA],
        input_output_aliases={2: 0},
        compiler_params=_params(("arbitrary",)),
        name="moe_dispatch",
    )(pos_blocks, h2, zeros)


def _expert_kernel(te_ref, tfirst_ref, tvalid_ref, x_ref, wgu_ref, bgu_ref, wd_ref, bd_ref, y_ref,
                   wgu_bf, wd_bf):
    i = pl.program_id(0)

    @pl.when(tfirst_ref[i] == 1)
    def _():
        wgu_bf[...] = wgu_ref[...].astype(BF16)
        wd_bf[...] = wd_ref[...].astype(BF16)

    @pl.when(tvalid_ref[i] == 1)
    def _():
        a = _dot(x_ref[...].astype(BF16), wgu_bf[...]) + bgu_ref[...]
        glu = jnp.minimum(a[:, :D_FF], SWIGLU_LIMIT)
        lin = jnp.clip(a[:, D_FF:], -SWIGLU_LIMIT, SWIGLU_LIMIT)
        act = (glu * jax.nn.sigmoid(SWIGLU_ALPHA * glu)) * (lin + 1.0)
        y_ref[...] = _dot(act.astype(BF16), wd_bf[...]) + bd_ref[...]

    @pl.when(tvalid_ref[i] == 0)
    def _():
        y_ref[...] = jnp.zeros_like(y_ref)


def _experts(layer, tile_expert, tile_first, tile_valid, xs, w_gu, b_gu, w_down, b_down):
    depth = w_gu.shape[0]
    e_of = lambda i, te, tf, tv: (layer, te[i], 0, 0)
    grid_spec = pltpu.PrefetchScalarGridSpec(
        num_scalar_prefetch=3,
        grid=(N_ETILES,),
        in_specs=[
            pl.BlockSpec((ETILE, D), lambda i, te, tf, tv: (i, 0)),
            pl.BlockSpec((None, None, D, 2 * D_FF), e_of),
            pl.BlockSpec((None, None, 1, 2 * D_FF), e_of),
            pl.BlockSpec((None, None, D_FF, D), e_of),
            pl.BlockSpec((None, None, 1, D), e_of),
        ],
        out_specs=pl.BlockSpec((ETILE, D), lambda i, te, tf, tv: (i, 0)),
        scratch_shapes=[pltpu.VMEM((D, 2 * D_FF), BF16), pltpu.VMEM((D_FF, D), BF16)],
    )
    return pl.pallas_call(
        _expert_kernel,
        grid_spec=grid_spec,
        out_shape=jax.ShapeDtypeStruct((P_ROWS, D), F32),
        compiler_params=_params(("arbitrary",)),
        name="moe_experts",
    )(tile_expert, tile_first, tile_valid, xs, w_gu, b_gu.reshape(depth, N_EXPERTS, 1, 2 * D_FF), w_down,
      b_down.reshape(depth, N_EXPERTS, 1, D))


def _combine_kernel(pos_ref, y_hbm, x1_ref, meta_ref, mod_ref, o_ref, buf, sem):
    def copy(j, k):
        return pltpu.make_async_copy(y_hbm.at[pl.ds(pos_ref[0, k, j], 1)], buf.at[k, pl.ds(j, 1)], sem)

    def issue(j, carry):
        for k in range(TOP_K):
            copy(j, k).start()
        return carry

    def drain(j, carry):
        for k in range(TOP_K):
            copy(j, k).wait()
        return carry

    lax.fori_loop(0, TM, issue, 0)
    lax.fori_loop(0, TM, drain, 0)
    meta = meta_ref[...]
    acc = jnp.zeros((TM, D), F32)
    for k in range(TOP_K):
        acc = acc + meta[:, _META_W + k:_META_W + k + 1] * buf[k]
    o_ref[...] = x1_ref[...] + mod_ref[0][5:6] * acc


def _combine(pos_blocks, y, x1, meta, mod):
    return pl.pallas_call(
        _combine_kernel,
        grid=(NBLK,),
        in_specs=[
            pl.BlockSpec((1, TOP_K, TM), lambda i: (i, 0, 0), memory_space=pltpu.SMEM),
            pl.BlockSpec(memory_space=pl.ANY),
            _TOK(D),
            _TOK(LANES),
            _MOD,
        ],
        out_specs=_TOK(D),
        out_shape=jax.ShapeDtypeStruct((T, D), F32),
        scratch_shapes=[pltpu.VMEM((TOP_K, TM, D), F32), pltpu.SemaphoreType.DMA],
        compiler_params=_params(("arbitrary",)),
        name="moe_combine",
    )(pos_blocks, y, x1, meta, mod)


def _moe(layer, x1, h2, meta, cnt, mod, w_gu, b_gu, w_down, b_down):
    counts = cnt[0, :N_EXPERTS].astype(jnp.int32)
    tiles = (counts + ETILE - 1) // ETILE
    tile_end = jnp.cumsum(tiles)
    offs = (tile_end - tiles) * ETILE
    tid = jnp.arange(N_ETILES, dtype=jnp.int32)
    te = jnp.sum((tile_end[None, :] <= tid[:, None]).astype(jnp.int32), axis=1)
    tvalid = (te < N_EXPERTS).astype(jnp.int32)
    last = jnp.max(jnp.where(tiles > 0, jnp.arange(N_EXPERTS, dtype=jnp.int32), 0))
    te = jnp.where(tvalid == 1, te, last)
    tfirst = jnp.concatenate([jnp.ones((1,), jnp.int32), (te[1:] != te[:-1]).astype(jnp.int32)])
    idx = meta[:, _META_IDX:_META_IDX + TOP_K].astype(jnp.int32)
    rank = meta[:, _META_RANK:_META_RANK + TOP_K].astype(jnp.int32)
    pos = offs[idx] + rank
    pos_blocks = pos.reshape(NBLK, TM, TOP_K).transpose(0, 2, 1)
    xs = _dispatch(pos_blocks, h2)
    y = _experts(layer, te, tfirst, tvalid, xs, w_gu, b_gu, w_down, b_down)
    return _combine(pos_blocks, y, x1, meta, mod)


_HK, _HV = H_C * DK_C, H_C * DV_C
_ODD_MAIN = 2 * _HK + 2 * _HV


def _odd_in_kernel(x_ref, mod_ref, nmix_ref, win_ref, wgk_ref, bgk_ref, q_ref, k_ref, v_ref, g_ref, la_ref):
    m = mod_ref[0]
    h = _rms(x_ref[...]) * nmix_ref[...] * (1.0 + m[1:2]) + m[0:1]
    a = _dot(h.astype(BF16), win_ref[...])
    q_ref[...] = a[:, :_HK] * (DK_C ** -0.5)
    k_ref[...] = a[:, _HK:2 * _HK]
    v_ref[...] = a[:, 2 * _HK:2 * _HK + _HV]
    g_ref[...] = a[:, 2 * _HK + _HV:_ODD_MAIN]
    z = _dot(a[:, _ODD_MAIN:].astype(BF16), wgk_ref[...]) + bgk_ref[...]
    la_ref[...] = (jnp.minimum(z, 0.0) - jnp.log(1.0 + jnp.exp(-jnp.abs(z)))) * (1.0 / GATE_TAU)


def _gla_kernel(has_init, nchunk, *refs):
    if has_init:
        (qf, kf, vf, laf, qb, kb, vb, lab, s0f, s0b, _, _, of_ref, ob_ref, st) = refs
    else:
        (qf, kf, vf, laf, qb, kb, vb, lab, of_ref, ob_ref, sf_ref, sb_ref, st) = refs
    j = pl.program_id(1)

    @pl.when(j == 0)
    def _():
        if has_init:
            st[0] = s0f[0]
            st[1] = s0b[0]
        else:
            st[...] = jnp.zeros_like(st)

    row = lax.broadcasted_iota(jnp.int32, (GLA_CHUNK, GLA_CHUNK), 0)
    col = lax.broadcasted_iota(jnp.int32, (GLA_CHUNK, GLA_CHUNK), 1)
    for d, (q_r, k_r, v_r, la_r, o_r) in enumerate(((qf, kf, vf, laf, of_ref), (qb, kb, vb, lab, ob_ref))):
        keep = (col <= row) if d == 0 else (col >= row)
        tri = jnp.where(keep, 1.0, 0.0)
        for h in range(H_C):
            kc = slice(h * DK_C, (h + 1) * DK_C)
            vc = slice(h * DV_C, (h + 1) * DV_C)
            g = la_r[:, kc]
            c = jnp.dot(tri, g, precision=lax.Precision.HIGHEST, preferred_element_type=F32)
            tot = jnp.sum(g, axis=0, keepdims=True)
            q, k, v = q_r[:, kc], k_r[:, kc], v_r[:, vc].astype(BF16)
            qe = (q * jnp.exp(c)).astype(BF16)
            kd = (k * jnp.exp(-c)).astype(BF16)
            att = jnp.where(keep, _dot_nt(qe, kd), 0.0)
            s_t = st[d, h]
            o_r[:, vc] = _dot_nt(qe, s_t.astype(BF16)) + _dot(att.astype(BF16), v)
            k2 = (k * jnp.exp(tot - c)).astype(BF16)
            st[d, h] = s_t * jnp.exp(tot) + _dot_tn(v, k2)

    if not has_init:
        @pl.when(j == nchunk - 1)
        def _():
            sf_ref[0] = st[0]
            sb_ref[0] = st[1]


def _gla_call(has_init, nseq, seqlen, row0, q, k, v, la, prev_f=None, prev_b=None, s0f=None, s0b=None):
    nchunk = seqlen // GLA_CHUNK
    blk0 = row0 // GLA_CHUNK
    fwd = lambda b, j: (blk0 + b * nchunk + j, 0)
    bwd = lambda b, j: (blk0 + b * nchunk + nchunk - 1 - j, 0)
    bwd_la = lambda b, j: (blk0 + b * nchunk + nchunk - 1 - j, 1)
    state_spec = pl.BlockSpec((1, H_C, DV_C, DK_C), lambda b, j: (b, 0, 0, 0))
    in_specs = [
        pl.BlockSpec((GLA_CHUNK, _HK), fwd), pl.BlockSpec((GLA_CHUNK, _HK), fwd),
        pl.BlockSpec((GLA_CHUNK, _HV), fwd), pl.BlockSpec((GLA_CHUNK, _HK), fwd),
        pl.BlockSpec((GLA_CHUNK, _HK), bwd), pl.BlockSpec((GLA_CHUNK, _HK), bwd),
        pl.BlockSpec((GLA_CHUNK, _HV), bwd), pl.BlockSpec((GLA_CHUNK, _HK), bwd_la),
    ]
    args = [q, k, v, la, q, k, v, la]
    aliases = {}
    if has_init:
        in_specs += [state_spec, state_spec] + [pl.BlockSpec(memory_space=pl.ANY)] * 2
        args += [s0f, s0b, prev_f, prev_b]
        aliases = {len(args) - 2: 0, len(args) - 1: 1}
    out_specs = [pl.BlockSpec((GLA_CHUNK, _HV), fwd), pl.BlockSpec((GLA_CHUNK, _HV), bwd)]
    out_shape = [jax.ShapeDtypeStruct((T, _HV), F32)] * 2
    if not has_init:
        out_specs += [state_spec, state_spec]
        out_shape += [jax.ShapeDtypeStruct((nseq, H_C, DV_C, DK_C), F32)] * 2
    return pl.pallas_call(
        functools.partial(_gla_kernel, has_init, nchunk),
        grid=(nseq, nchunk),
        in_specs=in_specs,
        out_specs=out_specs,
        out_shape=out_shape,
        scratch_shapes=[pltpu.VMEM((2, H_C, DV_C, DK_C), F32)],
        input_output_aliases=aliases,
        compiler_params=_params(("arbitrary", "arbitrary")),
        name="gla_sample" if has_init else "gla_prompt",
    )(*args)


def _rope_tables():
    half = ROPE // 2
    inv_freq = np.power(np.float32(ROPE_THETA), -np.arange(0, half, 2, dtype=np.float32) / np.float32(half))
    n = np.arange(SAMPLE_LEN)
    row = (n // GRID_W).astype(np.float32)
    col = (n % GRID_W).astype(np.float32)
    ang_r = (row[:, None] * inv_freq[None, :]).astype(np.float32)
    ang_c = (col[:, None] * inv_freq[None, :]).astype(np.float32)
    nf = half // 2
    c = np.ones((TM + SAMPLE_LEN, LANES), np.float32)
    s1 = np.zeros((TM + SAMPLE_LEN, LANES), np.float32)
    s2 = np.zeros((TM + SAMPLE_LEN, LANES), np.float32)
    for base, ang in ((NOPE, ang_r), (NOPE + half, ang_c)):
        c[TM:, base:base + nf] = np.cos(ang)
        c[TM:, base + nf:base + half] = np.cos(ang)
        s1[TM:, base:base + nf] = -np.sin(ang)
        s2[TM:, base + nf:base + half] = np.sin(ang)
    return jnp.asarray(c), jnp.asarray(s1), jnp.asarray(s2)


def _pad_heads(w, nheads, width, lo=0):
    k = w.shape[0]
    w = w.reshape(k, nheads, width)
    w = jnp.pad(w, ((0, 0), (0, 0), (lo, HP - lo - width)))
    return w.reshape(k, nheads * HP)


def _row128(v, lo=0):
    return jnp.pad(v, (lo, LANES - lo - v.shape[0])).reshape(1, LANES)


def _even_layer(x, mod, nmix, w_in, q_a_norm, w_uq, q_norm, kv_a_norm, w_ukv, k_norm, v_norm, w_s, b_s, w_out,
                cache_ckv, cache_kpe):
    s = np.cumsum([Q_LORA, KV_LORA, ROPE, W_B])
    w_q, w_ckv, w_kpe, w_u, w_v = (w_in[:, :s[0]], w_in[:, s[0]:s[1]], w_in[:, s[1]:s[2]], w_in[:, s[2]:s[3]],
                                   w_in[:, s[3]:])
    w_kpe = jnp.pad(w_kpe, ((0, 0), (NOPE, LANES - QK_DIM)))
    win = jnp.concatenate([w_q, w_ckv, w_u, w_v, w_kpe], axis=1).astype(BF16)
    wuq = _pad_heads(w_uq, H_A, QK_DIM).astype(BF16)
    ukv = w_ukv.reshape(KV_LORA, H_A, NOPE + V_A)
    wuk = _pad_heads(ukv[:, :, :NOPE].reshape(KV_LORA, H_A * NOPE), H_A, NOPE)
    wuv = _pad_heads(ukv[:, :, NOPE:].reshape(KV_LORA, H_A * V_A), H_A, V_A)
    wukv = jnp.concatenate([wuk, wuv], axis=1).astype(BF16)
    qgain = _row128(q_norm * (QK_DIM ** -0.5))
    kgain = _row128(k_norm)
    bias = b_s.reshape(G_B // 2, 2, CHUNK_B)
    bias = jnp.concatenate([jnp.broadcast_to(bias[:, 0, :, None], (G_B // 2, CHUNK_B, C_B)),
                            jnp.broadcast_to(bias[:, 1, :, None], (G_B // 2, CHUNK_B, C_B))], axis=-1)
    rc, rs1, rs2 = _rope_tables()
    rope_spec = pl.BlockSpec((TM, LANES), lambda i: (_rope_blk(i), 0))
    width = H_A * HP
    q, k, v, ob, ckv, kpe = pl.pallas_call(
        _even_in_kernel,
        grid=(NBLK,),
        in_specs=[
            _TOK(D), _MOD, _full((1, D)), _full((D, _WIN_N)), _full((1, Q_LORA)), _full((Q_LORA, width)),
            _full((1, LANES)), _full((1, KV_LORA)), _full((KV_LORA, 2 * width)), _full((1, LANES)),
            _full((1, W_B)), _full((G_B, CHUNK_B, CHUNK_B)), _full((G_B // 2, CHUNK_B, LANES)),
            rope_spec, rope_spec, rope_spec,
        ],
        out_specs=[_TOK(width), _TOK(width), _TOK(width), _TOK(W_B), _TOK(KV_LORA), _TOK(ROPE)],
        out_shape=[
            jax.ShapeDtypeStruct((T, width), BF16), jax.ShapeDtypeStruct((T, width), BF16),
            jax.ShapeDtypeStruct((T, width), BF16), jax.ShapeDtypeStruct((T, W_B), BF16),
            jax.ShapeDtypeStruct((T, KV_LORA), F32), jax.ShapeDtypeStruct((T, ROPE), F32),
        ],
        compiler_params=_params(("arbitrary",)),
        name="even_in",
    )(x, mod, nmix.reshape(1, D), win, q_a_norm.reshape(1, Q_LORA), wuq, qgain, kv_a_norm.reshape(1, KV_LORA),
      wukv, kgain, v_norm.reshape(1, W_B), w_s.astype(BF16), bias, rc, rs1, rs2)

    n_ctx = N_SAMPLE_SEQ * PAST_LEN
    kpe_ctx = jnp.pad(cache_kpe.reshape(n_ctx, ROPE), ((0, 0), (NOPE, LANES - QK_DIM)))
    k_ctx, v_ctx = pl.pallas_call(
        _ctx_kv_kernel,
        grid=(n_ctx // TM,),
        in_specs=[_TOK(KV_LORA), _TOK(LANES), _full((KV_LORA, 2 * width)), _full((1, LANES))],
        out_specs=[_TOK(width), _TOK(width)],
        out_shape=[jax.ShapeDtypeStruct((n_ctx, width), BF16)] * 2,
        compiler_params=_params(("arbitrary",)),
        name="ctx_kv",
    )(cache_ckv.reshape(n_ctx, KV_LORA), kpe_ctx, wukv, kgain)

    oa = _attention(q, k, v, k_ctx, v_ctx)
    woa = jnp.pad(w_out[:H_A * V_A].reshape(H_A, V_A, D), ((0, 0), (0, HP - V_A), (0, 0))).reshape(width, D)
    return oa, ob, woa.astype(BF16), w_out[H_A * V_A:].astype(BF16), ckv, kpe


def kernel(x_prompt, x_sample, cache_mla_ckv, cache_mla_kpe, state_gla_fwd, state_gla_bwd, c, c_ctx, ada_w, ada_b,
           norm_mix, norm_ffn, even_w_in, mla_q_a_norm, mla_w_uq, mla_q_norm, mla_kv_a_norm, mla_w_ukv, mla_k_norm,
           cmlp_v_norm, cmlp_w_s, cmlp_b_s, even_w_out, odd_w_in, gla_w_gk_fwd, gla_b_gk_fwd, gla_w_gk_bwd,
           gla_b_gk_bwd, gla_o_norm, odd_w_out, moe_w_router, moe_b_router, moe_w_gu, moe_b_gu, moe_w_down,
           moe_b_down):
    x0 = jnp.concatenate([x_prompt.reshape(N_PROMPT, D), x_sample.reshape(N_SAMPLE, D)], axis=0)
    cond8 = jnp.concatenate([c_ctx[None, :], c, jnp.zeros((SUBLANES - 1 - N_SAMPLE_SEQ, D), F32)], axis=0)
    mods = _adaln(cond8, ada_w, ada_b)
    wr = jnp.pad(moe_w_router, ((0, 0), (0, 0), (0, LANES - N_EXPERTS)))
    br = jnp.pad(moe_b_router, ((0, 0), (0, LANES - N_EXPERTS))).reshape(2, 1, LANES)

    oa, ob, woa, wob, ckv, kpe = _even_layer(
        x0, mods[0], norm_mix[0], even_w_in[0], mla_q_a_norm[0], mla_w_uq[0], mla_q_norm[0], mla_kv_a_norm[0],
        mla_w_ukv[0], mla_k_norm[0], cmlp_v_norm[0], cmlp_w_s[0], cmlp_b_s[0], even_w_out[0],
        cache_mla_ckv[:, 0], cache_mla_kpe[:, 0])
    width = H_A * HP
    x1, h2, meta, cnt = pl.pallas_call(
        _even_out_kernel,
        grid=(NBLK,),
        in_specs=[_TOK(width), _TOK(W_B), _TOK(D), _MOD, _full((width, D)), _full((W_B, D)), _full((1, D)),
                  _full((D, LANES)), _full((1, LANES))],
        out_specs=_PROLOGUE_OUT_SPECS,
        out_shape=_PROLOGUE_OUT_SHAPE,
        scratch_shapes=[pltpu.VMEM((SUBLANES, LANES), F32)],
        compiler_params=_params(("arbitrary",)),
        name="even_out",
    )(oa, ob, x0, mods[0], woa, wob, norm_ffn[0].reshape(1, D), wr[0], br[0])
    x2 = _moe(0, x1, h2, meta, cnt, mods[0], moe_w_gu, moe_b_gu, moe_w_down, moe_b_down)

    w_in = odd_w_in[0]
    win = jnp.concatenate([w_in, jnp.zeros((D, LANES - 2 * GATE_RANK), F32)], axis=1).astype(BF16)
    wgk = jnp.zeros((LANES, 2 * _HK), F32)
    wgk = wgk.at[:GATE_RANK, :_HK].set(gla_w_gk_fwd[0]).at[GATE_RANK:2 * GATE_RANK, _HK:].set(gla_w_gk_bwd[0])
    bgk = jnp.concatenate([gla_b_gk_fwd[0], gla_b_gk_bwd[0]]).reshape(1, 2 * _HK)
    q, k, v, g, la = pl.pallas_call(
        _odd_in_kernel,
        grid=(NBLK,),
        in_specs=[_TOK(D), _MOD, _full((1, D)), _full((D, _ODD_MAIN + LANES)), _full((LANES, 2 * _HK)),
                  _full((1, 2 * _HK))],
        out_specs=[_TOK(_HK), _TOK(_HK), _TOK(_HV), _TOK(_HV), _TOK(2 * _HK)],
        out_shape=[jax.ShapeDtypeStruct((T, _HK), F32), jax.ShapeDtypeStruct((T, _HK), F32),
                   jax.ShapeDtypeStruct((T, _HV), F32), jax.ShapeDtypeStruct((T, _HV), F32),
                   jax.ShapeDtypeStruct((T, 2 * _HK), F32)],
        compiler_params=_params(("arbitrary",)),
        name="odd_in",
    )(x2, mods[1], norm_mix[1].reshape(1, D), win, wgk.astype(BF16), bgk)

    of, obk, st_f, st_b = _gla_call(False, N_PROMPT_SEQ, PROMPT_LEN, 0, q, k, v, la)
    s0f = state_gla_fwd[:, 0].transpose(0, 1, 3, 2)
    s0b = state_gla_bwd[:, 0].transpose(0, 1, 3, 2)
    of, obk = _gla_call(True, N_SAMPLE_SEQ, SAMPLE_LEN, N_PROMPT, q, k, v, la, of, obk, s0f, s0b)

    x3, h2, meta, cnt = pl.pallas_call(
        _odd_out_kernel,
        grid=(NBLK,),
        in_specs=[_TOK(_HV), _TOK(_HV), _TOK(_HV), _TOK(D), _MOD, _full((1, DV_C)), _full((_HV, D)),
                  _full((1, D)), _full((D, LANES)), _full((1, LANES))],
        out_specs=_PROLOGUE_OUT_SPECS,
        out_shape=_PROLOGUE_OUT_SHAPE,
        scratch_shapes=[pltpu.VMEM((SUBLANES, LANES), F32)],
        compiler_params=_params(("arbitrary",)),
        name="odd_out",
    )(of, obk, g, x2, mods[1], gla_o_norm[0].reshape(1, DV_C), odd_w_out[0].astype(BF16),
      norm_ffn[1].reshape(1, D), wr[1], br[1])
    x4 = _moe(1, x3, h2, meta, cnt, mods[1], moe_w_gu, moe_b_gu, moe_w_down, moe_b_down)

    y_prompt = x4[:N_PROMPT].reshape(N_PROMPT_SEQ, PROMPT_LEN, D)
    y_sample = x4[N_PROMPT:].reshape(N_SAMPLE_SEQ, SAMPLE_LEN, D)
    new_ckv = ckv[:N_PROMPT].reshape(N_PROMPT_SEQ, 1, PROMPT_LEN, KV_LORA)
    new_kpe = kpe[:N_PROMPT].reshape(N_PROMPT_SEQ, 1, PROMPT_LEN, ROPE)
    new_fwd = st_f.transpose(0, 1, 3, 2)[:, None]
    new_bwd = st_b.transpose(0, 1, 3, 2)[:, None]
    return (y_prompt, y_sample, new_ckv, new_kpe, new_fwd, new_bwd)
```

```python
import functools

import numpy as np
import jax
import jax.numpy as jnp
from jax import lax
from jax.experimental import pallas as pl
from jax.experimental.pallas import tpu as pltpu

F32 = jnp.float32
BF16 = jnp.bfloat16

D = 1024
N_PROMPT_SEQ, PROMPT_LEN = 16, 256
N_SAMPLE_SEQ, SAMPLE_LEN = 4, 2048
PAST_LEN = 512
N_PROMPT = N_PROMPT_SEQ * PROMPT_LEN
N_SAMPLE = N_SAMPLE_SEQ * SAMPLE_LEN
T = N_PROMPT + N_SAMPLE
EPS = 1e-6
GRID_W = 64
H_A, Q_LORA, KV_LORA, NOPE, ROPE, V_A = 8, 512, 256, 64, 32, 64
QK_DIM = NOPE + ROPE
G_B, C_B, W_B, CHUNK_B = 8, 64, 512, 128
H_C, DK_C, DV_C, GATE_RANK, GATE_TAU, GLA_CHUNK = 4, 128, 256, 16, 16.0, 64
N_EXPERTS, TOP_K, D_FF = 32, 4, 1024
SWIGLU_LIMIT, SWIGLU_ALPHA = 7.0, 1.702
ROPE_THETA = 10000.0

LANES = 128
SUBLANES = 8
VMEM_LIMIT = 56 * 1024 * 1024

TM = 256
NBLK = T // TM
PROMPT_BLKS = N_PROMPT // TM
BLKS_PER_SAMPLE_SEQ = SAMPLE_LEN // TM
ETILE = 256
CH = SUBLANES
_MAX_LOCAL = TM * TOP_K + N_EXPERTS * (CH - 1)
ROWS_L = -(-_MAX_LOCAL // LANES) * LANES
N_ETILES = -(-NBLK * _MAX_LOCAL // ETILE) + N_EXPERTS
P_ROWS = N_ETILES * ETILE
HP = LANES


def _cond_row(i):
    return jnp.where(i < PROMPT_BLKS, 0, 1 + (i - PROMPT_BLKS) // BLKS_PER_SAMPLE_SEQ)


def _rope_blk(i):
    return jnp.where(i < PROMPT_BLKS, 0, 1 + (i - PROMPT_BLKS) % BLKS_PER_SAMPLE_SEQ)


def _rms(x):
    return x * lax.rsqrt(jnp.mean(x * x, axis=-1, keepdims=True) + EPS)


def _gelu(x):
    return 0.5 * x * (1.0 + jnp.tanh(0.7978845608028654 * (x + 0.044715 * (x * x * x))))


def _silu(x):
    return x * jax.nn.sigmoid(x)


def _dot(a, b):
    return jnp.dot(a, b, preferred_element_type=F32)


def _dot_nt(a, b):
    return lax.dot_general(a, b, (((1,), (1,)), ((), ())), preferred_element_type=F32)


def _dot_tn(a, b):
    return lax.dot_general(a, b, (((0,), (0,)), ((), ())), preferred_element_type=F32)


def _params(sem, vmem=VMEM_LIMIT):
    return pltpu.CompilerParams(dimension_semantics=sem, vmem_limit_bytes=vmem)


def _full(shape):
    nd = len(shape)
    return pl.BlockSpec(shape, lambda *_: (0,) * nd)


ADA_TN = 1536


def _adaln_kernel(c_ref, w_ref, b_ref, o_ref):
    s = _silu(c_ref[...]).astype(BF16)
    o_ref[0] = _dot(s, w_ref[0].astype(BF16)) + b_ref[0]


def _adaln(cond8, ada_w, ada_b):
    depth = ada_w.shape[0]
    n = ada_w.shape[2]
    out = pl.pallas_call(
        _adaln_kernel,
        grid=(depth, n // ADA_TN),
        in_specs=[
            pl.BlockSpec((SUBLANES, D), lambda l, j: (0, 0)),
            pl.BlockSpec((1, D, ADA_TN), lambda l, j: (l, 0, j)),
            pl.BlockSpec((1, 1, ADA_TN), lambda l, j: (l, 0, j)),
        ],
        out_specs=pl.BlockSpec((1, SUBLANES, ADA_TN), lambda l, j: (l, 0, j)),
        out_shape=jax.ShapeDtypeStruct((depth, SUBLANES, n), F32),
        compiler_params=_params(("arbitrary", "arbitrary")),
        name="adaln",
    )(cond8, ada_w, ada_b.reshape(depth, 1, n))
    return out.reshape(depth, SUBLANES, 6, D)


_QC0, _CKV0, _U0, _V0, _KPE0, _WIN_N = 0, 512, 768, 1280, 1792, 1920


def _rope(y, c, s1, s2):
    return y * c + pltpu.roll(y, LANES - 8, 1) * s1 + pltpu.roll(y, 8, 1) * s2


def _k_heads(k_raw, kpe128, kp_rot, kgain, k_ref):
    sskpe = jnp.sum(kpe128 * kpe128, axis=-1, keepdims=True)
    for h in range(H_A):
        kb = k_raw[:, h * HP:(h + 1) * HP]
        r = lax.rsqrt((jnp.sum(kb * kb, axis=-1, keepdims=True) + sskpe) * (1.0 / QK_DIM) + EPS)
        k_ref[:, h * HP:(h + 1) * HP] = ((kb * kgain + kp_rot) * r).astype(BF16)


def _even_in_kernel(x_ref, mod_ref, nmix_ref, win_ref, qan_ref, wuq_ref, qgain_ref, kvan_ref, wukv_ref,
                    kgain_ref, vnorm_ref, ws_ref, bs_ref, rc_ref, rs1_ref, rs2_ref,
                    q_ref, k_ref, v_ref, ob_ref, ckv_ref, kpe_ref):
    m = mod_ref[0]
    h = _rms(x_ref[...]) * nmix_ref[...] * (1.0 + m[1:2]) + m[0:1]
    a = _dot(h.astype(BF16), win_ref[...])
    qc = a[:, _QC0:_CKV0]
    ckv = a[:, _CKV0:_U0]
    u = a[:, _U0:_V0]
    vv = a[:, _V0:_KPE0]
    kpe128 = a[:, _KPE0:_WIN_N]

    ckv_n = _rms(ckv) * kvan_ref[...]
    ckv_ref[...] = ckv_n
    kpe_ref[...] = kpe128[:, NOPE:QK_DIM]

    rc, rs1, rs2 = rc_ref[...], rs1_ref[...], rs2_ref[...]
    qn = (_rms(qc) * qan_ref[...]).astype(BF16)
    qr = _dot(qn, wuq_ref[...])
    qgain = qgain_ref[...]
    for hh in range(H_A):
        blk = qr[:, hh * HP:(hh + 1) * HP]
        r = lax.rsqrt(jnp.sum(blk * blk, axis=-1, keepdims=True) * (1.0 / QK_DIM) + EPS)
        q_ref[:, hh * HP:(hh + 1) * HP] = _rope(blk * r * qgain, rc, rs1, rs2).astype(BF16)

    kv = _dot(ckv_n.astype(BF16), wukv_ref[...])
    v_ref[...] = kv[:, H_A * HP:].astype(BF16)
    kgain = kgain_ref[...]
    kp_rot = _rope(kpe128 * kgain, rc, rs1, rs2)
    _k_heads(kv[:, :H_A * HP], kpe128, kp_rot, kgain, k_ref)

    ug = _gelu(u)
    vn = (_rms(_gelu(vv)) * vnorm_ref[...]).astype(BF16)
    low = lax.broadcasted_iota(jnp.int32, (CHUNK_B, LANES), 1) < C_B
    for c in range(TM // CHUNK_B):
        rows = slice(c * CHUNK_B, (c + 1) * CHUNK_B)
        for p in range(G_B // 2):
            cols = slice(p * LANES, (p + 1) * LANES)
            blk = vn[rows, cols]
            mixed = jnp.where(low, _dot(ws_ref[2 * p], blk), _dot(ws_ref[2 * p + 1], blk)) + bs_ref[p]
            ob_ref[rows, cols] = (ug[rows, cols] * mixed).astype(BF16)


def _ctx_kv_kernel(ckv_ref, kpe_ref, wukv_ref, kgain_ref, k_ref, v_ref):
    kv = _dot(ckv_ref[...].astype(BF16), wukv_ref[...])
    v_ref[...] = kv[:, H_A * HP:].astype(BF16)
    kgain = kgain_ref[...]
    kpe128 = kpe_ref[...]
    _k_heads(kv[:, :H_A * HP], kpe128, kpe128 * kgain, kgain, k_ref)


def _attn_self_kernel(q_ref, k_ref, v_ref, o_ref):
    for h in range(H_A):
        cols = slice(h * HP, (h + 1) * HP)
        s = _dot_nt(q_ref[:, cols], k_ref[:, cols])
        p = jnp.exp(s - jnp.max(s, axis=-1, keepdims=True))
        inv = 1.0 / jnp.sum(p, axis=-1, keepdims=True)
        o_ref[:, cols] = (_dot(p.astype(BF16), v_ref[:, cols]) * inv).astype(BF16)


def _attn_ctx_kernel(q_ref, k_ref, v_ref, kc_ref, vc_ref, prev_ref, o_ref):
    del prev_ref
    for h in range(H_A):
        cols = slice(h * HP, (h + 1) * HP)
        q = q_ref[:, cols]
        s1 = _dot_nt(q, k_ref[:, cols])
        s2 = _dot_nt(q, kc_ref[:, cols])
        mx = jnp.maximum(jnp.max(s1, axis=-1, keepdims=True), jnp.max(s2, axis=-1, keepdims=True))
        p1 = jnp.exp(s1 - mx)
        p2 = jnp.exp(s2 - mx)
        inv = 1.0 / (jnp.sum(p1, axis=-1, keepdims=True) + jnp.sum(p2, axis=-1, keepdims=True))
        o = _dot(p1.astype(BF16), v_ref[:, cols]) + _dot(p2.astype(BF16), vc_ref[:, cols])
        o_ref[:, cols] = (o * inv).astype(BF16)


def _attention(q, k, v, k_ctx, v_ctx):
    width = H_A * HP
    o = pl.pallas_call(
        _attn_self_kernel,
        grid=(N_PROMPT_SEQ,),
        in_specs=[pl.BlockSpec((PROMPT_LEN, width), lambda i: (i, 0))] * 3,
        out_specs=pl.BlockSpec((PROMPT_LEN, width), lambda i: (i, 0)),
        out_shape=jax.ShapeDtypeStruct((T, width), BF16),
        compiler_params=_params(("arbitrary",)),
        name="attn_prompt",
    )(q, k, v)
    first = N_PROMPT // SAMPLE_LEN
    qblk = lambda b, j: (PROMPT_BLKS + b * BLKS_PER_SAMPLE_SEQ + j, 0)
    return pl.pallas_call(
        _attn_ctx_kernel,
        grid=(N_SAMPLE_SEQ, BLKS_PER_SAMPLE_SEQ),
        in_specs=[
            pl.BlockSpec((TM, width), qblk),
            pl.BlockSpec((SAMPLE_LEN, width), lambda b, j: (first + b, 0)),
            pl.BlockSpec((SAMPLE_LEN, width), lambda b, j: (first + b, 0)),
            pl.BlockSpec((PAST_LEN, width), lambda b, j: (b, 0)),
            pl.BlockSpec((PAST_LEN, width), lambda b, j: (b, 0)),
            pl.BlockSpec(memory_space=pl.ANY),
        ],
        out_specs=pl.BlockSpec((TM, width), qblk),
        out_shape=jax.ShapeDtypeStruct((T, width), BF16),
        input_output_aliases={5: 0},
        compiler_params=_params(("arbitrary", "arbitrary")),
        name="attn_sample",
    )(q, k, v, k_ctx, v_ctx, o)


_META_IDX, _META_RANK, _META_W = 0, TOP_K, 2 * TOP_K


def _moe_prologue(x1, m, nffn_ref, wr_ref, br_ref, x1_ref, h2_ref, meta_ref, cnt_ref):
    x1_ref[...] = x1
    h2 = _rms(x1) * nffn_ref[...] * (1.0 + m[4:5]) + m[3:4]
    h2_ref[...] = h2.astype(BF16)
    lane = lax.broadcasted_iota(jnp.int32, (TM, LANES), 1)
    lanef = lane.astype(F32)
    logits = jnp.dot(h2, wr_ref[...], precision=lax.Precision.HIGHEST, preferred_element_type=F32) + br_ref[...]
    work = jnp.where(lane < N_EXPERTS, logits, -jnp.inf)
    hots, vals = [], []
    for _ in range(TOP_K):
        mx = jnp.max(work, axis=-1, keepdims=True)
        idx = jnp.min(jnp.where(work == mx, lanef, float(LANES)), axis=-1, keepdims=True)
        hot = lanef == idx
        work = jnp.where(hot, -jnp.inf, work)
        hots.append((hot, idx))
        vals.append(mx)
    es = [jnp.exp(v - vals[0]) for v in vals]
    inv = 1.0 / (es[0] + es[1] + es[2] + es[3])
    sel = jnp.zeros((TM, LANES), F32)
    for hot, _ in hots:
        sel = jnp.where(hot, 1.0, sel)
    row = lax.broadcasted_iota(jnp.int32, (TM, TM), 0)
    col = lax.broadcasted_iota(jnp.int32, (TM, TM), 1)
    strict = jnp.where(row > col, 1.0, 0.0).astype(BF16)
    before = _dot(strict, sel.astype(BF16))
    meta = jnp.zeros((TM, LANES), F32)
    for kk, (hot, idx) in enumerate(hots):
        rank = jnp.sum(jnp.where(hot, before, 0.0), axis=-1, keepdims=True)
        meta = jnp.where(lane == _META_IDX + kk, idx, meta)
        meta = jnp.where(lane == _META_RANK + kk, rank, meta)
        meta = jnp.where(lane == _META_W + kk, es[kk] * inv, meta)
    meta_ref[...] = meta
    cnt_ref[0] = jnp.broadcast_to(jnp.sum(sel, axis=0, keepdims=True), (SUBLANES, LANES))


def _even_out_kernel(oa_ref, ob_ref, x_ref, mod_ref, woa_ref, wob_ref, nffn_ref, wr_ref, br_ref,
                     x1_ref, h2_ref, meta_ref, cnt_ref):
    m = mod_ref[0]
    out = _dot(oa_ref[...], woa_ref[...]) + _dot(ob_ref[...], wob_ref[...])
    x1 = x_ref[...] + m[2:3] * out
    _moe_prologue(x1, m, nffn_ref, wr_ref, br_ref, x1_ref, h2_ref, meta_ref, cnt_ref)


def _odd_out_kernel(of_ref, ob_ref, g_ref, x_ref, mod_ref, onorm_ref, wo_ref, nffn_ref, wr_ref, br_ref,
                    x1_ref, h2_ref, meta_ref, cnt_ref):
    m = mod_ref[0]
    onorm = onorm_ref[...]
    parts = []
    for h in range(H_C):
        cols = slice(h * DV_C, (h + 1) * DV_C)
        o = of_ref[:, cols] + ob_ref[:, cols]
        parts.append((_rms(o) * onorm * _silu(g_ref[:, cols])).astype(BF16))
    out = _dot(jnp.concatenate(parts, axis=-1), wo_ref[...])
    x1 = x_ref[...] + m[2:3] * out
    _moe_prologue(x1, m, nffn_ref, wr_ref, br_ref, x1_ref, h2_ref, meta_ref, cnt_ref)


_TOK = lambda w: pl.BlockSpec((TM, w), lambda i, *_: (i, 0))
_MOD = pl.BlockSpec((1, 6, D), lambda i, *_: (_cond_row(i), 0, 0))
_TILE_ROW = pl.BlockSpec((1, SUBLANES, LANES), lambda i, *_: (i, 0, 0))

_PROLOGUE_OUT_SPECS = [_TOK(D), _TOK(D), _TOK(LANES), _TILE_ROW]
_PROLOGUE_OUT_SHAPE = [
    jax.ShapeDtypeStruct((T, D), F32),
    jax.ShapeDtypeStruct((T, D), BF16),
    jax.ShapeDtypeStruct((T, LANES), F32),
    jax.ShapeDtypeStruct((NBLK, SUBLANES, LANES), F32),
]


def _local_positions(meta, seg_row):
    lanef = lax.broadcasted_iota(jnp.int32, (TM, LANES), 1).astype(F32)
    pos = []
    for k in range(TOP_K):
        hot = lanef == meta[:, _META_IDX + k:_META_IDX + k + 1]
        start = jnp.sum(jnp.where(hot, seg_row, 0.0), axis=-1, keepdims=True)
        pos.append(start + meta[:, _META_RANK + k:_META_RANK + k + 1])
    return pos


def _segment_copies(i, base_ref, seg_ref, nch_ref, make, wait):
    def per_expert(e, carry):
        q = i * N_EXPERTS + e
        local0, global0 = seg_ref[q], base_ref[q]

        def one(j, c):
            cp = make(pl.multiple_of(local0 + j * CH, CH), pl.multiple_of(global0 + j * CH, CH))
            if wait:
                cp.wait()
            else:
                cp.start()
            return c

        return lax.fori_loop(0, nch_ref[q], one, carry)

    lax.fori_loop(0, N_EXPERTS, per_expert, 0)


def _dispatch_kernel(base_ref, seg_ref, nch_ref, tail0_ref, tailn_ref, h_ref, meta_ref, segrow_ref, xs_hbm,
                     buf, zbuf, sem):
    i = pl.program_id(0)

    @pl.when(i == 0)
    def _():
        zbuf[...] = jnp.zeros_like(zbuf)
        for wait in (False, True):
            def per_expert(e, carry, wait=wait):
                def one(j, c):
                    cp = pltpu.make_async_copy(
                        zbuf, xs_hbm.at[pl.ds(pl.multiple_of(tail0_ref[e] + j * CH, CH), CH)], sem)
                    if wait:
                        cp.wait()
                    else:
                        cp.start()
                    return c

                return lax.fori_loop(0, tailn_ref[e], one, carry)

            lax.fori_loop(0, N_EXPERTS, per_expert, 0)

    pos = _local_positions(meta_ref[...], segrow_ref[0])
    riota = lax.broadcasted_iota(jnp.int32, (TM, ROWS_L), 1).astype(F32)
    pt = jnp.zeros((TM, ROWS_L), F32)
    for p in pos:
        pt = jnp.where(riota == p, 1.0, pt)
    buf[...] = _dot_tn(pt.astype(BF16), h_ref[...])

    make = lambda s, d: pltpu.make_async_copy(buf.at[pl.ds(s, CH)], xs_hbm.at[pl.ds(d, CH)], sem)
    _segment_copies(i, base_ref, seg_ref, nch_ref, make, False)
    _segment_copies(i, base_ref, seg_ref, nch_ref, make, True)


def _dispatch(plan, h2, meta):
    grid_spec = pltpu.PrefetchScalarGridSpec(
        num_scalar_prefetch=5,
        grid=(NBLK,),
        in_specs=[_TOK(D), _TOK(LANES), pl.BlockSpec((1, 1, LANES), lambda i, *_: (i, 0, 0))],
        out_specs=pl.BlockSpec(memory_space=pl.ANY),
        scratch_shapes=[pltpu.VMEM((ROWS_L, D), F32), pltpu.VMEM((CH, D), F32), pltpu.SemaphoreType.DMA],
    )
    return pl.pallas_call(
        _dispatch_kernel,
        grid_spec=grid_spec,
        out_shape=jax.ShapeDtypeStruct((P_ROWS, D), F32),
        compiler_params=_params(("arbitrary",)),
        name="moe_dispatch",
    )(plan["base"], plan["seg"], plan["nch"], plan["tail0"], plan["tailn"], h2, meta, plan["segrow"])


def _expert_kernel(te_ref, tfirst_ref, tvalid_ref, xblk_ref, x_ref, wgu_ref, bgu_ref, wd_ref, bd_ref, y_ref,
                   wgu_bf, wd_bf):
    del xblk_ref
    i = pl.program_id(0)

    @pl.when(tfirst_ref[i] == 1)
    def _():
        wgu_bf[...] = wgu_ref[...].astype(BF16)
        wd_bf[...] = wd_ref[...].astype(BF16)

    @pl.when(tvalid_ref[i] == 1)
    def _():
        a = _dot(x_ref[...].astype(BF16), wgu_bf[...]) + bgu_ref[...]
        glu = jnp.minimum(a[:, :D_FF], SWIGLU_LIMIT)
        lin = jnp.clip(a[:, D_FF:], -SWIGLU_LIMIT, SWIGLU_LIMIT)
        act = (glu * jax.nn.sigmoid(SWIGLU_ALPHA * glu)) * (lin + 1.0)
        y_ref[...] = _dot(act.astype(BF16), wd_bf[...]) + bd_ref[...]

    @pl.when(tvalid_ref[i] == 0)
    def _():
        y_ref[...] = jnp.zeros_like(y_ref)


def _experts(layer, plan, xs, w_gu, b_gu, w_down, b_down):
    depth = w_gu.shape[0]
    e_of = lambda i, te, *_: (layer, te[i], 0, 0)
    grid_spec = pltpu.PrefetchScalarGridSpec(
        num_scalar_prefetch=4,
        grid=(N_ETILES,),
        in_specs=[
            pl.BlockSpec((ETILE, D), lambda i, te, tf, tv, xb: (xb[i], 0)),
            pl.BlockSpec((None, None, D, 2 * D_FF), e_of),
            pl.BlockSpec((None, None, 1, 2 * D_FF), e_of),
            pl.BlockSpec((None, None, D_FF, D), e_of),
            pl.BlockSpec((None, None, 1, D), e_of),
        ],
        out_specs=pl.BlockSpec((ETILE, D), lambda i, *_: (i, 0)),
        scratch_shapes=[pltpu.VMEM((D, 2 * D_FF), BF16), pltpu.VMEM((D_FF, D), BF16)],
    )
    return pl.pallas_call(
        _expert_kernel,
        grid_spec=grid_spec,
        out_shape=jax.ShapeDtypeStruct((P_ROWS, D), F32),
        compiler_params=_params(("arbitrary",)),
        name="moe_experts",
    )(plan["te"], plan["tfirst"], plan["tvalid"], plan["xblk"], xs, w_gu,
      b_gu.reshape(depth, N_EXPERTS, 1, 2 * D_FF), w_down, b_down.reshape(depth, N_EXPERTS, 1, D))


def _combine_kernel(base_ref, seg_ref, nch_ref, y_hbm, x1_ref, meta_ref, segrow_ref, mod_ref, o_ref, ybuf, sem):
    i = pl.program_id(0)

    @pl.when(i == 0)
    def _():
        ybuf[...] = jnp.zeros_like(ybuf)

    make = lambda s, d: pltpu.make_async_copy(y_hbm.at[pl.ds(d, CH)], ybuf.at[pl.ds(s, CH)], sem)
    _segment_copies(i, base_ref, seg_ref, nch_ref, make, False)
    _segment_copies(i, base_ref, seg_ref, nch_ref, make, True)

    meta = meta_ref[...]
    pos = _local_positions(meta, segrow_ref[0])
    riota = lax.broadcasted_iota(jnp.int32, (TM, ROWS_L), 1).astype(F32)
    gates = jnp.zeros((TM, ROWS_L), F32)
    for k, p in enumerate(pos):
        gates = jnp.where(riota == p, meta[:, _META_W + k:_META_W + k + 1], gates)
    hi = gates.astype(BF16)
    lo = (gates - hi.astype(F32)).astype(BF16)
    yb = ybuf[...].astype(BF16)
    acc = _dot(hi, yb) + _dot(lo, yb)
    o_ref[...] = x1_ref[...] + mod_ref[0][5:6] * acc


def _combine(plan, y, x1, meta, mod):
    grid_spec = pltpu.PrefetchScalarGridSpec(
        num_scalar_prefetch=3,
        grid=(NBLK,),
        in_specs=[pl.BlockSpec(memory_space=pl.ANY), _TOK(D), _TOK(LANES),
                  pl.BlockSpec((1, 1, LANES), lambda i, *_: (i, 0, 0)), _MOD],
        out_specs=_TOK(D),
        scratch_shapes=[pltpu.VMEM((ROWS_L, D), F32), pltpu.SemaphoreType.DMA],
    )
    return pl.pallas_call(
        _combine_kernel,
        grid_spec=grid_spec,
        out_shape=jax.ShapeDtypeStruct((T, D), F32),
        compiler_params=_params(("arbitrary",)),
        name="moe_combine",
    )(plan["base"], plan["seg"], plan["nch"], y, x1, meta, plan["segrow"], mod)


def _moe_plan(cnt):
    cnt = cnt[:, 0, :N_EXPERTS].astype(jnp.int32)
    c8 = (cnt + CH - 1) // CH * CH
    tot = jnp.sum(c8, axis=0)
    tiles = (tot + ETILE - 1) // ETILE
    tile_end = jnp.cumsum(tiles)
    offs = (tile_end - tiles) * ETILE
    base = offs[None, :] + jnp.cumsum(c8, axis=0) - c8
    seg = jnp.cumsum(c8, axis=1) - c8
    tid = jnp.arange(N_ETILES, dtype=jnp.int32)
    te = jnp.sum((tile_end[None, :] <= tid[:, None]).astype(jnp.int32), axis=1)
    tvalid = (te < N_EXPERTS).astype(jnp.int32)
    last = jnp.max(jnp.where(tiles > 0, jnp.arange(N_EXPERTS, dtype=jnp.int32), 0))
    te = jnp.where(tvalid == 1, te, last)
    return {
        "base": base.reshape(-1), "seg": seg.reshape(-1), "nch": (c8 // CH).reshape(-1),
        "tail0": offs + tot, "tailn": (tiles * ETILE - tot) // CH,
        "segrow": jnp.pad(seg.astype(F32), ((0, 0), (0, LANES - N_EXPERTS))).reshape(NBLK, 1, LANES),
        "te": te, "tvalid": tvalid, "xblk": jnp.where(tvalid == 1, tid, 0),
        "tfirst": jnp.concatenate([jnp.ones((1,), jnp.int32), (te[1:] != te[:-1]).astype(jnp.int32)]),
    }


def _moe(layer, x1, h2, meta, cnt, mod, w_gu, b_gu, w_down, b_down):
    plan = _moe_plan(cnt)
    xs = _dispatch(plan, h2, meta)
    y = _experts(layer, plan, xs, w_gu, b_gu, w_down, b_down)
    return _combine(plan, y, x1, meta, mod)


_HK, _HV = H_C * DK_C, H_C * DV_C
_ODD_MAIN = 2 * _HK + 2 * _HV


def _odd_in_kernel(x_ref, mod_ref, nmix_ref, win_ref, wgk_ref, bgk_ref, q_ref, k_ref, v_ref, g_ref, la_ref):
    m = mod_ref[0]
    h = _rms(x_ref[...]) * nmix_ref[...] * (1.0 + m[1:2]) + m[0:1]
    a = _dot(h.astype(BF16), win_ref[...])
    q_ref[...] = a[:, :_HK] * (DK_C ** -0.5)
    k_ref[...] = a[:, _HK:2 * _HK]
    v_ref[...] = a[:, 2 * _HK:2 * _HK + _HV]
    g_ref[...] = a[:, 2 * _HK + _HV:_ODD_MAIN]
    z = _dot(a[:, _ODD_MAIN:].astype(BF16), wgk_ref[...]) + bgk_ref[...]
    la_ref[...] = (jnp.minimum(z, 0.0) - jnp.log(1.0 + jnp.exp(-jnp.abs(z)))) * (1.0 / GATE_TAU)


def _gla_kernel(has_init, nchunk, *refs):
    if has_init:
        (qf, kf, vf, laf, qb, kb, vb, lab, s0f, s0b, _, _, of_ref, ob_ref, st) = refs
    else:
        (qf, kf, vf, laf, qb, kb, vb, lab, of_ref, ob_ref, sf_ref, sb_ref, st) = refs
    j = pl.program_id(1)

    @pl.when(j == 0)
    def _():
        if has_init:
            st[0] = s0f[0]
            st[1] = s0b[0]
        else:
            st[...] = jnp.zeros_like(st)

    row = lax.broadcasted_iota(jnp.int32, (GLA_CHUNK, GLA_CHUNK), 0)
    col = lax.broadcasted_iota(jnp.int32, (GLA_CHUNK, GLA_CHUNK), 1)
    for d, (q_r, k_r, v_r, la_r, o_r) in enumerate(((qf, kf, vf, laf, of_ref), (qb, kb, vb, lab, ob_ref))):
        keep = (col <= row) if d == 0 else (col >= row)
        tri = jnp.where(keep, 1.0, 0.0)
        for h in range(H_C):
            kc = slice(h * DK_C, (h + 1) * DK_C)
            vc = slice(h * DV_C, (h + 1) * DV_C)
            g = la_r[:, kc]
            c = jnp.dot(tri, g, precision=lax.Precision.HIGHEST, preferred_element_type=F32)
            tot = jnp.sum(g, axis=0, keepdims=True)
            q, k, v = q_r[:, kc], k_r[:, kc], v_r[:, vc].astype(BF16)
            qe = (q * jnp.exp(c)).astype(BF16)
            kd = (k * jnp.exp(-c)).astype(BF16)
            att = jnp.where(keep, _dot_nt(qe, kd), 0.0)
            s_t = st[d, h]
            o_r[:, vc] = _dot_nt(qe, s_t.astype(BF16)) + _dot(att.astype(BF16), v)
            k2 = (k * jnp.exp(tot - c)).astype(BF16)
            st[d, h] = s_t * jnp.exp(tot) + _dot_tn(v, k2)

    if not has_init:
        @pl.when(j == nchunk - 1)
        def _():
            sf_ref[0] = st[0]
            sb_ref[0] = st[1]


def _gla_call(has_init, nseq, seqlen, row0, q, k, v, la, prev_f=None, prev_b=None, s0f=None, s0b=None):
    nchunk = seqlen // GLA_CHUNK
    blk0 = row0 // GLA_CHUNK
    fwd = lambda b, j: (blk0 + b * nchunk + j, 0)
    bwd = lambda b, j: (blk0 + b * nchunk + nchunk - 1 - j, 0)
    bwd_la = lambda b, j: (blk0 + b * nchunk + nchunk - 1 - j, 1)
    state_spec = pl.BlockSpec((1, H_C, DV_C, DK_C), lambda b, j: (b, 0, 0, 0))
    in_specs = [
        pl.BlockSpec((GLA_CHUNK, _HK), fwd), pl.BlockSpec((GLA_CHUNK, _HK), fwd),
        pl.BlockSpec((GLA_CHUNK, _HV), fwd), pl.BlockSpec((GLA_CHUNK, _HK), fwd),
        pl.BlockSpec((GLA_CHUNK, _HK), bwd), pl.BlockSpec((GLA_CHUNK, _HK), bwd),
        pl.BlockSpec((GLA_CHUNK, _HV), bwd), pl.BlockSpec((GLA_CHUNK, _HK), bwd_la),
    ]
    args = [q, k, v, la, q, k, v, la]
    aliases = {}
    if has_init:
        in_specs += [state_spec, state_spec] + [pl.BlockSpec(memory_space=pl.ANY)] * 2
        args += [s0f, s0b, prev_f, prev_b]
        aliases = {len(args) - 2: 0, len(args) - 1: 1}
    out_specs = [pl.BlockSpec((GLA_CHUNK, _HV), fwd), pl.BlockSpec((GLA_CHUNK, _HV), bwd)]
    out_shape = [jax.ShapeDtypeStruct((T, _HV), F32)] * 2
    if not has_init:
        out_specs += [state_spec, state_spec]
        out_shape += [jax.ShapeDtypeStruct((nseq, H_C, DV_C, DK_C), F32)] * 2
    return pl.pallas_call(
        functools.partial(_gla_kernel, has_init, nchunk),
        grid=(nseq, nchunk),
        in_specs=in_specs,
        out_specs=out_specs,
        out_shape=out_shape,
        scratch_shapes=[pltpu.VMEM((2, H_C, DV_C, DK_C), F32)],
        input_output_aliases=aliases,
        compiler_params=_params(("arbitrary", "arbitrary")),
        name="gla_sample" if has_init else "gla_prompt",
    )(*args)


def _rope_tables():
    half = ROPE // 2
    inv_freq = np.power(np.float32(ROPE_THETA), -np.arange(0, half, 2, dtype=np.float32) / np.float32(half))
    n = np.arange(SAMPLE_LEN)
    row = (n // GRID_W).astype(np.float32)
    col = (n % GRID_W).astype(np.float32)
    ang_r = (row[:, None] * inv_freq[None, :]).astype(np.float32)
    ang_c = (col[:, None] * inv_freq[None, :]).astype(np.float32)
    nf = half // 2
    c = np.ones((TM + SAMPLE_LEN, LANES), np.float32)
    s1 = np.zeros((TM + SAMPLE_LEN, LANES), np.float32)
    s2 = np.zeros((TM + SAMPLE_LEN, LANES), np.float32)
    for base, ang in ((NOPE, ang_r), (NOPE + half, ang_c)):
        c[TM:, base:base + nf] = np.cos(ang)
        c[TM:, base + nf:base + half] = np.cos(ang)
        s1[TM:, base:base + nf] = -np.sin(ang)
        s2[TM:, base + nf:base + half] = np.sin(ang)
    return jnp.asarray(c), jnp.asarray(s1), jnp.asarray(s2)


def _pad_heads(w, nheads, width, lo=0):
    k = w.shape[0]
    w = w.reshape(k, nheads, width)
    w = jnp.pad(w, ((0, 0), (0, 0), (lo, HP - lo - width)))
    return w.reshape(k, nheads * HP)


def _row128(v, lo=0):
    return jnp.pad(v, (lo, LANES - lo - v.shape[0])).reshape(1, LANES)


def _even_layer(x, mod, nmix, w_in, q_a_norm, w_uq, q_norm, kv_a_norm, w_ukv, k_norm, v_norm, w_s, b_s, w_out,
                cache_ckv, cache_kpe):
    s = np.cumsum([Q_LORA, KV_LORA, ROPE, W_B])
    w_q, w_ckv, w_kpe, w_u, w_v = (w_in[:, :s[0]], w_in[:, s[0]:s[1]], w_in[:, s[1]:s[2]], w_in[:, s[2]:s[3]],
                                   w_in[:, s[3]:])
    w_kpe = jnp.pad(w_kpe, ((0, 0), (NOPE, LANES - QK_DIM)))
    win = jnp.concatenate([w_q, w_ckv, w_u, w_v, w_kpe], axis=1).astype(BF16)
    wuq = _pad_heads(w_uq, H_A, QK_DIM).astype(BF16)
    ukv = w_ukv.reshape(KV_LORA, H_A, NOPE + V_A)
    wuk = _pad_heads(ukv[:, :, :NOPE].reshape(KV_LORA, H_A * NOPE), H_A, NOPE)
    wuv = _pad_heads(ukv[:, :, NOPE:].reshape(KV_LORA, H_A * V_A), H_A, V_A)
    wukv = jnp.concatenate([wuk, wuv], axis=1).astype(BF16)
    qgain = _row128(q_norm * (QK_DIM ** -0.5))
    kgain = _row128(k_norm)
    bias = b_s.reshape(G_B // 2, 2, CHUNK_B)
    bias = jnp.concatenate([jnp.broadcast_to(bias[:, 0, :, None], (G_B // 2, CHUNK_B, C_B)),
                            jnp.broadcast_to(bias[:, 1, :, None], (G_B // 2, CHUNK_B, C_B))], axis=-1)
    rc, rs1, rs2 = _rope_tables()
    rope_spec = pl.BlockSpec((TM, LANES), lambda i: (_rope_blk(i), 0))
    width = H_A * HP
    q, k, v, ob, ckv, kpe = pl.pallas_call(
        _even_in_kernel,
        grid=(NBLK,),
        in_specs=[
            _TOK(D), _MOD, _full((1, D)), _full((D, _WIN_N)), _full((1, Q_LORA)), _full((Q_LORA, width)),
            _full((1, LANES)), _full((1, KV_LORA)), _full((KV_LORA, 2 * width)), _full((1, LANES)),
            _full((1, W_B)), _full((G_B, CHUNK_B, CHUNK_B)), _full((G_B // 2, CHUNK_B, LANES)),
            rope_spec, rope_spec, rope_spec,
        ],
        out_specs=[_TOK(width), _TOK(width), _TOK(width), _TOK(W_B), _TOK(KV_LORA), _TOK(ROPE)],
        out_shape=[
            jax.ShapeDtypeStruct((T, width), BF16), jax.ShapeDtypeStruct((T, width), BF16),
            jax.ShapeDtypeStruct((T, width), BF16), jax.ShapeDtypeStruct((T, W_B), BF16),
            jax.ShapeDtypeStruct((T, KV_LORA), F32), jax.ShapeDtypeStruct((T, ROPE), F32),
        ],
        compiler_params=_params(("arbitrary",)),
        name="even_in",
    )(x, mod, nmix.reshape(1, D), win, q_a_norm.reshape(1, Q_LORA), wuq, qgain, kv_a_norm.reshape(1, KV_LORA),
      wukv, kgain, v_norm.reshape(1, W_B), w_s.astype(BF16), bias, rc, rs1, rs2)

    n_ctx = N_SAMPLE_SEQ * PAST_LEN
    kpe_ctx = jnp.pad(cache_kpe.reshape(n_ctx, ROPE), ((0, 0), (NOPE, LANES - QK_DIM)))
    k_ctx, v_ctx = pl.pallas_call(
        _ctx_kv_kernel,
        grid=(n_ctx // TM,),
        in_specs=[_TOK(KV_LORA), _TOK(LANES), _full((KV_LORA, 2 * width)), _full((1, LANES))],
        out_specs=[_TOK(width), _TOK(width)],
        out_shape=[jax.ShapeDtypeStruct((n_ctx, width), BF16)] * 2,
        compiler_params=_params(("arbitrary",)),
        name="ctx_kv",
    )(cache_ckv.reshape(n_ctx, KV_LORA), kpe_ctx, wukv, kgain)

    oa = _attention(q, k, v, k_ctx, v_ctx)
    woa = jnp.pad(w_out[:H_A * V_A].reshape(H_A, V_A, D), ((0, 0), (0, HP - V_A), (0, 0))).reshape(width, D)
    return oa, ob, woa.astype(BF16), w_out[H_A * V_A:].astype(BF16), ckv, kpe


def kernel(x_prompt, x_sample, cache_mla_ckv, cache_mla_kpe, state_gla_fwd, state_gla_bwd, c, c_ctx, ada_w, ada_b,
           norm_mix, norm_ffn, even_w_in, mla_q_a_norm, mla_w_uq, mla_q_norm, mla_kv_a_norm, mla_w_ukv, mla_k_norm,
           cmlp_v_norm, cmlp_w_s, cmlp_b_s, even_w_out, odd_w_in, gla_w_gk_fwd, gla_b_gk_fwd, gla_w_gk_bwd,
           gla_b_gk_bwd, gla_o_norm, odd_w_out, moe_w_router, moe_b_router, moe_w_gu, moe_b_gu, moe_w_down,
           moe_b_down):
    x0 = jnp.concatenate([x_prompt.reshape(N_PROMPT, D), x_sample.reshape(N_SAMPLE, D)], axis=0)
    cond8 = jnp.concatenate([c_ctx[None, :], c, jnp.zeros((SUBLANES - 1 - N_SAMPLE_SEQ, D), F32)], axis=0)
    mods = _adaln(cond8, ada_w, ada_b)
    wr = jnp.pad(moe_w_router, ((0, 0), (0, 0), (0, LANES - N_EXPERTS)))
    br = jnp.pad(moe_b_router, ((0, 0), (0, LANES - N_EXPERTS))).reshape(2, 1, LANES)

    oa, ob, woa, wob, ckv, kpe = _even_layer(
        x0, mods[0], norm_mix[0], even_w_in[0], mla_q_a_norm[0], mla_w_uq[0], mla_q_norm[0], mla_kv_a_norm[0],
        mla_w_ukv[0], mla_k_norm[0], cmlp_v_norm[0], cmlp_w_s[0], cmlp_b_s[0], even_w_out[0],
        cache_mla_ckv[:, 0], cache_mla_kpe[:, 0])
    width = H_A * HP
    x1, h2, meta, cnt = pl.pallas_call(
        _even_out_kernel,
        grid=(NBLK,),
        in_specs=[_TOK(width), _TOK(W_B), _TOK(D), _MOD, _full((width, D)), _full((W_B, D)), _full((1, D)),
                  _full((D, LANES)), _full((1, LANES))],
        out_specs=_PROLOGUE_OUT_SPECS,
        out_shape=_PROLOGUE_OUT_SHAPE,
        compiler_params=_params(("arbitrary",)),
        name="even_out",
    )(oa, ob, x0, mods[0], woa, wob, norm_ffn[0].reshape(1, D), wr[0], br[0])
    x2 = _moe(0, x1, h2, meta, cnt, mods[0], moe_w_gu, moe_b_gu, moe_w_down, moe_b_down)

    w_in = odd_w_in[0]
    win = jnp.concatenate([w_in, jnp.zeros((D, LANES - 2 * GATE_RANK), F32)], axis=1).astype(BF16)
    wgk = jnp.zeros((LANES, 2 * _HK), F32)
    wgk = wgk.at[:GATE_RANK, :_HK].set(gla_w_gk_fwd[0]).at[GATE_RANK:2 * GATE_RANK, _HK:].set(gla_w_gk_bwd[0])
    bgk = jnp.concatenate([gla_b_gk_fwd[0], gla_b_gk_bwd[0]]).reshape(1, 2 * _HK)
    q, k, v, g, la = pl.pallas_call(
        _odd_in_kernel,
        grid=(NBLK,),
        in_specs=[_TOK(D), _MOD, _full((1, D)), _full((D, _ODD_MAIN + LANES)), _full((LANES, 2 * _HK)),
                  _full((1, 2 * _HK))],
        out_specs=[_TOK(_HK), _TOK(_HK), _TOK(_HV), _TOK(_HV), _TOK(2 * _HK)],
        out_shape=[jax.ShapeDtypeStruct((T, _HK), F32), jax.ShapeDtypeStruct((T, _HK), F32),
                   jax.ShapeDtypeStruct((T, _HV), F32), jax.ShapeDtypeStruct((T, _HV), F32),
                   jax.ShapeDtypeStruct((T, 2 * _HK), F32)],
        compiler_params=_params(("arbitrary",)),
        name="odd_in",
    )(x2, mods[1], norm_mix[1].reshape(1, D), win, wgk.astype(BF16), bgk)

    of, obk, st_f, st_b = _gla_call(False, N_PROMPT_SEQ, PROMPT_LEN, 0, q, k, v, la)
    s0f = state_gla_fwd[:, 0].transpose(0, 1, 3, 2)
    s0b = state_gla_bwd[:, 0].transpose(0, 1, 3, 2)
    of, obk = _gla_call(True, N_SAMPLE_SEQ, SAMPLE_LEN, N_PROMPT, q, k, v, la, of, obk, s0f, s0b)

    x3, h2, meta, cnt = pl.pallas_call(
        _odd_out_kernel,
        grid=(NBLK,),
        in_specs=[_TOK(_HV), _TOK(_HV), _TOK(_HV), _TOK(D), _MOD, _full((1, DV_C)), _full((_HV, D)),
                  _full((1, D)), _full((D, LANES)), _full((1, LANES))],
        out_specs=_PROLOGUE_OUT_SPECS,
        out_shape=_PROLOGUE_OUT_SHAPE,
        compiler_params=_params(("arbitrary",)),
        name="odd_out",
    )(of, obk, g, x2, mods[1], gla_o_norm[0].reshape(1, DV_C), odd_w_out[0].astype(BF16),
      norm_ffn[1].reshape(1, D), wr[1], br[1])
    x4 = _moe(1, x3, h2, meta, cnt, mods[1], moe_w_gu, moe_b_gu, moe_w_down, moe_b_down)

    y_prompt = x4[:N_PROMPT].reshape(N_PROMPT_SEQ, PROMPT_LEN, D)
    y_sample = x4[N_PROMPT:].reshape(N_SAMPLE_SEQ, SAMPLE_LEN, D)
    new_ckv = ckv[:N_PROMPT].reshape(N_PROMPT_SEQ, 1, PROMPT_LEN, KV_LORA)
    new_kpe = kpe[:N_PROMPT].reshape(N_PROMPT_SEQ, 1, PROMPT_LEN, ROPE)
    new_fwd = st_f.transpose(0, 1, 3, 2)[:, None]
    new_bwd = st_b.transpose(0, 1, 3, 2)[:, None]
    return (y_prompt, y_sample, new_ckv, new_kpe, new_fwd, new_bwd)
```

```python
import functools

import numpy as np
import jax
import jax.numpy as jnp
from jax import lax
from jax.experimental import pallas as pl
from jax.experimental.pallas import tpu as pltpu

F32 = jnp.float32
BF16 = jnp.bfloat16

D = 1024
N_PROMPT_SEQ, PROMPT_LEN = 16, 256
N_SAMPLE_SEQ, SAMPLE_LEN = 4, 2048
PAST_LEN = 512
N_PROMPT = N_PROMPT_SEQ * PROMPT_LEN
N_SAMPLE = N_SAMPLE_SEQ * SAMPLE_LEN
T = N_PROMPT + N_SAMPLE
EPS = 1e-6
GRID_W = 64
H_A, Q_LORA, KV_LORA, NOPE, ROPE, V_A = 8, 512, 256, 64, 32, 64
QK_DIM = NOPE + ROPE
G_B, C_B, W_B, CHUNK_B = 8, 64, 512, 128
H_C, DK_C, DV_C, GATE_RANK, GATE_TAU, GLA_CHUNK = 4, 128, 256, 16, 16.0, 64
N_EXPERTS, TOP_K, D_FF = 32, 4, 1024
SWIGLU_LIMIT, SWIGLU_ALPHA = 7.0, 1.702
ROPE_THETA = 10000.0

LANES = 128
SUBLANES = 8
VMEM_LIMIT = 56 * 1024 * 1024

TM = 256
NBLK = T // TM
PROMPT_BLKS = N_PROMPT // TM
BLKS_PER_SAMPLE_SEQ = SAMPLE_LEN // TM
ETILE = 512
CH = SUBLANES
_MAX_LOCAL = TM * TOP_K + N_EXPERTS * (CH - 1)
ROWS_L = -(-_MAX_LOCAL // LANES) * LANES
N_ETILES = -(-NBLK * _MAX_LOCAL // ETILE) + N_EXPERTS
P_ROWS = N_ETILES * ETILE
HP = LANES


def _cond_row(i):
    return jnp.where(i < PROMPT_BLKS, 0, 1 + (i - PROMPT_BLKS) // BLKS_PER_SAMPLE_SEQ)


def _rope_blk(i):
    return jnp.where(i < PROMPT_BLKS, 0, 1 + (i - PROMPT_BLKS) % BLKS_PER_SAMPLE_SEQ)


def _rms(x):
    return x * lax.rsqrt(jnp.mean(x * x, axis=-1, keepdims=True) + EPS)


def _gelu(x):
    return 0.5 * x * (1.0 + jnp.tanh(0.7978845608028654 * (x + 0.044715 * (x * x * x))))


def _silu(x):
    return x * jax.nn.sigmoid(x)


def _dot(a, b):
    return jnp.dot(a, b, preferred_element_type=F32)


def _dot_nt(a, b):
    return lax.dot_general(a, b, (((1,), (1,)), ((), ())), preferred_element_type=F32)


def _dot_tn(a, b):
    return lax.dot_general(a, b, (((0,), (0,)), ((), ())), preferred_element_type=F32)


def _params(sem, vmem=VMEM_LIMIT):
    return pltpu.CompilerParams(dimension_semantics=sem, vmem_limit_bytes=vmem)


def _full(shape):
    nd = len(shape)
    return pl.BlockSpec(shape, lambda *_: (0,) * nd)


ADA_TN = 1536


def _adaln_kernel(c_ref, w_ref, b_ref, o_ref):
    s = _silu(c_ref[...]).astype(BF16)
    o_ref[0] = _dot(s, w_ref[0].astype(BF16)) + b_ref[0]


def _adaln(cond8, ada_w, ada_b):
    depth = ada_w.shape[0]
    n = ada_w.shape[2]
    out = pl.pallas_call(
        _adaln_kernel,
        grid=(depth, n // ADA_TN),
        in_specs=[
            pl.BlockSpec((SUBLANES, D), lambda l, j: (0, 0)),
            pl.BlockSpec((1, D, ADA_TN), lambda l, j: (l, 0, j)),
            pl.BlockSpec((1, 1, ADA_TN), lambda l, j: (l, 0, j)),
        ],
        out_specs=pl.BlockSpec((1, SUBLANES, ADA_TN), lambda l, j: (l, 0, j)),
        out_shape=jax.ShapeDtypeStruct((depth, SUBLANES, n), F32),
        compiler_params=_params(("arbitrary", "arbitrary")),
        name="adaln",
    )(cond8, ada_w, ada_b.reshape(depth, 1, n))
    return out.reshape(depth, SUBLANES, 6, D)


_QC0, _CKV0, _U0, _V0, _KPE0, _WIN_N = 0, 512, 768, 1280, 1792, 1920


def _rope(y, c, s1, s2):
    return y * c + pltpu.roll(y, LANES - 8, 1) * s1 + pltpu.roll(y, 8, 1) * s2


def _k_heads(k_raw, kpe128, kp_rot, kgain, k_ref):
    sskpe = jnp.sum(kpe128 * kpe128, axis=-1, keepdims=True)
    for h in range(H_A):
        kb = k_raw[:, h * HP:(h + 1) * HP]
        r = lax.rsqrt((jnp.sum(kb * kb, axis=-1, keepdims=True) + sskpe) * (1.0 / QK_DIM) + EPS)
        k_ref[:, h * HP:(h + 1) * HP] = ((kb * kgain + kp_rot) * r).astype(BF16)


def _even_in_kernel(x_ref, mod_ref, nmix_ref, win_ref, qan_ref, wuq_ref, qgain_ref, kvan_ref, wukv_ref,
                    kgain_ref, vnorm_ref, ws_ref, bs_ref, rc_ref, rs1_ref, rs2_ref,
                    q_ref, k_ref, v_ref, ob_ref, ckv_ref, kpe_ref):
    m = mod_ref[0]
    h = _rms(x_ref[...]) * nmix_ref[...] * (1.0 + m[1:2]) + m[0:1]
    a = _dot(h.astype(BF16), win_ref[...])
    qc = a[:, _QC0:_CKV0]
    ckv = a[:, _CKV0:_U0]
    u = a[:, _U0:_V0]
    vv = a[:, _V0:_KPE0]
    kpe128 = a[:, _KPE0:_WIN_N]

    ckv_n = _rms(ckv) * kvan_ref[...]
    ckv_ref[...] = ckv_n
    kpe_ref[...] = kpe128[:, NOPE:QK_DIM]

    rc, rs1, rs2 = rc_ref[...], rs1_ref[...], rs2_ref[...]
    qn = (_rms(qc) * qan_ref[...]).astype(BF16)
    qr = _dot(qn, wuq_ref[...])
    qgain = qgain_ref[...]
    for hh in range(H_A):
        blk = qr[:, hh * HP:(hh + 1) * HP]
        r = lax.rsqrt(jnp.sum(blk * blk, axis=-1, keepdims=True) * (1.0 / QK_DIM) + EPS)
        q_ref[:, hh * HP:(hh + 1) * HP] = _rope(blk * r * qgain, rc, rs1, rs2).astype(BF16)

    kv = _dot(ckv_n.astype(BF16), wukv_ref[...])
    v_ref[...] = kv[:, H_A * HP:].astype(BF16)
    kgain = kgain_ref[...]
    kp_rot = _rope(kpe128 * kgain, rc, rs1, rs2)
    _k_heads(kv[:, :H_A * HP], kpe128, kp_rot, kgain, k_ref)

    ug = _gelu(u)
    vn = (_rms(_gelu(vv)) * vnorm_ref[...]).astype(BF16)
    low = lax.broadcasted_iota(jnp.int32, (CHUNK_B, LANES), 1) < C_B
    for c in range(TM // CHUNK_B):
        rows = slice(c * CHUNK_B, (c + 1) * CHUNK_B)
        for p in range(G_B // 2):
            cols = slice(p * LANES, (p + 1) * LANES)
            blk = vn[rows, cols]
            both = _dot(ws_ref[p], blk)
            mixed = jnp.where(low, both[:CHUNK_B], both[CHUNK_B:]) + bs_ref[p]
            ob_ref[rows, cols] = (ug[rows, cols] * mixed).astype(BF16)


def _ctx_kv_kernel(ckv_ref, kpe_ref, wukv_ref, kgain_ref, k_ref, v_ref):
    kv = _dot(ckv_ref[...].astype(BF16), wukv_ref[...])
    v_ref[...] = kv[:, H_A * HP:].astype(BF16)
    kgain = kgain_ref[...]
    kpe128 = kpe_ref[...]
    _k_heads(kv[:, :H_A * HP], kpe128, kpe128 * kgain, kgain, k_ref)


def _attn_self_kernel(q_ref, k_ref, v_ref, o_ref):
    for h in range(H_A):
        cols = slice(h * HP, (h + 1) * HP)
        s = _dot_nt(q_ref[:, cols], k_ref[:, cols])
        p = jnp.exp(s - jnp.max(s, axis=-1, keepdims=True))
        inv = 1.0 / jnp.sum(p, axis=-1, keepdims=True)
        o_ref[:, cols] = (_dot(p.astype(BF16), v_ref[:, cols]) * inv).astype(BF16)


def _attn_ctx_kernel(q_ref, k_ref, v_ref, kc_ref, vc_ref, prev_ref, o_ref):
    del prev_ref
    for h in range(H_A):
        cols = slice(h * HP, (h + 1) * HP)
        q = q_ref[:, cols]
        s1 = _dot_nt(q, k_ref[:, cols])
        s2 = _dot_nt(q, kc_ref[:, cols])
        mx = jnp.maximum(jnp.max(s1, axis=-1, keepdims=True), jnp.max(s2, axis=-1, keepdims=True))
        p1 = jnp.exp(s1 - mx)
        p2 = jnp.exp(s2 - mx)
        inv = 1.0 / (jnp.sum(p1, axis=-1, keepdims=True) + jnp.sum(p2, axis=-1, keepdims=True))
        o = _dot(p1.astype(BF16), v_ref[:, cols]) + _dot(p2.astype(BF16), vc_ref[:, cols])
        o_ref[:, cols] = (o * inv).astype(BF16)


def _attention(q, k, v, k_ctx, v_ctx):
    width = H_A * HP
    o = pl.pallas_call(
        _attn_self_kernel,
        grid=(N_PROMPT_SEQ,),
        in_specs=[pl.BlockSpec((PROMPT_LEN, width), lambda i: (i, 0))] * 3,
        out_specs=pl.BlockSpec((PROMPT_LEN, width), lambda i: (i, 0)),
        out_shape=jax.ShapeDtypeStruct((T, width), BF16),
        compiler_params=_params(("arbitrary",)),
        name="attn_prompt",
    )(q, k, v)
    first = N_PROMPT // SAMPLE_LEN
    qblk = lambda b, j: (PROMPT_BLKS + b * BLKS_PER_SAMPLE_SEQ + j, 0)
    return pl.pallas_call(
        _attn_ctx_kernel,
        grid=(N_SAMPLE_SEQ, BLKS_PER_SAMPLE_SEQ),
        in_specs=[
            pl.BlockSpec((TM, width), qblk),
            pl.BlockSpec((SAMPLE_LEN, width), lambda b, j: (first + b, 0)),
            pl.BlockSpec((SAMPLE_LEN, width), lambda b, j: (first + b, 0)),
            pl.BlockSpec((PAST_LEN, width), lambda b, j: (b, 0)),
            pl.BlockSpec((PAST_LEN, width), lambda b, j: (b, 0)),
            pl.BlockSpec(memory_space=pl.ANY),
        ],
        out_specs=pl.BlockSpec((TM, width), qblk),
        out_shape=jax.ShapeDtypeStruct((T, width), BF16),
        input_output_aliases={5: 0},
        compiler_params=_params(("arbitrary", "arbitrary")),
        name="attn_sample",
    )(q, k, v, k_ctx, v_ctx, o)


_META_IDX, _META_RANK, _META_W = 0, TOP_K, 2 * TOP_K


def _moe_prologue(x1, m, nffn_ref, wr_ref, br_ref, x1_ref, h2_ref, meta_ref, cnt_ref):
    x1_ref[...] = x1
    h2 = _rms(x1) * nffn_ref[...] * (1.0 + m[4:5]) + m[3:4]
    h2_ref[...] = h2.astype(BF16)
    lane = lax.broadcasted_iota(jnp.int32, (TM, LANES), 1)
    lanef = lane.astype(F32)
    h_hi = h2.astype(BF16)
    h_lo = (h2 - h_hi.astype(F32)).astype(BF16)
    r = _dot(h_hi, wr_ref[...])
    logits = r[:, :LANES] + r[:, LANES:] + _dot(h_lo, wr_ref[:, :LANES]) + br_ref[...]
    work = jnp.where(lane < N_EXPERTS, logits, -jnp.inf)
    hots, vals = [], []
    for _ in range(TOP_K):
        mx = jnp.max(work, axis=-1, keepdims=True)
        idx = jnp.min(jnp.where(work == mx, lanef, float(LANES)), axis=-1, keepdims=True)
        hot = lanef == idx
        work = jnp.where(hot, -jnp.inf, work)
        hots.append((hot, idx))
        vals.append(mx)
    es = [jnp.exp(v - vals[0]) for v in vals]
    inv = 1.0 / (es[0] + es[1] + es[2] + es[3])
    sel = jnp.zeros((TM, LANES), F32)
    for hot, _ in hots:
        sel = jnp.where(hot, 1.0, sel)
    row = lax.broadcasted_iota(jnp.int32, (TM, TM), 0)
    col = lax.broadcasted_iota(jnp.int32, (TM, TM), 1)
    strict = jnp.where(row > col, 1.0, 0.0).astype(BF16)
    before = _dot(strict, sel.astype(BF16))
    meta = jnp.zeros((TM, LANES), F32)
    for kk, (hot, idx) in enumerate(hots):
        rank = jnp.sum(jnp.where(hot, before, 0.0), axis=-1, keepdims=True)
        meta = jnp.where(lane == _META_IDX + kk, idx, meta)
        meta = jnp.where(lane == _META_RANK + kk, rank, meta)
        meta = jnp.where(lane == _META_W + kk, es[kk] * inv, meta)
    meta_ref[...] = meta
    cnt_ref[0] = jnp.broadcast_to(jnp.sum(sel, axis=0, keepdims=True), (SUBLANES, LANES))


def _even_out_kernel(oa_ref, ob_ref, x_ref, mod_ref, woa_ref, wob_ref, nffn_ref, wr_ref, br_ref,
                     x1_ref, h2_ref, meta_ref, cnt_ref):
    m = mod_ref[0]
    out = _dot(oa_ref[...], woa_ref[...]) + _dot(ob_ref[...], wob_ref[...])
    x1 = x_ref[...] + m[2:3] * out
    _moe_prologue(x1, m, nffn_ref, wr_ref, br_ref, x1_ref, h2_ref, meta_ref, cnt_ref)


def _odd_out_kernel(of_ref, ob_ref, g_ref, x_ref, mod_ref, onorm_ref, wo_ref, nffn_ref, wr_ref, br_ref,
                    x1_ref, h2_ref, meta_ref, cnt_ref):
    m = mod_ref[0]
    onorm = onorm_ref[...]
    parts = []
    for h in range(H_C):
        cols = slice(h * DV_C, (h + 1) * DV_C)
        o = of_ref[:, cols] + ob_ref[:, cols]
        parts.append((_rms(o) * onorm * _silu(g_ref[:, cols])).astype(BF16))
    out = _dot(jnp.concatenate(parts, axis=-1), wo_ref[...])
    x1 = x_ref[...] + m[2:3] * out
    _moe_prologue(x1, m, nffn_ref, wr_ref, br_ref, x1_ref, h2_ref, meta_ref, cnt_ref)


_TOK = lambda w: pl.BlockSpec((TM, w), lambda i, *_: (i, 0))
_MOD = pl.BlockSpec((1, 6, D), lambda i, *_: (_cond_row(i), 0, 0))
_TILE_ROW = pl.BlockSpec((1, SUBLANES, LANES), lambda i, *_: (i, 0, 0))

_PROLOGUE_OUT_SPECS = [_TOK(D), _TOK(D), _TOK(LANES), _TILE_ROW]
_PROLOGUE_OUT_SHAPE = [
    jax.ShapeDtypeStruct((T, D), F32),
    jax.ShapeDtypeStruct((T, D), BF16),
    jax.ShapeDtypeStruct((T, LANES), F32),
    jax.ShapeDtypeStruct((NBLK, SUBLANES, LANES), F32),
]


def _local_positions(meta, seg_row):
    lanef = lax.broadcasted_iota(jnp.int32, (TM, LANES), 1).astype(F32)
    pos = []
    for k in range(TOP_K):
        hot = lanef == meta[:, _META_IDX + k:_META_IDX + k + 1]
        start = jnp.sum(jnp.where(hot, seg_row, 0.0), axis=-1, keepdims=True)
        pos.append(start + meta[:, _META_RANK + k:_META_RANK + k + 1])
    return pos


def _segment_copies(i, base_ref, seg_ref, nch_ref, make, wait):
    def per_expert(e, carry):
        q = i * N_EXPERTS + e
        local0, global0 = seg_ref[q], base_ref[q]

        def one(j, c):
            cp = make(pl.multiple_of(local0 + j * CH, CH), pl.multiple_of(global0 + j * CH, CH))
            if wait:
                cp.wait()
            else:
                cp.start()
            return c

        return lax.fori_loop(0, nch_ref[q], one, carry)

    lax.fori_loop(0, N_EXPERTS, per_expert, 0)


def _dispatch_kernel(base_ref, seg_ref, nch_ref, tail0_ref, tailn_ref, h_ref, meta_ref, segrow_ref, xs_hbm,
                     buf, zbuf, sem):
    i = pl.program_id(0)

    @pl.when(i == 0)
    def _():
        zbuf[...] = jnp.zeros_like(zbuf)
        for wait in (False, True):
            def per_expert(e, carry, wait=wait):
                def one(j, c):
                    cp = pltpu.make_async_copy(
                        zbuf, xs_hbm.at[pl.ds(pl.multiple_of(tail0_ref[e] + j * CH, CH), CH)], sem)
                    if wait:
                        cp.wait()
                    else:
                        cp.start()
                    return c

                return lax.fori_loop(0, tailn_ref[e], one, carry)

            lax.fori_loop(0, N_EXPERTS, per_expert, 0)

    pos = _local_positions(meta_ref[...], segrow_ref[0])
    riota = lax.broadcasted_iota(jnp.int32, (TM, ROWS_L), 1).astype(F32)
    pt = jnp.zeros((TM, ROWS_L), F32)
    for p in pos:
        pt = jnp.where(riota == p, 1.0, pt)
    buf[...] = _dot_tn(pt.astype(BF16), h_ref[...])

    make = lambda s, d: pltpu.make_async_copy(buf.at[pl.ds(s, CH)], xs_hbm.at[pl.ds(d, CH)], sem)
    _segment_copies(i, base_ref, seg_ref, nch_ref, make, False)
    _segment_copies(i, base_ref, seg_ref, nch_ref, make, True)


def _dispatch(plan, h2, meta):
    grid_spec = pltpu.PrefetchScalarGridSpec(
        num_scalar_prefetch=5,
        grid=(NBLK,),
        in_specs=[_TOK(D), _TOK(LANES), pl.BlockSpec((1, 1, LANES), lambda i, *_: (i, 0, 0))],
        out_specs=pl.BlockSpec(memory_space=pl.ANY),
        scratch_shapes=[pltpu.VMEM((ROWS_L, D), F32), pltpu.VMEM((CH, D), F32), pltpu.SemaphoreType.DMA],
    )
    return pl.pallas_call(
        _dispatch_kernel,
        grid_spec=grid_spec,
        out_shape=jax.ShapeDtypeStruct((P_ROWS, D), F32),
        compiler_params=_params(("arbitrary",)),
        name="moe_dispatch",
    )(plan["base"], plan["seg"], plan["nch"], plan["tail0"], plan["tailn"], h2, meta, plan["segrow"])


def _expert_kernel(te_ref, tfirst_ref, tvalid_ref, xblk_ref, x_ref, wgu_ref, bgu_ref, wd_ref, bd_ref, y_ref,
                   wgu_bf, wd_bf):
    del xblk_ref
    i = pl.program_id(0)

    @pl.when(tfirst_ref[i] == 1)
    def _():
        wgu_bf[...] = wgu_ref[...].astype(BF16)
        wd_bf[...] = wd_ref[...].astype(BF16)

    @pl.when(tvalid_ref[i] == 1)
    def _():
        a = _dot(x_ref[...].astype(BF16), wgu_bf[...]) + bgu_ref[...]
        glu = jnp.minimum(a[:, :D_FF], SWIGLU_LIMIT)
        lin = jnp.clip(a[:, D_FF:], -SWIGLU_LIMIT, SWIGLU_LIMIT)
        act = (glu * jax.nn.sigmoid(SWIGLU_ALPHA * glu)) * (lin + 1.0)
        y_ref[...] = _dot(act.astype(BF16), wd_bf[...]) + bd_ref[...]

    @pl.when(tvalid_ref[i] == 0)
    def _():
        y_ref[...] = jnp.zeros_like(y_ref)


def _experts(layer, plan, xs, w_gu, b_gu, w_down, b_down):
    depth = w_gu.shape[0]
    e_of = lambda i, te, *_: (layer, te[i], 0, 0)
    grid_spec = pltpu.PrefetchScalarGridSpec(
        num_scalar_prefetch=4,
        grid=(N_ETILES,),
        in_specs=[
            pl.BlockSpec((ETILE, D), lambda i, te, tf, tv, xb: (xb[i], 0)),
            pl.BlockSpec((None, None, D, 2 * D_FF), e_of),
            pl.BlockSpec((None, None, 1, 2 * D_FF), e_of),
            pl.BlockSpec((None, None, D_FF, D), e_of),
            pl.BlockSpec((None, None, 1, D), e_of),
        ],
        out_specs=pl.BlockSpec((ETILE, D), lambda i, *_: (i, 0)),
        scratch_shapes=[pltpu.VMEM((D, 2 * D_FF), BF16), pltpu.VMEM((D_FF, D), BF16)],
    )
    return pl.pallas_call(
        _expert_kernel,
        grid_spec=grid_spec,
        out_shape=jax.ShapeDtypeStruct((P_ROWS, D), F32),
        compiler_params=_params(("arbitrary",)),
        name="moe_experts",
    )(plan["te"], plan["tfirst"], plan["tvalid"], plan["xblk"], xs, w_gu,
      b_gu.reshape(depth, N_EXPERTS, 1, 2 * D_FF), w_down, b_down.reshape(depth, N_EXPERTS, 1, D))


def _combine_kernel(base_ref, seg_ref, nch_ref, y_hbm, x1_ref, meta_ref, segrow_ref, mod_ref, o_ref, ybuf, sem):
    i = pl.program_id(0)

    @pl.when(i == 0)
    def _():
        ybuf[...] = jnp.zeros_like(ybuf)

    make = lambda s, d: pltpu.make_async_copy(y_hbm.at[pl.ds(d, CH)], ybuf.at[pl.ds(s, CH)], sem)
    _segment_copies(i, base_ref, seg_ref, nch_ref, make, False)
    _segment_copies(i, base_ref, seg_ref, nch_ref, make, True)

    meta = meta_ref[...]
    pos = _local_positions(meta, segrow_ref[0])
    riota = lax.broadcasted_iota(jnp.int32, (TM, ROWS_L), 1).astype(F32)
    gates = jnp.zeros((TM, ROWS_L), F32)
    for k, p in enumerate(pos):
        gates = jnp.where(riota == p, meta[:, _META_W + k:_META_W + k + 1], gates)
    hi = gates.astype(BF16)
    lo = (gates - hi.astype(F32)).astype(BF16)
    yb = ybuf[...].astype(BF16)
    acc = _dot(hi, yb) + _dot(lo, yb)
    o_ref[...] = x1_ref[...] + mod_ref[0][5:6] * acc


def _combine(plan, y, x1, meta, mod):
    grid_spec = pltpu.PrefetchScalarGridSpec(
        num_scalar_prefetch=3,
        grid=(NBLK,),
        in_specs=[pl.BlockSpec(memory_space=pl.ANY), _TOK(D), _TOK(LANES),
                  pl.BlockSpec((1, 1, LANES), lambda i, *_: (i, 0, 0)), _MOD],
        out_specs=_TOK(D),
        scratch_shapes=[pltpu.VMEM((ROWS_L, D), F32), pltpu.SemaphoreType.DMA],
    )
    return pl.pallas_call(
        _combine_kernel,
        grid_spec=grid_spec,
        out_shape=jax.ShapeDtypeStruct((T, D), F32),
        compiler_params=_params(("arbitrary",)),
        name="moe_combine",
    )(plan["base"], plan["seg"], plan["nch"], y, x1, meta, plan["segrow"], mod)


def _moe_plan(cnt):
    cnt = cnt[:, 0, :N_EXPERTS].astype(jnp.int32)
    c8 = (cnt + CH - 1) // CH * CH
    tot = jnp.sum(c8, axis=0)
    tiles = (tot + ETILE - 1) // ETILE
    tile_end = jnp.cumsum(tiles)
    offs = (tile_end - tiles) * ETILE
    base = offs[None, :] + jnp.cumsum(c8, axis=0) - c8
    seg = jnp.cumsum(c8, axis=1) - c8
    tid = jnp.arange(N_ETILES, dtype=jnp.int32)
    te = jnp.sum((tile_end[None, :] <= tid[:, None]).astype(jnp.int32), axis=1)
    tvalid = (te < N_EXPERTS).astype(jnp.int32)
    last = jnp.max(jnp.where(tiles > 0, jnp.arange(N_EXPERTS, dtype=jnp.int32), 0))
    te = jnp.where(tvalid == 1, te, last)
    return {
        "base": base.reshape(-1), "seg": seg.reshape(-1), "nch": (c8 // CH).reshape(-1),
        "tail0": offs + tot, "tailn": (tiles * ETILE - tot) // CH,
        "segrow": jnp.pad(seg.astype(F32), ((0, 0), (0, LANES - N_EXPERTS))).reshape(NBLK, 1, LANES),
        "te": te, "tvalid": tvalid, "xblk": jnp.where(tvalid == 1, tid, 0),
        "tfirst": jnp.concatenate([jnp.ones((1,), jnp.int32), (te[1:] != te[:-1]).astype(jnp.int32)]),
    }


def _moe(layer, x1, h2, meta, cnt, mod, w_gu, b_gu, w_down, b_down):
    plan = _moe_plan(cnt)
    xs = _dispatch(plan, h2, meta)
    y = _experts(layer, plan, xs, w_gu, b_gu, w_down, b_down)
    return _combine(plan, y, x1, meta, mod)


_HK, _HV = H_C * DK_C, H_C * DV_C
_ODD_MAIN = 2 * _HK + 2 * _HV


def _odd_in_kernel(x_ref, mod_ref, nmix_ref, win_ref, wgk_ref, bgk_ref, q_ref, k_ref, v_ref, g_ref, la_ref):
    m = mod_ref[0]
    h = _rms(x_ref[...]) * nmix_ref[...] * (1.0 + m[1:2]) + m[0:1]
    a = _dot(h.astype(BF16), win_ref[...])
    q_ref[...] = a[:, :_HK] * (DK_C ** -0.5)
    k_ref[...] = a[:, _HK:2 * _HK]
    v_ref[...] = a[:, 2 * _HK:2 * _HK + _HV]
    g_ref[...] = a[:, 2 * _HK + _HV:_ODD_MAIN]
    z = _dot(a[:, _ODD_MAIN:].astype(BF16), wgk_ref[...]) + bgk_ref[...]
    la_ref[...] = (jnp.minimum(z, 0.0) - jnp.log(1.0 + jnp.exp(-jnp.abs(z)))) * (1.0 / GATE_TAU)


def _gla_kernel(has_init, nchunk, *refs):
    if has_init:
        (qf, kf, vf, laf, qb, kb, vb, lab, s0f, s0b, _, _, of_ref, ob_ref, st) = refs
    else:
        (qf, kf, vf, laf, qb, kb, vb, lab, of_ref, ob_ref, sf_ref, sb_ref, st) = refs
    j = pl.program_id(1)

    @pl.when(j == 0)
    def _():
        if has_init:
            st[0] = s0f[0]
            st[1] = s0b[0]
        else:
            st[...] = jnp.zeros_like(st)

    row = lax.broadcasted_iota(jnp.int32, (GLA_CHUNK, GLA_CHUNK), 0)
    col = lax.broadcasted_iota(jnp.int32, (GLA_CHUNK, GLA_CHUNK), 1)
    for d, (q_r, k_r, v_r, la_r, o_r) in enumerate(((qf, kf, vf, laf, of_ref), (qb, kb, vb, lab, ob_ref))):
        keep = (col <= row) if d == 0 else (col >= row)
        tri = jnp.where(keep, 1.0, 0.0)
        for h in range(H_C):
            kc = slice(h * DK_C, (h + 1) * DK_C)
            vc = slice(h * DV_C, (h + 1) * DV_C)
            g = la_r[:, kc]
            c = jnp.dot(tri, g, precision=lax.Precision.HIGHEST, preferred_element_type=F32)
            tot = jnp.sum(g, axis=0, keepdims=True)
            q, k, v = q_r[:, kc], k_r[:, kc], v_r[:, vc].astype(BF16)
            qe = (q * jnp.exp(c)).astype(BF16)
            kd = (k * jnp.exp(-c)).astype(BF16)
            att = jnp.where(keep, _dot_nt(qe, kd), 0.0)
            s_t = st[d, h]
            o_r[:, vc] = _dot_nt(qe, s_t.astype(BF16)) + _dot(att.astype(BF16), v)
            k2 = (k * jnp.exp(tot - c)).astype(BF16)
            st[d, h] = s_t * jnp.exp(tot) + _dot_tn(v, k2)

    if not has_init:
        @pl.when(j == nchunk - 1)
        def _():
            sf_ref[0] = st[0]
            sb_ref[0] = st[1]


def _gla_call(has_init, nseq, seqlen, row0, q, k, v, la, prev_f=None, prev_b=None, s0f=None, s0b=None):
    nchunk = seqlen // GLA_CHUNK
    blk0 = row0 // GLA_CHUNK
    fwd = lambda b, j: (blk0 + b * nchunk + j, 0)
    bwd = lambda b, j: (blk0 + b * nchunk + nchunk - 1 - j, 0)
    bwd_la = lambda b, j: (blk0 + b * nchunk + nchunk - 1 - j, 1)
    state_spec = pl.BlockSpec((1, H_C, DV_C, DK_C), lambda b, j: (b, 0, 0, 0))
    in_specs = [
        pl.BlockSpec((GLA_CHUNK, _HK), fwd), pl.BlockSpec((GLA_CHUNK, _HK), fwd),
        pl.BlockSpec((GLA_CHUNK, _HV), fwd), pl.BlockSpec((GLA_CHUNK, _HK), fwd),
        pl.BlockSpec((GLA_CHUNK, _HK), bwd), pl.BlockSpec((GLA_CHUNK, _HK), bwd),
        pl.BlockSpec((GLA_CHUNK, _HV), bwd), pl.BlockSpec((GLA_CHUNK, _HK), bwd_la),
    ]
    args = [q, k, v, la, q, k, v, la]
    aliases = {}
    if has_init:
        in_specs += [state_spec, state_spec] + [pl.BlockSpec(memory_space=pl.ANY)] * 2
        args += [s0f, s0b, prev_f, prev_b]
        aliases = {len(args) - 2: 0, len(args) - 1: 1}
    out_specs = [pl.BlockSpec((GLA_CHUNK, _HV), fwd), pl.BlockSpec((GLA_CHUNK, _HV), bwd)]
    out_shape = [jax.ShapeDtypeStruct((T, _HV), F32)] * 2
    if not has_init:
        out_specs += [state_spec, state_spec]
        out_shape += [jax.ShapeDtypeStruct((nseq, H_C, DV_C, DK_C), F32)] * 2
    return pl.pallas_call(
        functools.partial(_gla_kernel, has_init, nchunk),
        grid=(nseq, nchunk),
        in_specs=in_specs,
        out_specs=out_specs,
        out_shape=out_shape,
        scratch_shapes=[pltpu.VMEM((2, H_C, DV_C, DK_C), F32)],
        input_output_aliases=aliases,
        compiler_params=_params(("arbitrary", "arbitrary")),
        name="gla_sample" if has_init else "gla_prompt",
    )(*args)


def _rope_tables():
    half = ROPE // 2
    inv_freq = np.power(np.float32(ROPE_THETA), -np.arange(0, half, 2, dtype=np.float32) / np.float32(half))
    n = np.arange(SAMPLE_LEN)
    row = (n // GRID_W).astype(np.float32)
    col = (n % GRID_W).astype(np.float32)
    ang_r = (row[:, None] * inv_freq[None, :]).astype(np.float32)
    ang_c = (col[:, None] * inv_freq[None, :]).astype(np.float32)
    nf = half // 2
    c = np.ones((TM + SAMPLE_LEN, LANES), np.float32)
    s1 = np.zeros((TM + SAMPLE_LEN, LANES), np.float32)
    s2 = np.zeros((TM + SAMPLE_LEN, LANES), np.float32)
    for base, ang in ((NOPE, ang_r), (NOPE + half, ang_c)):
        c[TM:, base:base + nf] = np.cos(ang)
        c[TM:, base + nf:base + half] = np.cos(ang)
        s1[TM:, base:base + nf] = -np.sin(ang)
        s2[TM:, base + nf:base + half] = np.sin(ang)
    return jnp.asarray(c), jnp.asarray(s1), jnp.asarray(s2)


def _pad_heads(w, nheads, width, lo=0):
    k = w.shape[0]
    w = w.reshape(k, nheads, width)
    w = jnp.pad(w, ((0, 0), (0, 0), (lo, HP - lo - width)))
    return w.reshape(k, nheads * HP)


def _row128(v, lo=0):
    return jnp.pad(v, (lo, LANES - lo - v.shape[0])).reshape(1, LANES)


def _even_layer(x, mod, nmix, w_in, q_a_norm, w_uq, q_norm, kv_a_norm, w_ukv, k_norm, v_norm, w_s, b_s, w_out,
                cache_ckv, cache_kpe):
    s = np.cumsum([Q_LORA, KV_LORA, ROPE, W_B])
    w_q, w_ckv, w_kpe, w_u, w_v = (w_in[:, :s[0]], w_in[:, s[0]:s[1]], w_in[:, s[1]:s[2]], w_in[:, s[2]:s[3]],
                                   w_in[:, s[3]:])
    w_kpe = jnp.pad(w_kpe, ((0, 0), (NOPE, LANES - QK_DIM)))
    win = jnp.concatenate([w_q, w_ckv, w_u, w_v, w_kpe], axis=1).astype(BF16)
    wuq = _pad_heads(w_uq, H_A, QK_DIM).astype(BF16)
    ukv = w_ukv.reshape(KV_LORA, H_A, NOPE + V_A)
    wuk = _pad_heads(ukv[:, :, :NOPE].reshape(KV_LORA, H_A * NOPE), H_A, NOPE)
    wuv = _pad_heads(ukv[:, :, NOPE:].reshape(KV_LORA, H_A * V_A), H_A, V_A)
    wukv = jnp.concatenate([wuk, wuv], axis=1).astype(BF16)
    qgain = _row128(q_norm * (QK_DIM ** -0.5))
    kgain = _row128(k_norm)
    bias = b_s.reshape(G_B // 2, 2, CHUNK_B)
    bias = jnp.concatenate([jnp.broadcast_to(bias[:, 0, :, None], (G_B // 2, CHUNK_B, C_B)),
                            jnp.broadcast_to(bias[:, 1, :, None], (G_B // 2, CHUNK_B, C_B))], axis=-1)
    rc, rs1, rs2 = _rope_tables()
    rope_spec = pl.BlockSpec((TM, LANES), lambda i: (_rope_blk(i), 0))
    width = H_A * HP
    q, k, v, ob, ckv, kpe = pl.pallas_call(
        _even_in_kernel,
        grid=(NBLK,),
        in_specs=[
            _TOK(D), _MOD, _full((1, D)), _full((D, _WIN_N)), _full((1, Q_LORA)), _full((Q_LORA, width)),
            _full((1, LANES)), _full((1, KV_LORA)), _full((KV_LORA, 2 * width)), _full((1, LANES)),
            _full((1, W_B)), _full((G_B // 2, 2 * CHUNK_B, CHUNK_B)), _full((G_B // 2, CHUNK_B, LANES)),
            rope_spec, rope_spec, rope_spec,
        ],
        out_specs=[_TOK(width), _TOK(width), _TOK(width), _TOK(W_B), _TOK(KV_LORA), _TOK(ROPE)],
        out_shape=[
            jax.ShapeDtypeStruct((T, width), BF16), jax.ShapeDtypeStruct((T, width), BF16),
            jax.ShapeDtypeStruct((T, width), BF16), jax.ShapeDtypeStruct((T, W_B), BF16),
            jax.ShapeDtypeStruct((T, KV_LORA), F32), jax.ShapeDtypeStruct((T, ROPE), F32),
        ],
        compiler_params=_params(("arbitrary",)),
        name="even_in",
    )(x, mod, nmix.reshape(1, D), win, q_a_norm.reshape(1, Q_LORA), wuq, qgain, kv_a_norm.reshape(1, KV_LORA),
      wukv, kgain, v_norm.reshape(1, W_B), w_s.astype(BF16).reshape(G_B // 2, 2 * CHUNK_B, CHUNK_B), bias, rc, rs1,
      rs2)

    n_ctx = N_SAMPLE_SEQ * PAST_LEN
    kpe_ctx = jnp.pad(cache_kpe.reshape(n_ctx, ROPE), ((0, 0), (NOPE, LANES - QK_DIM)))
    k_ctx, v_ctx = pl.pallas_call(
        _ctx_kv_kernel,
        grid=(n_ctx // TM,),
        in_specs=[_TOK(KV_LORA), _TOK(LANES), _full((KV_LORA, 2 * width)), _full((1, LANES))],
        out_specs=[_TOK(width), _TOK(width)],
        out_shape=[jax.ShapeDtypeStruct((n_ctx, width), BF16)] * 2,
        compiler_params=_params(("arbitrary",)),
        name="ctx_kv",
    )(cache_ckv.reshape(n_ctx, KV_LORA), kpe_ctx, wukv, kgain)

    oa = _attention(q, k, v, k_ctx, v_ctx)
    woa = jnp.pad(w_out[:H_A * V_A].reshape(H_A, V_A, D), ((0, 0), (0, HP - V_A), (0, 0))).reshape(width, D)
    return oa, ob, woa.astype(BF16), w_out[H_A * V_A:].astype(BF16), ckv, kpe


def kernel(x_prompt, x_sample, cache_mla_ckv, cache_mla_kpe, state_gla_fwd, state_gla_bwd, c, c_ctx, ada_w, ada_b,
           norm_mix, norm_ffn, even_w_in, mla_q_a_norm, mla_w_uq, mla_q_norm, mla_kv_a_norm, mla_w_ukv, mla_k_norm,
           cmlp_v_norm, cmlp_w_s, cmlp_b_s, even_w_out, odd_w_in, gla_w_gk_fwd, gla_b_gk_fwd, gla_w_gk_bwd,
           gla_b_gk_bwd, gla_o_norm, odd_w_out, moe_w_router, moe_b_router, moe_w_gu, moe_b_gu, moe_w_down,
           moe_b_down):
    x0 = jnp.concatenate([x_prompt.reshape(N_PROMPT, D), x_sample.reshape(N_SAMPLE, D)], axis=0)
    cond8 = jnp.concatenate([c_ctx[None, :], c, jnp.zeros((SUBLANES - 1 - N_SAMPLE_SEQ, D), F32)], axis=0)
    mods = _adaln(cond8, ada_w, ada_b)
    wr = jnp.pad(moe_w_router, ((0, 0), (0, 0), (0, LANES - N_EXPERTS)))
    wr_hi = wr.astype(BF16)
    wr = jnp.concatenate([wr_hi, (wr - wr_hi.astype(F32)).astype(BF16)], axis=-1)
    br = jnp.pad(moe_b_router, ((0, 0), (0, LANES - N_EXPERTS))).reshape(2, 1, LANES)

    oa, ob, woa, wob, ckv, kpe = _even_layer(
        x0, mods[0], norm_mix[0], even_w_in[0], mla_q_a_norm[0], mla_w_uq[0], mla_q_norm[0], mla_kv_a_norm[0],
        mla_w_ukv[0], mla_k_norm[0], cmlp_v_norm[0], cmlp_w_s[0], cmlp_b_s[0], even_w_out[0],
        cache_mla_ckv[:, 0], cache_mla_kpe[:, 0])
    width = H_A * HP
    x1, h2, meta, cnt = pl.pallas_call(
        _even_out_kernel,
        grid=(NBLK,),
        in_specs=[_TOK(width), _TOK(W_B), _TOK(D), _MOD, _full((width, D)), _full((W_B, D)), _full((1, D)),
                  _full((D, 2 * LANES)), _full((1, LANES))],
        out_specs=_PROLOGUE_OUT_SPECS,
        out_shape=_PROLOGUE_OUT_SHAPE,
        compiler_params=_params(("arbitrary",)),
        name="even_out",
    )(oa, ob, x0, mods[0], woa, wob, norm_ffn[0].reshape(1, D), wr[0], br[0])
    x2 = _moe(0, x1, h2, meta, cnt, mods[0], moe_w_gu, moe_b_gu, moe_w_down, moe_b_down)

    w_in = odd_w_in[0]
    win = jnp.concatenate([w_in, jnp.zeros((D, LANES - 2 * GATE_RANK), F32)], axis=1).astype(BF16)
    wgk = jnp.zeros((LANES, 2 * _HK), F32)
    wgk = wgk.at[:GATE_RANK, :_HK].set(gla_w_gk_fwd[0]).at[GATE_RANK:2 * GATE_RANK, _HK:].set(gla_w_gk_bwd[0])
    bgk = jnp.concatenate([gla_b_gk_fwd[0], gla_b_gk_bwd[0]]).reshape(1, 2 * _HK)
    q, k, v, g, la = pl.pallas_call(
        _odd_in_kernel,
        grid=(NBLK,),
        in_specs=[_TOK(D), _MOD, _full((1, D)), _full((D, _ODD_MAIN + LANES)), _full((LANES, 2 * _HK)),
                  _full((1, 2 * _HK))],
        out_specs=[_TOK(_HK), _TOK(_HK), _TOK(_HV), _TOK(_HV), _TOK(2 * _HK)],
        out_shape=[jax.ShapeDtypeStruct((T, _HK), F32), jax.ShapeDtypeStruct((T, _HK), F32),
                   jax.ShapeDtypeStruct((T, _HV), F32), jax.ShapeDtypeStruct((T, _HV), F32),
                   jax.ShapeDtypeStruct((T, 2 * _HK), F32)],
        compiler_params=_params(("arbitrary",)),
        name="odd_in",
    )(x2, mods[1], norm_mix[1].reshape(1, D), win, wgk.astype(BF16), bgk)

    of, obk, st_f, st_b = _gla_call(False, N_PROMPT_SEQ, PROMPT_LEN, 0, q, k, v, la)
    s0f = state_gla_fwd[:, 0].transpose(0, 1, 3, 2)
    s0b = state_gla_bwd[:, 0].transpose(0, 1, 3, 2)
    of, obk = _gla_call(True, N_SAMPLE_SEQ, SAMPLE_LEN, N_PROMPT, q, k, v, la, of, obk, s0f, s0b)

    x3, h2, meta, cnt = pl.pallas_call(
        _odd_out_kernel,
        grid=(NBLK,),
        in_specs=[_TOK(_HV), _TOK(_HV), _TOK(_HV), _TOK(D), _MOD, _full((1, DV_C)), _full((_HV, D)),
                  _full((1, D)), _full((D, 2 * LANES)), _full((1, LANES))],
        out_specs=_PROLOGUE_OUT_SPECS,
        out_shape=_PROLOGUE_OUT_SHAPE,
        compiler_params=_params(("arbitrary",)),
        name="odd_out",
    )(of, obk, g, x2, mods[1], gla_o_norm[0].reshape(1, DV_C), odd_w_out[0].astype(BF16),
      norm_ffn[1].reshape(1, D), wr[1], br[1])
    x4 = _moe(1, x3, h2, meta, cnt, mods[1], moe_w_gu, moe_b_gu, moe_w_down, moe_b_down)

    y_prompt = x4[:N_PROMPT].reshape(N_PROMPT_SEQ, PROMPT_LEN, D)
    y_sample = x4[N_PROMPT:].reshape(N_SAMPLE_SEQ, SAMPLE_LEN, D)
    new_ckv = ckv[:N_PROMPT].reshape(N_PROMPT_SEQ, 1, PROMPT_LEN, KV_LORA)
    new_kpe = kpe[:N_PROMPT].reshape(N_PROMPT_SEQ, 1, PROMPT_LEN, ROPE)
    new_fwd = st_f.transpose(0, 1, 3, 2)[:, None]
    new_bwd = st_b.transpose(0, 1, 3, 2)[:, None]
    return (y_prompt, y_sample, new_ckv, new_kpe, new_fwd, new_bwd)
```

```python
import functools

import numpy as np
import jax
import jax.numpy as jnp
from jax import lax
from jax.experimental import pallas as pl
from jax.experimental.pallas import tpu as pltpu

F32 = jnp.float32
BF16 = jnp.bfloat16

D = 1024
N_PROMPT_SEQ, PROMPT_LEN = 16, 256
N_SAMPLE_SEQ, SAMPLE_LEN = 4, 2048
PAST_LEN = 512
N_PROMPT = N_PROMPT_SEQ * PROMPT_LEN
N_SAMPLE = N_SAMPLE_SEQ * SAMPLE_LEN
T = N_PROMPT + N_SAMPLE
EPS = 1e-6
GRID_W = 64
H_A, Q_LORA, KV_LORA, NOPE, ROPE, V_A = 8, 512, 256, 64, 32, 64
QK_DIM = NOPE + ROPE
G_B, C_B, W_B, CHUNK_B = 8, 64, 512, 128
H_C, DK_C, DV_C, GATE_RANK, GATE_TAU, GLA_CHUNK = 4, 128, 256, 16, 16.0, 64
N_EXPERTS, TOP_K, D_FF = 32, 4, 1024
SWIGLU_LIMIT, SWIGLU_ALPHA = 7.0, 1.702
ROPE_THETA = 10000.0

LANES = 128
SUBLANES = 8
VMEM_LIMIT = 56 * 1024 * 1024

TM = 256
NBLK = T // TM
PROMPT_BLKS = N_PROMPT // TM
BLKS_PER_SAMPLE_SEQ = SAMPLE_LEN // TM
ETILE = 512
CH = SUBLANES
_MAX_LOCAL = TM * TOP_K + N_EXPERTS * (CH - 1)
ROWS_L = -(-_MAX_LOCAL // LANES) * LANES
N_ETILES = -(-NBLK * _MAX_LOCAL // ETILE) + N_EXPERTS
P_ROWS = N_ETILES * ETILE
HP = LANES


def _cond_row(i):
    return jnp.where(i < PROMPT_BLKS, 0, 1 + (i - PROMPT_BLKS) // BLKS_PER_SAMPLE_SEQ)


def _rope_blk(i):
    return jnp.where(i < PROMPT_BLKS, 0, 1 + (i - PROMPT_BLKS) % BLKS_PER_SAMPLE_SEQ)


def _rms(x):
    return x * lax.rsqrt(jnp.mean(x * x, axis=-1, keepdims=True) + EPS)


def _gelu(x):
    return 0.5 * x * (1.0 + jnp.tanh(0.7978845608028654 * (x + 0.044715 * (x * x * x))))


def _silu(x):
    return x * jax.nn.sigmoid(x)


def _dot(a, b):
    return jnp.dot(a, b, preferred_element_type=F32)


def _dot_nt(a, b):
    return lax.dot_general(a, b, (((1,), (1,)), ((), ())), preferred_element_type=F32)


def _dot_tn(a, b):
    return lax.dot_general(a, b, (((0,), (0,)), ((), ())), preferred_element_type=F32)


def _params(sem, vmem=VMEM_LIMIT):
    return pltpu.CompilerParams(dimension_semantics=sem, vmem_limit_bytes=vmem)


def _full(shape):
    nd = len(shape)
    return pl.BlockSpec(shape, lambda *_: (0,) * nd)


ADA_TN = 1536


def _adaln_kernel(c_ref, w_ref, b_ref, o_ref):
    s = _silu(c_ref[...]).astype(BF16)
    o_ref[0] = _dot(s, w_ref[0].astype(BF16)) + b_ref[0]


def _adaln(cond8, ada_w, ada_b):
    depth = ada_w.shape[0]
    n = ada_w.shape[2]
    out = pl.pallas_call(
        _adaln_kernel,
        grid=(depth, n // ADA_TN),
        in_specs=[
            pl.BlockSpec((SUBLANES, D), lambda l, j: (0, 0)),
            pl.BlockSpec((1, D, ADA_TN), lambda l, j: (l, 0, j)),
            pl.BlockSpec((1, 1, ADA_TN), lambda l, j: (l, 0, j)),
        ],
        out_specs=pl.BlockSpec((1, SUBLANES, ADA_TN), lambda l, j: (l, 0, j)),
        out_shape=jax.ShapeDtypeStruct((depth, SUBLANES, n), F32),
        compiler_params=_params(("arbitrary", "arbitrary")),
        name="adaln",
    )(cond8, ada_w, ada_b.reshape(depth, 1, n))
    return out.reshape(depth, SUBLANES, 6, D)


_QC0, _CKV0, _U0, _V0, _KPE0, _WIN_N = 0, 512, 768, 1280, 1792, 1920


def _rope(y, c, s1, s2):
    return y * c + pltpu.roll(y, LANES - 8, 1) * s1 + pltpu.roll(y, 8, 1) * s2


def _k_heads(k_raw, kpe128, kp_rot, kgain, k_ref):
    sskpe = jnp.sum(kpe128 * kpe128, axis=-1, keepdims=True)
    for h in range(H_A):
        kb = k_raw[:, h * HP:(h + 1) * HP]
        r = lax.rsqrt((jnp.sum(kb * kb, axis=-1, keepdims=True) + sskpe) * (1.0 / QK_DIM) + EPS)
        k_ref[:, h * HP:(h + 1) * HP] = ((kb * kgain + kp_rot) * r).astype(BF16)


def _even_in_kernel(x_ref, mod_ref, nmix_ref, win_ref, qan_ref, wuq_ref, qgain_ref, kvan_ref, wukv_ref,
                    kgain_ref, vnorm_ref, ws_ref, bs_ref, rc_ref, rs1_ref, rs2_ref,
                    q_ref, k_ref, v_ref, ob_ref, ckv_ref, kpe_ref):
    m = mod_ref[0]
    h = _rms(x_ref[...]) * nmix_ref[...] * (1.0 + m[1:2]) + m[0:1]
    a = _dot(h.astype(BF16), win_ref[...])
    qc = a[:, _QC0:_CKV0]
    ckv = a[:, _CKV0:_U0]
    u = a[:, _U0:_V0]
    vv = a[:, _V0:_KPE0]
    kpe128 = a[:, _KPE0:_WIN_N]

    ckv_n = _rms(ckv) * kvan_ref[...]
    ckv_ref[...] = ckv_n
    kpe_ref[...] = kpe128[:, NOPE:QK_DIM]

    rc, rs1, rs2 = rc_ref[...], rs1_ref[...], rs2_ref[...]
    qn = (_rms(qc) * qan_ref[...]).astype(BF16)
    qr = _dot(qn, wuq_ref[...])
    qgain = qgain_ref[...]
    for hh in range(H_A):
        blk = qr[:, hh * HP:(hh + 1) * HP]
        r = lax.rsqrt(jnp.sum(blk * blk, axis=-1, keepdims=True) * (1.0 / QK_DIM) + EPS)
        q_ref[:, hh * HP:(hh + 1) * HP] = _rope(blk * r * qgain, rc, rs1, rs2).astype(BF16)

    kv = _dot(ckv_n.astype(BF16), wukv_ref[...])
    v_ref[...] = kv[:, H_A * HP:].astype(BF16)
    kgain = kgain_ref[...]
    kp_rot = _rope(kpe128 * kgain, rc, rs1, rs2)
    _k_heads(kv[:, :H_A * HP], kpe128, kp_rot, kgain, k_ref)

    ug = _gelu(u)
    vn = (_rms(_gelu(vv)) * vnorm_ref[...]).astype(BF16)
    low = lax.broadcasted_iota(jnp.int32, (CHUNK_B, LANES), 1) < C_B
    for c in range(TM // CHUNK_B):
        rows = slice(c * CHUNK_B, (c + 1) * CHUNK_B)
        for p in range(G_B // 2):
            cols = slice(p * LANES, (p + 1) * LANES)
            blk = vn[rows, cols]
            both = _dot(ws_ref[p], blk)
            mixed = jnp.where(low, both[:CHUNK_B], both[CHUNK_B:]) + bs_ref[p]
            ob_ref[rows, cols] = (ug[rows, cols] * mixed).astype(BF16)


def _ctx_kv_kernel(ckv_ref, kpe_ref, wukv_ref, kgain_ref, k_ref, v_ref):
    kv = _dot(ckv_ref[...].astype(BF16), wukv_ref[...])
    v_ref[...] = kv[:, H_A * HP:].astype(BF16)
    kgain = kgain_ref[...]
    kpe128 = kpe_ref[...]
    _k_heads(kv[:, :H_A * HP], kpe128, kpe128 * kgain, kgain, k_ref)


def _attn_self_kernel(q_ref, k_ref, v_ref, o_ref):
    for h in range(H_A):
        cols = slice(h * HP, (h + 1) * HP)
        s = _dot_nt(q_ref[:, cols], k_ref[:, cols])
        p = jnp.exp(s - jnp.max(s, axis=-1, keepdims=True))
        inv = 1.0 / jnp.sum(p, axis=-1, keepdims=True)
        o_ref[:, cols] = (_dot(p.astype(BF16), v_ref[:, cols]) * inv).astype(BF16)


def _attn_ctx_kernel(q_ref, k_ref, v_ref, kc_ref, vc_ref, prev_ref, o_ref):
    del prev_ref
    for h in range(H_A):
        cols = slice(h * HP, (h + 1) * HP)
        q = q_ref[:, cols]
        s1 = _dot_nt(q, k_ref[:, cols])
        s2 = _dot_nt(q, kc_ref[:, cols])
        mx = jnp.maximum(jnp.max(s1, axis=-1, keepdims=True), jnp.max(s2, axis=-1, keepdims=True))
        p1 = jnp.exp(s1 - mx)
        p2 = jnp.exp(s2 - mx)
        inv = 1.0 / (jnp.sum(p1, axis=-1, keepdims=True) + jnp.sum(p2, axis=-1, keepdims=True))
        o = _dot(p1.astype(BF16), v_ref[:, cols]) + _dot(p2.astype(BF16), vc_ref[:, cols])
        o_ref[:, cols] = (o * inv).astype(BF16)


def _attention(q, k, v, k_ctx, v_ctx):
    width = H_A * HP
    o = pl.pallas_call(
        _attn_self_kernel,
        grid=(N_PROMPT_SEQ,),
        in_specs=[pl.BlockSpec((PROMPT_LEN, width), lambda i: (i, 0))] * 3,
        out_specs=pl.BlockSpec((PROMPT_LEN, width), lambda i: (i, 0)),
        out_shape=jax.ShapeDtypeStruct((T, width), BF16),
        compiler_params=_params(("arbitrary",)),
        name="attn_prompt",
    )(q, k, v)
    first = N_PROMPT // SAMPLE_LEN
    qblk = lambda b, j: (PROMPT_BLKS + b * BLKS_PER_SAMPLE_SEQ + j, 0)
    return pl.pallas_call(
        _attn_ctx_kernel,
        grid=(N_SAMPLE_SEQ, BLKS_PER_SAMPLE_SEQ),
        in_specs=[
            pl.BlockSpec((TM, width), qblk),
            pl.BlockSpec((SAMPLE_LEN, width), lambda b, j: (first + b, 0)),
            pl.BlockSpec((SAMPLE_LEN, width), lambda b, j: (first + b, 0)),
            pl.BlockSpec((PAST_LEN, width), lambda b, j: (b, 0)),
            pl.BlockSpec((PAST_LEN, width), lambda b, j: (b, 0)),
            pl.BlockSpec(memory_space=pl.ANY),
        ],
        out_specs=pl.BlockSpec((TM, width), qblk),
        out_shape=jax.ShapeDtypeStruct((T, width), BF16),
        input_output_aliases={5: 0},
        compiler_params=_params(("arbitrary", "arbitrary")),
        name="attn_sample",
    )(q, k, v, k_ctx, v_ctx, o)


_META_IDX, _META_RANK, _META_W = 0, TOP_K, 2 * TOP_K


def _moe_prologue(x1, m, nffn_ref, wr_ref, br_ref, x1_ref, h2_ref, meta_ref, cnt_ref):
    x1_ref[...] = x1
    h2 = _rms(x1) * nffn_ref[...] * (1.0 + m[4:5]) + m[3:4]
    h2_ref[...] = h2.astype(BF16)
    lane = lax.broadcasted_iota(jnp.int32, (TM, LANES), 1)
    lanef = lane.astype(F32)
    h_hi = h2.astype(BF16)
    h_lo = (h2 - h_hi.astype(F32)).astype(BF16)
    r = _dot(h_hi, wr_ref[...])
    logits = r[:, :LANES] + r[:, LANES:] + _dot(h_lo, wr_ref[:, :LANES]) + br_ref[...]
    work = jnp.where(lane < N_EXPERTS, logits, -jnp.inf)
    hots, vals = [], []
    for _ in range(TOP_K):
        mx = jnp.max(work, axis=-1, keepdims=True)
        idx = jnp.min(jnp.where(work == mx, lanef, float(LANES)), axis=-1, keepdims=True)
        hot = lanef == idx
        work = jnp.where(hot, -jnp.inf, work)
        hots.append((hot, idx))
        vals.append(mx)
    es = [jnp.exp(v - vals[0]) for v in vals]
    inv = 1.0 / (es[0] + es[1] + es[2] + es[3])
    sel = jnp.zeros((TM, LANES), F32)
    for hot, _ in hots:
        sel = jnp.where(hot, 1.0, sel)
    row = lax.broadcasted_iota(jnp.int32, (TM, TM), 0)
    col = lax.broadcasted_iota(jnp.int32, (TM, TM), 1)
    strict = jnp.where(row > col, 1.0, 0.0).astype(BF16)
    before = _dot(strict, sel.astype(BF16))
    meta = jnp.zeros((TM, LANES), F32)
    for kk, (hot, idx) in enumerate(hots):
        rank = jnp.sum(jnp.where(hot, before, 0.0), axis=-1, keepdims=True)
        meta = jnp.where(lane == _META_IDX + kk, idx, meta)
        meta = jnp.where(lane == _META_RANK + kk, rank, meta)
        meta = jnp.where(lane == _META_W + kk, es[kk] * inv, meta)
    meta_ref[...] = meta
    cnt_ref[0] = jnp.broadcast_to(jnp.sum(sel, axis=0, keepdims=True), (SUBLANES, LANES))


def _even_out_kernel(oa_ref, ob_ref, x_ref, mod_ref, woa_ref, wob_ref, nffn_ref, wr_ref, br_ref,
                     x1_ref, h2_ref, meta_ref, cnt_ref):
    m = mod_ref[0]
    out = _dot(oa_ref[...], woa_ref[...]) + _dot(ob_ref[...], wob_ref[...])
    x1 = x_ref[...] + m[2:3] * out
    _moe_prologue(x1, m, nffn_ref, wr_ref, br_ref, x1_ref, h2_ref, meta_ref, cnt_ref)


def _odd_out_kernel(of_ref, ob_ref, g_ref, x_ref, mod_ref, onorm_ref, wo_ref, nffn_ref, wr_ref, br_ref,
                    x1_ref, h2_ref, meta_ref, cnt_ref):
    m = mod_ref[0]
    onorm = onorm_ref[...]
    parts = []
    for h in range(H_C):
        cols = slice(h * DV_C, (h + 1) * DV_C)
        o = of_ref[:, cols] + ob_ref[:, cols]
        parts.append((_rms(o) * onorm * _silu(g_ref[:, cols])).astype(BF16))
    out = _dot(jnp.concatenate(parts, axis=-1), wo_ref[...])
    x1 = x_ref[...] + m[2:3] * out
    _moe_prologue(x1, m, nffn_ref, wr_ref, br_ref, x1_ref, h2_ref, meta_ref, cnt_ref)


_TOK = lambda w: pl.BlockSpec((TM, w), lambda i, *_: (i, 0))
_MOD = pl.BlockSpec((1, 6, D), lambda i, *_: (_cond_row(i), 0, 0))
_TILE_ROW = pl.BlockSpec((1, SUBLANES, LANES), lambda i, *_: (i, 0, 0))

_PROLOGUE_OUT_SPECS = [_TOK(D), _TOK(D), _TOK(LANES), _TILE_ROW]
_PROLOGUE_OUT_SHAPE = [
    jax.ShapeDtypeStruct((T, D), F32),
    jax.ShapeDtypeStruct((T, D), BF16),
    jax.ShapeDtypeStruct((T, LANES), F32),
    jax.ShapeDtypeStruct((NBLK, SUBLANES, LANES), F32),
]


def _local_positions(meta, seg_row):
    lanef = lax.broadcasted_iota(jnp.int32, (TM, LANES), 1).astype(F32)
    pos = []
    for k in range(TOP_K):
        hot = lanef == meta[:, _META_IDX + k:_META_IDX + k + 1]
        start = jnp.sum(jnp.where(hot, seg_row, 0.0), axis=-1, keepdims=True)
        pos.append(start + meta[:, _META_RANK + k:_META_RANK + k + 1])
    return pos


def _segment_copies(t, base_ref, seg_ref, nch_ref, make):
    def per_expert(e, carry):
        q = t * N_EXPERTS + e
        local0, global0 = seg_ref[q], base_ref[q]

        def one(j, c):
            make(pl.multiple_of(local0 + j * CH, CH), pl.multiple_of(global0 + j * CH, CH)).start()
            return c

        return lax.fori_loop(0, nch_ref[q], one, carry)

    lax.fori_loop(0, N_EXPERTS, per_expert, 0)


def _drain(count, chunk_copy):
    def one(j, c):
        chunk_copy.wait()
        return c

    lax.fori_loop(0, count, one, 0)


def _dispatch_kernel(base_ref, seg_ref, nch_ref, ntile_ref, tail0_ref, tailn_ref, h_ref, meta_ref, segrow_ref,
                     xs_hbm, buf, zbuf, sem):
    i = pl.program_id(0)
    slot = i % 2
    chunk = lambda s: pltpu.make_async_copy(buf.at[s, pl.ds(0, CH)], xs_hbm.at[pl.ds(0, CH)], sem.at[s])

    @pl.when(i == 0)
    def _():
        zbuf[...] = jnp.zeros_like(zbuf)

        def per_expert(e, carry):
            def one(j, c):
                pltpu.make_async_copy(zbuf, xs_hbm.at[pl.ds(pl.multiple_of(tail0_ref[e] + j * CH, CH), CH)],
                                      sem.at[2]).start()
                return c

            lax.fori_loop(0, tailn_ref[e], one, 0)
            return carry + tailn_ref[e]

        total = lax.fori_loop(0, N_EXPERTS, per_expert, 0)
        _drain(total, pltpu.make_async_copy(zbuf, xs_hbm.at[pl.ds(0, CH)], sem.at[2]))

    pos = _local_positions(meta_ref[...], segrow_ref[0])
    riota = lax.broadcasted_iota(jnp.int32, (TM, ROWS_L), 1).astype(F32)
    pt = jnp.zeros((TM, ROWS_L), F32)
    for p in pos:
        pt = jnp.where(riota == p, 1.0, pt)
    buf[slot] = _dot_tn(pt.astype(BF16), h_ref[...])

    make = lambda s, d: pltpu.make_async_copy(buf.at[slot, pl.ds(s, CH)], xs_hbm.at[pl.ds(d, CH)], sem.at[slot])
    _segment_copies(i, base_ref, seg_ref, nch_ref, make)

    @pl.when(i > 0)
    def _():
        _drain(ntile_ref[i - 1], chunk(1 - slot))

    @pl.when(i == NBLK - 1)
    def _():
        _drain(ntile_ref[i], chunk(slot))


def _dispatch(plan, h2, meta):
    grid_spec = pltpu.PrefetchScalarGridSpec(
        num_scalar_prefetch=6,
        grid=(NBLK,),
        in_specs=[_TOK(D), _TOK(LANES), pl.BlockSpec((1, 1, LANES), lambda i, *_: (i, 0, 0))],
        out_specs=pl.BlockSpec(memory_space=pl.ANY),
        scratch_shapes=[pltpu.VMEM((2, ROWS_L, D), F32), pltpu.VMEM((CH, D), F32),
                        pltpu.SemaphoreType.DMA((3,))],
    )
    return pl.pallas_call(
        _dispatch_kernel,
        grid_spec=grid_spec,
        out_shape=jax.ShapeDtypeStruct((P_ROWS, D), F32),
        compiler_params=_params(("arbitrary",)),
        name="moe_dispatch",
    )(plan["base"], plan["seg"], plan["nch"], plan["ntile"], plan["tail0"], plan["tailn"], h2, meta, plan["segrow"])


def _expert_kernel(te_ref, tfirst_ref, tvalid_ref, xblk_ref, x_ref, wgu_ref, bgu_ref, wd_ref, bd_ref, y_ref,
                   wgu_bf, wd_bf):
    del xblk_ref
    i = pl.program_id(0)

    @pl.when(tfirst_ref[i] == 1)
    def _():
        wgu_bf[...] = wgu_ref[...].astype(BF16)
        wd_bf[...] = wd_ref[...].astype(BF16)

    @pl.when(tvalid_ref[i] == 1)
    def _():
        a = _dot(x_ref[...].astype(BF16), wgu_bf[...]) + bgu_ref[...]
        glu = jnp.minimum(a[:, :D_FF], SWIGLU_LIMIT)
        lin = jnp.clip(a[:, D_FF:], -SWIGLU_LIMIT, SWIGLU_LIMIT)
        act = (glu * jax.nn.sigmoid(SWIGLU_ALPHA * glu)) * (lin + 1.0)
        y_ref[...] = _dot(act.astype(BF16), wd_bf[...]) + bd_ref[...]

    @pl.when(tvalid_ref[i] == 0)
    def _():
        y_ref[...] = jnp.zeros_like(y_ref)


def _experts(layer, plan, xs, w_gu, b_gu, w_down, b_down):
    depth = w_gu.shape[0]
    e_of = lambda i, te, *_: (layer, te[i], 0, 0)
    grid_spec = pltpu.PrefetchScalarGridSpec(
        num_scalar_prefetch=4,
        grid=(N_ETILES,),
        in_specs=[
            pl.BlockSpec((ETILE, D), lambda i, te, tf, tv, xb: (xb[i], 0)),
            pl.BlockSpec((None, None, D, 2 * D_FF), e_of),
            pl.BlockSpec((None, None, 1, 2 * D_FF), e_of),
            pl.BlockSpec((None, None, D_FF, D), e_of),
            pl.BlockSpec((None, None, 1, D), e_of),
        ],
        out_specs=pl.BlockSpec((ETILE, D), lambda i, *_: (i, 0)),
        scratch_shapes=[pltpu.VMEM((D, 2 * D_FF), BF16), pltpu.VMEM((D_FF, D), BF16)],
    )
    return pl.pallas_call(
        _expert_kernel,
        grid_spec=grid_spec,
        out_shape=jax.ShapeDtypeStruct((P_ROWS, D), F32),
        compiler_params=_params(("arbitrary",)),
        name="moe_experts",
    )(plan["te"], plan["tfirst"], plan["tvalid"], plan["xblk"], xs, w_gu,
      b_gu.reshape(depth, N_EXPERTS, 1, 2 * D_FF), w_down, b_down.reshape(depth, N_EXPERTS, 1, D))


def _combine_kernel(base_ref, seg_ref, nch_ref, ntile_ref, y_hbm, x1_ref, meta_ref, segrow_ref, mod_ref, o_ref,
                    ybuf, sem):
    i = pl.program_id(0)
    slot = i % 2

    def fetch(t, s):
        make = lambda loc, glob: pltpu.make_async_copy(y_hbm.at[pl.ds(glob, CH)], ybuf.at[s, pl.ds(loc, CH)],
                                                       sem.at[s])
        _segment_copies(t, base_ref, seg_ref, nch_ref, make)

    @pl.when(i == 0)
    def _():
        ybuf[...] = jnp.zeros_like(ybuf)
        fetch(i, slot)

    @pl.when(i + 1 < NBLK)
    def _():
        fetch(i + 1, 1 - slot)

    _drain(ntile_ref[i], pltpu.make_async_copy(y_hbm.at[pl.ds(0, CH)], ybuf.at[slot, pl.ds(0, CH)], sem.at[slot]))

    meta = meta_ref[...]
    pos = _local_positions(meta, segrow_ref[0])
    riota = lax.broadcasted_iota(jnp.int32, (TM, ROWS_L), 1).astype(F32)
    gates = jnp.zeros((TM, ROWS_L), F32)
    for k, p in enumerate(pos):
        gates = jnp.where(riota == p, meta[:, _META_W + k:_META_W + k + 1], gates)
    hi = gates.astype(BF16)
    lo = (gates - hi.astype(F32)).astype(BF16)
    yb = ybuf[slot].astype(BF16)
    acc = _dot(hi, yb) + _dot(lo, yb)
    o_ref[...] = x1_ref[...] + mod_ref[0][5:6] * acc


def _combine(plan, y, x1, meta, mod):
    grid_spec = pltpu.PrefetchScalarGridSpec(
        num_scalar_prefetch=4,
        grid=(NBLK,),
        in_specs=[pl.BlockSpec(memory_space=pl.ANY), _TOK(D), _TOK(LANES),
                  pl.BlockSpec((1, 1, LANES), lambda i, *_: (i, 0, 0)), _MOD],
        out_specs=_TOK(D),
        scratch_shapes=[pltpu.VMEM((2, ROWS_L, D), F32), pltpu.SemaphoreType.DMA((2,))],
    )
    return pl.pallas_call(
        _combine_kernel,
        grid_spec=grid_spec,
        out_shape=jax.ShapeDtypeStruct((T, D), F32),
        compiler_params=_params(("arbitrary",)),
        name="moe_combine",
    )(plan["base"], plan["seg"], plan["nch"], plan["ntile"], y, x1, meta, plan["segrow"], mod)


def _moe_plan(cnt):
    cnt = cnt[:, 0, :N_EXPERTS].astype(jnp.int32)
    c8 = (cnt + CH - 1) // CH * CH
    tot = jnp.sum(c8, axis=0)
    tiles = (tot + ETILE - 1) // ETILE
    tile_end = jnp.cumsum(tiles)
    offs = (tile_end - tiles) * ETILE
    base = offs[None, :] + jnp.cumsum(c8, axis=0) - c8
    seg = jnp.cumsum(c8, axis=1) - c8
    tid = jnp.arange(N_ETILES, dtype=jnp.int32)
    te = jnp.sum((tile_end[None, :] <= tid[:, None]).astype(jnp.int32), axis=1)
    tvalid = (te < N_EXPERTS).astype(jnp.int32)
    last = jnp.max(jnp.where(tiles > 0, jnp.arange(N_EXPERTS, dtype=jnp.int32), 0))
    te = jnp.where(tvalid == 1, te, last)
    return {
        "base": base.reshape(-1), "seg": seg.reshape(-1), "nch": (c8 // CH).reshape(-1),
        "ntile": jnp.sum(c8, axis=1) // CH, "tail0": offs + tot, "tailn": (tiles * ETILE - tot) // CH,
        "segrow": jnp.pad(seg.astype(F32), ((0, 0), (0, LANES - N_EXPERTS))).reshape(NBLK, 1, LANES),
        "te": te, "tvalid": tvalid, "xblk": jnp.where(tvalid == 1, tid, 0),
        "tfirst": jnp.concatenate([jnp.ones((1,), jnp.int32), (te[1:] != te[:-1]).astype(jnp.int32)]),
    }


def _moe(layer, x1, h2, meta, cnt, mod, w_gu, b_gu, w_down, b_down):
    plan = _moe_plan(cnt)
    xs = _dispatch(plan, h2, meta)
    y = _experts(layer, plan, xs, w_gu, b_gu, w_down, b_down)
    return _combine(plan, y, x1, meta, mod)


_HK, _HV = H_C * DK_C, H_C * DV_C
_ODD_MAIN = 2 * _HK + 2 * _HV


def _odd_in_kernel(x_ref, mod_ref, nmix_ref, win_ref, wgk_ref, bgk_ref, q_ref, k_ref, v_ref, g_ref, la_ref):
    m = mod_ref[0]
    h = _rms(x_ref[...]) * nmix_ref[...] * (1.0 + m[1:2]) + m[0:1]
    a = _dot(h.astype(BF16), win_ref[...])
    q_ref[...] = a[:, :_HK] * (DK_C ** -0.5)
    k_ref[...] = a[:, _HK:2 * _HK]
    v_ref[...] = a[:, 2 * _HK:2 * _HK + _HV]
    g_ref[...] = a[:, 2 * _HK + _HV:_ODD_MAIN]
    z = _dot(a[:, _ODD_MAIN:].astype(BF16), wgk_ref[...]) + bgk_ref[...]
    la_ref[...] = (jnp.minimum(z, 0.0) - jnp.log(1.0 + jnp.exp(-jnp.abs(z)))) * (1.0 / GATE_TAU)


def _gla_kernel(has_init, nchunk, *refs):
    if has_init:
        (qf, kf, vf, laf, qb, kb, vb, lab, s0f, s0b, _, _, of_ref, ob_ref, st) = refs
    else:
        (qf, kf, vf, laf, qb, kb, vb, lab, of_ref, ob_ref, sf_ref, sb_ref, st) = refs
    j = pl.program_id(1)

    @pl.when(j == 0)
    def _():
        if has_init:
            st[0] = s0f[0]
            st[1] = s0b[0]
        else:
            st[...] = jnp.zeros_like(st)

    row = lax.broadcasted_iota(jnp.int32, (GLA_CHUNK, GLA_CHUNK), 0)
    col = lax.broadcasted_iota(jnp.int32, (GLA_CHUNK, GLA_CHUNK), 1)
    for d, (q_r, k_r, v_r, la_r, o_r) in enumerate(((qf, kf, vf, laf, of_ref), (qb, kb, vb, lab, ob_ref))):
        keep = (col <= row) if d == 0 else (col >= row)
        tri = jnp.where(keep, 1.0, 0.0)
        for h in range(H_C):
            kc = slice(h * DK_C, (h + 1) * DK_C)
            vc = slice(h * DV_C, (h + 1) * DV_C)
            g = la_r[:, kc]
            c = jnp.dot(tri, g, precision=lax.Precision.HIGHEST, preferred_element_type=F32)
            tot = jnp.sum(g, axis=0, keepdims=True)
            q, k, v = q_r[:, kc], k_r[:, kc], v_r[:, vc].astype(BF16)
            qe = (q * jnp.exp(c)).astype(BF16)
            kd = (k * jnp.exp(-c)).astype(BF16)
            att = jnp.where(keep, _dot_nt(qe, kd), 0.0)
            s_t = st[d, h]
            o_r[:, vc] = _dot_nt(qe, s_t.astype(BF16)) + _dot(att.astype(BF16), v)
            k2 = (k * jnp.exp(tot - c)).astype(BF16)
            st[d, h] = s_t * jnp.exp(tot) + _dot_tn(v, k2)

    if not has_init:
        @pl.when(j == nchunk - 1)
        def _():
            sf_ref[0] = st[0]
            sb_ref[0] = st[1]


def _gla_call(has_init, nseq, seqlen, row0, q, k, v, la, prev_f=None, prev_b=None, s0f=None, s0b=None):
    nchunk = seqlen // GLA_CHUNK
    blk0 = row0 // GLA_CHUNK
    fwd = lambda b, j: (blk0 + b * nchunk + j, 0)
    bwd = lambda b, j: (blk0 + b * nchunk + nchunk - 1 - j, 0)
    bwd_la = lambda b, j: (blk0 + b * nchunk + nchunk - 1 - j, 1)
    state_spec = pl.BlockSpec((1, H_C, DV_C, DK_C), lambda b, j: (b, 0, 0, 0))
    in_specs = [
        pl.BlockSpec((GLA_CHUNK, _HK), fwd), pl.BlockSpec((GLA_CHUNK, _HK), fwd),
        pl.BlockSpec((GLA_CHUNK, _HV), fwd), pl.BlockSpec((GLA_CHUNK, _HK), fwd),
        pl.BlockSpec((GLA_CHUNK, _HK), bwd), pl.BlockSpec((GLA_CHUNK, _HK), bwd),
        pl.BlockSpec((GLA_CHUNK, _HV), bwd), pl.BlockSpec((GLA_CHUNK, _HK), bwd_la),
    ]
    args = [q, k, v, la, q, k, v, la]
    aliases = {}
    if has_init:
        in_specs += [state_spec, state_spec] + [pl.BlockSpec(memory_space=pl.ANY)] * 2
        args += [s0f, s0b, prev_f, prev_b]
        aliases = {len(args) - 2: 0, len(args) - 1: 1}
    out_specs = [pl.BlockSpec((GLA_CHUNK, _HV), fwd), pl.BlockSpec((GLA_CHUNK, _HV), bwd)]
    out_shape = [jax.ShapeDtypeStruct((T, _HV), F32)] * 2
    if not has_init:
        out_specs += [state_spec, state_spec]
        out_shape += [jax.ShapeDtypeStruct((nseq, H_C, DV_C, DK_C), F32)] * 2
    return pl.pallas_call(
        functools.partial(_gla_kernel, has_init, nchunk),
        grid=(nseq, nchunk),
        in_specs=in_specs,
        out_specs=out_specs,
        out_shape=out_shape,
        scratch_shapes=[pltpu.VMEM((2, H_C, DV_C, DK_C), F32)],
        input_output_aliases=aliases,
        compiler_params=_params(("arbitrary", "arbitrary")),
        name="gla_sample" if has_init else "gla_prompt",
    )(*args)


def _rope_tables():
    half = ROPE // 2
    inv_freq = np.power(np.float32(ROPE_THETA), -np.arange(0, half, 2, dtype=np.float32) / np.float32(half))
    n = np.arange(SAMPLE_LEN)
    row = (n // GRID_W).astype(np.float32)
    col = (n % GRID_W).astype(np.float32)
    ang_r = (row[:, None] * inv_freq[None, :]).astype(np.float32)
    ang_c = (col[:, None] * inv_freq[None, :]).astype(np.float32)
    nf = half // 2
    c = np.ones((TM + SAMPLE_LEN, LANES), np.float32)
    s1 = np.zeros((TM + SAMPLE_LEN, LANES), np.float32)
    s2 = np.zeros((TM + SAMPLE_LEN, LANES), np.float32)
    for base, ang in ((NOPE, ang_r), (NOPE + half, ang_c)):
        c[TM:, base:base + nf] = np.cos(ang)
        c[TM:, base + nf:base + half] = np.cos(ang)
        s1[TM:, base:base + nf] = -np.sin(ang)
        s2[TM:, base + nf:base + half] = np.sin(ang)
    return jnp.asarray(c), jnp.asarray(s1), jnp.asarray(s2)


def _pad_heads(w, nheads, width, lo=0):
    k = w.shape[0]
    w = w.reshape(k, nheads, width)
    w = jnp.pad(w, ((0, 0), (0, 0), (lo, HP - lo - width)))
    return w.reshape(k, nheads * HP)


def _row128(v, lo=0):
    return jnp.pad(v, (lo, LANES - lo - v.shape[0])).reshape(1, LANES)


def _even_layer(x, mod, nmix, w_in, q_a_norm, w_uq, q_norm, kv_a_norm, w_ukv, k_norm, v_norm, w_s, b_s, w_out,
                cache_ckv, cache_kpe):
    s = np.cumsum([Q_LORA, KV_LORA, ROPE, W_B])
    w_q, w_ckv, w_kpe, w_u, w_v = (w_in[:, :s[0]], w_in[:, s[0]:s[1]], w_in[:, s[1]:s[2]], w_in[:, s[2]:s[3]],
                                   w_in[:, s[3]:])
    w_kpe = jnp.pad(w_kpe, ((0, 0), (NOPE, LANES - QK_DIM)))
    win = jnp.concatenate([w_q, w_ckv, w_u, w_v, w_kpe], axis=1).astype(BF16)
    wuq = _pad_heads(w_uq, H_A, QK_DIM).astype(BF16)
    ukv = w_ukv.reshape(KV_LORA, H_A, NOPE + V_A)
    wuk = _pad_heads(ukv[:, :, :NOPE].reshape(KV_LORA, H_A * NOPE), H_A, NOPE)
    wuv = _pad_heads(ukv[:, :, NOPE:].reshape(KV_LORA, H_A * V_A), H_A, V_A)
    wukv = jnp.concatenate([wuk, wuv], axis=1).astype(BF16)
    qgain = _row128(q_norm * (QK_DIM ** -0.5))
    kgain = _row128(k_norm)
    bias = b_s.reshape(G_B // 2, 2, CHUNK_B)
    bias = jnp.concatenate([jnp.broadcast_to(bias[:, 0, :, None], (G_B // 2, CHUNK_B, C_B)),
                            jnp.broadcast_to(bias[:, 1, :, None], (G_B // 2, CHUNK_B, C_B))], axis=-1)
    rc, rs1, rs2 = _rope_tables()
    rope_spec = pl.BlockSpec((TM, LANES), lambda i: (_rope_blk(i), 0))
    width = H_A * HP
    q, k, v, ob, ckv, kpe = pl.pallas_call(
        _even_in_kernel,
        grid=(NBLK,),
        in_specs=[
            _TOK(D), _MOD, _full((1, D)), _full((D, _WIN_N)), _full((1, Q_LORA)), _full((Q_LORA, width)),
            _full((1, LANES)), _full((1, KV_LORA)), _full((KV_LORA, 2 * width)), _full((1, LANES)),
            _full((1, W_B)), _full((G_B // 2, 2 * CHUNK_B, CHUNK_B)), _full((G_B // 2, CHUNK_B, LANES)),
            rope_spec, rope_spec, rope_spec,
        ],
        out_specs=[_TOK(width), _TOK(width), _TOK(width), _TOK(W_B), _TOK(KV_LORA), _TOK(ROPE)],
        out_shape=[
            jax.ShapeDtypeStruct((T, width), BF16), jax.ShapeDtypeStruct((T, width), BF16),
            jax.ShapeDtypeStruct((T, width), BF16), jax.ShapeDtypeStruct((T, W_B), BF16),
            jax.ShapeDtypeStruct((T, KV_LORA), F32), jax.ShapeDtypeStruct((T, ROPE), F32),
        ],
        compiler_params=_params(("arbitrary",)),
        name="even_in",
    )(x, mod, nmix.reshape(1, D), win, q_a_norm.reshape(1, Q_LORA), wuq, qgain, kv_a_norm.reshape(1, KV_LORA),
      wukv, kgain, v_norm.reshape(1, W_B), w_s.astype(BF16).reshape(G_B // 2, 2 * CHUNK_B, CHUNK_B), bias, rc, rs1,
      rs2)

    n_ctx = N_SAMPLE_SEQ * PAST_LEN
    kpe_ctx = jnp.pad(cache_kpe.reshape(n_ctx, ROPE), ((0, 0), (NOPE, LANES - QK_DIM)))
    k_ctx, v_ctx = pl.pallas_call(
        _ctx_kv_kernel,
        grid=(n_ctx // TM,),
        in_specs=[_TOK(KV_LORA), _TOK(LANES), _full((KV_LORA, 2 * width)), _full((1, LANES))],
        out_specs=[_TOK(width), _TOK(width)],
        out_shape=[jax.ShapeDtypeStruct((n_ctx, width), BF16)] * 2,
        compiler_params=_params(("arbitrary",)),
        name="ctx_kv",
    )(cache_ckv.reshape(n_ctx, KV_LORA), kpe_ctx, wukv, kgain)

    oa = _attention(q, k, v, k_ctx, v_ctx)
    woa = jnp.pad(w_out[:H_A * V_A].reshape(H_A, V_A, D), ((0, 0), (0, HP - V_A), (0, 0))).reshape(width, D)
    return oa, ob, woa.astype(BF16), w_out[H_A * V_A:].astype(BF16), ckv, kpe


def kernel(x_prompt, x_sample, cache_mla_ckv, cache_mla_kpe, state_gla_fwd, state_gla_bwd, c, c_ctx, ada_w, ada_b,
           norm_mix, norm_ffn, even_w_in, mla_q_a_norm, mla_w_uq, mla_q_norm, mla_kv_a_norm, mla_w_ukv, mla_k_norm,
           cmlp_v_norm, cmlp_w_s, cmlp_b_s, even_w_out, odd_w_in, gla_w_gk_fwd, gla_b_gk_fwd, gla_w_gk_bwd,
           gla_b_gk_bwd, gla_o_norm, odd_w_out, moe_w_router, moe_b_router, moe_w_gu, moe_b_gu, moe_w_down,
           moe_b_down):
    x0 = jnp.concatenate([x_prompt.reshape(N_PROMPT, D), x_sample.reshape(N_SAMPLE, D)], axis=0)
    cond8 = jnp.concatenate([c_ctx[None, :], c, jnp.zeros((SUBLANES - 1 - N_SAMPLE_SEQ, D), F32)], axis=0)
    mods = _adaln(cond8, ada_w, ada_b)
    wr = jnp.pad(moe_w_router, ((0, 0), (0, 0), (0, LANES - N_EXPERTS)))
    wr_hi = wr.astype(BF16)
    wr = jnp.concatenate([wr_hi, (wr - wr_hi.astype(F32)).astype(BF16)], axis=-1)
    br = jnp.pad(moe_b_router, ((0, 0), (0, LANES - N_EXPERTS))).reshape(2, 1, LANES)

    oa, ob, woa, wob, ckv, kpe = _even_layer(
        x0, mods[0], norm_mix[0], even_w_in[0], mla_q_a_norm[0], mla_w_uq[0], mla_q_norm[0], mla_kv_a_norm[0],
        mla_w_ukv[0], mla_k_norm[0], cmlp_v_norm[0], cmlp_w_s[0], cmlp_b_s[0], even_w_out[0],
        cache_mla_ckv[:, 0], cache_mla_kpe[:, 0])
    width = H_A * HP
    x1, h2, meta, cnt = pl.pallas_call(
        _even_out_kernel,
        grid=(NBLK,),
        in_specs=[_TOK(width), _TOK(W_B), _TOK(D), _MOD, _full((width, D)), _full((W_B, D)), _full((1, D)),
                  _full((D, 2 * LANES)), _full((1, LANES))],
        out_specs=_PROLOGUE_OUT_SPECS,
        out_shape=_PROLOGUE_OUT_SHAPE,
        compiler_params=_params(("arbitrary",)),
        name="even_out",
    )(oa, ob, x0, mods[0], woa, wob, norm_ffn[0].reshape(1, D), wr[0], br[0])
    x2 = _moe(0, x1, h2, meta, cnt, mods[0], moe_w_gu, moe_b_gu, moe_w_down, moe_b_down)

    w_in = odd_w_in[0]
    win = jnp.concatenate([w_in, jnp.zeros((D, LANES - 2 * GATE_RANK), F32)], axis=1).astype(BF16)
    wgk = jnp.zeros((LANES, 2 * _HK), F32)
    wgk = wgk.at[:GATE_RANK, :_HK].set(gla_w_gk_fwd[0]).at[GATE_RANK:2 * GATE_RANK, _HK:].set(gla_w_gk_bwd[0])
    bgk = jnp.concatenate([gla_b_gk_fwd[0], gla_b_gk_bwd[0]]).reshape(1, 2 * _HK)
    q, k, v, g, la = pl.pallas_call(
        _odd_in_kernel,
        grid=(NBLK,),
        in_specs=[_TOK(D), _MOD, _full((1, D)), _full((D, _ODD_MAIN + LANES)), _full((LANES, 2 * _HK)),
                  _full((1, 2 * _HK))],
        out_specs=[_TOK(_HK), _TOK(_HK), _TOK(_HV), _TOK(_HV), _TOK(2 * _HK)],
        out_shape=[jax.ShapeDtypeStruct((T, _HK), F32), jax.ShapeDtypeStruct((T, _HK), F32),
                   jax.ShapeDtypeStruct((T, _HV), F32), jax.ShapeDtypeStruct((T, _HV), F32),
                   jax.ShapeDtypeStruct((T, 2 * _HK), F32)],
        compiler_params=_params(("arbitrary",)),
        name="odd_in",
    )(x2, mods[1], norm_mix[1].reshape(1, D), win, wgk.astype(BF16), bgk)

    of, obk, st_f, st_b = _gla_call(False, N_PROMPT_SEQ, PROMPT_LEN, 0, q, k, v, la)
    s0f = state_gla_fwd[:, 0].transpose(0, 1, 3, 2)
    s0b = state_gla_bwd[:, 0].transpose(0, 1, 3, 2)
    of, obk = _gla_call(True, N_SAMPLE_SEQ, SAMPLE_LEN, N_PROMPT, q, k, v, la, of, obk, s0f, s0b)

    x3, h2, meta, cnt = pl.pallas_call(
        _odd_out_kernel,
        grid=(NBLK,),
        in_specs=[_TOK(_HV), _TOK(_HV), _TOK(_HV), _TOK(D), _MOD, _full((1, DV_C)), _full((_HV, D)),
                  _full((1, D)), _full((D, 2 * LANES)), _full((1, LANES))],
        out_specs=_PROLOGUE_OUT_SPECS,
        out_shape=_PROLOGUE_OUT_SHAPE,
        compiler_params=_params(("arbitrary",)),
        name="odd_out",
    )(of, obk, g, x2, mods[1], gla_o_norm[0].reshape(1, DV_C), odd_w_out[0].astype(BF16),
      norm_ffn[1].reshape(1, D), wr[1], br[1])
    x4 = _moe(1, x3, h2, meta, cnt, mods[1], moe_w_gu, moe_b_gu, moe_w_down, moe_b_down)

    y_prompt = x4[:N_PROMPT].reshape(N_PROMPT_SEQ, PROMPT_LEN, D)
    y_sample = x4[N_PROMPT:].reshape(N_SAMPLE_SEQ, SAMPLE_LEN, D)
    new_ckv = ckv[:N_PROMPT].reshape(N_PROMPT_SEQ, 1, PROMPT_LEN, KV_LORA)
    new_kpe = kpe[:N_PROMPT].reshape(N_PROMPT_SEQ, 1, PROMPT_LEN, ROPE)
    new_fwd = st_f.transpose(0, 1, 3, 2)[:, None]
    new_bwd = st_b.transpose(0, 1, 3, 2)[:, None]
    return (y_prompt, y_sample, new_ckv, new_kpe, new_fwd, new_bwd)
```

```python
import functools

import numpy as np
import jax
import jax.numpy as jnp
from jax import lax
from jax.experimental import pallas as pl
from jax.experimental.pallas import tpu as pltpu

F32 = jnp.float32
BF16 = jnp.bfloat16

D = 1024
N_PROMPT_SEQ, PROMPT_LEN = 16, 256
N_SAMPLE_SEQ, SAMPLE_LEN = 4, 2048
PAST_LEN = 512
N_PROMPT = N_PROMPT_SEQ * PROMPT_LEN
N_SAMPLE = N_SAMPLE_SEQ * SAMPLE_LEN
T = N_PROMPT + N_SAMPLE
EPS = 1e-6
GRID_W = 64
H_A, Q_LORA, KV_LORA, NOPE, ROPE, V_A = 8, 512, 256, 64, 32, 64
QK_DIM = NOPE + ROPE
G_B, C_B, W_B, CHUNK_B = 8, 64, 512, 128
H_C, DK_C, DV_C, GATE_RANK, GATE_TAU, GLA_CHUNK = 4, 128, 256, 16, 16.0, 64
N_EXPERTS, TOP_K, D_FF = 32, 4, 1024
SWIGLU_LIMIT, SWIGLU_ALPHA = 7.0, 1.702
ROPE_THETA = 10000.0

LANES = 128
SUBLANES = 8
VMEM_LIMIT = 56 * 1024 * 1024

TM = 256
NBLK = T // TM
SAMPLE_BLKS = N_SAMPLE // TM
BLKS_PER_SAMPLE_SEQ = SAMPLE_LEN // TM
GLA_NS = 4
ETILE = 512
CH = SUBLANES
_MAX_LOCAL = TM * TOP_K + N_EXPERTS * (CH - 1)
ROWS_L = -(-_MAX_LOCAL // LANES) * LANES
N_ETILES = -(-NBLK * _MAX_LOCAL // ETILE) + N_EXPERTS
P_ROWS = N_ETILES * ETILE
HP = LANES


def _cond_row(i):
    return jnp.where(i < SAMPLE_BLKS, 1 + i // BLKS_PER_SAMPLE_SEQ, 0)


def _rope_blk(i):
    return jnp.where(i < SAMPLE_BLKS, 1 + i % BLKS_PER_SAMPLE_SEQ, 0)


def _rms(x):
    return x * lax.rsqrt(jnp.mean(x * x, axis=-1, keepdims=True) + EPS)


def _gelu(x):
    return 0.5 * x * (1.0 + jnp.tanh(0.7978845608028654 * (x + 0.044715 * (x * x * x))))


def _silu(x):
    return x * jax.nn.sigmoid(x)


def _dot(a, b):
    return jnp.dot(a, b, preferred_element_type=F32)


def _dot_nt(a, b):
    return lax.dot_general(a, b, (((1,), (1,)), ((), ())), preferred_element_type=F32)


def _dot_tn(a, b):
    return lax.dot_general(a, b, (((0,), (0,)), ((), ())), preferred_element_type=F32)


def _params(sem, vmem=VMEM_LIMIT):
    return pltpu.CompilerParams(dimension_semantics=sem, vmem_limit_bytes=vmem)


def _full(shape):
    nd = len(shape)
    return pl.BlockSpec(shape, lambda *_: (0,) * nd)


ADA_TN = 1536


def _adaln_kernel(c_ref, w_ref, b_ref, o_ref):
    s = _silu(c_ref[...]).astype(BF16)
    o_ref[0] = _dot(s, w_ref[0].astype(BF16)) + b_ref[0]


def _adaln(cond8, ada_w, ada_b):
    depth = ada_w.shape[0]
    n = ada_w.shape[2]
    out = pl.pallas_call(
        _adaln_kernel,
        grid=(depth, n // ADA_TN),
        in_specs=[
            pl.BlockSpec((SUBLANES, D), lambda l, j: (0, 0)),
            pl.BlockSpec((1, D, ADA_TN), lambda l, j: (l, 0, j)),
            pl.BlockSpec((1, 1, ADA_TN), lambda l, j: (l, 0, j)),
        ],
        out_specs=pl.BlockSpec((1, SUBLANES, ADA_TN), lambda l, j: (l, 0, j)),
        out_shape=jax.ShapeDtypeStruct((depth, SUBLANES, n), F32),
        compiler_params=_params(("arbitrary", "arbitrary")),
        name="adaln",
    )(cond8, ada_w, ada_b.reshape(depth, 1, n))
    return out.reshape(depth, SUBLANES, 6, D)


_QC0, _CKV0, _U0, _V0, _KPE0, _WIN_N = 0, 512, 768, 1280, 1792, 1920


def _rope(y, c, s1, s2):
    return y * c + pltpu.roll(y, LANES - 8, 1) * s1 + pltpu.roll(y, 8, 1) * s2


def _k_heads(k_raw, kpe128, kp_rot, kgain, k_ref):
    sskpe = jnp.sum(kpe128 * kpe128, axis=-1, keepdims=True)
    for h in range(H_A):
        kb = k_raw[:, h * HP:(h + 1) * HP]
        r = lax.rsqrt((jnp.sum(kb * kb, axis=-1, keepdims=True) + sskpe) * (1.0 / QK_DIM) + EPS)
        k_ref[:, h * HP:(h + 1) * HP] = ((kb * kgain + kp_rot) * r).astype(BF16)


def _even_in_kernel(x_ref, mod_ref, nmix_ref, win_ref, qan_ref, wuq_ref, qgain_ref, kvan_ref, wukv_ref,
                    kgain_ref, vnorm_ref, ws_ref, bs_ref, rc_ref, rs1_ref, rs2_ref,
                    q_ref, k_ref, v_ref, ob_ref, ckv_ref, kpe_ref):
    m = mod_ref[0]
    h = _rms(x_ref[...]) * nmix_ref[...] * (1.0 + m[1:2]) + m[0:1]
    a = _dot(h.astype(BF16), win_ref[...])
    qc = a[:, _QC0:_CKV0]
    ckv = a[:, _CKV0:_U0]
    u = a[:, _U0:_V0]
    vv = a[:, _V0:_KPE0]
    kpe128 = a[:, _KPE0:_WIN_N]

    ckv_n = _rms(ckv) * kvan_ref[...]
    ckv_ref[...] = ckv_n
    kpe_ref[...] = kpe128[:, NOPE:QK_DIM]

    rc, rs1, rs2 = rc_ref[...], rs1_ref[...], rs2_ref[...]
    qn = (_rms(qc) * qan_ref[...]).astype(BF16)
    qr = _dot(qn, wuq_ref[...])
    qgain = qgain_ref[...]
    for hh in range(H_A):
        blk = qr[:, hh * HP:(hh + 1) * HP]
        r = lax.rsqrt(jnp.sum(blk * blk, axis=-1, keepdims=True) * (1.0 / QK_DIM) + EPS)
        q_ref[:, hh * HP:(hh + 1) * HP] = _rope(blk * r * qgain, rc, rs1, rs2).astype(BF16)

    kv = _dot(ckv_n.astype(BF16), wukv_ref[...])
    v_ref[...] = kv[:, H_A * HP:].astype(BF16)
    kgain = kgain_ref[...]
    kp_rot = _rope(kpe128 * kgain, rc, rs1, rs2)
    _k_heads(kv[:, :H_A * HP], kpe128, kp_rot, kgain, k_ref)

    ug = _gelu(u)
    vn = (_rms(_gelu(vv)) * vnorm_ref[...]).astype(BF16)
    low = lax.broadcasted_iota(jnp.int32, (CHUNK_B, LANES), 1) < C_B
    for c in range(TM // CHUNK_B):
        rows = slice(c * CHUNK_B, (c + 1) * CHUNK_B)
        for p in range(G_B // 2):
            cols = slice(p * LANES, (p + 1) * LANES)
            blk = vn[rows, cols]
            both = _dot(ws_ref[p], blk)
            mixed = jnp.where(low, both[:CHUNK_B], both[CHUNK_B:]) + bs_ref[p]
            ob_ref[rows, cols] = (ug[rows, cols] * mixed).astype(BF16)


def _ctx_kv_kernel(ckv_ref, kpe_ref, wukv_ref, kgain_ref, k_ref, v_ref):
    kv = _dot(ckv_ref[...].astype(BF16), wukv_ref[...])
    v_ref[...] = kv[:, H_A * HP:].astype(BF16)
    kgain = kgain_ref[...]
    kpe128 = kpe_ref[...]
    _k_heads(kv[:, :H_A * HP], kpe128, kpe128 * kgain, kgain, k_ref)


def _attn_self_kernel(q_ref, k_ref, v_ref, o_ref):
    for h in range(H_A):
        cols = slice(h * HP, (h + 1) * HP)
        s = _dot_nt(q_ref[:, cols], k_ref[:, cols])
        p = jnp.exp(s - jnp.max(s, axis=-1, keepdims=True))
        inv = 1.0 / jnp.sum(p, axis=-1, keepdims=True)
        o_ref[:, cols] = (_dot(p.astype(BF16), v_ref[:, cols]) * inv).astype(BF16)


def _attn_ctx_kernel(q_ref, k_ref, v_ref, kc_ref, vc_ref, prev_ref, o_ref):
    del prev_ref
    for h in range(H_A):
        cols = slice(h * HP, (h + 1) * HP)
        q = q_ref[:, cols]
        s1 = _dot_nt(q, k_ref[:, cols])
        s2 = _dot_nt(q, kc_ref[:, cols])
        mx = jnp.maximum(jnp.max(s1, axis=-1, keepdims=True), jnp.max(s2, axis=-1, keepdims=True))
        p1 = jnp.exp(s1 - mx)
        p2 = jnp.exp(s2 - mx)
        inv = 1.0 / (jnp.sum(p1, axis=-1, keepdims=True) + jnp.sum(p2, axis=-1, keepdims=True))
        o = _dot(p1.astype(BF16), v_ref[:, cols]) + _dot(p2.astype(BF16), vc_ref[:, cols])
        o_ref[:, cols] = (o * inv).astype(BF16)


def _attention(q, k, v, k_ctx, v_ctx):
    width = H_A * HP
    first = N_SAMPLE // PROMPT_LEN
    o = pl.pallas_call(
        _attn_self_kernel,
        grid=(N_PROMPT_SEQ,),
        in_specs=[pl.BlockSpec((PROMPT_LEN, width), lambda i: (first + i, 0))] * 3,
        out_specs=pl.BlockSpec((PROMPT_LEN, width), lambda i: (first + i, 0)),
        out_shape=jax.ShapeDtypeStruct((T, width), BF16),
        compiler_params=_params(("arbitrary",)),
        name="attn_prompt",
    )(q, k, v)
    qblk = lambda b, j: (b * BLKS_PER_SAMPLE_SEQ + j, 0)
    return pl.pallas_call(
        _attn_ctx_kernel,
        grid=(N_SAMPLE_SEQ, BLKS_PER_SAMPLE_SEQ),
        in_specs=[
            pl.BlockSpec((TM, width), qblk),
            pl.BlockSpec((SAMPLE_LEN, width), lambda b, j: (b, 0)),
            pl.BlockSpec((SAMPLE_LEN, width), lambda b, j: (b, 0)),
            pl.BlockSpec((PAST_LEN, width), lambda b, j: (b, 0)),
            pl.BlockSpec((PAST_LEN, width), lambda b, j: (b, 0)),
            pl.BlockSpec(memory_space=pl.ANY),
        ],
        out_specs=pl.BlockSpec((TM, width), qblk),
        out_shape=jax.ShapeDtypeStruct((T, width), BF16),
        input_output_aliases={5: 0},
        compiler_params=_params(("arbitrary", "arbitrary")),
        name="attn_sample",
    )(q, k, v, k_ctx, v_ctx, o)


_META_IDX, _META_RANK, _META_W = 0, TOP_K, 2 * TOP_K


def _moe_prologue(x1, m, nffn_ref, wr_ref, br_ref, x1_ref, h2_ref, meta_ref, cnt_ref):
    x1_ref[...] = x1
    h2 = _rms(x1) * nffn_ref[...] * (1.0 + m[4:5]) + m[3:4]
    h2_ref[...] = h2.astype(BF16)
    lane = lax.broadcasted_iota(jnp.int32, (TM, LANES), 1)
    lanef = lane.astype(F32)
    h_hi = h2.astype(BF16)
    h_lo = (h2 - h_hi.astype(F32)).astype(BF16)
    r = _dot(h_hi, wr_ref[...])
    logits = r[:, :LANES] + r[:, LANES:] + _dot(h_lo, wr_ref[:, :LANES]) + br_ref[...]
    work = jnp.where(lane < N_EXPERTS, logits, -jnp.inf)
    hots, vals = [], []
    for _ in range(TOP_K):
        mx = jnp.max(work, axis=-1, keepdims=True)
        idx = jnp.min(jnp.where(work == mx, lanef, float(LANES)), axis=-1, keepdims=True)
        hot = lanef == idx
        work = jnp.where(hot, -jnp.inf, work)
        hots.append((hot, idx))
        vals.append(mx)
    es = [jnp.exp(v - vals[0]) for v in vals]
    inv = 1.0 / (es[0] + es[1] + es[2] + es[3])
    sel = jnp.zeros((TM, LANES), F32)
    for hot, _ in hots:
        sel = jnp.where(hot, 1.0, sel)
    row = lax.broadcasted_iota(jnp.int32, (TM, TM), 0)
    col = lax.broadcasted_iota(jnp.int32, (TM, TM), 1)
    strict = jnp.where(row > col, 1.0, 0.0).astype(BF16)
    before = _dot(strict, sel.astype(BF16))
    meta = jnp.zeros((TM, LANES), F32)
    for kk, (hot, idx) in enumerate(hots):
        rank = jnp.sum(jnp.where(hot, before, 0.0), axis=-1, keepdims=True)
        meta = jnp.where(lane == _META_IDX + kk, idx, meta)
        meta = jnp.where(lane == _META_RANK + kk, rank, meta)
        meta = jnp.where(lane == _META_W + kk, es[kk] * inv, meta)
    meta_ref[...] = meta
    cnt_ref[0] = jnp.broadcast_to(jnp.sum(sel, axis=0, keepdims=True), (SUBLANES, LANES))


def _even_out_kernel(oa_ref, ob_ref, x_ref, mod_ref, woa_ref, wob_ref, nffn_ref, wr_ref, br_ref,
                     x1_ref, h2_ref, meta_ref, cnt_ref):
    m = mod_ref[0]
    out = _dot(oa_ref[...], woa_ref[...]) + _dot(ob_ref[...], wob_ref[...])
    x1 = x_ref[...] + m[2:3] * out
    _moe_prologue(x1, m, nffn_ref, wr_ref, br_ref, x1_ref, h2_ref, meta_ref, cnt_ref)


def _odd_out_kernel(of_ref, ob_ref, g_ref, x_ref, mod_ref, onorm_ref, wo_ref, nffn_ref, wr_ref, br_ref,
                    x1_ref, h2_ref, meta_ref, cnt_ref):
    m = mod_ref[0]
    onorm = onorm_ref[...]
    parts = []
    for h in range(H_C):
        cols = slice(h * DV_C, (h + 1) * DV_C)
        o = of_ref[:, cols] + ob_ref[:, cols]
        parts.append((_rms(o) * onorm * _silu(g_ref[:, cols])).astype(BF16))
    out = _dot(jnp.concatenate(parts, axis=-1), wo_ref[...])
    x1 = x_ref[...] + m[2:3] * out
    _moe_prologue(x1, m, nffn_ref, wr_ref, br_ref, x1_ref, h2_ref, meta_ref, cnt_ref)


_TOK = lambda w: pl.BlockSpec((TM, w), lambda i, *_: (i, 0))
_MOD = pl.BlockSpec((1, 6, D), lambda i, *_: (_cond_row(i), 0, 0))
_TILE_ROW = pl.BlockSpec((1, SUBLANES, LANES), lambda i, *_: (i, 0, 0))

_PROLOGUE_OUT_SPECS = [_TOK(D), _TOK(D), _TOK(LANES), _TILE_ROW]
_PROLOGUE_OUT_SHAPE = [
    jax.ShapeDtypeStruct((T, D), F32),
    jax.ShapeDtypeStruct((T, D), BF16),
    jax.ShapeDtypeStruct((T, LANES), F32),
    jax.ShapeDtypeStruct((NBLK, SUBLANES, LANES), F32),
]


def _local_positions(meta, seg_row):
    lanef = lax.broadcasted_iota(jnp.int32, (TM, LANES), 1).astype(F32)
    pos = []
    for k in range(TOP_K):
        hot = lanef == meta[:, _META_IDX + k:_META_IDX + k + 1]
        start = jnp.sum(jnp.where(hot, seg_row, 0.0), axis=-1, keepdims=True)
        pos.append(start + meta[:, _META_RANK + k:_META_RANK + k + 1])
    return pos


def _segment_copies(t, base_ref, seg_ref, nch_ref, make):
    def per_expert(e, carry):
        q = t * N_EXPERTS + e
        local0, global0 = seg_ref[q], base_ref[q]

        def one(j, c):
            make(pl.multiple_of(local0 + j * CH, CH), pl.multiple_of(global0 + j * CH, CH)).start()
            return c

        return lax.fori_loop(0, nch_ref[q], one, carry)

    lax.fori_loop(0, N_EXPERTS, per_expert, 0)


def _drain(count, chunk_copy):
    def one(j, c):
        chunk_copy.wait()
        return c

    lax.fori_loop(0, count, one, 0)


def _dispatch_kernel(base_ref, seg_ref, nch_ref, ntile_ref, tail0_ref, tailn_ref, h_ref, meta_ref, segrow_ref,
                     xs_hbm, buf, zbuf, sem):
    i = pl.program_id(0)
    slot = i % 2
    chunk = lambda s: pltpu.make_async_copy(buf.at[s, pl.ds(0, CH)], xs_hbm.at[pl.ds(0, CH)], sem.at[s])

    @pl.when(i == 0)
    def _():
        zbuf[...] = jnp.zeros_like(zbuf)

        def per_expert(e, carry):
            def one(j, c):
                pltpu.make_async_copy(zbuf, xs_hbm.at[pl.ds(pl.multiple_of(tail0_ref[e] + j * CH, CH), CH)],
                                      sem.at[2]).start()
                return c

            lax.fori_loop(0, tailn_ref[e], one, 0)
            return carry + tailn_ref[e]

        total = lax.fori_loop(0, N_EXPERTS, per_expert, 0)
        _drain(total, pltpu.make_async_copy(zbuf, xs_hbm.at[pl.ds(0, CH)], sem.at[2]))

    pos = _local_positions(meta_ref[...], segrow_ref[0])
    riota = lax.broadcasted_iota(jnp.int32, (TM, ROWS_L), 1).astype(F32)
    pt = jnp.zeros((TM, ROWS_L), F32)
    for p in pos:
        pt = jnp.where(riota == p, 1.0, pt)
    buf[slot] = _dot_tn(pt.astype(BF16), h_ref[...])

    make = lambda s, d: pltpu.make_async_copy(buf.at[slot, pl.ds(s, CH)], xs_hbm.at[pl.ds(d, CH)], sem.at[slot])
    _segment_copies(i, base_ref, seg_ref, nch_ref, make)

    @pl.when(i > 0)
    def _():
        _drain(ntile_ref[i - 1], chunk(1 - slot))

    @pl.when(i == NBLK - 1)
    def _():
        _drain(ntile_ref[i], chunk(slot))


def _dispatch(plan, h2, meta):
    grid_spec = pltpu.PrefetchScalarGridSpec(
        num_scalar_prefetch=6,
        grid=(NBLK,),
        in_specs=[_TOK(D), _TOK(LANES), pl.BlockSpec((1, 1, LANES), lambda i, *_: (i, 0, 0))],
        out_specs=pl.BlockSpec(memory_space=pl.ANY),
        scratch_shapes=[pltpu.VMEM((2, ROWS_L, D), F32), pltpu.VMEM((CH, D), F32),
                        pltpu.SemaphoreType.DMA((3,))],
    )
    return pl.pallas_call(
        _dispatch_kernel,
        grid_spec=grid_spec,
        out_shape=jax.ShapeDtypeStruct((P_ROWS, D), F32),
        compiler_params=_params(("arbitrary",)),
        name="moe_dispatch",
    )(plan["base"], plan["seg"], plan["nch"], plan["ntile"], plan["tail0"], plan["tailn"], h2, meta, plan["segrow"])


def _expert_kernel(te_ref, tfirst_ref, tvalid_ref, xblk_ref, x_ref, wgu_ref, bgu_ref, wd_ref, bd_ref, y_ref,
                   wgu_bf, wd_bf):
    del xblk_ref
    i = pl.program_id(0)

    @pl.when(tfirst_ref[i] == 1)
    def _():
        wgu_bf[...] = wgu_ref[...].astype(BF16)
        wd_bf[...] = wd_ref[...].astype(BF16)

    @pl.when(tvalid_ref[i] == 1)
    def _():
        a = _dot(x_ref[...].astype(BF16), wgu_bf[...]) + bgu_ref[...]
        glu = jnp.minimum(a[:, :D_FF], SWIGLU_LIMIT)
        lin = jnp.clip(a[:, D_FF:], -SWIGLU_LIMIT, SWIGLU_LIMIT)
        act = (glu * jax.nn.sigmoid(SWIGLU_ALPHA * glu)) * (lin + 1.0)
        y_ref[...] = _dot(act.astype(BF16), wd_bf[...]) + bd_ref[...]

    @pl.when(tvalid_ref[i] == 0)
    def _():
        y_ref[...] = jnp.zeros_like(y_ref)


def _experts(layer, plan, xs, w_gu, b_gu, w_down, b_down):
    depth = w_gu.shape[0]
    e_of = lambda i, te, *_: (layer, te[i], 0, 0)
    grid_spec = pltpu.PrefetchScalarGridSpec(
        num_scalar_prefetch=4,
        grid=(N_ETILES,),
        in_specs=[
            pl.BlockSpec((ETILE, D), lambda i, te, tf, tv, xb: (xb[i], 0)),
            pl.BlockSpec((None, None, D, 2 * D_FF), e_of),
            pl.BlockSpec((None, None, 1, 2 * D_FF), e_of),
            pl.BlockSpec((None, None, D_FF, D), e_of),
            pl.BlockSpec((None, None, 1, D), e_of),
        ],
        out_specs=pl.BlockSpec((ETILE, D), lambda i, *_: (i, 0)),
        scratch_shapes=[pltpu.VMEM((D, 2 * D_FF), BF16), pltpu.VMEM((D_FF, D), BF16)],
    )
    return pl.pallas_call(
        _expert_kernel,
        grid_spec=grid_spec,
        out_shape=jax.ShapeDtypeStruct((P_ROWS, D), F32),
        compiler_params=_params(("arbitrary",)),
        name="moe_experts",
    )(plan["te"], plan["tfirst"], plan["tvalid"], plan["xblk"], xs, w_gu,
      b_gu.reshape(depth, N_EXPERTS, 1, 2 * D_FF), w_down, b_down.reshape(depth, N_EXPERTS, 1, D))


def _combine_kernel(base_ref, seg_ref, nch_ref, ntile_ref, y_hbm, x1_ref, meta_ref, segrow_ref, mod_ref, o_ref,
                    ybuf, sem):
    i = pl.program_id(0)
    slot = i % 2

    def fetch(t, s):
        make = lambda loc, glob: pltpu.make_async_copy(y_hbm.at[pl.ds(glob, CH)], ybuf.at[s, pl.ds(loc, CH)],
                                                       sem.at[s])
        _segment_copies(t, base_ref, seg_ref, nch_ref, make)

    @pl.when(i == 0)
    def _():
        ybuf[...] = jnp.zeros_like(ybuf)
        fetch(i, slot)

    @pl.when(i + 1 < NBLK)
    def _():
        fetch(i + 1, 1 - slot)

    _drain(ntile_ref[i], pltpu.make_async_copy(y_hbm.at[pl.ds(0, CH)], ybuf.at[slot, pl.ds(0, CH)], sem.at[slot]))

    meta = meta_ref[...]
    pos = _local_positions(meta, segrow_ref[0])
    riota = lax.broadcasted_iota(jnp.int32, (TM, ROWS_L), 1).astype(F32)
    gates = jnp.zeros((TM, ROWS_L), F32)
    for k, p in enumerate(pos):
        gates = jnp.where(riota == p, meta[:, _META_W + k:_META_W + k + 1], gates)
    hi = gates.astype(BF16)
    lo = (gates - hi.astype(F32)).astype(BF16)
    yb = ybuf[slot].astype(BF16)
    acc = _dot(hi, yb) + _dot(lo, yb)
    o_ref[...] = x1_ref[...] + mod_ref[0][5:6] * acc


def _combine(plan, y, x1, meta, mod):
    grid_spec = pltpu.PrefetchScalarGridSpec(
        num_scalar_prefetch=4,
        grid=(NBLK,),
        in_specs=[pl.BlockSpec(memory_space=pl.ANY), _TOK(D), _TOK(LANES),
                  pl.BlockSpec((1, 1, LANES), lambda i, *_: (i, 0, 0)), _MOD],
        out_specs=_TOK(D),
        scratch_shapes=[pltpu.VMEM((2, ROWS_L, D), F32), pltpu.SemaphoreType.DMA((2,))],
    )
    return pl.pallas_call(
        _combine_kernel,
        grid_spec=grid_spec,
        out_shape=jax.ShapeDtypeStruct((T, D), F32),
        compiler_params=_params(("arbitrary",)),
        name="moe_combine",
    )(plan["base"], plan["seg"], plan["nch"], plan["ntile"], y, x1, meta, plan["segrow"], mod)


def _moe_plan(cnt):
    cnt = cnt[:, 0, :N_EXPERTS].astype(jnp.int32)
    c8 = (cnt + CH - 1) // CH * CH
    tot = jnp.sum(c8, axis=0)
    tiles = (tot + ETILE - 1) // ETILE
    tile_end = jnp.cumsum(tiles)
    offs = (tile_end - tiles) * ETILE
    base = offs[None, :] + jnp.cumsum(c8, axis=0) - c8
    seg = jnp.cumsum(c8, axis=1) - c8
    tid = jnp.arange(N_ETILES, dtype=jnp.int32)
    te = jnp.sum((tile_end[None, :] <= tid[:, None]).astype(jnp.int32), axis=1)
    tvalid = (te < N_EXPERTS).astype(jnp.int32)
    last = jnp.max(jnp.where(tiles > 0, jnp.arange(N_EXPERTS, dtype=jnp.int32), 0))
    te = jnp.where(tvalid == 1, te, last)
    return {
        "base": base.reshape(-1), "seg": seg.reshape(-1), "nch": (c8 // CH).reshape(-1),
        "ntile": jnp.sum(c8, axis=1) // CH, "tail0": offs + tot, "tailn": (tiles * ETILE - tot) // CH,
        "segrow": jnp.pad(seg.astype(F32), ((0, 0), (0, LANES - N_EXPERTS))).reshape(NBLK, 1, LANES),
        "te": te, "tvalid": tvalid, "xblk": jnp.where(tvalid == 1, tid, 0),
        "tfirst": jnp.concatenate([jnp.ones((1,), jnp.int32), (te[1:] != te[:-1]).astype(jnp.int32)]),
    }


def _moe(layer, x1, h2, meta, cnt, mod, w_gu, b_gu, w_down, b_down):
    plan = _moe_plan(cnt)
    xs = _dispatch(plan, h2, meta)
    y = _experts(layer, plan, xs, w_gu, b_gu, w_down, b_down)
    return _combine(plan, y, x1, meta, mod)


_HK, _HV = H_C * DK_C, H_C * DV_C
_ODD_MAIN = 2 * _HK + 2 * _HV


def _odd_in_kernel(x_ref, mod_ref, nmix_ref, win_ref, wgk_ref, bgk_ref, q_ref, k_ref, v_ref, g_ref, la_ref):
    m = mod_ref[0]
    h = _rms(x_ref[...]) * nmix_ref[...] * (1.0 + m[1:2]) + m[0:1]
    a = _dot(h.astype(BF16), win_ref[...])
    q_ref[...] = a[:, :_HK] * (DK_C ** -0.5)
    k_ref[...] = a[:, _HK:2 * _HK]
    v_ref[...] = a[:, 2 * _HK:2 * _HK + _HV]
    g_ref[...] = a[:, 2 * _HK + _HV:_ODD_MAIN]
    z = _dot(a[:, _ODD_MAIN:].astype(BF16), wgk_ref[...]) + bgk_ref[...]
    la_ref[...] = (jnp.minimum(z, 0.0) - jnp.log(1.0 + jnp.exp(-jnp.abs(z)))) * (1.0 / GATE_TAU)


def _gla_kernel(has_init, nchunk, *refs):
    if has_init:
        (qf, kf, vf, laf, qb, kb, vb, lab, s0f, s0b, _, _, of_ref, ob_ref, st) = refs
    else:
        (qf, kf, vf, laf, qb, kb, vb, lab, of_ref, ob_ref, sf_ref, sb_ref, st) = refs
    j = pl.program_id(1)

    @pl.when(j == 0)
    def _():
        if has_init:
            st[0] = s0f[...]
            st[1] = s0b[...]
        else:
            st[...] = jnp.zeros_like(st)

    row = lax.broadcasted_iota(jnp.int32, (GLA_CHUNK, GLA_CHUNK), 0)
    col = lax.broadcasted_iota(jnp.int32, (GLA_CHUNK, GLA_CHUNK), 1)
    for d, (q_r, k_r, v_r, la_r, o_r) in enumerate(((qf, kf, vf, laf, of_ref), (qb, kb, vb, lab, ob_ref))):
        keep = (col <= row) if d == 0 else (col >= row)
        tri = jnp.where(keep, 1.0, 0.0).astype(BF16)
        for s in range(GLA_NS):
            g = la_r[s]
            g_hi = g.astype(BF16)
            g_r = g - g_hi.astype(F32)
            g_mid = g_r.astype(BF16)
            g_lo = (g_r - g_mid.astype(F32)).astype(BF16)
            c = _dot(tri, g_hi) + _dot(tri, g_mid) + _dot(tri, g_lo)
            tot = jnp.sum(g, axis=0, keepdims=True)
            decay = jnp.exp(tot)
            e_neg = jnp.exp(-c)
            k = k_r[s]
            qe = (q_r[s] * jnp.exp(c)).astype(BF16)
            kd = (k * e_neg).astype(BF16)
            k2 = (k * (e_neg * decay)).astype(BF16)
            v = v_r[s].astype(BF16)
            for h in range(H_C):
                kc = slice(h * DK_C, (h + 1) * DK_C)
                vc = slice(h * DV_C, (h + 1) * DV_C)
                att = jnp.where(keep, _dot_nt(qe[:, kc], kd[:, kc]), 0.0)
                s_t = st[d, s, h]
                o_r[s, :, vc] = _dot_nt(qe[:, kc], s_t.astype(BF16)) + _dot(att.astype(BF16), v[:, vc])
                st[d, s, h] = s_t * decay[:, kc] + _dot_tn(v[:, vc], k2[:, kc])

    if not has_init:
        @pl.when(j == nchunk - 1)
        def _():
            sf_ref[...] = st[0]
            sb_ref[...] = st[1]


def _gla_call(has_init, nseq, seqlen, row0, q, k, v, la, prev_f=None, prev_b=None, s0f=None, s0b=None):
    nchunk = seqlen // GLA_CHUNK
    nview = T // seqlen
    g0 = row0 // seqlen // GLA_NS
    view = lambda a: a.reshape(nview, seqlen, a.shape[-1])
    fwd = lambda g, j: (g0 + g, j, 0)
    bwd = lambda g, j: (g0 + g, nchunk - 1 - j, 0)
    bwd_la = lambda g, j: (g0 + g, nchunk - 1 - j, 1)
    blk = lambda w, m: pl.BlockSpec((GLA_NS, GLA_CHUNK, w), m)
    state_spec = pl.BlockSpec((GLA_NS, H_C, DV_C, DK_C), lambda g, j: (g, 0, 0, 0))
    in_specs = [blk(_HK, fwd), blk(_HK, fwd), blk(_HV, fwd), blk(_HK, fwd),
                blk(_HK, bwd), blk(_HK, bwd), blk(_HV, bwd), blk(_HK, bwd_la)]
    args = [view(q), view(k), view(v), view(la)] * 2
    aliases = {}
    if has_init:
        in_specs += [state_spec, state_spec] + [pl.BlockSpec(memory_space=pl.ANY)] * 2
        args += [s0f, s0b, view(prev_f), view(prev_b)]
        aliases = {len(args) - 2: 0, len(args) - 1: 1}
    out_specs = [blk(_HV, fwd), blk(_HV, bwd)]
    out_shape = [jax.ShapeDtypeStruct((nview, seqlen, _HV), F32)] * 2
    if not has_init:
        out_specs += [state_spec, state_spec]
        out_shape += [jax.ShapeDtypeStruct((nseq, H_C, DV_C, DK_C), F32)] * 2
    outs = pl.pallas_call(
        functools.partial(_gla_kernel, has_init, nchunk),
        grid=(nseq // GLA_NS, nchunk),
        in_specs=in_specs,
        out_specs=out_specs,
        out_shape=out_shape,
        scratch_shapes=[pltpu.VMEM((2, GLA_NS, H_C, DV_C, DK_C), F32)],
        input_output_aliases=aliases,
        compiler_params=_params(("arbitrary", "arbitrary")),
        name="gla_sample" if has_init else "gla_prompt",
    )(*args)
    return [outs[0].reshape(T, _HV), outs[1].reshape(T, _HV)] + list(outs[2:])


def _rope_tables():
    half = ROPE // 2
    inv_freq = np.power(np.float32(ROPE_THETA), -np.arange(0, half, 2, dtype=np.float32) / np.float32(half))
    n = np.arange(SAMPLE_LEN)
    row = (n // GRID_W).astype(np.float32)
    col = (n % GRID_W).astype(np.float32)
    ang_r = (row[:, None] * inv_freq[None, :]).astype(np.float32)
    ang_c = (col[:, None] * inv_freq[None, :]).astype(np.float32)
    nf = half // 2
    c = np.ones((TM + SAMPLE_LEN, LANES), np.float32)
    s1 = np.zeros((TM + SAMPLE_LEN, LANES), np.float32)
    s2 = np.zeros((TM + SAMPLE_LEN, LANES), np.float32)
    for base, ang in ((NOPE, ang_r), (NOPE + half, ang_c)):
        c[TM:, base:base + nf] = np.cos(ang)
        c[TM:, base + nf:base + half] = np.cos(ang)
        s1[TM:, base:base + nf] = -np.sin(ang)
        s2[TM:, base + nf:base + half] = np.sin(ang)
    return jnp.asarray(c), jnp.asarray(s1), jnp.asarray(s2)


def _pad_heads(w, nheads, width, lo=0):
    k = w.shape[0]
    w = w.reshape(k, nheads, width)
    w = jnp.pad(w, ((0, 0), (0, 0), (lo, HP - lo - width)))
    return w.reshape(k, nheads * HP)


def _row128(v, lo=0):
    return jnp.pad(v, (lo, LANES - lo - v.shape[0])).reshape(1, LANES)


def _even_layer(x, mod, nmix, w_in, q_a_norm, w_uq, q_norm, kv_a_norm, w_ukv, k_norm, v_norm, w_s, b_s, w_out,
                cache_ckv, cache_kpe):
    s = np.cumsum([Q_LORA, KV_LORA, ROPE, W_B])
    w_q, w_ckv, w_kpe, w_u, w_v = (w_in[:, :s[0]], w_in[:, s[0]:s[1]], w_in[:, s[1]:s[2]], w_in[:, s[2]:s[3]],
                                   w_in[:, s[3]:])
    w_kpe = jnp.pad(w_kpe, ((0, 0), (NOPE, LANES - QK_DIM)))
    win = jnp.concatenate([w_q, w_ckv, w_u, w_v, w_kpe], axis=1).astype(BF16)
    wuq = _pad_heads(w_uq, H_A, QK_DIM).astype(BF16)
    ukv = w_ukv.reshape(KV_LORA, H_A, NOPE + V_A)
    wuk = _pad_heads(ukv[:, :, :NOPE].reshape(KV_LORA, H_A * NOPE), H_A, NOPE)
    wuv = _pad_heads(ukv[:, :, NOPE:].reshape(KV_LORA, H_A * V_A), H_A, V_A)
    wukv = jnp.concatenate([wuk, wuv], axis=1).astype(BF16)
    qgain = _row128(q_norm * (QK_DIM ** -0.5))
    kgain = _row128(k_norm)
    bias = b_s.reshape(G_B // 2, 2, CHUNK_B)
    bias = jnp.concatenate([jnp.broadcast_to(bias[:, 0, :, None], (G_B // 2, CHUNK_B, C_B)),
                            jnp.broadcast_to(bias[:, 1, :, None], (G_B // 2, CHUNK_B, C_B))], axis=-1)
    rc, rs1, rs2 = _rope_tables()
    rope_spec = pl.BlockSpec((TM, LANES), lambda i: (_rope_blk(i), 0))
    width = H_A * HP
    q, k, v, ob, ckv, kpe = pl.pallas_call(
        _even_in_kernel,
        grid=(NBLK,),
        in_specs=[
            _TOK(D), _MOD, _full((1, D)), _full((D, _WIN_N)), _full((1, Q_LORA)), _full((Q_LORA, width)),
            _full((1, LANES)), _full((1, KV_LORA)), _full((KV_LORA, 2 * width)), _full((1, LANES)),
            _full((1, W_B)), _full((G_B // 2, 2 * CHUNK_B, CHUNK_B)), _full((G_B // 2, CHUNK_B, LANES)),
            rope_spec, rope_spec, rope_spec,
        ],
        out_specs=[_TOK(width), _TOK(width), _TOK(width), _TOK(W_B), _TOK(KV_LORA), _TOK(ROPE)],
        out_shape=[
            jax.ShapeDtypeStruct((T, width), BF16), jax.ShapeDtypeStruct((T, width), BF16),
            jax.ShapeDtypeStruct((T, width), BF16), jax.ShapeDtypeStruct((T, W_B), BF16),
            jax.ShapeDtypeStruct((T, KV_LORA), F32), jax.ShapeDtypeStruct((T, ROPE), F32),
        ],
        compiler_params=_params(("arbitrary",)),
        name="even_in",
    )(x, mod, nmix.reshape(1, D), win, q_a_norm.reshape(1, Q_LORA), wuq, qgain, kv_a_norm.reshape(1, KV_LORA),
      wukv, kgain, v_norm.reshape(1, W_B), w_s.astype(BF16).reshape(G_B // 2, 2 * CHUNK_B, CHUNK_B), bias, rc, rs1,
      rs2)

    n_ctx = N_SAMPLE_SEQ * PAST_LEN
    kpe_ctx = jnp.pad(cache_kpe.reshape(n_ctx, ROPE), ((0, 0), (NOPE, LANES - QK_DIM)))
    k_ctx, v_ctx = pl.pallas_call(
        _ctx_kv_kernel,
        grid=(n_ctx // TM,),
        in_specs=[_TOK(KV_LORA), _TOK(LANES), _full((KV_LORA, 2 * width)), _full((1, LANES))],
        out_specs=[_TOK(width), _TOK(width)],
        out_shape=[jax.ShapeDtypeStruct((n_ctx, width), BF16)] * 2,
        compiler_params=_params(("arbitrary",)),
        name="ctx_kv",
    )(cache_ckv.reshape(n_ctx, KV_LORA), kpe_ctx, wukv, kgain)

    oa = _attention(q, k, v, k_ctx, v_ctx)
    woa = jnp.pad(w_out[:H_A * V_A].reshape(H_A, V_A, D), ((0, 0), (0, HP - V_A), (0, 0))).reshape(width, D)
    return oa, ob, woa.astype(BF16), w_out[H_A * V_A:].astype(BF16), ckv, kpe


def kernel(x_prompt, x_sample, cache_mla_ckv, cache_mla_kpe, state_gla_fwd, state_gla_bwd, c, c_ctx, ada_w, ada_b,
           norm_mix, norm_ffn, even_w_in, mla_q_a_norm, mla_w_uq, mla_q_norm, mla_kv_a_norm, mla_w_ukv, mla_k_norm,
           cmlp_v_norm, cmlp_w_s, cmlp_b_s, even_w_out, odd_w_in, gla_w_gk_fwd, gla_b_gk_fwd, gla_w_gk_bwd,
           gla_b_gk_bwd, gla_o_norm, odd_w_out, moe_w_router, moe_b_router, moe_w_gu, moe_b_gu, moe_w_down,
           moe_b_down):
    x0 = jnp.concatenate([x_sample.reshape(N_SAMPLE, D), x_prompt.reshape(N_PROMPT, D)], axis=0)
    cond8 = jnp.concatenate([c_ctx[None, :], c, jnp.zeros((SUBLANES - 1 - N_SAMPLE_SEQ, D), F32)], axis=0)
    mods = _adaln(cond8, ada_w, ada_b)
    wr = jnp.pad(moe_w_router, ((0, 0), (0, 0), (0, LANES - N_EXPERTS)))
    wr_hi = wr.astype(BF16)
    wr = jnp.concatenate([wr_hi, (wr - wr_hi.astype(F32)).astype(BF16)], axis=-1)
    br = jnp.pad(moe_b_router, ((0, 0), (0, LANES - N_EXPERTS))).reshape(2, 1, LANES)

    oa, ob, woa, wob, ckv, kpe = _even_layer(
        x0, mods[0], norm_mix[0], even_w_in[0], mla_q_a_norm[0], mla_w_uq[0], mla_q_norm[0], mla_kv_a_norm[0],
        mla_w_ukv[0], mla_k_norm[0], cmlp_v_norm[0], cmlp_w_s[0], cmlp_b_s[0], even_w_out[0],
        cache_mla_ckv[:, 0], cache_mla_kpe[:, 0])
    width = H_A * HP
    x1, h2, meta, cnt = pl.pallas_call(
        _even_out_kernel,
        grid=(NBLK,),
        in_specs=[_TOK(width), _TOK(W_B), _TOK(D), _MOD, _full((width, D)), _full((W_B, D)), _full((1, D)),
                  _full((D, 2 * LANES)), _full((1, LANES))],
        out_specs=_PROLOGUE_OUT_SPECS,
        out_shape=_PROLOGUE_OUT_SHAPE,
        compiler_params=_params(("arbitrary",)),
        name="even_out",
    )(oa, ob, x0, mods[0], woa, wob, norm_ffn[0].reshape(1, D), wr[0], br[0])
    x2 = _moe(0, x1, h2, meta, cnt, mods[0], moe_w_gu, moe_b_gu, moe_w_down, moe_b_down)

    w_in = odd_w_in[0]
    win = jnp.concatenate([w_in, jnp.zeros((D, LANES - 2 * GATE_RANK), F32)], axis=1).astype(BF16)
    wgk = jnp.zeros((LANES, 2 * _HK), F32)
    wgk = wgk.at[:GATE_RANK, :_HK].set(gla_w_gk_fwd[0]).at[GATE_RANK:2 * GATE_RANK, _HK:].set(gla_w_gk_bwd[0])
    bgk = jnp.concatenate([gla_b_gk_fwd[0], gla_b_gk_bwd[0]]).reshape(1, 2 * _HK)
    q, k, v, g, la = pl.pallas_call(
        _odd_in_kernel,
        grid=(NBLK,),
        in_specs=[_TOK(D), _MOD, _full((1, D)), _full((D, _ODD_MAIN + LANES)), _full((LANES, 2 * _HK)),
                  _full((1, 2 * _HK))],
        out_specs=[_TOK(_HK), _TOK(_HK), _TOK(_HV), _TOK(_HV), _TOK(2 * _HK)],
        out_shape=[jax.ShapeDtypeStruct((T, _HK), F32), jax.ShapeDtypeStruct((T, _HK), F32),
                   jax.ShapeDtypeStruct((T, _HV), F32), jax.ShapeDtypeStruct((T, _HV), F32),
                   jax.ShapeDtypeStruct((T, 2 * _HK), F32)],
        compiler_params=_params(("arbitrary",)),
        name="odd_in",
    )(x2, mods[1], norm_mix[1].reshape(1, D), win, wgk.astype(BF16), bgk)

    of, obk, st_f, st_b = _gla_call(False, N_PROMPT_SEQ, PROMPT_LEN, N_SAMPLE, q, k, v, la)
    s0f = state_gla_fwd[:, 0].transpose(0, 1, 3, 2)
    s0b = state_gla_bwd[:, 0].transpose(0, 1, 3, 2)
    of, obk = _gla_call(True, N_SAMPLE_SEQ, SAMPLE_LEN, 0, q, k, v, la, of, obk, s0f, s0b)

    x3, h2, meta, cnt = pl.pallas_call(
        _odd_out_kernel,
        grid=(NBLK,),
        in_specs=[_TOK(_HV), _TOK(_HV), _TOK(_HV), _TOK(D), _MOD, _full((1, DV_C)), _full((_HV, D)),
                  _full((1, D)), _full((D, 2 * LANES)), _full((1, LANES))],
        out_specs=_PROLOGUE_OUT_SPECS,
        out_shape=_PROLOGUE_OUT_SHAPE,
        compiler_params=_params(("arbitrary",)),
        name="odd_out",
    )(of, obk, g, x2, mods[1], gla_o_norm[0].reshape(1, DV_C), odd_w_out[0].astype(BF16),
      norm_ffn[1].reshape(1, D), wr[1], br[1])
    x4 = _moe(1, x3, h2, meta, cnt, mods[1], moe_w_gu, moe_b_gu, moe_w_down, moe_b_down)

    y_sample = x4[:N_SAMPLE].reshape(N_SAMPLE_SEQ, SAMPLE_LEN, D)
    y_prompt = x4[N_SAMPLE:].reshape(N_PROMPT_SEQ, PROMPT_LEN, D)
    new_ckv = ckv[N_SAMPLE:].reshape(N_PROMPT_SEQ, 1, PROMPT_LEN, KV_LORA)
    new_kpe = kpe[N_SAMPLE:].reshape(N_PROMPT_SEQ, 1, PROMPT_LEN, ROPE)
    new_fwd = st_f.transpose(0, 1, 3, 2)[:, None]
    new_bwd = st_b.transpose(0, 1, 3, 2)[:, None]
    return (y_prompt, y_sample, new_ckv, new_kpe, new_fwd, new_bwd)
```

```python
import functools

import numpy as np
import jax
import jax.numpy as jnp
from jax import lax
from jax.experimental import pallas as pl
from jax.experimental.pallas import tpu as pltpu

F32 = jnp.float32
BF16 = jnp.bfloat16

D = 1024
N_PROMPT_SEQ, PROMPT_LEN = 16, 256
N_SAMPLE_SEQ, SAMPLE_LEN = 4, 2048
PAST_LEN = 512
N_PROMPT = N_PROMPT_SEQ * PROMPT_LEN
N_SAMPLE = N_SAMPLE_SEQ * SAMPLE_LEN
T = N_PROMPT + N_SAMPLE
EPS = 1e-6
GRID_W = 64
H_A, Q_LORA, KV_LORA, NOPE, ROPE, V_A = 8, 512, 256, 64, 32, 64
QK_DIM = NOPE + ROPE
G_B, C_B, W_B, CHUNK_B = 8, 64, 512, 128
H_C, DK_C, DV_C, GATE_RANK, GATE_TAU, GLA_CHUNK = 4, 128, 256, 16, 16.0, 64
N_EXPERTS, TOP_K, D_FF = 32, 4, 1024
SWIGLU_LIMIT, SWIGLU_ALPHA = 7.0, 1.702
ROPE_THETA = 10000.0

LANES = 128
SUBLANES = 8
VMEM_LIMIT = 56 * 1024 * 1024

TM = 256
NBLK = T // TM
SAMPLE_BLKS = N_SAMPLE // TM
BLKS_PER_SAMPLE_SEQ = SAMPLE_LEN // TM
GLA_NS = 4
ETILE = 512
TMD = 2 * TM
NBD = T // TMD
CH = 2 * SUBLANES
_MAX_LOCAL = TMD * TOP_K + N_EXPERTS * (CH - 1)
ROWS_L = -(-_MAX_LOCAL // LANES) * LANES
N_ETILES = -(-NBD * _MAX_LOCAL // ETILE) + N_EXPERTS
P_ROWS = N_ETILES * ETILE
HP = LANES


def _cond_row(i):
    return jnp.where(i < SAMPLE_BLKS, 1 + i // BLKS_PER_SAMPLE_SEQ, 0)


def _rope_blk(i):
    return jnp.where(i < SAMPLE_BLKS, 1 + i % BLKS_PER_SAMPLE_SEQ, 0)


def _rms(x):
    return x * lax.rsqrt(jnp.mean(x * x, axis=-1, keepdims=True) + EPS)


def _gelu(x):
    return 0.5 * x * (1.0 + jnp.tanh(0.7978845608028654 * (x + 0.044715 * (x * x * x))))


def _silu(x):
    return x * jax.nn.sigmoid(x)


def _dot(a, b):
    return jnp.dot(a, b, preferred_element_type=F32)


def _dot_nt(a, b):
    return lax.dot_general(a, b, (((1,), (1,)), ((), ())), preferred_element_type=F32)


def _dot_tn(a, b):
    return lax.dot_general(a, b, (((0,), (0,)), ((), ())), preferred_element_type=F32)


def _params(sem, vmem=VMEM_LIMIT):
    return pltpu.CompilerParams(dimension_semantics=sem, vmem_limit_bytes=vmem)


def _full(shape):
    nd = len(shape)
    return pl.BlockSpec(shape, lambda *_: (0,) * nd)


ADA_TN = 1536


def _adaln_kernel(c_ref, w_ref, b_ref, o_ref):
    s = _silu(c_ref[...]).astype(BF16)
    o_ref[0] = _dot(s, w_ref[0].astype(BF16)) + b_ref[0]


def _adaln(cond8, ada_w, ada_b):
    depth = ada_w.shape[0]
    n = ada_w.shape[2]
    out = pl.pallas_call(
        _adaln_kernel,
        grid=(depth, n // ADA_TN),
        in_specs=[
            pl.BlockSpec((SUBLANES, D), lambda l, j: (0, 0)),
            pl.BlockSpec((1, D, ADA_TN), lambda l, j: (l, 0, j)),
            pl.BlockSpec((1, 1, ADA_TN), lambda l, j: (l, 0, j)),
        ],
        out_specs=pl.BlockSpec((1, SUBLANES, ADA_TN), lambda l, j: (l, 0, j)),
        out_shape=jax.ShapeDtypeStruct((depth, SUBLANES, n), F32),
        compiler_params=_params(("arbitrary", "arbitrary")),
        name="adaln",
    )(cond8, ada_w, ada_b.reshape(depth, 1, n))
    return out.reshape(depth, SUBLANES, 6, D)


_QC0, _CKV0, _U0, _V0, _KPE0, _WIN_N = 0, 512, 768, 1280, 1792, 1920


def _rope(y, c, s1, s2):
    return y * c + pltpu.roll(y, LANES - 8, 1) * s1 + pltpu.roll(y, 8, 1) * s2


def _k_heads(k_raw, kpe128, kp_rot, kgain, k_ref):
    sskpe = jnp.sum(kpe128 * kpe128, axis=-1, keepdims=True)
    for h in range(H_A):
        kb = k_raw[:, h * HP:(h + 1) * HP]
        r = lax.rsqrt((jnp.sum(kb * kb, axis=-1, keepdims=True) + sskpe) * (1.0 / QK_DIM) + EPS)
        k_ref[:, h * HP:(h + 1) * HP] = ((kb * kgain + kp_rot) * r).astype(BF16)


def _even_in_kernel(x_ref, mod_ref, nmix_ref, win_ref, qan_ref, wuq_ref, qgain_ref, kvan_ref, wukv_ref,
                    kgain_ref, vnorm_ref, ws_ref, bs_ref, rc_ref, rs1_ref, rs2_ref,
                    q_ref, k_ref, v_ref, ob_ref, ckv_ref, kpe_ref):
    m = mod_ref[0]
    h = _rms(x_ref[...]) * nmix_ref[...] * (1.0 + m[1:2]) + m[0:1]
    a = _dot(h.astype(BF16), win_ref[...])
    qc = a[:, _QC0:_CKV0]
    ckv = a[:, _CKV0:_U0]
    u = a[:, _U0:_V0]
    vv = a[:, _V0:_KPE0]
    kpe128 = a[:, _KPE0:_WIN_N]

    ckv_n = _rms(ckv) * kvan_ref[...]
    ckv_ref[...] = ckv_n
    kpe_ref[...] = kpe128[:, NOPE:QK_DIM]

    rc, rs1, rs2 = rc_ref[...], rs1_ref[...], rs2_ref[...]
    qn = (_rms(qc) * qan_ref[...]).astype(BF16)
    qr = _dot(qn, wuq_ref[...])
    qgain = qgain_ref[...]
    for hh in range(H_A):
        blk = qr[:, hh * HP:(hh + 1) * HP]
        r = lax.rsqrt(jnp.sum(blk * blk, axis=-1, keepdims=True) * (1.0 / QK_DIM) + EPS)
        q_ref[:, hh * HP:(hh + 1) * HP] = _rope(blk * r * qgain, rc, rs1, rs2).astype(BF16)

    kv = _dot(ckv_n.astype(BF16), wukv_ref[...])
    v_ref[...] = kv[:, H_A * HP:].astype(BF16)
    kgain = kgain_ref[...]
    kp_rot = _rope(kpe128 * kgain, rc, rs1, rs2)
    _k_heads(kv[:, :H_A * HP], kpe128, kp_rot, kgain, k_ref)

    ug = _gelu(u)
    vn = (_rms(_gelu(vv)) * vnorm_ref[...]).astype(BF16)
    low = lax.broadcasted_iota(jnp.int32, (CHUNK_B, LANES), 1) < C_B
    for c in range(TM // CHUNK_B):
        rows = slice(c * CHUNK_B, (c + 1) * CHUNK_B)
        for p in range(G_B // 2):
            cols = slice(p * LANES, (p + 1) * LANES)
            blk = vn[rows, cols]
            both = _dot(ws_ref[p], blk)
            mixed = jnp.where(low, both[:CHUNK_B], both[CHUNK_B:]) + bs_ref[p]
            ob_ref[rows, cols] = (ug[rows, cols] * mixed).astype(BF16)


def _ctx_kv_kernel(ckv_ref, kpe_ref, wukv_ref, kgain_ref, k_ref, v_ref):
    kv = _dot(ckv_ref[...].astype(BF16), wukv_ref[...])
    v_ref[...] = kv[:, H_A * HP:].astype(BF16)
    kgain = kgain_ref[...]
    kpe128 = kpe_ref[...]
    _k_heads(kv[:, :H_A * HP], kpe128, kpe128 * kgain, kgain, k_ref)


def _attn_self_kernel(q_ref, k_ref, v_ref, o_ref):
    for h in range(H_A):
        cols = slice(h * HP, (h + 1) * HP)
        s = _dot_nt(q_ref[:, cols], k_ref[:, cols])
        p = jnp.exp(s - jnp.max(s, axis=-1, keepdims=True))
        inv = 1.0 / jnp.sum(p, axis=-1, keepdims=True)
        o_ref[:, cols] = (_dot(p.astype(BF16), v_ref[:, cols]) * inv).astype(BF16)


def _attn_ctx_kernel(q_ref, k_ref, v_ref, kc_ref, vc_ref, prev_ref, o_ref):
    del prev_ref
    for h in range(H_A):
        cols = slice(h * HP, (h + 1) * HP)
        q = q_ref[:, cols]
        s1 = _dot_nt(q, k_ref[:, cols])
        s2 = _dot_nt(q, kc_ref[:, cols])
        mx = jnp.maximum(jnp.max(s1, axis=-1, keepdims=True), jnp.max(s2, axis=-1, keepdims=True))
        p1 = jnp.exp(s1 - mx)
        p2 = jnp.exp(s2 - mx)
        inv = 1.0 / (jnp.sum(p1, axis=-1, keepdims=True) + jnp.sum(p2, axis=-1, keepdims=True))
        o = _dot(p1.astype(BF16), v_ref[:, cols]) + _dot(p2.astype(BF16), vc_ref[:, cols])
        o_ref[:, cols] = (o * inv).astype(BF16)


def _attention(q, k, v, k_ctx, v_ctx):
    width = H_A * HP
    first = N_SAMPLE // PROMPT_LEN
    o = pl.pallas_call(
        _attn_self_kernel,
        grid=(N_PROMPT_SEQ,),
        in_specs=[pl.BlockSpec((PROMPT_LEN, width), lambda i: (first + i, 0))] * 3,
        out_specs=pl.BlockSpec((PROMPT_LEN, width), lambda i: (first + i, 0)),
        out_shape=jax.ShapeDtypeStruct((T, width), BF16),
        compiler_params=_params(("arbitrary",)),
        name="attn_prompt",
    )(q, k, v)
    qblk = lambda b, j: (b * BLKS_PER_SAMPLE_SEQ + j, 0)
    return pl.pallas_call(
        _attn_ctx_kernel,
        grid=(N_SAMPLE_SEQ, BLKS_PER_SAMPLE_SEQ),
        in_specs=[
            pl.BlockSpec((TM, width), qblk),
            pl.BlockSpec((SAMPLE_LEN, width), lambda b, j: (b, 0)),
            pl.BlockSpec((SAMPLE_LEN, width), lambda b, j: (b, 0)),
            pl.BlockSpec((PAST_LEN, width), lambda b, j: (b, 0)),
            pl.BlockSpec((PAST_LEN, width), lambda b, j: (b, 0)),
            pl.BlockSpec(memory_space=pl.ANY),
        ],
        out_specs=pl.BlockSpec((TM, width), qblk),
        out_shape=jax.ShapeDtypeStruct((T, width), BF16),
        input_output_aliases={5: 0},
        compiler_params=_params(("arbitrary", "arbitrary")),
        name="attn_sample",
    )(q, k, v, k_ctx, v_ctx, o)


_META_IDX, _META_RANK, _META_W = 0, TOP_K, 2 * TOP_K


def _moe_prologue(x1, m, nffn_ref, wr_ref, br_ref, x1_ref, h2_ref, meta_ref, cnt_ref):
    x1_ref[...] = x1
    h2 = _rms(x1) * nffn_ref[...] * (1.0 + m[4:5]) + m[3:4]
    h2_ref[...] = h2.astype(BF16)
    lane = lax.broadcasted_iota(jnp.int32, (TM, LANES), 1)
    lanef = lane.astype(F32)
    h_hi = h2.astype(BF16)
    h_lo = (h2 - h_hi.astype(F32)).astype(BF16)
    r = _dot(h_hi, wr_ref[...])
    logits = r[:, :LANES] + r[:, LANES:] + _dot(h_lo, wr_ref[:, :LANES]) + br_ref[...]
    work = jnp.where(lane < N_EXPERTS, logits, -jnp.inf)
    hots, vals = [], []
    for _ in range(TOP_K):
        mx = jnp.max(work, axis=-1, keepdims=True)
        idx = jnp.min(jnp.where(work == mx, lanef, float(LANES)), axis=-1, keepdims=True)
        hot = lanef == idx
        work = jnp.where(hot, -jnp.inf, work)
        hots.append((hot, idx))
        vals.append(mx)
    es = [jnp.exp(v - vals[0]) for v in vals]
    inv = 1.0 / (es[0] + es[1] + es[2] + es[3])
    sel = jnp.zeros((TM, LANES), F32)
    for hot, _ in hots:
        sel = jnp.where(hot, 1.0, sel)
    row = lax.broadcasted_iota(jnp.int32, (TM, TM), 0)
    col = lax.broadcasted_iota(jnp.int32, (TM, TM), 1)
    strict = jnp.where(row > col, 1.0, 0.0).astype(BF16)
    before = _dot(strict, sel.astype(BF16))
    meta = jnp.zeros((TM, LANES), F32)
    for kk, (hot, idx) in enumerate(hots):
        rank = jnp.sum(jnp.where(hot, before, 0.0), axis=-1, keepdims=True)
        meta = jnp.where(lane == _META_IDX + kk, idx, meta)
        meta = jnp.where(lane == _META_RANK + kk, rank, meta)
        meta = jnp.where(lane == _META_W + kk, es[kk] * inv, meta)
    meta_ref[...] = meta
    cnt_ref[0] = jnp.broadcast_to(jnp.sum(sel, axis=0, keepdims=True), (SUBLANES, LANES))


def _even_out_kernel(oa_ref, ob_ref, x_ref, mod_ref, woa_ref, wob_ref, nffn_ref, wr_ref, br_ref,
                     x1_ref, h2_ref, meta_ref, cnt_ref):
    m = mod_ref[0]
    out = _dot(oa_ref[...], woa_ref[...]) + _dot(ob_ref[...], wob_ref[...])
    x1 = x_ref[...] + m[2:3] * out
    _moe_prologue(x1, m, nffn_ref, wr_ref, br_ref, x1_ref, h2_ref, meta_ref, cnt_ref)


def _odd_out_kernel(of_ref, ob_ref, g_ref, x_ref, mod_ref, onorm_ref, wo_ref, nffn_ref, wr_ref, br_ref,
                    x1_ref, h2_ref, meta_ref, cnt_ref):
    m = mod_ref[0]
    onorm = onorm_ref[...]
    parts = []
    for h in range(H_C):
        cols = slice(h * DV_C, (h + 1) * DV_C)
        o = of_ref[:, cols] + ob_ref[:, cols]
        parts.append((_rms(o) * onorm * _silu(g_ref[:, cols])).astype(BF16))
    out = _dot(jnp.concatenate(parts, axis=-1), wo_ref[...])
    x1 = x_ref[...] + m[2:3] * out
    _moe_prologue(x1, m, nffn_ref, wr_ref, br_ref, x1_ref, h2_ref, meta_ref, cnt_ref)


_TOK = lambda w: pl.BlockSpec((TM, w), lambda i, *_: (i, 0))
_MOD = pl.BlockSpec((1, 6, D), lambda i, *_: (_cond_row(i), 0, 0))
_TILE_ROW = pl.BlockSpec((1, SUBLANES, LANES), lambda i, *_: (i, 0, 0))

_PROLOGUE_OUT_SPECS = [_TOK(D), _TOK(D), _TOK(LANES), _TILE_ROW]
_PROLOGUE_OUT_SHAPE = [
    jax.ShapeDtypeStruct((T, D), F32),
    jax.ShapeDtypeStruct((T, D), BF16),
    jax.ShapeDtypeStruct((T, LANES), F32),
    jax.ShapeDtypeStruct((NBLK, SUBLANES, LANES), F32),
]


def _local_positions(meta, seg_rows):
    lanef = lax.broadcasted_iota(jnp.int32, (TM, LANES), 1).astype(F32)
    pos = []
    for k in range(TOP_K):
        halves = []
        for half in range(TMD // TM):
            m = meta[half * TM:(half + 1) * TM]
            hot = lanef == m[:, _META_IDX + k:_META_IDX + k + 1]
            start = jnp.sum(jnp.where(hot, seg_rows[half:half + 1], 0.0), axis=-1, keepdims=True)
            halves.append(start + m[:, _META_RANK + k:_META_RANK + k + 1])
        pos.append(jnp.concatenate(halves, axis=0))
    return pos


def _segment_copies(t, base_ref, seg_ref, nch_ref, make):
    def per_expert(e, carry):
        q = t * N_EXPERTS + e
        local0, global0 = seg_ref[q], base_ref[q]

        def one(j, c):
            make(pl.multiple_of(local0 + j * CH, CH), pl.multiple_of(global0 + j * CH, CH)).start()
            return c

        return lax.fori_loop(0, nch_ref[q], one, carry)

    lax.fori_loop(0, N_EXPERTS, per_expert, 0)


def _drain(count, chunk_copy):
    def one(j, c):
        chunk_copy.wait()
        return c

    lax.fori_loop(0, count, one, 0)


def _dispatch_kernel(base_ref, seg_ref, nch_ref, ntile_ref, tail0_ref, tailn_ref, h_ref, meta_ref, segrow_ref,
                     xs_hbm, buf, zbuf, sem):
    i = pl.program_id(0)
    slot = i % 2
    chunk = lambda s: pltpu.make_async_copy(buf.at[s, pl.ds(0, CH)], xs_hbm.at[pl.ds(0, CH)], sem.at[s])

    @pl.when(i == 0)
    def _():
        zbuf[...] = jnp.zeros_like(zbuf)

        def per_expert(e, carry):
            def one(j, c):
                pltpu.make_async_copy(zbuf, xs_hbm.at[pl.ds(pl.multiple_of(tail0_ref[e] + j * CH, CH), CH)],
                                      sem.at[2]).start()
                return c

            lax.fori_loop(0, tailn_ref[e], one, 0)
            return carry + tailn_ref[e]

        total = lax.fori_loop(0, N_EXPERTS, per_expert, 0)
        _drain(total, pltpu.make_async_copy(zbuf, xs_hbm.at[pl.ds(0, CH)], sem.at[2]))

    pos = _local_positions(meta_ref[...], segrow_ref[0])
    riota = lax.broadcasted_iota(jnp.int32, (TMD, ROWS_L), 1).astype(F32)
    pt = jnp.zeros((TMD, ROWS_L), F32)
    for p in pos:
        pt = jnp.where(riota == p, 1.0, pt)
    buf[slot] = _dot_tn(pt.astype(BF16), h_ref[...]).astype(BF16)

    make = lambda s, d: pltpu.make_async_copy(buf.at[slot, pl.ds(s, CH)], xs_hbm.at[pl.ds(d, CH)], sem.at[slot])
    _segment_copies(i, base_ref, seg_ref, nch_ref, make)

    @pl.when(i > 0)
    def _():
        _drain(ntile_ref[i - 1], chunk(1 - slot))

    @pl.when(i == NBD - 1)
    def _():
        _drain(ntile_ref[i], chunk(slot))


_TOKD = lambda w: pl.BlockSpec((TMD, w), lambda i, *_: (i, 0))
_SEGROWS = pl.BlockSpec((1, TMD // TM, LANES), lambda i, *_: (i, 0, 0))


def _dispatch(plan, h2, meta):
    grid_spec = pltpu.PrefetchScalarGridSpec(
        num_scalar_prefetch=6,
        grid=(NBD,),
        in_specs=[_TOKD(D), _TOKD(LANES), _SEGROWS],
        out_specs=pl.BlockSpec(memory_space=pl.ANY),
        scratch_shapes=[pltpu.VMEM((2, ROWS_L, D), BF16), pltpu.VMEM((CH, D), BF16),
                        pltpu.SemaphoreType.DMA((3,))],
    )
    return pl.pallas_call(
        _dispatch_kernel,
        grid_spec=grid_spec,
        out_shape=jax.ShapeDtypeStruct((P_ROWS, D), BF16),
        compiler_params=_params(("arbitrary",)),
        name="moe_dispatch",
    )(plan["base"], plan["seg"], plan["nch"], plan["ntile"], plan["tail0"], plan["tailn"], h2, meta, plan["segrow"])


def _expert_kernel(te_ref, tfirst_ref, tvalid_ref, xblk_ref, x_ref, wgu_ref, bgu_ref, wd_ref, bd_ref, y_ref,
                   wgu_bf, wd_bf):
    del xblk_ref
    i = pl.program_id(0)

    @pl.when(tfirst_ref[i] == 1)
    def _():
        wgu_bf[...] = wgu_ref[...].astype(BF16)
        wd_bf[...] = wd_ref[...].astype(BF16)

    def ffn(x):
        a = _dot(x, wgu_bf[...]) + bgu_ref[...]
        glu = jnp.minimum(a[:, :D_FF], SWIGLU_LIMIT)
        lin = jnp.clip(a[:, D_FF:], -SWIGLU_LIMIT, SWIGLU_LIMIT)
        act = (glu * jax.nn.sigmoid(SWIGLU_ALPHA * glu)) * (lin + 1.0)
        return (_dot(act.astype(BF16), wd_bf[...]) + bd_ref[...]).astype(BF16)

    half = ETILE // 2

    @pl.when(tvalid_ref[i] == 2)
    def _():
        y_ref[...] = ffn(x_ref[...])

    @pl.when(tvalid_ref[i] == 1)
    def _():
        y_ref[:half] = ffn(x_ref[:half])
        y_ref[half:] = jnp.zeros((ETILE - half, D), BF16)

    @pl.when(tvalid_ref[i] == 0)
    def _():
        y_ref[...] = jnp.zeros_like(y_ref)


def _experts(layer, plan, xs, w_gu, b_gu, w_down, b_down):
    depth = w_gu.shape[0]
    e_of = lambda i, te, *_: (layer, te[i], 0, 0)
    grid_spec = pltpu.PrefetchScalarGridSpec(
        num_scalar_prefetch=4,
        grid=(N_ETILES,),
        in_specs=[
            pl.BlockSpec((ETILE, D), lambda i, te, tf, tv, xb: (xb[i], 0)),
            pl.BlockSpec((None, None, D, 2 * D_FF), e_of),
            pl.BlockSpec((None, None, 1, 2 * D_FF), e_of),
            pl.BlockSpec((None, None, D_FF, D), e_of),
            pl.BlockSpec((None, None, 1, D), e_of),
        ],
        out_specs=pl.BlockSpec((ETILE, D), lambda i, *_: (i, 0)),
        scratch_shapes=[pltpu.VMEM((D, 2 * D_FF), BF16), pltpu.VMEM((D_FF, D), BF16)],
    )
    return pl.pallas_call(
        _expert_kernel,
        grid_spec=grid_spec,
        out_shape=jax.ShapeDtypeStruct((P_ROWS, D), BF16),
        compiler_params=_params(("arbitrary",)),
        name="moe_experts",
    )(plan["te"], plan["tfirst"], plan["tvalid"], plan["xblk"], xs, w_gu,
      b_gu.reshape(depth, N_EXPERTS, 1, 2 * D_FF), w_down, b_down.reshape(depth, N_EXPERTS, 1, D))


def _combine_kernel(base_ref, seg_ref, nch_ref, ntile_ref, y_hbm, x1_ref, meta_ref, segrow_ref, mod_ref, o_ref,
                    ybuf, sem):
    i = pl.program_id(0)
    slot = i % 2

    def fetch(t, s):
        make = lambda loc, glob: pltpu.make_async_copy(y_hbm.at[pl.ds(glob, CH)], ybuf.at[s, pl.ds(loc, CH)],
                                                       sem.at[s])
        _segment_copies(t, base_ref, seg_ref, nch_ref, make)

    @pl.when(i == 0)
    def _():
        ybuf[...] = jnp.zeros_like(ybuf)
        fetch(i, slot)

    @pl.when(i + 1 < NBD)
    def _():
        fetch(i + 1, 1 - slot)

    _drain(ntile_ref[i], pltpu.make_async_copy(y_hbm.at[pl.ds(0, CH)], ybuf.at[slot, pl.ds(0, CH)], sem.at[slot]))

    meta = meta_ref[...]
    pos = _local_positions(meta, segrow_ref[0])
    riota = lax.broadcasted_iota(jnp.int32, (TMD, ROWS_L), 1).astype(F32)
    gates = jnp.zeros((TMD, ROWS_L), F32)
    for k, p in enumerate(pos):
        gates = jnp.where(riota == p, meta[:, _META_W + k:_META_W + k + 1], gates)
    hi = gates.astype(BF16)
    lo = (gates - hi.astype(F32)).astype(BF16)
    yb = ybuf[slot]
    acc = _dot(hi, yb) + _dot(lo, yb)
    o_ref[...] = x1_ref[...] + mod_ref[0][5:6] * acc


def _combine(plan, y, x1, meta, mod):
    grid_spec = pltpu.PrefetchScalarGridSpec(
        num_scalar_prefetch=4,
        grid=(NBD,),
        in_specs=[pl.BlockSpec(memory_space=pl.ANY), _TOKD(D), _TOKD(LANES), _SEGROWS,
                  pl.BlockSpec((1, 6, D), lambda i, *_: (_cond_row(i * (TMD // TM)), 0, 0))],
        out_specs=_TOKD(D),
        scratch_shapes=[pltpu.VMEM((2, ROWS_L, D), BF16), pltpu.SemaphoreType.DMA((2,))],
    )
    return pl.pallas_call(
        _combine_kernel,
        grid_spec=grid_spec,
        out_shape=jax.ShapeDtypeStruct((T, D), F32),
        compiler_params=_params(("arbitrary",)),
        name="moe_combine",
    )(plan["base"], plan["seg"], plan["nch"], plan["ntile"], y, x1, meta, plan["segrow"], mod)


def _moe_plan(cnt):
    per = TMD // TM
    cnt = cnt[:, 0, :N_EXPERTS].astype(jnp.int32).reshape(NBD, per, N_EXPERTS)
    cpad = (jnp.sum(cnt, axis=1) + CH - 1) // CH * CH
    tot = jnp.sum(cpad, axis=0)
    tiles = (tot + ETILE - 1) // ETILE
    tile_end = jnp.cumsum(tiles)
    offs = (tile_end - tiles) * ETILE
    base = offs[None, :] + jnp.cumsum(cpad, axis=0) - cpad
    seg = jnp.cumsum(cpad, axis=1) - cpad
    segrows = seg[:, None, :] + jnp.cumsum(cnt, axis=1) - cnt
    tid = jnp.arange(N_ETILES, dtype=jnp.int32)
    te = jnp.sum((tile_end[None, :] <= tid[:, None]).astype(jnp.int32), axis=1)
    used = te < N_EXPERTS
    last = jnp.max(jnp.where(tiles > 0, jnp.arange(N_EXPERTS, dtype=jnp.int32), 0))
    te = jnp.where(used, te, last)
    rows = tot[te] - (tid - (tile_end - tiles)[te]) * ETILE
    tvalid = jnp.where(used, jnp.where(rows > ETILE // 2, 2, 1), 0).astype(jnp.int32)
    return {
        "base": base.reshape(-1), "seg": seg.reshape(-1), "nch": (cpad // CH).reshape(-1),
        "ntile": jnp.sum(cpad, axis=1) // CH, "tail0": offs + tot, "tailn": (tiles * ETILE - tot) // CH,
        "segrow": jnp.pad(segrows.astype(F32), ((0, 0), (0, 0), (0, LANES - N_EXPERTS))),
        "te": te, "tvalid": tvalid, "xblk": jnp.where(used, tid, 0),
        "tfirst": jnp.concatenate([jnp.ones((1,), jnp.int32), (te[1:] != te[:-1]).astype(jnp.int32)]),
    }


def _moe(layer, x1, h2, meta, cnt, mod, w_gu, b_gu, w_down, b_down):
    plan = _moe_plan(cnt)
    xs = _dispatch(plan, h2, meta)
    y = _experts(layer, plan, xs, w_gu, b_gu, w_down, b_down)
    return _combine(plan, y, x1, meta, mod)


_HK, _HV = H_C * DK_C, H_C * DV_C
_ODD_MAIN = 2 * _HK + 2 * _HV


def _odd_in_kernel(x_ref, mod_ref, nmix_ref, win_ref, wgk_ref, bgk_ref, q_ref, k_ref, v_ref, g_ref, la_ref):
    m = mod_ref[0]
    h = _rms(x_ref[...]) * nmix_ref[...] * (1.0 + m[1:2]) + m[0:1]
    a = _dot(h.astype(BF16), win_ref[...])
    q_ref[...] = a[:, :_HK] * (DK_C ** -0.5)
    k_ref[...] = a[:, _HK:2 * _HK]
    v_ref[...] = a[:, 2 * _HK:2 * _HK + _HV]
    g_ref[...] = a[:, 2 * _HK + _HV:_ODD_MAIN]
    z = _dot(a[:, _ODD_MAIN:].astype(BF16), wgk_ref[...]) + bgk_ref[...]
    la_ref[...] = (jnp.minimum(z, 0.0) - jnp.log(1.0 + jnp.exp(-jnp.abs(z)))) * (1.0 / GATE_TAU)


def _gla_kernel(has_init, nchunk, *refs):
    if has_init:
        (qf, kf, vf, laf, qb, kb, vb, lab, s0f, s0b, _, _, of_ref, ob_ref, st) = refs
    else:
        (qf, kf, vf, laf, qb, kb, vb, lab, of_ref, ob_ref, sf_ref, sb_ref, st) = refs
    j = pl.program_id(1)

    @pl.when(j == 0)
    def _():
        if has_init:
            st[0] = s0f[...]
            st[1] = s0b[...]
        else:
            st[...] = jnp.zeros_like(st)

    row = lax.broadcasted_iota(jnp.int32, (GLA_CHUNK, GLA_CHUNK), 0)
    col = lax.broadcasted_iota(jnp.int32, (GLA_CHUNK, GLA_CHUNK), 1)
    for d, (q_r, k_r, v_r, la_r, o_r) in enumerate(((qf, kf, vf, laf, of_ref), (qb, kb, vb, lab, ob_ref))):
        keep = (col <= row) if d == 0 else (col >= row)
        tri = jnp.where(keep, 1.0, 0.0).astype(BF16)
        for s in range(GLA_NS):
            g = la_r[s]
            g_hi = g.astype(BF16)
            g_r = g - g_hi.astype(F32)
            g_mid = g_r.astype(BF16)
            g_lo = (g_r - g_mid.astype(F32)).astype(BF16)
            c = _dot(tri, g_hi) + _dot(tri, g_mid) + _dot(tri, g_lo)
            tot = jnp.sum(g, axis=0, keepdims=True)
            decay = jnp.exp(tot)
            e_neg = jnp.exp(-c)
            k = k_r[s]
            qe = (q_r[s] * jnp.exp(c)).astype(BF16)
            kd = (k * e_neg).astype(BF16)
            k2 = (k * (e_neg * decay)).astype(BF16)
            v = v_r[s].astype(BF16)
            for h in range(H_C):
                kc = slice(h * DK_C, (h + 1) * DK_C)
                vc = slice(h * DV_C, (h + 1) * DV_C)
                att = jnp.where(keep, _dot_nt(qe[:, kc], kd[:, kc]), 0.0)
                s_t = st[d, s, h]
                o_r[s, :, vc] = _dot_nt(qe[:, kc], s_t.astype(BF16)) + _dot(att.astype(BF16), v[:, vc])
                st[d, s, h] = s_t * decay[:, kc] + _dot_tn(v[:, vc], k2[:, kc])

    if not has_init:
        @pl.when(j == nchunk - 1)
        def _():
            sf_ref[...] = st[0]
            sb_ref[...] = st[1]


def _gla_call(has_init, nseq, seqlen, row0, q, k, v, la, prev_f=None, prev_b=None, s0f=None, s0b=None):
    nchunk = seqlen // GLA_CHUNK
    nview = T // seqlen
    g0 = row0 // seqlen // GLA_NS
    view = lambda a: a.reshape(nview, seqlen, a.shape[-1])
    fwd = lambda g, j: (g0 + g, j, 0)
    bwd = lambda g, j: (g0 + g, nchunk - 1 - j, 0)
    bwd_la = lambda g, j: (g0 + g, nchunk - 1 - j, 1)
    blk = lambda w, m: pl.BlockSpec((GLA_NS, GLA_CHUNK, w), m)
    state_spec = pl.BlockSpec((GLA_NS, H_C, DV_C, DK_C), lambda g, j: (g, 0, 0, 0))
    in_specs = [blk(_HK, fwd), blk(_HK, fwd), blk(_HV, fwd), blk(_HK, fwd),
                blk(_HK, bwd), blk(_HK, bwd), blk(_HV, bwd), blk(_HK, bwd_la)]
    args = [view(q), view(k), view(v), view(la)] * 2
    aliases = {}
    if has_init:
        in_specs += [state_spec, state_spec] + [pl.BlockSpec(memory_space=pl.ANY)] * 2
        args += [s0f, s0b, view(prev_f), view(prev_b)]
        aliases = {len(args) - 2: 0, len(args) - 1: 1}
    out_specs = [blk(_HV, fwd), blk(_HV, bwd)]
    out_shape = [jax.ShapeDtypeStruct((nview, seqlen, _HV), F32)] * 2
    if not has_init:
        out_specs += [state_spec, state_spec]
        out_shape += [jax.ShapeDtypeStruct((nseq, H_C, DV_C, DK_C), F32)] * 2
    outs = pl.pallas_call(
        functools.partial(_gla_kernel, has_init, nchunk),
        grid=(nseq // GLA_NS, nchunk),
        in_specs=in_specs,
        out_specs=out_specs,
        out_shape=out_shape,
        scratch_shapes=[pltpu.VMEM((2, GLA_NS, H_C, DV_C, DK_C), F32)],
        input_output_aliases=aliases,
        compiler_params=_params(("arbitrary", "arbitrary")),
        name="gla_sample" if has_init else "gla_prompt",
    )(*args)
    return [outs[0].reshape(T, _HV), outs[1].reshape(T, _HV)] + list(outs[2:])


def _rope_tables():
    half = ROPE // 2
    inv_freq = np.power(np.float32(ROPE_THETA), -np.arange(0, half, 2, dtype=np.float32) / np.float32(half))
    n = np.arange(SAMPLE_LEN)
    row = (n // GRID_W).astype(np.float32)
    col = (n % GRID_W).astype(np.float32)
    ang_r = (row[:, None] * inv_freq[None, :]).astype(np.float32)
    ang_c = (col[:, None] * inv_freq[None, :]).astype(np.float32)
    nf = half // 2
    c = np.ones((TM + SAMPLE_LEN, LANES), np.float32)
    s1 = np.zeros((TM + SAMPLE_LEN, LANES), np.float32)
    s2 = np.zeros((TM + SAMPLE_LEN, LANES), np.float32)
    for base, ang in ((NOPE, ang_r), (NOPE + half, ang_c)):
        c[TM:, base:base + nf] = np.cos(ang)
        c[TM:, base + nf:base + half] = np.cos(ang)
        s1[TM:, base:base + nf] = -np.sin(ang)
        s2[TM:, base + nf:base + half] = np.sin(ang)
    return jnp.asarray(c), jnp.asarray(s1), jnp.asarray(s2)


def _pad_heads(w, nheads, width, lo=0):
    k = w.shape[0]
    w = w.reshape(k, nheads, width)
    w = jnp.pad(w, ((0, 0), (0, 0), (lo, HP - lo - width)))
    return w.reshape(k, nheads * HP)


def _row128(v, lo=0):
    return jnp.pad(v, (lo, LANES - lo - v.shape[0])).reshape(1, LANES)


def _even_layer(x, mod, nmix, w_in, q_a_norm, w_uq, q_norm, kv_a_norm, w_ukv, k_norm, v_norm, w_s, b_s, w_out,
                cache_ckv, cache_kpe):
    s = np.cumsum([Q_LORA, KV_LORA, ROPE, W_B])
    w_q, w_ckv, w_kpe, w_u, w_v = (w_in[:, :s[0]], w_in[:, s[0]:s[1]], w_in[:, s[1]:s[2]], w_in[:, s[2]:s[3]],
                                   w_in[:, s[3]:])
    w_kpe = jnp.pad(w_kpe, ((0, 0), (NOPE, LANES - QK_DIM)))
    win = jnp.concatenate([w_q, w_ckv, w_u, w_v, w_kpe], axis=1).astype(BF16)
    wuq = _pad_heads(w_uq, H_A, QK_DIM).astype(BF16)
    ukv = w_ukv.reshape(KV_LORA, H_A, NOPE + V_A)
    wuk = _pad_heads(ukv[:, :, :NOPE].reshape(KV_LORA, H_A * NOPE), H_A, NOPE)
    wuv = _pad_heads(ukv[:, :, NOPE:].reshape(KV_LORA, H_A * V_A), H_A, V_A)
    wukv = jnp.concatenate([wuk, wuv], axis=1).astype(BF16)
    qgain = _row128(q_norm * (QK_DIM ** -0.5))
    kgain = _row128(k_norm)
    bias = b_s.reshape(G_B // 2, 2, CHUNK_B)
    bias = jnp.concatenate([jnp.broadcast_to(bias[:, 0, :, None], (G_B // 2, CHUNK_B, C_B)),
                            jnp.broadcast_to(bias[:, 1, :, None], (G_B // 2, CHUNK_B, C_B))], axis=-1)
    rc, rs1, rs2 = _rope_tables()
    rope_spec = pl.BlockSpec((TM, LANES), lambda i: (_rope_blk(i), 0))
    width = H_A * HP
    q, k, v, ob, ckv, kpe = pl.pallas_call(
        _even_in_kernel,
        grid=(NBLK,),
        in_specs=[
            _TOK(D), _MOD, _full((1, D)), _full((D, _WIN_N)), _full((1, Q_LORA)), _full((Q_LORA, width)),
            _full((1, LANES)), _full((1, KV_LORA)), _full((KV_LORA, 2 * width)), _full((1, LANES)),
            _full((1, W_B)), _full((G_B // 2, 2 * CHUNK_B, CHUNK_B)), _full((G_B // 2, CHUNK_B, LANES)),
            rope_spec, rope_spec, rope_spec,
        ],
        out_specs=[_TOK(width), _TOK(width), _TOK(width), _TOK(W_B), _TOK(KV_LORA), _TOK(ROPE)],
        out_shape=[
            jax.ShapeDtypeStruct((T, width), BF16), jax.ShapeDtypeStruct((T, width), BF16),
            jax.ShapeDtypeStruct((T, width), BF16), jax.ShapeDtypeStruct((T, W_B), BF16),
            jax.ShapeDtypeStruct((T, KV_LORA), F32), jax.ShapeDtypeStruct((T, ROPE), F32),
        ],
        compiler_params=_params(("arbitrary",)),
        name="even_in",
    )(x, mod, nmix.reshape(1, D), win, q_a_norm.reshape(1, Q_LORA), wuq, qgain, kv_a_norm.reshape(1, KV_LORA),
      wukv, kgain, v_norm.reshape(1, W_B), w_s.astype(BF16).reshape(G_B // 2, 2 * CHUNK_B, CHUNK_B), bias, rc, rs1,
      rs2)

    n_ctx = N_SAMPLE_SEQ * PAST_LEN
    kpe_ctx = jnp.pad(cache_kpe.reshape(n_ctx, ROPE), ((0, 0), (NOPE, LANES - QK_DIM)))
    k_ctx, v_ctx = pl.pallas_call(
        _ctx_kv_kernel,
        grid=(n_ctx // TM,),
        in_specs=[_TOK(KV_LORA), _TOK(LANES), _full((KV_LORA, 2 * width)), _full((1, LANES))],
        out_specs=[_TOK(width), _TOK(width)],
        out_shape=[jax.ShapeDtypeStruct((n_ctx, width), BF16)] * 2,
        compiler_params=_params(("arbitrary",)),
        name="ctx_kv",
    )(cache_ckv.reshape(n_ctx, KV_LORA), kpe_ctx, wukv, kgain)

    oa = _attention(q, k, v, k_ctx, v_ctx)
    woa = jnp.pad(w_out[:H_A * V_A].reshape(H_A, V_A, D), ((0, 0), (0, HP - V_A), (0, 0))).reshape(width, D)
    return oa, ob, woa.astype(BF16), w_out[H_A * V_A:].astype(BF16), ckv, kpe


def kernel(x_prompt, x_sample, cache_mla_ckv, cache_mla_kpe, state_gla_fwd, state_gla_bwd, c, c_ctx, ada_w, ada_b,
           norm_mix, norm_ffn, even_w_in, mla_q_a_norm, mla_w_uq, mla_q_norm, mla_kv_a_norm, mla_w_ukv, mla_k_norm,
           cmlp_v_norm, cmlp_w_s, cmlp_b_s, even_w_out, odd_w_in, gla_w_gk_fwd, gla_b_gk_fwd, gla_w_gk_bwd,
           gla_b_gk_bwd, gla_o_norm, odd_w_out, moe_w_router, moe_b_router, moe_w_gu, moe_b_gu, moe_w_down,
           moe_b_down):
    x0 = jnp.concatenate([x_sample.reshape(N_SAMPLE, D), x_prompt.reshape(N_PROMPT, D)], axis=0)
    cond8 = jnp.concatenate([c_ctx[None, :], c, jnp.zeros((SUBLANES - 1 - N_SAMPLE_SEQ, D), F32)], axis=0)
    mods = _adaln(cond8, ada_w, ada_b)
    wr = jnp.pad(moe_w_router, ((0, 0), (0, 0), (0, LANES - N_EXPERTS)))
    wr_hi = wr.astype(BF16)
    wr = jnp.concatenate([wr_hi, (wr - wr_hi.astype(F32)).astype(BF16)], axis=-1)
    br = jnp.pad(moe_b_router, ((0, 0), (0, LANES - N_EXPERTS))).reshape(2, 1, LANES)

    oa, ob, woa, wob, ckv, kpe = _even_layer(
        x0, mods[0], norm_mix[0], even_w_in[0], mla_q_a_norm[0], mla_w_uq[0], mla_q_norm[0], mla_kv_a_norm[0],
        mla_w_ukv[0], mla_k_norm[0], cmlp_v_norm[0], cmlp_w_s[0], cmlp_b_s[0], even_w_out[0],
        cache_mla_ckv[:, 0], cache_mla_kpe[:, 0])
    width = H_A * HP
    x1, h2, meta, cnt = pl.pallas_call(
        _even_out_kernel,
        grid=(NBLK,),
        in_specs=[_TOK(width), _TOK(W_B), _TOK(D), _MOD, _full((width, D)), _full((W_B, D)), _full((1, D)),
                  _full((D, 2 * LANES)), _full((1, LANES))],
        out_specs=_PROLOGUE_OUT_SPECS,
        out_shape=_PROLOGUE_OUT_SHAPE,
        compiler_params=_params(("arbitrary",)),
        name="even_out",
    )(oa, ob, x0, mods[0], woa, wob, norm_ffn[0].reshape(1, D), wr[0], br[0])
    x2 = _moe(0, x1, h2, meta, cnt, mods[0], moe_w_gu, moe_b_gu, moe_w_down, moe_b_down)

    w_in = odd_w_in[0]
    win = jnp.concatenate([w_in, jnp.zeros((D, LANES - 2 * GATE_RANK), F32)], axis=1).astype(BF16)
    wgk = jnp.zeros((LANES, 2 * _HK), F32)
    wgk = wgk.at[:GATE_RANK, :_HK].set(gla_w_gk_fwd[0]).at[GATE_RANK:2 * GATE_RANK, _HK:].set(gla_w_gk_bwd[0])
    bgk = jnp.concatenate([gla_b_gk_fwd[0], gla_b_gk_bwd[0]]).reshape(1, 2 * _HK)
    q, k, v, g, la = pl.pallas_call(
        _odd_in_kernel,
        grid=(NBLK,),
        in_specs=[_TOK(D), _MOD, _full((1, D)), _full((D, _ODD_MAIN + LANES)), _full((LANES, 2 * _HK)),
                  _full((1, 2 * _HK))],
        out_specs=[_TOK(_HK), _TOK(_HK), _TOK(_HV), _TOK(_HV), _TOK(2 * _HK)],
        out_shape=[jax.ShapeDtypeStruct((T, _HK), F32), jax.ShapeDtypeStruct((T, _HK), F32),
                   jax.ShapeDtypeStruct((T, _HV), F32), jax.ShapeDtypeStruct((T, _HV), F32),
                   jax.ShapeDtypeStruct((T, 2 * _HK), F32)],
        compiler_params=_params(("arbitrary",)),
        name="odd_in",
    )(x2, mods[1], norm_mix[1].reshape(1, D), win, wgk.astype(BF16), bgk)

    of, obk, st_f, st_b = _gla_call(False, N_PROMPT_SEQ, PROMPT_LEN, N_SAMPLE, q, k, v, la)
    s0f = state_gla_fwd[:, 0].transpose(0, 1, 3, 2)
    s0b = state_gla_bwd[:, 0].transpose(0, 1, 3, 2)
    of, obk = _gla_call(True, N_SAMPLE_SEQ, SAMPLE_LEN, 0, q, k, v, la, of, obk, s0f, s0b)

    x3, h2, meta, cnt = pl.pallas_call(
        _odd_out_kernel,
        grid=(NBLK,),
        in_specs=[_TOK(_HV), _TOK(_HV), _TOK(_HV), _TOK(D), _MOD, _full((1, DV_C)), _full((_HV, D)),
                  _full((1, D)), _full((D, 2 * LANES)), _full((1, LANES))],
        out_specs=_PROLOGUE_OUT_SPECS,
        out_shape=_PROLOGUE_OUT_SHAPE,
        compiler_params=_params(("arbitrary",)),
        name="odd_out",
    )(of, obk, g, x2, mods[1], gla_o_norm[0].reshape(1, DV_C), odd_w_out[0].astype(BF16),
      norm_ffn[1].reshape(1, D), wr[1], br[1])
    x4 = _moe(1, x3, h2, meta, cnt, mods[1], moe_w_gu, moe_b_gu, moe_w_down, moe_b_down)

    y_sample = x4[:N_SAMPLE].reshape(N_SAMPLE_SEQ, SAMPLE_LEN, D)
    y_prompt = x4[N_SAMPLE:].reshape(N_PROMPT_SEQ, PROMPT_LEN, D)
    new_ckv = ckv[N_SAMPLE:].reshape(N_PROMPT_SEQ, 1, PROMPT_LEN, KV_LORA)
    new_kpe = kpe[N_SAMPLE:].reshape(N_PROMPT_SEQ, 1, PROMPT_LEN, ROPE)
    new_fwd = st_f.transpose(0, 1, 3, 2)[:, None]
    new_bwd = st_b.transpose(0, 1, 3, 2)[:, None]
    return (y_prompt, y_sample, new_ckv, new_kpe, new_fwd, new_bwd)
```

```python
import functools

import numpy as np
import jax
import jax.numpy as jnp
from jax import lax
from jax.experimental import pallas as pl
from jax.experimental.pallas import tpu as pltpu

F32 = jnp.float32
BF16 = jnp.bfloat16

D = 1024
N_PROMPT_SEQ, PROMPT_LEN = 16, 256
N_SAMPLE_SEQ, SAMPLE_LEN = 4, 2048
PAST_LEN = 512
N_PROMPT = N_PROMPT_SEQ * PROMPT_LEN
N_SAMPLE = N_SAMPLE_SEQ * SAMPLE_LEN
T = N_PROMPT + N_SAMPLE
EPS = 1e-6
GRID_W = 64
H_A, Q_LORA, KV_LORA, NOPE, ROPE, V_A = 8, 512, 256, 64, 32, 64
QK_DIM = NOPE + ROPE
G_B, C_B, W_B, CHUNK_B = 8, 64, 512, 128
H_C, DK_C, DV_C, GATE_RANK, GATE_TAU, GLA_CHUNK = 4, 128, 256, 16, 16.0, 64
N_EXPERTS, TOP_K, D_FF = 32, 4, 1024
SWIGLU_LIMIT, SWIGLU_ALPHA = 7.0, 1.702
ROPE_THETA = 10000.0
LOG2_E = 1.4426950408889634

LANES = 128
SUBLANES = 8
VMEM_LIMIT = 56 * 1024 * 1024

TM = 256
NBLK = T // TM
SAMPLE_BLKS = N_SAMPLE // TM
BLKS_PER_SAMPLE_SEQ = SAMPLE_LEN // TM
GLA_NS = 4
ETILE = 512
TMD = 2 * TM
NBD = T // TMD
NBD_SAMPLE = N_SAMPLE // TMD
CH = 2 * SUBLANES
_MAX_LOCAL = TMD * TOP_K + N_EXPERTS * (CH - 1)
ROWS_L = -(-_MAX_LOCAL // LANES) * LANES
N_ETILES = -(-NBD * _MAX_LOCAL // ETILE) + N_EXPERTS
P_ROWS = N_ETILES * ETILE
HP = LANES


def _cond_row(i):
    return jnp.where(i < SAMPLE_BLKS, 1 + i // BLKS_PER_SAMPLE_SEQ, 0)


def _rope_blk(i):
    return jnp.where(i < SAMPLE_BLKS, 1 + i % BLKS_PER_SAMPLE_SEQ, 0)


def _rms(x):
    return x * lax.rsqrt(jnp.mean(x * x, axis=-1, keepdims=True) + EPS)


def _gelu(x):
    return 0.5 * x * (1.0 + jnp.tanh(0.7978845608028654 * (x + 0.044715 * (x * x * x))))


def _silu(x):
    return x * jax.nn.sigmoid(x)


def _dot(a, b):
    return jnp.dot(a, b, preferred_element_type=F32)


def _dot_nt(a, b):
    return lax.dot_general(a, b, (((1,), (1,)), ((), ())), preferred_element_type=F32)


def _dot_tn(a, b):
    return lax.dot_general(a, b, (((0,), (0,)), ((), ())), preferred_element_type=F32)


def _params(sem, vmem=VMEM_LIMIT):
    return pltpu.CompilerParams(dimension_semantics=sem, vmem_limit_bytes=vmem)


def _full(shape):
    nd = len(shape)
    return pl.BlockSpec(shape, lambda *_: (0,) * nd)


ADA_TN = 1536


def _adaln_kernel(c_ref, w_ref, b_ref, o_ref):
    s = _silu(c_ref[...]).astype(BF16)
    o_ref[0] = _dot(s, w_ref[0].astype(BF16)) + b_ref[0]


def _adaln(cond8, ada_w, ada_b):
    depth = ada_w.shape[0]
    n = ada_w.shape[2]
    out = pl.pallas_call(
        _adaln_kernel,
        grid=(depth, n // ADA_TN),
        in_specs=[
            pl.BlockSpec((SUBLANES, D), lambda l, j: (0, 0)),
            pl.BlockSpec((1, D, ADA_TN), lambda l, j: (l, 0, j)),
            pl.BlockSpec((1, 1, ADA_TN), lambda l, j: (l, 0, j)),
        ],
        out_specs=pl.BlockSpec((1, SUBLANES, ADA_TN), lambda l, j: (l, 0, j)),
        out_shape=jax.ShapeDtypeStruct((depth, SUBLANES, n), F32),
        compiler_params=_params(("arbitrary", "arbitrary")),
        name="adaln",
    )(cond8, ada_w, ada_b.reshape(depth, 1, n))
    return out.reshape(depth, SUBLANES, 6, D)


_QC0, _CKV0, _U0, _V0, _KPE0, _WIN_N = 0, 512, 768, 1280, 1792, 1920


def _rope(y, c, s1, s2):
    return y * c + pltpu.roll(y, LANES - 8, 1) * s1 + pltpu.roll(y, 8, 1) * s2


def _k_heads(k_raw, kpe128, kp_rot, kgain, k_ref):
    sskpe = jnp.sum(kpe128 * kpe128, axis=-1, keepdims=True)
    for h in range(H_A):
        kb = k_raw[:, h * HP:(h + 1) * HP]
        r = lax.rsqrt((jnp.sum(kb * kb, axis=-1, keepdims=True) + sskpe) * (1.0 / QK_DIM) + EPS)
        k_ref[:, h * HP:(h + 1) * HP] = ((kb * kgain + kp_rot) * r).astype(BF16)


def _input_rows(xs_ref, xp_ref):
    return jnp.where(pl.program_id(0) < SAMPLE_BLKS, xs_ref[...], xp_ref[...])


_X_SAMPLE = pl.BlockSpec((TM, D), lambda i, *_: (jnp.minimum(i, SAMPLE_BLKS - 1), 0))
_X_PROMPT = pl.BlockSpec((TM, D), lambda i, *_: (jnp.maximum(i - SAMPLE_BLKS, 0), 0))


def _even_in_kernel(xs_ref, xp_ref, mod_ref, nmix_ref, win_ref, qan_ref, wuq_ref, qgain_ref, kvan_ref, wukv_ref,
                    kgain_ref, vnorm_ref, ws_ref, bs_ref, rc_ref, rs1_ref, rs2_ref,
                    q_ref, k_ref, v_ref, ob_ref, ckv_ref, kpe_ref):
    m = mod_ref[0]
    h = _rms(_input_rows(xs_ref, xp_ref)) * nmix_ref[...] * (1.0 + m[1:2]) + m[0:1]
    a = _dot(h.astype(BF16), win_ref[...])
    qc = a[:, _QC0:_CKV0]
    ckv = a[:, _CKV0:_U0]
    u = a[:, _U0:_V0]
    vv = a[:, _V0:_KPE0]
    kpe128 = a[:, _KPE0:_WIN_N]

    ckv_n = _rms(ckv) * kvan_ref[...]
    ckv_ref[...] = ckv_n
    kpe_ref[...] = kpe128[:, NOPE:QK_DIM]

    rc, rs1, rs2 = rc_ref[...], rs1_ref[...], rs2_ref[...]
    qn = (_rms(qc) * qan_ref[...]).astype(BF16)
    qr = _dot(qn, wuq_ref[...])
    qgain = qgain_ref[...]
    for hh in range(H_A):
        blk = qr[:, hh * HP:(hh + 1) * HP]
        r = lax.rsqrt(jnp.sum(blk * blk, axis=-1, keepdims=True) * (1.0 / QK_DIM) + EPS)
        q_ref[:, hh * HP:(hh + 1) * HP] = _rope(blk * r * qgain, rc, rs1, rs2).astype(BF16)

    kv = _dot(ckv_n.astype(BF16), wukv_ref[...])
    v_ref[...] = kv[:, H_A * HP:].astype(BF16)
    kgain = kgain_ref[...]
    kp_rot = _rope(kpe128 * kgain, rc, rs1, rs2)
    _k_heads(kv[:, :H_A * HP], kpe128, kp_rot, kgain, k_ref)

    ug = _gelu(u)
    vn = (_rms(_gelu(vv)) * vnorm_ref[...]).astype(BF16)
    low = lax.broadcasted_iota(jnp.int32, (CHUNK_B, LANES), 1) < C_B
    for c in range(TM // CHUNK_B):
        rows = slice(c * CHUNK_B, (c + 1) * CHUNK_B)
        for p in range(G_B // 2):
            cols = slice(p * LANES, (p + 1) * LANES)
            blk = vn[rows, cols]
            both = _dot(ws_ref[p], blk)
            mixed = jnp.where(low, both[:CHUNK_B], both[CHUNK_B:]) + bs_ref[p]
            ob_ref[rows, cols] = (ug[rows, cols] * mixed).astype(BF16)


def _ctx_kv_kernel(ckv_ref, kpe_ref, wukv_ref, kgain_ref, k_ref, v_ref):
    kv = _dot(ckv_ref[...].astype(BF16), wukv_ref[...])
    v_ref[...] = kv[:, H_A * HP:].astype(BF16)
    kgain = kgain_ref[...]
    kpe128 = kpe_ref[...]
    _k_heads(kv[:, :H_A * HP], kpe128, kpe128 * kgain, kgain, k_ref)


def _attn_self_kernel(q_ref, k_ref, v_ref, o_ref):
    for h in range(H_A):
        cols = slice(h * HP, (h + 1) * HP)
        s = _dot_nt(q_ref[:, cols], k_ref[:, cols])
        p = jnp.exp2(s - jnp.max(s, axis=-1, keepdims=True))
        inv = 1.0 / jnp.sum(p, axis=-1, keepdims=True)
        o_ref[:, cols] = (_dot(p.astype(BF16), v_ref[:, cols]) * inv).astype(BF16)


def _attn_ctx_kernel(q_ref, k_ref, v_ref, kc_ref, vc_ref, prev_ref, o_ref):
    del prev_ref
    for h in range(H_A):
        cols = slice(h * HP, (h + 1) * HP)
        q = q_ref[:, cols]
        s1 = _dot_nt(q, k_ref[:, cols])
        s2 = _dot_nt(q, kc_ref[:, cols])
        mx = jnp.maximum(jnp.max(s1, axis=-1, keepdims=True), jnp.max(s2, axis=-1, keepdims=True))
        p1 = jnp.exp2(s1 - mx)
        p2 = jnp.exp2(s2 - mx)
        inv = 1.0 / (jnp.sum(p1, axis=-1, keepdims=True) + jnp.sum(p2, axis=-1, keepdims=True))
        o = _dot(p1.astype(BF16), v_ref[:, cols]) + _dot(p2.astype(BF16), vc_ref[:, cols])
        o_ref[:, cols] = (o * inv).astype(BF16)


def _attention(q, k, v, k_ctx, v_ctx):
    width = H_A * HP
    first = N_SAMPLE // PROMPT_LEN
    o = pl.pallas_call(
        _attn_self_kernel,
        grid=(N_PROMPT_SEQ,),
        in_specs=[pl.BlockSpec((PROMPT_LEN, width), lambda i: (first + i, 0))] * 3,
        out_specs=pl.BlockSpec((PROMPT_LEN, width), lambda i: (first + i, 0)),
        out_shape=jax.ShapeDtypeStruct((T, width), BF16),
        compiler_params=_params(("arbitrary",)),
        name="attn_prompt",
    )(q, k, v)
    qblk = lambda b, j: (b * BLKS_PER_SAMPLE_SEQ + j, 0)
    return pl.pallas_call(
        _attn_ctx_kernel,
        grid=(N_SAMPLE_SEQ, BLKS_PER_SAMPLE_SEQ),
        in_specs=[
            pl.BlockSpec((TM, width), qblk),
            pl.BlockSpec((SAMPLE_LEN, width), lambda b, j: (b, 0)),
            pl.BlockSpec((SAMPLE_LEN, width), lambda b, j: (b, 0)),
            pl.BlockSpec((PAST_LEN, width), lambda b, j: (b, 0)),
            pl.BlockSpec((PAST_LEN, width), lambda b, j: (b, 0)),
            pl.BlockSpec(memory_space=pl.ANY),
        ],
        out_specs=pl.BlockSpec((TM, width), qblk),
        out_shape=jax.ShapeDtypeStruct((T, width), BF16),
        input_output_aliases={5: 0},
        compiler_params=_params(("arbitrary", "arbitrary")),
        name="attn_sample",
    )(q, k, v, k_ctx, v_ctx, o)


_META_IDX, _META_RANK, _META_W = 0, TOP_K, 2 * TOP_K


def _moe_prologue(x1, m, nffn_ref, wr_ref, br_ref, x1_ref, h2_ref, meta_ref, cnt_ref):
    x1_ref[...] = x1
    h2 = _rms(x1) * nffn_ref[...] * (1.0 + m[4:5]) + m[3:4]
    h2_ref[...] = h2.astype(BF16)
    lane = lax.broadcasted_iota(jnp.int32, (TM, LANES), 1)
    lanef = lane.astype(F32)
    h_hi = h2.astype(BF16)
    h_lo = (h2 - h_hi.astype(F32)).astype(BF16)
    r = _dot(h_hi, wr_ref[...])
    logits = r[:, :LANES] + r[:, LANES:] + _dot(h_lo, wr_ref[:, :LANES]) + br_ref[...]
    work = jnp.where(lane < N_EXPERTS, logits, -jnp.inf)
    hots, vals = [], []
    for _ in range(TOP_K):
        mx = jnp.max(work, axis=-1, keepdims=True)
        idx = jnp.min(jnp.where(work == mx, lanef, float(LANES)), axis=-1, keepdims=True)
        hot = lanef == idx
        work = jnp.where(hot, -jnp.inf, work)
        hots.append((hot, idx))
        vals.append(mx)
    es = [jnp.exp(v - vals[0]) for v in vals]
    inv = 1.0 / (es[0] + es[1] + es[2] + es[3])
    sel = jnp.zeros((TM, LANES), F32)
    for hot, _ in hots:
        sel = jnp.where(hot, 1.0, sel)
    row = lax.broadcasted_iota(jnp.int32, (TM, TM), 0)
    col = lax.broadcasted_iota(jnp.int32, (TM, TM), 1)
    strict = jnp.where(row > col, 1.0, 0.0).astype(BF16)
    before = _dot(strict, sel.astype(BF16))
    meta = jnp.zeros((TM, LANES), F32)
    for kk, (hot, idx) in enumerate(hots):
        rank = jnp.sum(jnp.where(hot, before, 0.0), axis=-1, keepdims=True)
        meta = jnp.where(lane == _META_IDX + kk, idx, meta)
        meta = jnp.where(lane == _META_RANK + kk, rank, meta)
        meta = jnp.where(lane == _META_W + kk, es[kk] * inv, meta)
    meta_ref[...] = meta
    cnt_ref[0] = jnp.broadcast_to(jnp.sum(sel, axis=0, keepdims=True), (SUBLANES, LANES))


def _even_out_kernel(oa_ref, ob_ref, xs_ref, xp_ref, mod_ref, woa_ref, wob_ref, nffn_ref, wr_ref, br_ref,
                     x1_ref, h2_ref, meta_ref, cnt_ref):
    m = mod_ref[0]
    out = _dot(oa_ref[...], woa_ref[...]) + _dot(ob_ref[...], wob_ref[...])
    x1 = _input_rows(xs_ref, xp_ref) + m[2:3] * out
    _moe_prologue(x1, m, nffn_ref, wr_ref, br_ref, x1_ref, h2_ref, meta_ref, cnt_ref)


def _odd_out_kernel(of_ref, ob_ref, g_ref, x_ref, mod_ref, onorm_ref, wo_ref, nffn_ref, wr_ref, br_ref,
                    x1_ref, h2_ref, meta_ref, cnt_ref):
    m = mod_ref[0]
    onorm = onorm_ref[...]
    parts = []
    for h in range(H_C):
        cols = slice(h * DV_C, (h + 1) * DV_C)
        o = of_ref[:, cols] + ob_ref[:, cols]
        parts.append((_rms(o) * onorm * _silu(g_ref[:, cols])).astype(BF16))
    out = _dot(jnp.concatenate(parts, axis=-1), wo_ref[...])
    x1 = x_ref[...] + m[2:3] * out
    _moe_prologue(x1, m, nffn_ref, wr_ref, br_ref, x1_ref, h2_ref, meta_ref, cnt_ref)


_TOK = lambda w: pl.BlockSpec((TM, w), lambda i, *_: (i, 0))
_MOD = pl.BlockSpec((1, 6, D), lambda i, *_: (_cond_row(i), 0, 0))
_TILE_ROW = pl.BlockSpec((1, SUBLANES, LANES), lambda i, *_: (i, 0, 0))

_PROLOGUE_OUT_SPECS = [_TOK(D), _TOK(D), _TOK(LANES), _TILE_ROW]
_PROLOGUE_OUT_SHAPE = [
    jax.ShapeDtypeStruct((T, D), F32),
    jax.ShapeDtypeStruct((T, D), BF16),
    jax.ShapeDtypeStruct((T, LANES), F32),
    jax.ShapeDtypeStruct((NBLK, SUBLANES, LANES), F32),
]


def _local_positions(meta, seg_rows):
    lanef = lax.broadcasted_iota(jnp.int32, (TM, LANES), 1).astype(F32)
    pos = []
    for k in range(TOP_K):
        halves = []
        for half in range(TMD // TM):
            m = meta[half * TM:(half + 1) * TM]
            hot = lanef == m[:, _META_IDX + k:_META_IDX + k + 1]
            start = jnp.sum(jnp.where(hot, seg_rows[half:half + 1], 0.0), axis=-1, keepdims=True)
            halves.append(start + m[:, _META_RANK + k:_META_RANK + k + 1])
        pos.append(jnp.concatenate(halves, axis=0))
    return pos


def _segment_copies(t, base_ref, seg_ref, nch_ref, make):
    def per_expert(e, carry):
        q = t * N_EXPERTS + e
        local0, global0 = seg_ref[q], base_ref[q]

        def one(j, c):
            make(pl.multiple_of(local0 + j * CH, CH), pl.multiple_of(global0 + j * CH, CH)).start()
            return c

        return lax.fori_loop(0, nch_ref[q], one, carry)

    lax.fori_loop(0, N_EXPERTS, per_expert, 0)


def _drain(count, chunk_copy):
    def one(j, c):
        chunk_copy.wait()
        return c

    lax.fori_loop(0, count, one, 0)


def _dispatch_kernel(base_ref, seg_ref, nch_ref, ntile_ref, tail0_ref, tailn_ref, h_ref, meta_ref, segrow_ref,
                     xs_hbm, buf, zbuf, sem):
    i = pl.program_id(0)
    slot = i % 2
    chunk = lambda s: pltpu.make_async_copy(buf.at[s, pl.ds(0, CH)], xs_hbm.at[pl.ds(0, CH)], sem.at[s])

    @pl.when(i == 0)
    def _():
        zbuf[...] = jnp.zeros_like(zbuf)

        def per_expert(e, carry):
            def one(j, c):
                pltpu.make_async_copy(zbuf, xs_hbm.at[pl.ds(pl.multiple_of(tail0_ref[e] + j * CH, CH), CH)],
                                      sem.at[2]).start()
                return c

            lax.fori_loop(0, tailn_ref[e], one, 0)
            return carry + tailn_ref[e]

        total = lax.fori_loop(0, N_EXPERTS, per_expert, 0)
        _drain(total, pltpu.make_async_copy(zbuf, xs_hbm.at[pl.ds(0, CH)], sem.at[2]))

    pos = _local_positions(meta_ref[...], segrow_ref[0])
    riota = lax.broadcasted_iota(jnp.int32, (TMD, ROWS_L), 1).astype(F32)
    pt = jnp.zeros((TMD, ROWS_L), F32)
    for p in pos:
        pt = jnp.where(riota == p, 1.0, pt)
    buf[slot] = _dot_tn(pt.astype(BF16), h_ref[...]).astype(BF16)

    make = lambda s, d: pltpu.make_async_copy(buf.at[slot, pl.ds(s, CH)], xs_hbm.at[pl.ds(d, CH)], sem.at[slot])
    _segment_copies(i, base_ref, seg_ref, nch_ref, make)

    @pl.when(i > 0)
    def _():
        _drain(ntile_ref[i - 1], chunk(1 - slot))

    @pl.when(i == NBD - 1)
    def _():
        _drain(ntile_ref[i], chunk(slot))


_TOKD = lambda w: pl.BlockSpec((TMD, w), lambda i, *_: (i, 0))
_SEGROWS = pl.BlockSpec((1, TMD // TM, LANES), lambda i, *_: (i, 0, 0))


def _dispatch(plan, h2, meta):
    grid_spec = pltpu.PrefetchScalarGridSpec(
        num_scalar_prefetch=6,
        grid=(NBD,),
        in_specs=[_TOKD(D), _TOKD(LANES), _SEGROWS],
        out_specs=pl.BlockSpec(memory_space=pl.ANY),
        scratch_shapes=[pltpu.VMEM((2, ROWS_L, D), BF16), pltpu.VMEM((CH, D), BF16),
                        pltpu.SemaphoreType.DMA((3,))],
    )
    return pl.pallas_call(
        _dispatch_kernel,
        grid_spec=grid_spec,
        out_shape=jax.ShapeDtypeStruct((P_ROWS, D), BF16),
        compiler_params=_params(("arbitrary",)),
        name="moe_dispatch",
    )(plan["base"], plan["seg"], plan["nch"], plan["ntile"], plan["tail0"], plan["tailn"], h2, meta, plan["segrow"])


def _expert_kernel(te_ref, tfirst_ref, tvalid_ref, xblk_ref, x_ref, wgu_ref, bgu_ref, wd_ref, bd_ref, y_ref,
                   wgu_bf, wd_bf):
    del xblk_ref
    i = pl.program_id(0)

    @pl.when(tfirst_ref[i] == 1)
    def _():
        wgu_bf[...] = wgu_ref[...].astype(BF16)
        wd_bf[...] = wd_ref[...].astype(BF16)

    def ffn(x):
        a = _dot(x, wgu_bf[...]) + bgu_ref[...]
        glu = jnp.minimum(a[:, :D_FF], SWIGLU_LIMIT)
        lin = jnp.clip(a[:, D_FF:], -SWIGLU_LIMIT, SWIGLU_LIMIT)
        act = (glu * jax.nn.sigmoid(SWIGLU_ALPHA * glu)) * (lin + 1.0)
        return (_dot(act.astype(BF16), wd_bf[...]) + bd_ref[...]).astype(BF16)

    half = ETILE // 2

    @pl.when(tvalid_ref[i] == 2)
    def _():
        y_ref[...] = ffn(x_ref[...])

    @pl.when(tvalid_ref[i] == 1)
    def _():
        y_ref[:half] = ffn(x_ref[:half])
        y_ref[half:] = jnp.zeros((ETILE - half, D), BF16)

    @pl.when(tvalid_ref[i] == 0)
    def _():
        y_ref[...] = jnp.zeros_like(y_ref)


def _experts(layer, plan, xs, w_gu, b_gu, w_down, b_down):
    depth = w_gu.shape[0]
    e_of = lambda i, te, *_: (layer, te[i], 0, 0)
    grid_spec = pltpu.PrefetchScalarGridSpec(
        num_scalar_prefetch=4,
        grid=(N_ETILES,),
        in_specs=[
            pl.BlockSpec((ETILE, D), lambda i, te, tf, tv, xb: (xb[i], 0)),
            pl.BlockSpec((None, None, D, 2 * D_FF), e_of),
            pl.BlockSpec((None, None, 1, 2 * D_FF), e_of),
            pl.BlockSpec((None, None, D_FF, D), e_of),
            pl.BlockSpec((None, None, 1, D), e_of),
        ],
        out_specs=pl.BlockSpec((ETILE, D), lambda i, *_: (i, 0)),
        scratch_shapes=[pltpu.VMEM((D, 2 * D_FF), BF16), pltpu.VMEM((D_FF, D), BF16)],
    )
    return pl.pallas_call(
        _expert_kernel,
        grid_spec=grid_spec,
        out_shape=jax.ShapeDtypeStruct((P_ROWS, D), BF16),
        compiler_params=_params(("arbitrary",)),
        name="moe_experts",
    )(plan["te"], plan["tfirst"], plan["tvalid"], plan["xblk"], xs, w_gu,
      b_gu.reshape(depth, N_EXPERTS, 1, 2 * D_FF), w_down, b_down.reshape(depth, N_EXPERTS, 1, D))


def _combine_kernel(split, base_ref, seg_ref, nch_ref, ntile_ref, y_hbm, x1_ref, meta_ref, segrow_ref, mod_ref,
                    *rest):
    *o_refs, ybuf, sem = rest
    i = pl.program_id(0)
    slot = i % 2

    def fetch(t, s):
        make = lambda loc, glob: pltpu.make_async_copy(y_hbm.at[pl.ds(glob, CH)], ybuf.at[s, pl.ds(loc, CH)],
                                                       sem.at[s])
        _segment_copies(t, base_ref, seg_ref, nch_ref, make)

    @pl.when(i == 0)
    def _():
        ybuf[...] = jnp.zeros_like(ybuf)
        fetch(i, slot)

    @pl.when(i + 1 < NBD)
    def _():
        fetch(i + 1, 1 - slot)

    _drain(ntile_ref[i], pltpu.make_async_copy(y_hbm.at[pl.ds(0, CH)], ybuf.at[slot, pl.ds(0, CH)], sem.at[slot]))

    meta = meta_ref[...]
    pos = _local_positions(meta, segrow_ref[0])
    riota = lax.broadcasted_iota(jnp.int32, (TMD, ROWS_L), 1).astype(F32)
    gates = jnp.zeros((TMD, ROWS_L), F32)
    for k, p in enumerate(pos):
        gates = jnp.where(riota == p, meta[:, _META_W + k:_META_W + k + 1], gates)
    acc = _dot(gates.astype(BF16), ybuf[slot])
    out = x1_ref[...] + mod_ref[0][5:6] * acc
    if split:
        os_ref, op_ref = o_refs

        @pl.when(i < NBD_SAMPLE)
        def _():
            os_ref[...] = out

        @pl.when(i >= NBD_SAMPLE)
        def _():
            op_ref[...] = out
    else:
        o_refs[0][...] = out


def _combine(plan, y, x1, meta, mod, split):
    if split:
        out_specs = [pl.BlockSpec((TMD, D), lambda i, *_: (jnp.minimum(i, NBD_SAMPLE - 1), 0)),
                     pl.BlockSpec((TMD, D), lambda i, *_: (jnp.maximum(i - NBD_SAMPLE, 0), 0))]
        out_shape = [jax.ShapeDtypeStruct((N_SAMPLE, D), F32), jax.ShapeDtypeStruct((N_PROMPT, D), F32)]
    else:
        out_specs, out_shape = _TOKD(D), jax.ShapeDtypeStruct((T, D), F32)
    grid_spec = pltpu.PrefetchScalarGridSpec(
        num_scalar_prefetch=4,
        grid=(NBD,),
        in_specs=[pl.BlockSpec(memory_space=pl.ANY), _TOKD(D), _TOKD(LANES), _SEGROWS,
                  pl.BlockSpec((1, 6, D), lambda i, *_: (_cond_row(i * (TMD // TM)), 0, 0))],
        out_specs=out_specs,
        scratch_shapes=[pltpu.VMEM((2, ROWS_L, D), BF16), pltpu.SemaphoreType.DMA((2,))],
    )
    return pl.pallas_call(
        functools.partial(_combine_kernel, split),
        grid_spec=grid_spec,
        out_shape=out_shape,
        compiler_params=_params(("arbitrary",)),
        name="moe_combine",
    )(plan["base"], plan["seg"], plan["nch"], plan["ntile"], y, x1, meta, plan["segrow"], mod)


def _moe_plan(cnt):
    per = TMD // TM
    cnt = cnt[:, 0, :N_EXPERTS].astype(jnp.int32).reshape(NBD, per, N_EXPERTS)
    cpad = (jnp.sum(cnt, axis=1) + CH - 1) // CH * CH
    tot = jnp.sum(cpad, axis=0)
    tiles = (tot + ETILE - 1) // ETILE
    tile_end = jnp.cumsum(tiles)
    offs = (tile_end - tiles) * ETILE
    base = offs[None, :] + jnp.cumsum(cpad, axis=0) - cpad
    seg = jnp.cumsum(cpad, axis=1) - cpad
    segrows = seg[:, None, :] + jnp.cumsum(cnt, axis=1) - cnt
    tid = jnp.arange(N_ETILES, dtype=jnp.int32)
    te = jnp.sum((tile_end[None, :] <= tid[:, None]).astype(jnp.int32), axis=1)
    used = te < N_EXPERTS
    last = jnp.max(jnp.where(tiles > 0, jnp.arange(N_EXPERTS, dtype=jnp.int32), 0))
    te = jnp.where(used, te, last)
    rows = tot[te] - (tid - (tile_end - tiles)[te]) * ETILE
    tvalid = jnp.where(used, jnp.where(rows > ETILE // 2, 2, 1), 0).astype(jnp.int32)
    return {
        "base": base.reshape(-1), "seg": seg.reshape(-1), "nch": (cpad // CH).reshape(-1),
        "ntile": jnp.sum(cpad, axis=1) // CH, "tail0": offs + tot, "tailn": (tiles * ETILE - tot) // CH,
        "segrow": jnp.pad(segrows.astype(F32), ((0, 0), (0, 0), (0, LANES - N_EXPERTS))),
        "te": te, "tvalid": tvalid, "xblk": jnp.where(used, tid, 0),
        "tfirst": jnp.concatenate([jnp.ones((1,), jnp.int32), (te[1:] != te[:-1]).astype(jnp.int32)]),
    }


def _moe(layer, x1, h2, meta, cnt, mod, w_gu, b_gu, w_down, b_down, split_out=False):
    plan = _moe_plan(cnt)
    xs = _dispatch(plan, h2, meta)
    y = _experts(layer, plan, xs, w_gu, b_gu, w_down, b_down)
    return _combine(plan, y, x1, meta, mod, split_out)


_HK, _HV = H_C * DK_C, H_C * DV_C
_ODD_MAIN = 2 * _HK + 2 * _HV


def _odd_in_kernel(x_ref, mod_ref, nmix_ref, win_ref, wgk_ref, bgk_ref, q_ref, k_ref, v_ref, g_ref, la_ref):
    m = mod_ref[0]
    h = _rms(x_ref[...]) * nmix_ref[...] * (1.0 + m[1:2]) + m[0:1]
    a = _dot(h.astype(BF16), win_ref[...])
    q_ref[...] = a[:, :_HK] * (DK_C ** -0.5)
    k_ref[...] = a[:, _HK:2 * _HK]
    v_ref[...] = a[:, 2 * _HK:2 * _HK + _HV]
    g_ref[...] = a[:, 2 * _HK + _HV:_ODD_MAIN]
    z = _dot(a[:, _ODD_MAIN:].astype(BF16), wgk_ref[...]) + bgk_ref[...]
    la_ref[...] = (jnp.minimum(z, 0.0) - jnp.log(1.0 + jnp.exp(-jnp.abs(z)))) * (1.0 / GATE_TAU)


def _gla_kernel(has_init, nchunk, *refs):
    if has_init:
        (qf, kf, vf, laf, qb, kb, vb, lab, s0f, s0b, _, _, of_ref, ob_ref, st) = refs
    else:
        (qf, kf, vf, laf, qb, kb, vb, lab, of_ref, ob_ref, sf_ref, sb_ref, st) = refs
    j = pl.program_id(1)

    @pl.when(j == 0)
    def _():
        if has_init:
            st[0] = s0f[...]
            st[1] = s0b[...]
        else:
            st[...] = jnp.zeros_like(st)

    row = lax.broadcasted_iota(jnp.int32, (GLA_CHUNK, GLA_CHUNK), 0)
    col = lax.broadcasted_iota(jnp.int32, (GLA_CHUNK, GLA_CHUNK), 1)
    for d, (q_r, k_r, v_r, la_r, o_r) in enumerate(((qf, kf, vf, laf, of_ref), (qb, kb, vb, lab, ob_ref))):
        keep = (col <= row) if d == 0 else (col >= row)
        tri = jnp.where(keep, 1.0, 0.0).astype(BF16)
        for s in range(GLA_NS):
            g = la_r[s]
            g_hi = g.astype(BF16)
            g_r = g - g_hi.astype(F32)
            g_mid = g_r.astype(BF16)
            g_lo = (g_r - g_mid.astype(F32)).astype(BF16)
            c = _dot(tri, g_hi) + _dot(tri, g_mid) + _dot(tri, g_lo)
            tot = jnp.sum(g, axis=0, keepdims=True)
            decay = jnp.exp(tot)
            e_neg = jnp.exp(-c)
            k = k_r[s]
            qe = (q_r[s] * jnp.exp(c)).astype(BF16)
            kd = (k * e_neg).astype(BF16)
            k2 = (k * (e_neg * decay)).astype(BF16)
            v = v_r[s].astype(BF16)
            for h in range(H_C):
                kc = slice(h * DK_C, (h + 1) * DK_C)
                vc = slice(h * DV_C, (h + 1) * DV_C)
                att = jnp.where(keep, _dot_nt(qe[:, kc], kd[:, kc]), 0.0)
                s_t = st[d, s, h]
                o_r[s, :, vc] = _dot_nt(qe[:, kc], s_t.astype(BF16)) + _dot(att.astype(BF16), v[:, vc])
                st[d, s, h] = s_t * decay[:, kc] + _dot_tn(v[:, vc], k2[:, kc])

    if not has_init:
        @pl.when(j == nchunk - 1)
        def _():
            sf_ref[...] = st[0]
            sb_ref[...] = st[1]


def _gla_call(has_init, nseq, seqlen, row0, q, k, v, la, prev_f=None, prev_b=None, s0f=None, s0b=None):
    nchunk = seqlen // GLA_CHUNK
    nview = T // seqlen
    g0 = row0 // seqlen // GLA_NS
    view = lambda a: a.reshape(nview, seqlen, a.shape[-1])
    fwd = lambda g, j: (g0 + g, j, 0)
    bwd = lambda g, j: (g0 + g, nchunk - 1 - j, 0)
    bwd_la = lambda g, j: (g0 + g, nchunk - 1 - j, 1)
    blk = lambda w, m: pl.BlockSpec((GLA_NS, GLA_CHUNK, w), m)
    state_spec = pl.BlockSpec((GLA_NS, H_C, DV_C, DK_C), lambda g, j: (g, 0, 0, 0))
    in_specs = [blk(_HK, fwd), blk(_HK, fwd), blk(_HV, fwd), blk(_HK, fwd),
                blk(_HK, bwd), blk(_HK, bwd), blk(_HV, bwd), blk(_HK, bwd_la)]
    args = [view(q), view(k), view(v), view(la)] * 2
    aliases = {}
    if has_init:
        in_specs += [state_spec, state_spec] + [pl.BlockSpec(memory_space=pl.ANY)] * 2
        args += [s0f, s0b, view(prev_f), view(prev_b)]
        aliases = {len(args) - 2: 0, len(args) - 1: 1}
    out_specs = [blk(_HV, fwd), blk(_HV, bwd)]
    out_shape = [jax.ShapeDtypeStruct((nview, seqlen, _HV), F32)] * 2
    if not has_init:
        out_specs += [state_spec, state_spec]
        out_shape += [jax.ShapeDtypeStruct((nseq, H_C, DV_C, DK_C), F32)] * 2
    outs = pl.pallas_call(
        functools.partial(_gla_kernel, has_init, nchunk),
        grid=(nseq // GLA_NS, nchunk),
        in_specs=in_specs,
        out_specs=out_specs,
        out_shape=out_shape,
        scratch_shapes=[pltpu.VMEM((2, GLA_NS, H_C, DV_C, DK_C), F32)],
        input_output_aliases=aliases,
        compiler_params=_params(("arbitrary", "arbitrary")),
        name="gla_sample" if has_init else "gla_prompt",
    )(*args)
    return [outs[0].reshape(T, _HV), outs[1].reshape(T, _HV)] + list(outs[2:])


def _rope_tables():
    half = ROPE // 2
    inv_freq = np.power(np.float32(ROPE_THETA), -np.arange(0, half, 2, dtype=np.float32) / np.float32(half))
    n = np.arange(SAMPLE_LEN)
    row = (n // GRID_W).astype(np.float32)
    col = (n % GRID_W).astype(np.float32)
    ang_r = (row[:, None] * inv_freq[None, :]).astype(np.float32)
    ang_c = (col[:, None] * inv_freq[None, :]).astype(np.float32)
    nf = half // 2
    c = np.ones((TM + SAMPLE_LEN, LANES), np.float32)
    s1 = np.zeros((TM + SAMPLE_LEN, LANES), np.float32)
    s2 = np.zeros((TM + SAMPLE_LEN, LANES), np.float32)
    for base, ang in ((NOPE, ang_r), (NOPE + half, ang_c)):
        c[TM:, base:base + nf] = np.cos(ang)
        c[TM:, base + nf:base + half] = np.cos(ang)
        s1[TM:, base:base + nf] = -np.sin(ang)
        s2[TM:, base + nf:base + half] = np.sin(ang)
    return jnp.asarray(c), jnp.asarray(s1), jnp.asarray(s2)


def _pad_heads(w, nheads, width, lo=0):
    k = w.shape[0]
    w = w.reshape(k, nheads, width)
    w = jnp.pad(w, ((0, 0), (0, 0), (lo, HP - lo - width)))
    return w.reshape(k, nheads * HP)


def _row128(v, lo=0):
    return jnp.pad(v, (lo, LANES - lo - v.shape[0])).reshape(1, LANES)


def _even_layer(xs, xp, mod, nmix, w_in, q_a_norm, w_uq, q_norm, kv_a_norm, w_ukv, k_norm, v_norm, w_s, b_s, w_out,
                cache_ckv, cache_kpe):
    s = np.cumsum([Q_LORA, KV_LORA, ROPE, W_B])
    w_q, w_ckv, w_kpe, w_u, w_v = (w_in[:, :s[0]], w_in[:, s[0]:s[1]], w_in[:, s[1]:s[2]], w_in[:, s[2]:s[3]],
                                   w_in[:, s[3]:])
    w_kpe = jnp.pad(w_kpe, ((0, 0), (NOPE, LANES - QK_DIM)))
    win = jnp.concatenate([w_q, w_ckv, w_u, w_v, w_kpe], axis=1).astype(BF16)
    wuq = _pad_heads(w_uq, H_A, QK_DIM).astype(BF16)
    ukv = w_ukv.reshape(KV_LORA, H_A, NOPE + V_A)
    wuk = _pad_heads(ukv[:, :, :NOPE].reshape(KV_LORA, H_A * NOPE), H_A, NOPE)
    wuv = _pad_heads(ukv[:, :, NOPE:].reshape(KV_LORA, H_A * V_A), H_A, V_A)
    wukv = jnp.concatenate([wuk, wuv], axis=1).astype(BF16)
    qgain = _row128(q_norm * (QK_DIM ** -0.5 * LOG2_E))
    kgain = _row128(k_norm)
    bias = b_s.reshape(G_B // 2, 2, CHUNK_B)
    bias = jnp.concatenate([jnp.broadcast_to(bias[:, 0, :, None], (G_B // 2, CHUNK_B, C_B)),
                            jnp.broadcast_to(bias[:, 1, :, None], (G_B // 2, CHUNK_B, C_B))], axis=-1)
    rc, rs1, rs2 = _rope_tables()
    rope_spec = pl.BlockSpec((TM, LANES), lambda i: (_rope_blk(i), 0))
    width = H_A * HP
    q, k, v, ob, ckv, kpe = pl.pallas_call(
        _even_in_kernel,
        grid=(NBLK,),
        in_specs=[
            _X_SAMPLE, _X_PROMPT, _MOD, _full((1, D)), _full((D, _WIN_N)), _full((1, Q_LORA)),
            _full((Q_LORA, width)),
            _full((1, LANES)), _full((1, KV_LORA)), _full((KV_LORA, 2 * width)), _full((1, LANES)),
            _full((1, W_B)), _full((G_B // 2, 2 * CHUNK_B, CHUNK_B)), _full((G_B // 2, CHUNK_B, LANES)),
            rope_spec, rope_spec, rope_spec,
        ],
        out_specs=[_TOK(width), _TOK(width), _TOK(width), _TOK(W_B), _TOK(KV_LORA), _TOK(ROPE)],
        out_shape=[
            jax.ShapeDtypeStruct((T, width), BF16), jax.ShapeDtypeStruct((T, width), BF16),
            jax.ShapeDtypeStruct((T, width), BF16), jax.ShapeDtypeStruct((T, W_B), BF16),
            jax.ShapeDtypeStruct((T, KV_LORA), F32), jax.ShapeDtypeStruct((T, ROPE), F32),
        ],
        compiler_params=_params(("arbitrary",)),
        name="even_in",
    )(xs, xp, mod, nmix.reshape(1, D), win, q_a_norm.reshape(1, Q_LORA), wuq, qgain, kv_a_norm.reshape(1, KV_LORA),
      wukv, kgain, v_norm.reshape(1, W_B), w_s.astype(BF16).reshape(G_B // 2, 2 * CHUNK_B, CHUNK_B), bias, rc, rs1,
      rs2)

    n_ctx = N_SAMPLE_SEQ * PAST_LEN
    kpe_ctx = jnp.pad(cache_kpe.reshape(n_ctx, ROPE), ((0, 0), (NOPE, LANES - QK_DIM)))
    k_ctx, v_ctx = pl.pallas_call(
        _ctx_kv_kernel,
        grid=(n_ctx // TM,),
        in_specs=[_TOK(KV_LORA), _TOK(LANES), _full((KV_LORA, 2 * width)), _full((1, LANES))],
        out_specs=[_TOK(width), _TOK(width)],
        out_shape=[jax.ShapeDtypeStruct((n_ctx, width), BF16)] * 2,
        compiler_params=_params(("arbitrary",)),
        name="ctx_kv",
    )(cache_ckv.reshape(n_ctx, KV_LORA), kpe_ctx, wukv, kgain)

    oa = _attention(q, k, v, k_ctx, v_ctx)
    woa = jnp.pad(w_out[:H_A * V_A].reshape(H_A, V_A, D), ((0, 0), (0, HP - V_A), (0, 0))).reshape(width, D)
    return oa, ob, woa.astype(BF16), w_out[H_A * V_A:].astype(BF16), ckv, kpe


def kernel(x_prompt, x_sample, cache_mla_ckv, cache_mla_kpe, state_gla_fwd, state_gla_bwd, c, c_ctx, ada_w, ada_b,
           norm_mix, norm_ffn, even_w_in, mla_q_a_norm, mla_w_uq, mla_q_norm, mla_kv_a_norm, mla_w_ukv, mla_k_norm,
           cmlp_v_norm, cmlp_w_s, cmlp_b_s, even_w_out, odd_w_in, gla_w_gk_fwd, gla_b_gk_fwd, gla_w_gk_bwd,
           gla_b_gk_bwd, gla_o_norm, odd_w_out, moe_w_router, moe_b_router, moe_w_gu, moe_b_gu, moe_w_down,
           moe_b_down):
    xs0, xp0 = x_sample.reshape(N_SAMPLE, D), x_prompt.reshape(N_PROMPT, D)
    cond8 = jnp.concatenate([c_ctx[None, :], c, jnp.zeros((SUBLANES - 1 - N_SAMPLE_SEQ, D), F32)], axis=0)
    mods = _adaln(cond8, ada_w, ada_b)
    wr = jnp.pad(moe_w_router, ((0, 0), (0, 0), (0, LANES - N_EXPERTS)))
    wr_hi = wr.astype(BF16)
    wr = jnp.concatenate([wr_hi, (wr - wr_hi.astype(F32)).astype(BF16)], axis=-1)
    br = jnp.pad(moe_b_router, ((0, 0), (0, LANES - N_EXPERTS))).reshape(2, 1, LANES)

    oa, ob, woa, wob, ckv, kpe = _even_layer(
        xs0, xp0, mods[0], norm_mix[0], even_w_in[0], mla_q_a_norm[0], mla_w_uq[0], mla_q_norm[0], mla_kv_a_norm[0],
        mla_w_ukv[0], mla_k_norm[0], cmlp_v_norm[0], cmlp_w_s[0], cmlp_b_s[0], even_w_out[0],
        cache_mla_ckv[:, 0], cache_mla_kpe[:, 0])
    width = H_A * HP
    x1, h2, meta, cnt = pl.pallas_call(
        _even_out_kernel,
        grid=(NBLK,),
        in_specs=[_TOK(width), _TOK(W_B), _X_SAMPLE, _X_PROMPT, _MOD, _full((width, D)), _full((W_B, D)),
                  _full((1, D)),
                  _full((D, 2 * LANES)), _full((1, LANES))],
        out_specs=_PROLOGUE_OUT_SPECS,
        out_shape=_PROLOGUE_OUT_SHAPE,
        compiler_params=_params(("arbitrary",)),
        name="even_out",
    )(oa, ob, xs0, xp0, mods[0], woa, wob, norm_ffn[0].reshape(1, D), wr[0], br[0])
    x2 = _moe(0, x1, h2, meta, cnt, mods[0], moe_w_gu, moe_b_gu, moe_w_down, moe_b_down)

    w_in = odd_w_in[0]
    win = jnp.concatenate([w_in, jnp.zeros((D, LANES - 2 * GATE_RANK), F32)], axis=1).astype(BF16)
    wgk = jnp.zeros((LANES, 2 * _HK), F32)
    wgk = wgk.at[:GATE_RANK, :_HK].set(gla_w_gk_fwd[0]).at[GATE_RANK:2 * GATE_RANK, _HK:].set(gla_w_gk_bwd[0])
    bgk = jnp.concatenate([gla_b_gk_fwd[0], gla_b_gk_bwd[0]]).reshape(1, 2 * _HK)
    q, k, v, g, la = pl.pallas_call(
        _odd_in_kernel,
        grid=(NBLK,),
        in_specs=[_TOK(D), _MOD, _full((1, D)), _full((D, _ODD_MAIN + LANES)), _full((LANES, 2 * _HK)),
                  _full((1, 2 * _HK))],
        out_specs=[_TOK(_HK), _TOK(_HK), _TOK(_HV), _TOK(_HV), _TOK(2 * _HK)],
        out_shape=[jax.ShapeDtypeStruct((T, _HK), F32), jax.ShapeDtypeStruct((T, _HK), F32),
                   jax.ShapeDtypeStruct((T, _HV), F32), jax.ShapeDtypeStruct((T, _HV), F32),
                   jax.ShapeDtypeStruct((T, 2 * _HK), F32)],
        compiler_params=_params(("arbitrary",)),
        name="odd_in",
    )(x2, mods[1], norm_mix[1].reshape(1, D), win, wgk.astype(BF16), bgk)

    of, obk, st_f, st_b = _gla_call(False, N_PROMPT_SEQ, PROMPT_LEN, N_SAMPLE, q, k, v, la)
    s0f = state_gla_fwd[:, 0].transpose(0, 1, 3, 2)
    s0b = state_gla_bwd[:, 0].transpose(0, 1, 3, 2)
    of, obk = _gla_call(True, N_SAMPLE_SEQ, SAMPLE_LEN, 0, q, k, v, la, of, obk, s0f, s0b)

    x3, h2, meta, cnt = pl.pallas_call(
        _odd_out_kernel,
        grid=(NBLK,),
        in_specs=[_TOK(_HV), _TOK(_HV), _TOK(_HV), _TOK(D), _MOD, _full((1, DV_C)), _full((_HV, D)),
                  _full((1, D)), _full((D, 2 * LANES)), _full((1, LANES))],
        out_specs=_PROLOGUE_OUT_SPECS,
        out_shape=_PROLOGUE_OUT_SHAPE,
        compiler_params=_params(("arbitrary",)),
        name="odd_out",
    )(of, obk, g, x2, mods[1], gla_o_norm[0].reshape(1, DV_C), odd_w_out[0].astype(BF16),
      norm_ffn[1].reshape(1, D), wr[1], br[1])
    ys, yp = _moe(1, x3, h2, meta, cnt, mods[1], moe_w_gu, moe_b_gu, moe_w_down, moe_b_down, split_out=True)

    y_sample = ys.reshape(N_SAMPLE_SEQ, SAMPLE_LEN, D)
    y_prompt = yp.reshape(N_PROMPT_SEQ, PROMPT_LEN, D)
    new_ckv = ckv[N_SAMPLE:].reshape(N_PROMPT_SEQ, 1, PROMPT_LEN, KV_LORA)
    new_kpe = kpe[N_SAMPLE:].reshape(N_PROMPT_SEQ, 1, PROMPT_LEN, ROPE)
    new_fwd = st_f.transpose(0, 1, 3, 2)[:, None]
    new_bwd = st_b.transpose(0, 1, 3, 2)[:, None]
    return (y_prompt, y_sample, new_ckv, new_kpe, new_fwd, new_bwd)
```

```python
import functools

import numpy as np
import jax
import jax.numpy as jnp
from jax import lax
from jax.experimental import pallas as pl
from jax.experimental.pallas import tpu as pltpu

F32 = jnp.float32
BF16 = jnp.bfloat16

D = 1024
N_PROMPT_SEQ, PROMPT_LEN = 16, 256
N_SAMPLE_SEQ, SAMPLE_LEN = 4, 2048
PAST_LEN = 512
N_PROMPT = N_PROMPT_SEQ * PROMPT_LEN
N_SAMPLE = N_SAMPLE_SEQ * SAMPLE_LEN
T = N_PROMPT + N_SAMPLE
EPS = 1e-6
GRID_W = 64
H_A, Q_LORA, KV_LORA, NOPE, ROPE, V_A = 8, 512, 256, 64, 32, 64
QK_DIM = NOPE + ROPE
G_B, C_B, W_B, CHUNK_B = 8, 64, 512, 128
H_C, DK_C, DV_C, GATE_RANK, GATE_TAU, GLA_CHUNK = 4, 128, 256, 16, 16.0, 64
N_EXPERTS, TOP_K, D_FF = 32, 4, 1024
SWIGLU_LIMIT, SWIGLU_ALPHA = 7.0, 1.702
ROPE_THETA = 10000.0
LOG2_E = 1.4426950408889634

LANES = 128
SUBLANES = 8
VMEM_LIMIT = 56 * 1024 * 1024

TM = 512
NBLK = T // TM
SAMPLE_BLKS = N_SAMPLE // TM
BLKS_PER_SAMPLE_SEQ = SAMPLE_LEN // TM
ATT_TQ = 256
GLA_NS = 4
ETILE = 512
TMD = TM
NBD = T // TMD
NBD_SAMPLE = N_SAMPLE // TMD
CH = 2 * SUBLANES
_MAX_LOCAL = TMD * TOP_K + N_EXPERTS * (CH - 1)
ROWS_L = -(-_MAX_LOCAL // LANES) * LANES
N_ETILES = -(-NBD * _MAX_LOCAL // ETILE) + N_EXPERTS
P_ROWS = N_ETILES * ETILE
HP = LANES


def _cond_row(i):
    return jnp.where(i < SAMPLE_BLKS, 1 + i // BLKS_PER_SAMPLE_SEQ, 0)


def _rope_blk(i):
    return jnp.where(i < SAMPLE_BLKS, 1 + i % BLKS_PER_SAMPLE_SEQ, 0)


def _rms(x):
    return x * lax.rsqrt(jnp.mean(x * x, axis=-1, keepdims=True) + EPS)


def _gelu(x):
    return 0.5 * x * (1.0 + jnp.tanh(0.7978845608028654 * (x + 0.044715 * (x * x * x))))


def _silu(x):
    return x * jax.nn.sigmoid(x)


def _dot(a, b):
    return jnp.dot(a, b, preferred_element_type=F32)


def _dot_nt(a, b):
    return lax.dot_general(a, b, (((1,), (1,)), ((), ())), preferred_element_type=F32)


def _dot_tn(a, b):
    return lax.dot_general(a, b, (((0,), (0,)), ((), ())), preferred_element_type=F32)


def _params(sem, vmem=VMEM_LIMIT):
    return pltpu.CompilerParams(dimension_semantics=sem, vmem_limit_bytes=vmem)


def _full(shape):
    nd = len(shape)
    return pl.BlockSpec(shape, lambda *_: (0,) * nd)


ADA_TN = 1536


def _adaln_kernel(c_ref, w_ref, b_ref, o_ref):
    s = _silu(c_ref[...]).astype(BF16)
    o_ref[0] = _dot(s, w_ref[0].astype(BF16)) + b_ref[0]


def _adaln(cond8, ada_w, ada_b):
    depth = ada_w.shape[0]
    n = ada_w.shape[2]
    out = pl.pallas_call(
        _adaln_kernel,
        grid=(depth, n // ADA_TN),
        in_specs=[
            pl.BlockSpec((SUBLANES, D), lambda l, j: (0, 0)),
            pl.BlockSpec((1, D, ADA_TN), lambda l, j: (l, 0, j)),
            pl.BlockSpec((1, 1, ADA_TN), lambda l, j: (l, 0, j)),
        ],
        out_specs=pl.BlockSpec((1, SUBLANES, ADA_TN), lambda l, j: (l, 0, j)),
        out_shape=jax.ShapeDtypeStruct((depth, SUBLANES, n), F32),
        compiler_params=_params(("arbitrary", "arbitrary")),
        name="adaln",
    )(cond8, ada_w, ada_b.reshape(depth, 1, n))
    return out.reshape(depth, SUBLANES, 6, D)


_QC0, _CKV0, _U0, _V0, _KPE0, _WIN_N = 0, 512, 768, 1280, 1792, 1920


def _rope(y, c, s1, s2):
    return y * c + pltpu.roll(y, LANES - 8, 1) * s1 + pltpu.roll(y, 8, 1) * s2


def _k_heads(k_raw, kpe128, kp_rot, kgain, k_ref):
    sskpe = jnp.sum(kpe128 * kpe128, axis=-1, keepdims=True)
    for h in range(H_A):
        kb = k_raw[:, h * HP:(h + 1) * HP]
        r = lax.rsqrt((jnp.sum(kb * kb, axis=-1, keepdims=True) + sskpe) * (1.0 / QK_DIM) + EPS)
        k_ref[:, h * HP:(h + 1) * HP] = ((kb * kgain + kp_rot) * r).astype(BF16)


def _input_rows(xs_ref, xp_ref):
    return jnp.where(pl.program_id(0) < SAMPLE_BLKS, xs_ref[...], xp_ref[...])


_X_SAMPLE = pl.BlockSpec((TM, D), lambda i, *_: (jnp.minimum(i, SAMPLE_BLKS - 1), 0))
_X_PROMPT = pl.BlockSpec((TM, D), lambda i, *_: (jnp.maximum(i - SAMPLE_BLKS, 0), 0))


def _even_in_kernel(xs_ref, xp_ref, mod_ref, nmix_ref, win_ref, qan_ref, wuq_ref, qgain_ref, kvan_ref, wukv_ref,
                    kgain_ref, vnorm_ref, ws_ref, bs_ref, rc_ref, rs1_ref, rs2_ref,
                    q_ref, k_ref, v_ref, ob_ref, ckv_ref, kpe_ref):
    m = mod_ref[0]
    h = _rms(_input_rows(xs_ref, xp_ref)) * nmix_ref[...] * (1.0 + m[1:2]) + m[0:1]
    a = _dot(h.astype(BF16), win_ref[...])
    qc = a[:, _QC0:_CKV0]
    ckv = a[:, _CKV0:_U0]
    u = a[:, _U0:_V0]
    vv = a[:, _V0:_KPE0]
    kpe128 = a[:, _KPE0:_WIN_N]

    ckv_n = _rms(ckv) * kvan_ref[...]
    ckv_ref[...] = ckv_n
    kpe_ref[...] = kpe128[:, NOPE:QK_DIM]

    rc, rs1, rs2 = rc_ref[...], rs1_ref[...], rs2_ref[...]
    qn = (_rms(qc) * qan_ref[...]).astype(BF16)
    qr = _dot(qn, wuq_ref[...])
    qgain = qgain_ref[...]
    for hh in range(H_A):
        blk = qr[:, hh * HP:(hh + 1) * HP]
        r = lax.rsqrt(jnp.sum(blk * blk, axis=-1, keepdims=True) * (1.0 / QK_DIM) + EPS)
        q_ref[:, hh * HP:(hh + 1) * HP] = _rope(blk * r * qgain, rc, rs1, rs2).astype(BF16)

    kv = _dot(ckv_n.astype(BF16), wukv_ref[...])
    v_ref[...] = kv[:, H_A * HP:].astype(BF16)
    kgain = kgain_ref[...]
    kp_rot = _rope(kpe128 * kgain, rc, rs1, rs2)
    _k_heads(kv[:, :H_A * HP], kpe128, kp_rot, kgain, k_ref)

    ug = _gelu(u)
    vn = (_rms(_gelu(vv)) * vnorm_ref[...]).astype(BF16)
    low = lax.broadcasted_iota(jnp.int32, (CHUNK_B, LANES), 1) < C_B
    for c in range(TM // CHUNK_B):
        rows = slice(c * CHUNK_B, (c + 1) * CHUNK_B)
        for p in range(G_B // 2):
            cols = slice(p * LANES, (p + 1) * LANES)
            blk = vn[rows, cols]
            both = _dot(ws_ref[p], blk)
            mixed = jnp.where(low, both[:CHUNK_B], both[CHUNK_B:]) + bs_ref[p]
            ob_ref[rows, cols] = (ug[rows, cols] * mixed).astype(BF16)


def _ctx_kv_kernel(ckv_ref, kpe_ref, wukv_ref, kgain_ref, k_ref, v_ref):
    kv = _dot(ckv_ref[...].astype(BF16), wukv_ref[...])
    v_ref[...] = kv[:, H_A * HP:].astype(BF16)
    kgain = kgain_ref[...]
    kpe128 = kpe_ref[...]
    _k_heads(kv[:, :H_A * HP], kpe128, kpe128 * kgain, kgain, k_ref)


def _attn_self_kernel(q_ref, k_ref, v_ref, o_ref):
    for h in range(H_A):
        cols = slice(h * HP, (h + 1) * HP)
        s = _dot_nt(q_ref[:, cols], k_ref[:, cols])
        p = jnp.exp2(s - jnp.max(s, axis=-1, keepdims=True))
        inv = 1.0 / jnp.sum(p, axis=-1, keepdims=True)
        o_ref[:, cols] = (_dot(p.astype(BF16), v_ref[:, cols]) * inv).astype(BF16)


def _attn_ctx_kernel(q_ref, k_ref, v_ref, kc_ref, vc_ref, prev_ref, o_ref):
    del prev_ref
    for h in range(H_A):
        cols = slice(h * HP, (h + 1) * HP)
        q = q_ref[:, cols]
        s1 = _dot_nt(q, k_ref[:, cols])
        s2 = _dot_nt(q, kc_ref[:, cols])
        mx = jnp.maximum(jnp.max(s1, axis=-1, keepdims=True), jnp.max(s2, axis=-1, keepdims=True))
        p1 = jnp.exp2(s1 - mx)
        p2 = jnp.exp2(s2 - mx)
        inv = 1.0 / (jnp.sum(p1, axis=-1, keepdims=True) + jnp.sum(p2, axis=-1, keepdims=True))
        o = _dot(p1.astype(BF16), v_ref[:, cols]) + _dot(p2.astype(BF16), vc_ref[:, cols])
        o_ref[:, cols] = (o * inv).astype(BF16)


def _attention(q, k, v, k_ctx, v_ctx):
    width = H_A * HP
    first = N_SAMPLE // PROMPT_LEN
    o = pl.pallas_call(
        _attn_self_kernel,
        grid=(N_PROMPT_SEQ,),
        in_specs=[pl.BlockSpec((PROMPT_LEN, width), lambda i: (first + i, 0))] * 3,
        out_specs=pl.BlockSpec((PROMPT_LEN, width), lambda i: (first + i, 0)),
        out_shape=jax.ShapeDtypeStruct((T, width), BF16),
        compiler_params=_params(("arbitrary",)),
        name="attn_prompt",
    )(q, k, v)
    qblocks = SAMPLE_LEN // ATT_TQ
    qblk = lambda b, j: (b * qblocks + j, 0)
    return pl.pallas_call(
        _attn_ctx_kernel,
        grid=(N_SAMPLE_SEQ, qblocks),
        in_specs=[
            pl.BlockSpec((ATT_TQ, width), qblk),
            pl.BlockSpec((SAMPLE_LEN, width), lambda b, j: (b, 0)),
            pl.BlockSpec((SAMPLE_LEN, width), lambda b, j: (b, 0)),
            pl.BlockSpec((PAST_LEN, width), lambda b, j: (b, 0)),
            pl.BlockSpec((PAST_LEN, width), lambda b, j: (b, 0)),
            pl.BlockSpec(memory_space=pl.ANY),
        ],
        out_specs=pl.BlockSpec((ATT_TQ, width), qblk),
        out_shape=jax.ShapeDtypeStruct((T, width), BF16),
        input_output_aliases={5: 0},
        compiler_params=_params(("arbitrary", "arbitrary")),
        name="attn_sample",
    )(q, k, v, k_ctx, v_ctx, o)


_META_IDX, _META_RANK, _META_W = 0, TOP_K, 2 * TOP_K


def _moe_prologue(x1, m, nffn_ref, wr_ref, br_ref, x1_ref, h2_ref, meta_ref, cnt_ref):
    x1_ref[...] = x1
    h2 = _rms(x1) * nffn_ref[...] * (1.0 + m[4:5]) + m[3:4]
    h2_ref[...] = h2.astype(BF16)
    lane = lax.broadcasted_iota(jnp.int32, (TM, LANES), 1)
    lanef = lane.astype(F32)
    h_hi = h2.astype(BF16)
    h_lo = (h2 - h_hi.astype(F32)).astype(BF16)
    r = _dot(h_hi, wr_ref[...])
    logits = r[:, :LANES] + r[:, LANES:] + _dot(h_lo, wr_ref[:, :LANES]) + br_ref[...]
    work = jnp.where(lane < N_EXPERTS, logits, -jnp.inf)
    hots, vals = [], []
    for _ in range(TOP_K):
        mx = jnp.max(work, axis=-1, keepdims=True)
        idx = jnp.min(jnp.where(work == mx, lanef, float(LANES)), axis=-1, keepdims=True)
        hot = lanef == idx
        work = jnp.where(hot, -jnp.inf, work)
        hots.append((hot, idx))
        vals.append(mx)
    es = [jnp.exp(v - vals[0]) for v in vals]
    inv = 1.0 / (es[0] + es[1] + es[2] + es[3])
    sel = jnp.zeros((TM, LANES), F32)
    for hot, _ in hots:
        sel = jnp.where(hot, 1.0, sel)
    row = lax.broadcasted_iota(jnp.int32, (TM, TM), 0)
    col = lax.broadcasted_iota(jnp.int32, (TM, TM), 1)
    strict = jnp.where(row > col, 1.0, 0.0).astype(BF16)
    before = _dot(strict, sel.astype(BF16))
    meta = jnp.zeros((TM, LANES), F32)
    for kk, (hot, idx) in enumerate(hots):
        rank = jnp.sum(jnp.where(hot, before, 0.0), axis=-1, keepdims=True)
        meta = jnp.where(lane == _META_IDX + kk, idx, meta)
        meta = jnp.where(lane == _META_RANK + kk, rank, meta)
        meta = jnp.where(lane == _META_W + kk, es[kk] * inv, meta)
    meta_ref[...] = meta
    cnt_ref[0] = jnp.broadcast_to(jnp.sum(sel, axis=0, keepdims=True), (SUBLANES, LANES))


def _even_out_kernel(oa_ref, ob_ref, xs_ref, xp_ref, mod_ref, woa_ref, wob_ref, nffn_ref, wr_ref, br_ref,
                     x1_ref, h2_ref, meta_ref, cnt_ref):
    m = mod_ref[0]
    out = _dot(oa_ref[...], woa_ref[...]) + _dot(ob_ref[...], wob_ref[...])
    x1 = _input_rows(xs_ref, xp_ref) + m[2:3] * out
    _moe_prologue(x1, m, nffn_ref, wr_ref, br_ref, x1_ref, h2_ref, meta_ref, cnt_ref)


def _odd_out_kernel(of_ref, ob_ref, g_ref, x_ref, mod_ref, onorm_ref, wo_ref, nffn_ref, wr_ref, br_ref,
                    x1_ref, h2_ref, meta_ref, cnt_ref):
    m = mod_ref[0]
    onorm = onorm_ref[...]
    parts = []
    for h in range(H_C):
        cols = slice(h * DV_C, (h + 1) * DV_C)
        o = of_ref[:, cols] + ob_ref[:, cols]
        parts.append((_rms(o) * onorm * _silu(g_ref[:, cols])).astype(BF16))
    out = _dot(jnp.concatenate(parts, axis=-1), wo_ref[...])
    x1 = x_ref[...] + m[2:3] * out
    _moe_prologue(x1, m, nffn_ref, wr_ref, br_ref, x1_ref, h2_ref, meta_ref, cnt_ref)


_TOK = lambda w: pl.BlockSpec((TM, w), lambda i, *_: (i, 0))
_MOD = pl.BlockSpec((1, 6, D), lambda i, *_: (_cond_row(i), 0, 0))
_TILE_ROW = pl.BlockSpec((1, SUBLANES, LANES), lambda i, *_: (i, 0, 0))

_PROLOGUE_OUT_SPECS = [_TOK(D), _TOK(D), _TOK(LANES), _TILE_ROW]
_PROLOGUE_OUT_SHAPE = [
    jax.ShapeDtypeStruct((T, D), F32),
    jax.ShapeDtypeStruct((T, D), BF16),
    jax.ShapeDtypeStruct((T, LANES), F32),
    jax.ShapeDtypeStruct((NBLK, SUBLANES, LANES), F32),
]


def _local_positions(meta, seg_rows):
    lanef = lax.broadcasted_iota(jnp.int32, (TM, LANES), 1).astype(F32)
    pos = []
    for k in range(TOP_K):
        halves = []
        for half in range(TMD // TM):
            m = meta[half * TM:(half + 1) * TM]
            hot = lanef == m[:, _META_IDX + k:_META_IDX + k + 1]
            start = jnp.sum(jnp.where(hot, seg_rows[half:half + 1], 0.0), axis=-1, keepdims=True)
            halves.append(start + m[:, _META_RANK + k:_META_RANK + k + 1])
        pos.append(jnp.concatenate(halves, axis=0))
    return pos


def _segment_copies(t, base_ref, seg_ref, nch_ref, make):
    def per_expert(e, carry):
        q = t * N_EXPERTS + e
        local0, global0 = seg_ref[q], base_ref[q]

        def one(j, c):
            make(pl.multiple_of(local0 + j * CH, CH), pl.multiple_of(global0 + j * CH, CH)).start()
            return c

        return lax.fori_loop(0, nch_ref[q], one, carry)

    lax.fori_loop(0, N_EXPERTS, per_expert, 0)


def _drain(count, chunk_copy):
    def one(j, c):
        chunk_copy.wait()
        return c

    lax.fori_loop(0, count, one, 0)


def _dispatch_kernel(base_ref, seg_ref, nch_ref, ntile_ref, tail0_ref, tailn_ref, h_ref, meta_ref, segrow_ref,
                     xs_hbm, buf, zbuf, sem):
    i = pl.program_id(0)
    slot = i % 2
    chunk = lambda s: pltpu.make_async_copy(buf.at[s, pl.ds(0, CH)], xs_hbm.at[pl.ds(0, CH)], sem.at[s])

    @pl.when(i == 0)
    def _():
        zbuf[...] = jnp.zeros_like(zbuf)

        def per_expert(e, carry):
            def one(j, c):
                pltpu.make_async_copy(zbuf, xs_hbm.at[pl.ds(pl.multiple_of(tail0_ref[e] + j * CH, CH), CH)],
                                      sem.at[2]).start()
                return c

            lax.fori_loop(0, tailn_ref[e], one, 0)
            return carry + tailn_ref[e]

        total = lax.fori_loop(0, N_EXPERTS, per_expert, 0)
        _drain(total, pltpu.make_async_copy(zbuf, xs_hbm.at[pl.ds(0, CH)], sem.at[2]))

    pos = _local_positions(meta_ref[...], segrow_ref[0])
    riota = lax.broadcasted_iota(jnp.int32, (TMD, ROWS_L), 1).astype(F32)
    pt = jnp.zeros((TMD, ROWS_L), F32)
    for p in pos:
        pt = jnp.where(riota == p, 1.0, pt)
    buf[slot] = _dot_tn(pt.astype(BF16), h_ref[...]).astype(BF16)

    make = lambda s, d: pltpu.make_async_copy(buf.at[slot, pl.ds(s, CH)], xs_hbm.at[pl.ds(d, CH)], sem.at[slot])
    _segment_copies(i, base_ref, seg_ref, nch_ref, make)

    @pl.when(i > 0)
    def _():
        _drain(ntile_ref[i - 1], chunk(1 - slot))

    @pl.when(i == NBD - 1)
    def _():
        _drain(ntile_ref[i], chunk(slot))


_TOKD = lambda w: pl.BlockSpec((TMD, w), lambda i, *_: (i, 0))
_SEGROWS = pl.BlockSpec((1, TMD // TM, LANES), lambda i, *_: (i, 0, 0))


def _dispatch(plan, h2, meta):
    grid_spec = pltpu.PrefetchScalarGridSpec(
        num_scalar_prefetch=6,
        grid=(NBD,),
        in_specs=[_TOKD(D), _TOKD(LANES), _SEGROWS],
        out_specs=pl.BlockSpec(memory_space=pl.ANY),
        scratch_shapes=[pltpu.VMEM((2, ROWS_L, D), BF16), pltpu.VMEM((CH, D), BF16),
                        pltpu.SemaphoreType.DMA((3,))],
    )
    return pl.pallas_call(
        _dispatch_kernel,
        grid_spec=grid_spec,
        out_shape=jax.ShapeDtypeStruct((P_ROWS, D), BF16),
        compiler_params=_params(("arbitrary",)),
        name="moe_dispatch",
    )(plan["base"], plan["seg"], plan["nch"], plan["ntile"], plan["tail0"], plan["tailn"], h2, meta, plan["segrow"])


def _expert_kernel(layer, te_ref, tfirst_ref, tvalid_ref, xblk_ref, tnext_ref, wslot_ref, x_ref, wgu_hbm, bgu_ref,
                   wd_hbm, bd_ref, y_ref, wgu_f32, wd_f32, wgu_bf, wd_bf, sem):
    del xblk_ref
    i = pl.program_id(0)

    def weight_copies(e, slot):
        return (pltpu.make_async_copy(wgu_hbm.at[layer, e], wgu_f32.at[slot], sem.at[0, slot]),
                pltpu.make_async_copy(wd_hbm.at[layer, e], wd_f32.at[slot], sem.at[1, slot]))

    @pl.when(tfirst_ref[i] == 1)
    def _():
        slot = wslot_ref[i]

        @pl.when(i == 0)
        def _():
            for cp in weight_copies(te_ref[i], slot):
                cp.start()

        for cp in weight_copies(te_ref[i], slot):
            cp.wait()
        wgu_bf[...] = wgu_f32[slot].astype(BF16)
        wd_bf[...] = wd_f32[slot].astype(BF16)

        @pl.when(tnext_ref[i] >= 0)
        def _():
            for cp in weight_copies(tnext_ref[i], 1 - slot):
                cp.start()

    def ffn(x):
        a = _dot(x, wgu_bf[...]) + bgu_ref[...]
        glu = jnp.minimum(a[:, :D_FF], SWIGLU_LIMIT)
        lin = jnp.clip(a[:, D_FF:], -SWIGLU_LIMIT, SWIGLU_LIMIT)
        act = (glu * jax.nn.sigmoid(SWIGLU_ALPHA * glu)) * (lin + 1.0)
        return (_dot(act.astype(BF16), wd_bf[...]) + bd_ref[...]).astype(BF16)

    half = ETILE // 2

    @pl.when(tvalid_ref[i] == 2)
    def _():
        y_ref[...] = ffn(x_ref[...])

    @pl.when(tvalid_ref[i] == 1)
    def _():
        y_ref[:half] = ffn(x_ref[:half])
        y_ref[half:] = jnp.zeros((ETILE - half, D), BF16)


def _experts(layer, plan, xs, w_gu, b_gu, w_down, b_down):
    depth = w_gu.shape[0]
    e_of = lambda i, te, *_: (layer, te[i], 0, 0)
    grid_spec = pltpu.PrefetchScalarGridSpec(
        num_scalar_prefetch=6,
        grid=(N_ETILES,),
        in_specs=[
            pl.BlockSpec((ETILE, D), lambda i, te, tf, tv, xb, *_: (xb[i], 0)),
            pl.BlockSpec(memory_space=pl.ANY),
            pl.BlockSpec((None, None, 1, 2 * D_FF), e_of),
            pl.BlockSpec(memory_space=pl.ANY),
            pl.BlockSpec((None, None, 1, D), e_of),
        ],
        out_specs=pl.BlockSpec((ETILE, D), lambda i, te, tf, tv, xb, *_: (xb[i], 0)),
        scratch_shapes=[pltpu.VMEM((2, D, 2 * D_FF), F32), pltpu.VMEM((2, D_FF, D), F32),
                        pltpu.VMEM((D, 2 * D_FF), BF16), pltpu.VMEM((D_FF, D), BF16),
                        pltpu.SemaphoreType.DMA((2, 2))],
    )
    return pl.pallas_call(
        functools.partial(_expert_kernel, layer),
        grid_spec=grid_spec,
        out_shape=jax.ShapeDtypeStruct((P_ROWS, D), BF16),
        compiler_params=_params(("arbitrary",)),
        name="moe_experts",
    )(plan["te"], plan["tfirst"], plan["tvalid"], plan["xblk"], plan["tnext"], plan["wslot"], xs, w_gu,
      b_gu.reshape(depth, N_EXPERTS, 1, 2 * D_FF), w_down, b_down.reshape(depth, N_EXPERTS, 1, D))


def _combine_kernel(split, base_ref, seg_ref, nch_ref, ntile_ref, y_hbm, x1_ref, meta_ref, segrow_ref, mod_ref,
                    *rest):
    *o_refs, ybuf, sem = rest
    i = pl.program_id(0)
    slot = i % 2

    def fetch(t, s):
        make = lambda loc, glob: pltpu.make_async_copy(y_hbm.at[pl.ds(glob, CH)], ybuf.at[s, pl.ds(loc, CH)],
                                                       sem.at[s])
        _segment_copies(t, base_ref, seg_ref, nch_ref, make)

    @pl.when(i == 0)
    def _():
        ybuf[...] = jnp.zeros_like(ybuf)
        fetch(i, slot)

    @pl.when(i + 1 < NBD)
    def _():
        fetch(i + 1, 1 - slot)

    _drain(ntile_ref[i], pltpu.make_async_copy(y_hbm.at[pl.ds(0, CH)], ybuf.at[slot, pl.ds(0, CH)], sem.at[slot]))

    meta = meta_ref[...]
    pos = _local_positions(meta, segrow_ref[0])
    riota = lax.broadcasted_iota(jnp.int32, (TMD, ROWS_L), 1).astype(F32)
    gates = jnp.zeros((TMD, ROWS_L), F32)
    for k, p in enumerate(pos):
        gates = jnp.where(riota == p, meta[:, _META_W + k:_META_W + k + 1], gates)
    acc = _dot(gates.astype(BF16), ybuf[slot])
    out = x1_ref[...] + mod_ref[0][5:6] * acc
    if split:
        os_ref, op_ref = o_refs

        @pl.when(i < NBD_SAMPLE)
        def _():
            os_ref[...] = out

        @pl.when(i >= NBD_SAMPLE)
        def _():
            op_ref[...] = out
    else:
        o_refs[0][...] = out


def _combine(plan, y, x1, meta, mod, split):
    if split:
        out_specs = [pl.BlockSpec((TMD, D), lambda i, *_: (jnp.minimum(i, NBD_SAMPLE - 1), 0)),
                     pl.BlockSpec((TMD, D), lambda i, *_: (jnp.maximum(i - NBD_SAMPLE, 0), 0))]
        out_shape = [jax.ShapeDtypeStruct((N_SAMPLE, D), F32), jax.ShapeDtypeStruct((N_PROMPT, D), F32)]
    else:
        out_specs, out_shape = _TOKD(D), jax.ShapeDtypeStruct((T, D), F32)
    grid_spec = pltpu.PrefetchScalarGridSpec(
        num_scalar_prefetch=4,
        grid=(NBD,),
        in_specs=[pl.BlockSpec(memory_space=pl.ANY), _TOKD(D), _TOKD(LANES), _SEGROWS,
                  pl.BlockSpec((1, 6, D), lambda i, *_: (_cond_row(i * (TMD // TM)), 0, 0))],
        out_specs=out_specs,
        scratch_shapes=[pltpu.VMEM((2, ROWS_L, D), BF16), pltpu.SemaphoreType.DMA((2,))],
    )
    return pl.pallas_call(
        functools.partial(_combine_kernel, split),
        grid_spec=grid_spec,
        out_shape=out_shape,
        compiler_params=_params(("arbitrary",)),
        name="moe_combine",
    )(plan["base"], plan["seg"], plan["nch"], plan["ntile"], y, x1, meta, plan["segrow"], mod)


def _moe_plan(cnt):
    per = TMD // TM
    cnt = cnt[:, 0, :N_EXPERTS].astype(jnp.int32).reshape(NBD, per, N_EXPERTS)
    cpad = (jnp.sum(cnt, axis=1) + CH - 1) // CH * CH
    tot = jnp.sum(cpad, axis=0)
    tiles = (tot + ETILE - 1) // ETILE
    tile_end = jnp.cumsum(tiles)
    offs = (tile_end - tiles) * ETILE
    base = offs[None, :] + jnp.cumsum(cpad, axis=0) - cpad
    seg = jnp.cumsum(cpad, axis=1) - cpad
    segrows = seg[:, None, :] + jnp.cumsum(cnt, axis=1) - cnt
    tid = jnp.arange(N_ETILES, dtype=jnp.int32)
    te = jnp.sum((tile_end[None, :] <= tid[:, None]).astype(jnp.int32), axis=1)
    used = te < N_EXPERTS
    last = jnp.max(jnp.where(tiles > 0, jnp.arange(N_EXPERTS, dtype=jnp.int32), 0))
    te = jnp.where(used, te, last)
    rows = tot[te] - (tid - (tile_end - tiles)[te]) * ETILE
    tvalid = jnp.where(used, jnp.where(rows > ETILE // 2, 2, 1), 0).astype(jnp.int32)
    tfirst = jnp.concatenate([jnp.ones((1,), jnp.int32), (te[1:] != te[:-1]).astype(jnp.int32)])
    eid = jnp.arange(N_EXPERTS, dtype=jnp.int32)
    later = (eid[None, :] > eid[:, None]) & (tiles[None, :] > 0)
    nxt = jnp.min(jnp.where(later, eid[None, :], N_EXPERTS), axis=1)
    nxt = jnp.where(nxt < N_EXPERTS, nxt, -1)
    return {
        "tnext": nxt[te], "wslot": (jnp.cumsum(tfirst) - 1) % 2, "tfirst": tfirst,
        "base": base.reshape(-1), "seg": seg.reshape(-1), "nch": (cpad // CH).reshape(-1),
        "ntile": jnp.sum(cpad, axis=1) // CH, "tail0": offs + tot, "tailn": (tiles * ETILE - tot) // CH,
        "segrow": jnp.pad(segrows.astype(F32), ((0, 0), (0, 0), (0, LANES - N_EXPERTS))),
        "te": te, "tvalid": tvalid, "xblk": jnp.where(used, tid, tile_end[-1] - 1),
    }


def _moe(layer, x1, h2, meta, cnt, mod, w_gu, b_gu, w_down, b_down, split_out=False):
    plan = _moe_plan(cnt)
    xs = _dispatch(plan, h2, meta)
    y = _experts(layer, plan, xs, w_gu, b_gu, w_down, b_down)
    return _combine(plan, y, x1, meta, mod, split_out)


_HK, _HV = H_C * DK_C, H_C * DV_C
_ODD_MAIN = 2 * _HK + 2 * _HV


def _odd_in_kernel(x_ref, mod_ref, nmix_ref, win_ref, wgk_ref, bgk_ref, q_ref, k_ref, v_ref, g_ref, la_ref):
    m = mod_ref[0]
    h = _rms(x_ref[...]) * nmix_ref[...] * (1.0 + m[1:2]) + m[0:1]
    a = _dot(h.astype(BF16), win_ref[...])
    q_ref[...] = a[:, :_HK] * (DK_C ** -0.5)
    k_ref[...] = a[:, _HK:2 * _HK]
    v_ref[...] = a[:, 2 * _HK:2 * _HK + _HV]
    g_ref[...] = a[:, 2 * _HK + _HV:_ODD_MAIN]
    z = _dot(a[:, _ODD_MAIN:].astype(BF16), wgk_ref[...]) + bgk_ref[...]
    la_ref[...] = (jnp.minimum(z, 0.0) - jnp.log(1.0 + jnp.exp(-jnp.abs(z)))) * (1.0 / GATE_TAU)


def _gla_kernel(has_init, nchunk, *refs):
    if has_init:
        (qf, kf, vf, laf, qb, kb, vb, lab, s0f, s0b, _, _, of_ref, ob_ref, st) = refs
    else:
        (qf, kf, vf, laf, qb, kb, vb, lab, of_ref, ob_ref, sf_ref, sb_ref, st) = refs
    j = pl.program_id(1)

    @pl.when(j == 0)
    def _():
        if has_init:
            st[0] = s0f[...]
            st[1] = s0b[...]
        else:
            st[...] = jnp.zeros_like(st)

    row = lax.broadcasted_iota(jnp.int32, (GLA_CHUNK, GLA_CHUNK), 0)
    col = lax.broadcasted_iota(jnp.int32, (GLA_CHUNK, GLA_CHUNK), 1)
    for d, (q_r, k_r, v_r, la_r, o_r) in enumerate(((qf, kf, vf, laf, of_ref), (qb, kb, vb, lab, ob_ref))):
        keep = (col <= row) if d == 0 else (col >= row)
        tri = jnp.where(keep, 1.0, 0.0).astype(BF16)
        for s in range(GLA_NS):
            g = la_r[s]
            g_hi = g.astype(BF16)
            g_r = g - g_hi.astype(F32)
            g_mid = g_r.astype(BF16)
            g_lo = (g_r - g_mid.astype(F32)).astype(BF16)
            c = _dot(tri, g_hi) + _dot(tri, g_mid) + _dot(tri, g_lo)
            tot = jnp.sum(g, axis=0, keepdims=True)
            decay = jnp.exp(tot)
            e_neg = jnp.exp(-c)
            k = k_r[s]
            qe = (q_r[s] * jnp.exp(c)).astype(BF16)
            kd = (k * e_neg).astype(BF16)
            k2 = (k * (e_neg * decay)).astype(BF16)
            v = v_r[s].astype(BF16)
            for h in range(H_C):
                kc = slice(h * DK_C, (h + 1) * DK_C)
                vc = slice(h * DV_C, (h + 1) * DV_C)
                att = jnp.where(keep, _dot_nt(qe[:, kc], kd[:, kc]), 0.0)
                s_t = st[d, s, h]
                o_r[s, :, vc] = _dot_nt(qe[:, kc], s_t.astype(BF16)) + _dot(att.astype(BF16), v[:, vc])
                st[d, s, h] = s_t * decay[:, kc] + _dot_tn(v[:, vc], k2[:, kc])

    if not has_init:
        @pl.when(j == nchunk - 1)
        def _():
            sf_ref[...] = st[0]
            sb_ref[...] = st[1]


def _gla_call(has_init, nseq, seqlen, row0, q, k, v, la, prev_f=None, prev_b=None, s0f=None, s0b=None):
    nchunk = seqlen // GLA_CHUNK
    nview = T // seqlen
    g0 = row0 // seqlen // GLA_NS
    view = lambda a: a.reshape(nview, seqlen, a.shape[-1])
    fwd = lambda g, j: (g0 + g, j, 0)
    bwd = lambda g, j: (g0 + g, nchunk - 1 - j, 0)
    bwd_la = lambda g, j: (g0 + g, nchunk - 1 - j, 1)
    blk = lambda w, m: pl.BlockSpec((GLA_NS, GLA_CHUNK, w), m)
    state_spec = pl.BlockSpec((GLA_NS, H_C, DV_C, DK_C), lambda g, j: (g, 0, 0, 0))
    in_specs = [blk(_HK, fwd), blk(_HK, fwd), blk(_HV, fwd), blk(_HK, fwd),
                blk(_HK, bwd), blk(_HK, bwd), blk(_HV, bwd), blk(_HK, bwd_la)]
    args = [view(q), view(k), view(v), view(la)] * 2
    aliases = {}
    if has_init:
        in_specs += [state_spec, state_spec] + [pl.BlockSpec(memory_space=pl.ANY)] * 2
        args += [s0f, s0b, view(prev_f), view(prev_b)]
        aliases = {len(args) - 2: 0, len(args) - 1: 1}
    out_specs = [blk(_HV, fwd), blk(_HV, bwd)]
    out_shape = [jax.ShapeDtypeStruct((nview, seqlen, _HV), F32)] * 2
    if not has_init:
        out_specs += [state_spec, state_spec]
        out_shape += [jax.ShapeDtypeStruct((nseq, H_C, DV_C, DK_C), F32)] * 2
    outs = pl.pallas_call(
        functools.partial(_gla_kernel, has_init, nchunk),
        grid=(nseq // GLA_NS, nchunk),
        in_specs=in_specs,
        out_specs=out_specs,
        out_shape=out_shape,
        scratch_shapes=[pltpu.VMEM((2, GLA_NS, H_C, DV_C, DK_C), F32)],
        input_output_aliases=aliases,
        compiler_params=_params(("arbitrary", "arbitrary")),
        name="gla_sample" if has_init else "gla_prompt",
    )(*args)
    return [outs[0].reshape(T, _HV), outs[1].reshape(T, _HV)] + list(outs[2:])


def _rope_tables():
    half = ROPE // 2
    inv_freq = np.power(np.float32(ROPE_THETA), -np.arange(0, half, 2, dtype=np.float32) / np.float32(half))
    n = np.arange(SAMPLE_LEN)
    row = (n // GRID_W).astype(np.float32)
    col = (n % GRID_W).astype(np.float32)
    ang_r = (row[:, None] * inv_freq[None, :]).astype(np.float32)
    ang_c = (col[:, None] * inv_freq[None, :]).astype(np.float32)
    nf = half // 2
    c = np.ones((TM + SAMPLE_LEN, LANES), np.float32)
    s1 = np.zeros((TM + SAMPLE_LEN, LANES), np.float32)
    s2 = np.zeros((TM + SAMPLE_LEN, LANES), np.float32)
    for base, ang in ((NOPE, ang_r), (NOPE + half, ang_c)):
        c[TM:, base:base + nf] = np.cos(ang)
        c[TM:, base + nf:base + half] = np.cos(ang)
        s1[TM:, base:base + nf] = -np.sin(ang)
        s2[TM:, base + nf:base + half] = np.sin(ang)
    return jnp.asarray(c), jnp.asarray(s1), jnp.asarray(s2)


def _pad_heads(w, nheads, width, lo=0):
    k = w.shape[0]
    w = w.reshape(k, nheads, width)
    w = jnp.pad(w, ((0, 0), (0, 0), (lo, HP - lo - width)))
    return w.reshape(k, nheads * HP)


def _row128(v, lo=0):
    return jnp.pad(v, (lo, LANES - lo - v.shape[0])).reshape(1, LANES)


def _even_layer(xs, xp, mod, nmix, w_in, q_a_norm, w_uq, q_norm, kv_a_norm, w_ukv, k_norm, v_norm, w_s, b_s, w_out,
                cache_ckv, cache_kpe):
    s = np.cumsum([Q_LORA, KV_LORA, ROPE, W_B])
    w_q, w_ckv, w_kpe, w_u, w_v = (w_in[:, :s[0]], w_in[:, s[0]:s[1]], w_in[:, s[1]:s[2]], w_in[:, s[2]:s[3]],
                                   w_in[:, s[3]:])
    w_kpe = jnp.pad(w_kpe, ((0, 0), (NOPE, LANES - QK_DIM)))
    win = jnp.concatenate([w_q, w_ckv, w_u, w_v, w_kpe], axis=1).astype(BF16)
    wuq = _pad_heads(w_uq, H_A, QK_DIM).astype(BF16)
    ukv = w_ukv.reshape(KV_LORA, H_A, NOPE + V_A)
    wuk = _pad_heads(ukv[:, :, :NOPE].reshape(KV_LORA, H_A * NOPE), H_A, NOPE)
    wuv = _pad_heads(ukv[:, :, NOPE:].reshape(KV_LORA, H_A * V_A), H_A, V_A)
    wukv = jnp.concatenate([wuk, wuv], axis=1).astype(BF16)
    qgain = _row128(q_norm * (QK_DIM ** -0.5 * LOG2_E))
    kgain = _row128(k_norm)
    bias = b_s.reshape(G_B // 2, 2, CHUNK_B)
    bias = jnp.concatenate([jnp.broadcast_to(bias[:, 0, :, None], (G_B // 2, CHUNK_B, C_B)),
                            jnp.broadcast_to(bias[:, 1, :, None], (G_B // 2, CHUNK_B, C_B))], axis=-1)
    rc, rs1, rs2 = _rope_tables()
    rope_spec = pl.BlockSpec((TM, LANES), lambda i: (_rope_blk(i), 0))
    width = H_A * HP
    q, k, v, ob, ckv, kpe = pl.pallas_call(
        _even_in_kernel,
        grid=(NBLK,),
        in_specs=[
            _X_SAMPLE, _X_PROMPT, _MOD, _full((1, D)), _full((D, _WIN_N)), _full((1, Q_LORA)),
            _full((Q_LORA, width)),
            _full((1, LANES)), _full((1, KV_LORA)), _full((KV_LORA, 2 * width)), _full((1, LANES)),
            _full((1, W_B)), _full((G_B // 2, 2 * CHUNK_B, CHUNK_B)), _full((G_B // 2, CHUNK_B, LANES)),
            rope_spec, rope_spec, rope_spec,
        ],
        out_specs=[_TOK(width), _TOK(width), _TOK(width), _TOK(W_B), _TOK(KV_LORA), _TOK(ROPE)],
        out_shape=[
            jax.ShapeDtypeStruct((T, width), BF16), jax.ShapeDtypeStruct((T, width), BF16),
            jax.ShapeDtypeStruct((T, width), BF16), jax.ShapeDtypeStruct((T, W_B), BF16),
            jax.ShapeDtypeStruct((T, KV_LORA), F32), jax.ShapeDtypeStruct((T, ROPE), F32),
        ],
        compiler_params=_params(("arbitrary",)),
        name="even_in",
    )(xs, xp, mod, nmix.reshape(1, D), win, q_a_norm.reshape(1, Q_LORA), wuq, qgain, kv_a_norm.reshape(1, KV_LORA),
      wukv, kgain, v_norm.reshape(1, W_B), w_s.astype(BF16).reshape(G_B // 2, 2 * CHUNK_B, CHUNK_B), bias, rc, rs1,
      rs2)

    n_ctx = N_SAMPLE_SEQ * PAST_LEN
    kpe_ctx = jnp.pad(cache_kpe.reshape(n_ctx, ROPE), ((0, 0), (NOPE, LANES - QK_DIM)))
    k_ctx, v_ctx = pl.pallas_call(
        _ctx_kv_kernel,
        grid=(n_ctx // TM,),
        in_specs=[_TOK(KV_LORA), _TOK(LANES), _full((KV_LORA, 2 * width)), _full((1, LANES))],
        out_specs=[_TOK(width), _TOK(width)],
        out_shape=[jax.ShapeDtypeStruct((n_ctx, width), BF16)] * 2,
        compiler_params=_params(("arbitrary",)),
        name="ctx_kv",
    )(cache_ckv.reshape(n_ctx, KV_LORA), kpe_ctx, wukv, kgain)

    oa = _attention(q, k, v, k_ctx, v_ctx)
    woa = jnp.pad(w_out[:H_A * V_A].reshape(H_A, V_A, D), ((0, 0), (0, HP - V_A), (0, 0))).reshape(width, D)
    return oa, ob, woa.astype(BF16), w_out[H_A * V_A:].astype(BF16), ckv, kpe


def kernel(x_prompt, x_sample, cache_mla_ckv, cache_mla_kpe, state_gla_fwd, state_gla_bwd, c, c_ctx, ada_w, ada_b,
           norm_mix, norm_ffn, even_w_in, mla_q_a_norm, mla_w_uq, mla_q_norm, mla_kv_a_norm, mla_w_ukv, mla_k_norm,
           cmlp_v_norm, cmlp_w_s, cmlp_b_s, even_w_out, odd_w_in, gla_w_gk_fwd, gla_b_gk_fwd, gla_w_gk_bwd,
           gla_b_gk_bwd, gla_o_norm, odd_w_out, moe_w_router, moe_b_router, moe_w_gu, moe_b_gu, moe_w_down,
           moe_b_down):
    xs0, xp0 = x_sample.reshape(N_SAMPLE, D), x_prompt.reshape(N_PROMPT, D)
    cond8 = jnp.concatenate([c_ctx[None, :], c, jnp.zeros((SUBLANES - 1 - N_SAMPLE_SEQ, D), F32)], axis=0)
    mods = _adaln(cond8, ada_w, ada_b)
    wr = jnp.pad(moe_w_router, ((0, 0), (0, 0), (0, LANES - N_EXPERTS)))
    wr_hi = wr.astype(BF16)
    wr = jnp.concatenate([wr_hi, (wr - wr_hi.astype(F32)).astype(BF16)], axis=-1)
    br = jnp.pad(moe_b_router, ((0, 0), (0, LANES - N_EXPERTS))).reshape(2, 1, LANES)

    oa, ob, woa, wob, ckv, kpe = _even_layer(
        xs0, xp0, mods[0], norm_mix[0], even_w_in[0], mla_q_a_norm[0], mla_w_uq[0], mla_q_norm[0], mla_kv_a_norm[0],
        mla_w_ukv[0], mla_k_norm[0], cmlp_v_norm[0], cmlp_w_s[0], cmlp_b_s[0], even_w_out[0],
        cache_mla_ckv[:, 0], cache_mla_kpe[:, 0])
    width = H_A * HP
    x1, h2, meta, cnt = pl.pallas_call(
        _even_out_kernel,
        grid=(NBLK,),
        in_specs=[_TOK(width), _TOK(W_B), _X_SAMPLE, _X_PROMPT, _MOD, _full((width, D)), _full((W_B, D)),
                  _full((1, D)),
                  _full((D, 2 * LANES)), _full((1, LANES))],
        out_specs=_PROLOGUE_OUT_SPECS,
        out_shape=_PROLOGUE_OUT_SHAPE,
        compiler_params=_params(("arbitrary",)),
        name="even_out",
    )(oa, ob, xs0, xp0, mods[0], woa, wob, norm_ffn[0].reshape(1, D), wr[0], br[0])
    x2 = _moe(0, x1, h2, meta, cnt, mods[0], moe_w_gu, moe_b_gu, moe_w_down, moe_b_down)

    w_in = odd_w_in[0]
    win = jnp.concatenate([w_in, jnp.zeros((D, LANES - 2 * GATE_RANK), F32)], axis=1).astype(BF16)
    wgk = jnp.zeros((LANES, 2 * _HK), F32)
    wgk = wgk.at[:GATE_RANK, :_HK].set(gla_w_gk_fwd[0]).at[GATE_RANK:2 * GATE_RANK, _HK:].set(gla_w_gk_bwd[0])
    bgk = jnp.concatenate([gla_b_gk_fwd[0], gla_b_gk_bwd[0]]).reshape(1, 2 * _HK)
    q, k, v, g, la = pl.pallas_call(
        _odd_in_kernel,
        grid=(NBLK,),
        in_specs=[_TOK(D), _MOD, _full((1, D)), _full((D, _ODD_MAIN + LANES)), _full((LANES, 2 * _HK)),
                  _full((1, 2 * _HK))],
        out_specs=[_TOK(_HK), _TOK(_HK), _TOK(_HV), _TOK(_HV), _TOK(2 * _HK)],
        out_shape=[jax.ShapeDtypeStruct((T, _HK), F32), jax.ShapeDtypeStruct((T, _HK), F32),
                   jax.ShapeDtypeStruct((T, _HV), F32), jax.ShapeDtypeStruct((T, _HV), F32),
                   jax.ShapeDtypeStruct((T, 2 * _HK), F32)],
        compiler_params=_params(("arbitrary",)),
        name="odd_in",
    )(x2, mods[1], norm_mix[1].reshape(1, D), win, wgk.astype(BF16), bgk)

    of, obk, st_f, st_b = _gla_call(False, N_PROMPT_SEQ, PROMPT_LEN, N_SAMPLE, q, k, v, la)
    s0f = state_gla_fwd[:, 0].transpose(0, 1, 3, 2)
    s0b = state_gla_bwd[:, 0].transpose(0, 1, 3, 2)
    of, obk = _gla_call(True, N_SAMPLE_SEQ, SAMPLE_LEN, 0, q, k, v, la, of, obk, s0f, s0b)

    x3, h2, meta, cnt = pl.pallas_call(
        _odd_out_kernel,
        grid=(NBLK,),
        in_specs=[_TOK(_HV), _TOK(_HV), _TOK(_HV), _TOK(D), _MOD, _full((1, DV_C)), _full((_HV, D)),
                  _full((1, D)), _full((D, 2 * LANES)), _full((1, LANES))],
        out_specs=_PROLOGUE_OUT_SPECS,
        out_shape=_PROLOGUE_OUT_SHAPE,
        compiler_params=_params(("arbitrary",)),
        name="odd_out",
    )(of, obk, g, x2, mods[1], gla_o_norm[0].reshape(1, DV_C), odd_w_out[0].astype(BF16),
      norm_ffn[1].reshape(1, D), wr[1], br[1])
    ys, yp = _moe(1, x3, h2, meta, cnt, mods[1], moe_w_gu, moe_b_gu, moe_w_down, moe_b_down, split_out=True)

    y_sample = ys.reshape(N_SAMPLE_SEQ, SAMPLE_LEN, D)
    y_prompt = yp.reshape(N_PROMPT_SEQ, PROMPT_LEN, D)
    new_ckv = ckv[N_SAMPLE:].reshape(N_PROMPT_SEQ, 1, PROMPT_LEN, KV_LORA)
    new_kpe = kpe[N_SAMPLE:].reshape(N_PROMPT_SEQ, 1, PROMPT_LEN, ROPE)
    new_fwd = st_f.transpose(0, 1, 3, 2)[:, None]
    new_bwd = st_b.transpose(0, 1, 3, 2)[:, None]
    return (y_prompt, y_sample, new_ckv, new_kpe, new_fwd, new_bwd)
```

```python
import functools

import numpy as np
import jax
import jax.numpy as jnp
from jax import lax
from jax.experimental import pallas as pl
from jax.experimental.pallas import tpu as pltpu

F32 = jnp.float32
BF16 = jnp.bfloat16

D = 1024
N_PROMPT_SEQ, PROMPT_LEN = 16, 256
N_SAMPLE_SEQ, SAMPLE_LEN = 4, 2048
PAST_LEN = 512
N_PROMPT = N_PROMPT_SEQ * PROMPT_LEN
N_SAMPLE = N_SAMPLE_SEQ * SAMPLE_LEN
T = N_PROMPT + N_SAMPLE
EPS = 1e-6
GRID_W = 64
H_A, Q_LORA, KV_LORA, NOPE, ROPE, V_A = 8, 512, 256, 64, 32, 64
QK_DIM = NOPE + ROPE
G_B, C_B, W_B, CHUNK_B = 8, 64, 512, 128
H_C, DK_C, DV_C, GATE_RANK, GATE_TAU, GLA_CHUNK = 4, 128, 256, 16, 16.0, 64
N_EXPERTS, TOP_K, D_FF = 32, 4, 1024
SWIGLU_LIMIT, SWIGLU_ALPHA = 7.0, 1.702
ROPE_THETA = 10000.0
LOG2_E = 1.4426950408889634

LANES = 128
SUBLANES = 8
VMEM_LIMIT = 56 * 1024 * 1024

TM = 512
NBLK = T // TM
SAMPLE_BLKS = N_SAMPLE // TM
BLKS_PER_SAMPLE_SEQ = SAMPLE_LEN // TM
ATT_TQ = 256
GLA_NS = 4
GLA_SAFE_LOG_DECAY = 60.0
ETILE = 512
TMD = TM
NBD = T // TMD
NBD_SAMPLE = N_SAMPLE // TMD
CH = 2 * SUBLANES
_MAX_LOCAL = TMD * TOP_K + N_EXPERTS * (CH - 1)
ROWS_L = -(-_MAX_LOCAL // LANES) * LANES
N_ETILES = -(-NBD * _MAX_LOCAL // ETILE) + N_EXPERTS
P_ROWS = N_ETILES * ETILE
HP = LANES


def _cond_row(i):
    return jnp.where(i < SAMPLE_BLKS, 1 + i // BLKS_PER_SAMPLE_SEQ, 0)


def _rope_blk(i):
    return jnp.where(i < SAMPLE_BLKS, 1 + i % BLKS_PER_SAMPLE_SEQ, 0)


def _rms(x):
    return x * lax.rsqrt(jnp.mean(x * x, axis=-1, keepdims=True) + EPS)


def _gelu(x):
    return 0.5 * x * (1.0 + jnp.tanh(0.7978845608028654 * (x + 0.044715 * (x * x * x))))


def _silu(x):
    return x * jax.nn.sigmoid(x)


def _dot(a, b):
    return jnp.dot(a, b, preferred_element_type=F32)


def _dot_nt(a, b):
    return lax.dot_general(a, b, (((1,), (1,)), ((), ())), preferred_element_type=F32)


def _dot_tn(a, b):
    return lax.dot_general(a, b, (((0,), (0,)), ((), ())), preferred_element_type=F32)


def _params(sem, vmem=VMEM_LIMIT):
    return pltpu.CompilerParams(dimension_semantics=sem, vmem_limit_bytes=vmem)


def _full(shape):
    nd = len(shape)
    return pl.BlockSpec(shape, lambda *_: (0,) * nd)


ADA_TN = 1536


def _adaln_kernel(c_ref, w_ref, b_ref, o_ref):
    s = _silu(c_ref[...]).astype(BF16)
    o_ref[0] = _dot(s, w_ref[0].astype(BF16)) + b_ref[0]


def _adaln(cond8, ada_w, ada_b):
    depth = ada_w.shape[0]
    n = ada_w.shape[2]
    out = pl.pallas_call(
        _adaln_kernel,
        grid=(depth, n // ADA_TN),
        in_specs=[
            pl.BlockSpec((SUBLANES, D), lambda l, j: (0, 0)),
            pl.BlockSpec((1, D, ADA_TN), lambda l, j: (l, 0, j)),
            pl.BlockSpec((1, 1, ADA_TN), lambda l, j: (l, 0, j)),
        ],
        out_specs=pl.BlockSpec((1, SUBLANES, ADA_TN), lambda l, j: (l, 0, j)),
        out_shape=jax.ShapeDtypeStruct((depth, SUBLANES, n), F32),
        compiler_params=_params(("arbitrary", "arbitrary")),
        name="adaln",
    )(cond8, ada_w, ada_b.reshape(depth, 1, n))
    return out.reshape(depth, SUBLANES, 6, D)


_QC0, _CKV0, _U0, _V0, _KPE0, _WIN_N = 0, 512, 768, 1280, 1792, 1920


def _rope(y, c, s1, s2):
    return y * c + pltpu.roll(y, LANES - 8, 1) * s1 + pltpu.roll(y, 8, 1) * s2


def _k_heads(k_raw, kpe128, kp_rot, kgain, k_ref):
    sskpe = jnp.sum(kpe128 * kpe128, axis=-1, keepdims=True)
    for h in range(H_A):
        kb = k_raw[:, h * HP:(h + 1) * HP]
        r = lax.rsqrt((jnp.sum(kb * kb, axis=-1, keepdims=True) + sskpe) * (1.0 / QK_DIM) + EPS)
        k_ref[:, h * HP:(h + 1) * HP] = ((kb * kgain + kp_rot) * r).astype(BF16)


def _input_rows(xs_ref, xp_ref):
    return jnp.where(pl.program_id(0) < SAMPLE_BLKS, xs_ref[...], xp_ref[...])


_X_SAMPLE = pl.BlockSpec((TM, D), lambda i, *_: (jnp.minimum(i, SAMPLE_BLKS - 1), 0))
_X_PROMPT = pl.BlockSpec((TM, D), lambda i, *_: (jnp.maximum(i - SAMPLE_BLKS, 0), 0))


def _even_in_kernel(xs_ref, xp_ref, mod_ref, nmix_ref, win_ref, qan_ref, wuq_ref, qgain_ref, kvan_ref, wukv_ref,
                    kgain_ref, vnorm_ref, ws_ref, bs_ref, rc_ref, rs1_ref, rs2_ref,
                    q_ref, k_ref, v_ref, ob_ref, ckv_ref, kpe_ref):
    m = mod_ref[0]
    h = _rms(_input_rows(xs_ref, xp_ref)) * nmix_ref[...] * (1.0 + m[1:2]) + m[0:1]
    a = _dot(h.astype(BF16), win_ref[...])
    qc = a[:, _QC0:_CKV0]
    ckv = a[:, _CKV0:_U0]
    u = a[:, _U0:_V0]
    vv = a[:, _V0:_KPE0]
    kpe128 = a[:, _KPE0:_WIN_N]

    ckv_n = _rms(ckv) * kvan_ref[...]
    ckv_ref[...] = ckv_n
    kpe_ref[...] = kpe128[:, NOPE:QK_DIM]

    rc, rs1, rs2 = rc_ref[...], rs1_ref[...], rs2_ref[...]
    qn = (_rms(qc) * qan_ref[...]).astype(BF16)
    qr = _dot(qn, wuq_ref[...])
    qgain = qgain_ref[...]
    for hh in range(H_A):
        blk = qr[:, hh * HP:(hh + 1) * HP]
        r = lax.rsqrt(jnp.sum(blk * blk, axis=-1, keepdims=True) * (1.0 / QK_DIM) + EPS)
        q_ref[:, hh * HP:(hh + 1) * HP] = _rope(blk * r * qgain, rc, rs1, rs2).astype(BF16)

    kv = _dot(ckv_n.astype(BF16), wukv_ref[...])
    v_ref[...] = kv[:, H_A * HP:].astype(BF16)
    kgain = kgain_ref[...]
    kp_rot = _rope(kpe128 * kgain, rc, rs1, rs2)
    _k_heads(kv[:, :H_A * HP], kpe128, kp_rot, kgain, k_ref)

    ug = _gelu(u)
    vn = (_rms(_gelu(vv)) * vnorm_ref[...]).astype(BF16)
    low = lax.broadcasted_iota(jnp.int32, (CHUNK_B, LANES), 1) < C_B
    for c in range(TM // CHUNK_B):
        rows = slice(c * CHUNK_B, (c + 1) * CHUNK_B)
        for p in range(G_B // 2):
            cols = slice(p * LANES, (p + 1) * LANES)
            blk = vn[rows, cols]
            both = _dot(ws_ref[p], blk)
            mixed = jnp.where(low, both[:CHUNK_B], both[CHUNK_B:]) + bs_ref[p]
            ob_ref[rows, cols] = (ug[rows, cols] * mixed).astype(BF16)


def _ctx_kv_kernel(ckv_ref, kpe_ref, wukv_ref, kgain_ref, k_ref, v_ref):
    kv = _dot(ckv_ref[...].astype(BF16), wukv_ref[...])
    v_ref[...] = kv[:, H_A * HP:].astype(BF16)
    kgain = kgain_ref[...]
    kpe128 = kpe_ref[...]
    _k_heads(kv[:, :H_A * HP], kpe128, kpe128 * kgain, kgain, k_ref)


def _attn_self_kernel(q_ref, k_ref, v_ref, o_ref):
    for h in range(H_A):
        cols = slice(h * HP, (h + 1) * HP)
        s = _dot_nt(q_ref[:, cols], k_ref[:, cols])
        p = jnp.exp2(s - jnp.max(s, axis=-1, keepdims=True))
        inv = 1.0 / jnp.sum(p, axis=-1, keepdims=True)
        o_ref[:, cols] = (_dot(p.astype(BF16), v_ref[:, cols]) * inv).astype(BF16)


def _attn_ctx_kernel(q_ref, k_ref, v_ref, kc_ref, vc_ref, prev_ref, o_ref):
    del prev_ref
    for h in range(H_A):
        cols = slice(h * HP, (h + 1) * HP)
        q = q_ref[:, cols]
        s1 = _dot_nt(q, k_ref[:, cols])
        s2 = _dot_nt(q, kc_ref[:, cols])
        mx = jnp.maximum(jnp.max(s1, axis=-1, keepdims=True), jnp.max(s2, axis=-1, keepdims=True))
        p1 = jnp.exp2(s1 - mx)
        p2 = jnp.exp2(s2 - mx)
        inv = 1.0 / (jnp.sum(p1, axis=-1, keepdims=True) + jnp.sum(p2, axis=-1, keepdims=True))
        o = _dot(p1.astype(BF16), v_ref[:, cols]) + _dot(p2.astype(BF16), vc_ref[:, cols])
        o_ref[:, cols] = (o * inv).astype(BF16)


def _attention(q, k, v, k_ctx, v_ctx):
    width = H_A * HP
    first = N_SAMPLE // PROMPT_LEN
    o = pl.pallas_call(
        _attn_self_kernel,
        grid=(N_PROMPT_SEQ,),
        in_specs=[pl.BlockSpec((PROMPT_LEN, width), lambda i: (first + i, 0))] * 3,
        out_specs=pl.BlockSpec((PROMPT_LEN, width), lambda i: (first + i, 0)),
        out_shape=jax.ShapeDtypeStruct((T, width), BF16),
        compiler_params=_params(("arbitrary",)),
        name="attn_prompt",
    )(q, k, v)
    qblocks = SAMPLE_LEN // ATT_TQ
    qblk = lambda b, j: (b * qblocks + j, 0)
    return pl.pallas_call(
        _attn_ctx_kernel,
        grid=(N_SAMPLE_SEQ, qblocks),
        in_specs=[
            pl.BlockSpec((ATT_TQ, width), qblk),
            pl.BlockSpec((SAMPLE_LEN, width), lambda b, j: (b, 0)),
            pl.BlockSpec((SAMPLE_LEN, width), lambda b, j: (b, 0)),
            pl.BlockSpec((PAST_LEN, width), lambda b, j: (b, 0)),
            pl.BlockSpec((PAST_LEN, width), lambda b, j: (b, 0)),
            pl.BlockSpec(memory_space=pl.ANY),
        ],
        out_specs=pl.BlockSpec((ATT_TQ, width), qblk),
        out_shape=jax.ShapeDtypeStruct((T, width), BF16),
        input_output_aliases={5: 0},
        compiler_params=_params(("arbitrary", "arbitrary")),
        name="attn_sample",
    )(q, k, v, k_ctx, v_ctx, o)


_META_IDX, _META_RANK, _META_W = 0, TOP_K, 2 * TOP_K


def _moe_prologue(x1, m, nffn_ref, wr_ref, br_ref, x1_ref, h2_ref, meta_ref, cnt_ref):
    x1_ref[...] = x1
    h2 = _rms(x1) * nffn_ref[...] * (1.0 + m[4:5]) + m[3:4]
    h2_ref[...] = h2.astype(BF16)
    lane = lax.broadcasted_iota(jnp.int32, (TM, LANES), 1)
    lanef = lane.astype(F32)
    h_hi = h2.astype(BF16)
    h_lo = (h2 - h_hi.astype(F32)).astype(BF16)
    r = _dot(h_hi, wr_ref[...])
    logits = r[:, :LANES] + r[:, LANES:] + _dot(h_lo, wr_ref[:, :LANES]) + br_ref[...]
    work = jnp.where(lane < N_EXPERTS, logits, -jnp.inf)
    hots, vals = [], []
    for _ in range(TOP_K):
        mx = jnp.max(work, axis=-1, keepdims=True)
        idx = jnp.min(jnp.where(work == mx, lanef, float(LANES)), axis=-1, keepdims=True)
        hot = lanef == idx
        work = jnp.where(hot, -jnp.inf, work)
        hots.append((hot, idx))
        vals.append(mx)
    es = [jnp.exp(v - vals[0]) for v in vals]
    inv = 1.0 / (es[0] + es[1] + es[2] + es[3])
    sel = jnp.zeros((TM, LANES), F32)
    for hot, _ in hots:
        sel = jnp.where(hot, 1.0, sel)
    row = lax.broadcasted_iota(jnp.int32, (TM, TM), 0)
    col = lax.broadcasted_iota(jnp.int32, (TM, TM), 1)
    strict = jnp.where(row > col, 1.0, 0.0).astype(BF16)
    before = _dot(strict, sel.astype(BF16))
    meta = jnp.zeros((TM, LANES), F32)
    for kk, (hot, idx) in enumerate(hots):
        rank = jnp.sum(jnp.where(hot, before, 0.0), axis=-1, keepdims=True)
        meta = jnp.where(lane == _META_IDX + kk, idx, meta)
        meta = jnp.where(lane == _META_RANK + kk, rank, meta)
        meta = jnp.where(lane == _META_W + kk, es[kk] * inv, meta)
    meta_ref[...] = meta
    cnt_ref[0] = jnp.broadcast_to(jnp.sum(sel, axis=0, keepdims=True), (SUBLANES, LANES))


def _even_out_kernel(oa_ref, ob_ref, xs_ref, xp_ref, mod_ref, woa_ref, wob_ref, nffn_ref, wr_ref, br_ref,
                     x1_ref, h2_ref, meta_ref, cnt_ref):
    m = mod_ref[0]
    out = _dot(oa_ref[...], woa_ref[...]) + _dot(ob_ref[...], wob_ref[...])
    x1 = _input_rows(xs_ref, xp_ref) + m[2:3] * out
    _moe_prologue(x1, m, nffn_ref, wr_ref, br_ref, x1_ref, h2_ref, meta_ref, cnt_ref)


def _odd_out_kernel(of_ref, ob_ref, g_ref, x_ref, mod_ref, onorm_ref, wo_ref, nffn_ref, wr_ref, br_ref,
                    x1_ref, h2_ref, meta_ref, cnt_ref):
    m = mod_ref[0]
    onorm = onorm_ref[...]
    parts = []
    for h in range(H_C):
        cols = slice(h * DV_C, (h + 1) * DV_C)
        o = of_ref[:, cols] + ob_ref[:, cols]
        parts.append((_rms(o) * onorm * _silu(g_ref[:, cols])).astype(BF16))
    out = _dot(jnp.concatenate(parts, axis=-1), wo_ref[...])
    x1 = x_ref[...] + m[2:3] * out
    _moe_prologue(x1, m, nffn_ref, wr_ref, br_ref, x1_ref, h2_ref, meta_ref, cnt_ref)


_TOK = lambda w: pl.BlockSpec((TM, w), lambda i, *_: (i, 0))
_MOD = pl.BlockSpec((1, 6, D), lambda i, *_: (_cond_row(i), 0, 0))
_TILE_ROW = pl.BlockSpec((1, SUBLANES, LANES), lambda i, *_: (i, 0, 0))

_PROLOGUE_OUT_SPECS = [_TOK(D), _TOK(D), _TOK(LANES), _TILE_ROW]
_PROLOGUE_OUT_SHAPE = [
    jax.ShapeDtypeStruct((T, D), F32),
    jax.ShapeDtypeStruct((T, D), BF16),
    jax.ShapeDtypeStruct((T, LANES), F32),
    jax.ShapeDtypeStruct((NBLK, SUBLANES, LANES), F32),
]


def _local_positions(meta, seg_rows):
    lanef = lax.broadcasted_iota(jnp.int32, (TM, LANES), 1).astype(F32)
    pos = []
    for k in range(TOP_K):
        halves = []
        for half in range(TMD // TM):
            m = meta[half * TM:(half + 1) * TM]
            hot = lanef == m[:, _META_IDX + k:_META_IDX + k + 1]
            start = jnp.sum(jnp.where(hot, seg_rows[half:half + 1], 0.0), axis=-1, keepdims=True)
            halves.append(start + m[:, _META_RANK + k:_META_RANK + k + 1])
        pos.append(jnp.concatenate(halves, axis=0))
    return pos


def _segment_copies(t, base_ref, seg_ref, nch_ref, make):
    def per_expert(e, carry):
        q = t * N_EXPERTS + e
        local0, global0 = seg_ref[q], base_ref[q]

        def one(j, c):
            make(pl.multiple_of(local0 + j * CH, CH), pl.multiple_of(global0 + j * CH, CH)).start()
            return c

        return lax.fori_loop(0, nch_ref[q], one, carry)

    lax.fori_loop(0, N_EXPERTS, per_expert, 0)


def _drain(count, chunk_copy):
    def one(j, c):
        chunk_copy.wait()
        return c

    lax.fori_loop(0, count, one, 0)


def _dispatch_kernel(base_ref, seg_ref, nch_ref, ntile_ref, tail0_ref, tailn_ref, h_ref, meta_ref, segrow_ref,
                     xs_hbm, buf, zbuf, sem):
    i = pl.program_id(0)
    slot = i % 2
    chunk = lambda s: pltpu.make_async_copy(buf.at[s, pl.ds(0, CH)], xs_hbm.at[pl.ds(0, CH)], sem.at[s])

    @pl.when(i == 0)
    def _():
        zbuf[...] = jnp.zeros_like(zbuf)

        def per_expert(e, carry):
            def one(j, c):
                pltpu.make_async_copy(zbuf, xs_hbm.at[pl.ds(pl.multiple_of(tail0_ref[e] + j * CH, CH), CH)],
                                      sem.at[2]).start()
                return c

            lax.fori_loop(0, tailn_ref[e], one, 0)
            return carry + tailn_ref[e]

        total = lax.fori_loop(0, N_EXPERTS, per_expert, 0)
        _drain(total, pltpu.make_async_copy(zbuf, xs_hbm.at[pl.ds(0, CH)], sem.at[2]))

    pos = _local_positions(meta_ref[...], segrow_ref[0])
    riota = lax.broadcasted_iota(jnp.int32, (TMD, ROWS_L), 1).astype(F32)
    pt = jnp.zeros((TMD, ROWS_L), F32)
    for p in pos:
        pt = jnp.where(riota == p, 1.0, pt)
    buf[slot] = _dot_tn(pt.astype(BF16), h_ref[...]).astype(BF16)

    make = lambda s, d: pltpu.make_async_copy(buf.at[slot, pl.ds(s, CH)], xs_hbm.at[pl.ds(d, CH)], sem.at[slot])
    _segment_copies(i, base_ref, seg_ref, nch_ref, make)

    @pl.when(i > 0)
    def _():
        _drain(ntile_ref[i - 1], chunk(1 - slot))

    @pl.when(i == NBD - 1)
    def _():
        _drain(ntile_ref[i], chunk(slot))


_TOKD = lambda w: pl.BlockSpec((TMD, w), lambda i, *_: (i, 0))
_SEGROWS = pl.BlockSpec((1, TMD // TM, LANES), lambda i, *_: (i, 0, 0))


def _dispatch(plan, h2, meta):
    grid_spec = pltpu.PrefetchScalarGridSpec(
        num_scalar_prefetch=6,
        grid=(NBD,),
        in_specs=[_TOKD(D), _TOKD(LANES), _SEGROWS],
        out_specs=pl.BlockSpec(memory_space=pl.ANY),
        scratch_shapes=[pltpu.VMEM((2, ROWS_L, D), BF16), pltpu.VMEM((CH, D), BF16),
                        pltpu.SemaphoreType.DMA((3,))],
    )
    return pl.pallas_call(
        _dispatch_kernel,
        grid_spec=grid_spec,
        out_shape=jax.ShapeDtypeStruct((P_ROWS, D), BF16),
        compiler_params=_params(("arbitrary",)),
        name="moe_dispatch",
    )(plan["base"], plan["seg"], plan["nch"], plan["ntile"], plan["tail0"], plan["tailn"], h2, meta, plan["segrow"])


def _expert_kernel(layer, te_ref, tfirst_ref, tvalid_ref, xblk_ref, tnext_ref, wslot_ref, x_ref, wgu_hbm, bgu_ref,
                   wd_hbm, bd_ref, y_ref, wgu_f32, wd_f32, wgu_bf, wd_bf, sem):
    del xblk_ref
    i = pl.program_id(0)

    def weight_copies(e, slot):
        return (pltpu.make_async_copy(wgu_hbm.at[layer, e], wgu_f32.at[slot], sem.at[0, slot]),
                pltpu.make_async_copy(wd_hbm.at[layer, e], wd_f32.at[slot], sem.at[1, slot]))

    @pl.when(tfirst_ref[i] == 1)
    def _():
        slot = wslot_ref[i]

        @pl.when(i == 0)
        def _():
            for cp in weight_copies(te_ref[i], slot):
                cp.start()

        for cp in weight_copies(te_ref[i], slot):
            cp.wait()
        wgu_bf[...] = wgu_f32[slot].astype(BF16)
        wd_bf[...] = wd_f32[slot].astype(BF16)

        @pl.when(tnext_ref[i] >= 0)
        def _():
            for cp in weight_copies(tnext_ref[i], 1 - slot):
                cp.start()

    def ffn(x):
        a = _dot(x, wgu_bf[...]) + bgu_ref[...]
        glu = jnp.minimum(a[:, :D_FF], SWIGLU_LIMIT)
        lin = jnp.clip(a[:, D_FF:], -SWIGLU_LIMIT, SWIGLU_LIMIT)
        act = (glu * jax.nn.sigmoid(SWIGLU_ALPHA * glu)) * (lin + 1.0)
        return (_dot(act.astype(BF16), wd_bf[...]) + bd_ref[...]).astype(BF16)

    half = ETILE // 2

    @pl.when(tvalid_ref[i] == 2)
    def _():
        y_ref[...] = ffn(x_ref[...])

    @pl.when(tvalid_ref[i] == 1)
    def _():
        y_ref[:half] = ffn(x_ref[:half])
        y_ref[half:] = jnp.zeros((ETILE - half, D), BF16)


def _experts(layer, plan, xs, w_gu, b_gu, w_down, b_down):
    depth = w_gu.shape[0]
    e_of = lambda i, te, *_: (layer, te[i], 0, 0)
    grid_spec = pltpu.PrefetchScalarGridSpec(
        num_scalar_prefetch=6,
        grid=(N_ETILES,),
        in_specs=[
            pl.BlockSpec((ETILE, D), lambda i, te, tf, tv, xb, *_: (xb[i], 0)),
            pl.BlockSpec(memory_space=pl.ANY),
            pl.BlockSpec((None, None, 1, 2 * D_FF), e_of),
            pl.BlockSpec(memory_space=pl.ANY),
            pl.BlockSpec((None, None, 1, D), e_of),
        ],
        out_specs=pl.BlockSpec((ETILE, D), lambda i, te, tf, tv, xb, *_: (xb[i], 0)),
        scratch_shapes=[pltpu.VMEM((2, D, 2 * D_FF), F32), pltpu.VMEM((2, D_FF, D), F32),
                        pltpu.VMEM((D, 2 * D_FF), BF16), pltpu.VMEM((D_FF, D), BF16),
                        pltpu.SemaphoreType.DMA((2, 2))],
    )
    return pl.pallas_call(
        functools.partial(_expert_kernel, layer),
        grid_spec=grid_spec,
        out_shape=jax.ShapeDtypeStruct((P_ROWS, D), BF16),
        compiler_params=_params(("arbitrary",)),
        name="moe_experts",
    )(plan["te"], plan["tfirst"], plan["tvalid"], plan["xblk"], plan["tnext"], plan["wslot"], xs, w_gu,
      b_gu.reshape(depth, N_EXPERTS, 1, 2 * D_FF), w_down, b_down.reshape(depth, N_EXPERTS, 1, D))


def _combine_kernel(split, base_ref, seg_ref, nch_ref, ntile_ref, y_hbm, x1_ref, meta_ref, segrow_ref, mod_ref,
                    *rest):
    *o_refs, ybuf, sem = rest
    i = pl.program_id(0)
    slot = i % 2

    def fetch(t, s):
        make = lambda loc, glob: pltpu.make_async_copy(y_hbm.at[pl.ds(glob, CH)], ybuf.at[s, pl.ds(loc, CH)],
                                                       sem.at[s])
        _segment_copies(t, base_ref, seg_ref, nch_ref, make)

    @pl.when(i == 0)
    def _():
        ybuf[...] = jnp.zeros_like(ybuf)
        fetch(i, slot)

    @pl.when(i + 1 < NBD)
    def _():
        fetch(i + 1, 1 - slot)

    _drain(ntile_ref[i], pltpu.make_async_copy(y_hbm.at[pl.ds(0, CH)], ybuf.at[slot, pl.ds(0, CH)], sem.at[slot]))

    meta = meta_ref[...]
    pos = _local_positions(meta, segrow_ref[0])
    riota = lax.broadcasted_iota(jnp.int32, (TMD, ROWS_L), 1).astype(F32)
    gates = jnp.zeros((TMD, ROWS_L), F32)
    for k, p in enumerate(pos):
        gates = jnp.where(riota == p, meta[:, _META_W + k:_META_W + k + 1], gates)
    acc = _dot(gates.astype(BF16), ybuf[slot])
    out = x1_ref[...] + mod_ref[0][5:6] * acc
    if split:
        os_ref, op_ref = o_refs

        @pl.when(i < NBD_SAMPLE)
        def _():
            os_ref[...] = out

        @pl.when(i >= NBD_SAMPLE)
        def _():
            op_ref[...] = out
    else:
        o_refs[0][...] = out


def _combine(plan, y, x1, meta, mod, split):
    if split:
        out_specs = [pl.BlockSpec((TMD, D), lambda i, *_: (jnp.minimum(i, NBD_SAMPLE - 1), 0)),
                     pl.BlockSpec((TMD, D), lambda i, *_: (jnp.maximum(i - NBD_SAMPLE, 0), 0))]
        out_shape = [jax.ShapeDtypeStruct((N_SAMPLE, D), F32), jax.ShapeDtypeStruct((N_PROMPT, D), F32)]
    else:
        out_specs, out_shape = _TOKD(D), jax.ShapeDtypeStruct((T, D), F32)
    grid_spec = pltpu.PrefetchScalarGridSpec(
        num_scalar_prefetch=4,
        grid=(NBD,),
        in_specs=[pl.BlockSpec(memory_space=pl.ANY), _TOKD(D), _TOKD(LANES), _SEGROWS,
                  pl.BlockSpec((1, 6, D), lambda i, *_: (_cond_row(i * (TMD // TM)), 0, 0))],
        out_specs=out_specs,
        scratch_shapes=[pltpu.VMEM((2, ROWS_L, D), BF16), pltpu.SemaphoreType.DMA((2,))],
    )
    return pl.pallas_call(
        functools.partial(_combine_kernel, split),
        grid_spec=grid_spec,
        out_shape=out_shape,
        compiler_params=_params(("arbitrary",)),
        name="moe_combine",
    )(plan["base"], plan["seg"], plan["nch"], plan["ntile"], y, x1, meta, plan["segrow"], mod)


def _moe_plan(cnt):
    per = TMD // TM
    cnt = cnt[:, 0, :N_EXPERTS].astype(jnp.int32).reshape(NBD, per, N_EXPERTS)
    cpad = (jnp.sum(cnt, axis=1) + CH - 1) // CH * CH
    tot = jnp.sum(cpad, axis=0)
    tiles = (tot + ETILE - 1) // ETILE
    tile_end = jnp.cumsum(tiles)
    offs = (tile_end - tiles) * ETILE
    base = offs[None, :] + jnp.cumsum(cpad, axis=0) - cpad
    seg = jnp.cumsum(cpad, axis=1) - cpad
    segrows = seg[:, None, :] + jnp.cumsum(cnt, axis=1) - cnt
    tid = jnp.arange(N_ETILES, dtype=jnp.int32)
    te = jnp.sum((tile_end[None, :] <= tid[:, None]).astype(jnp.int32), axis=1)
    used = te < N_EXPERTS
    last = jnp.max(jnp.where(tiles > 0, jnp.arange(N_EXPERTS, dtype=jnp.int32), 0))
    te = jnp.where(used, te, last)
    rows = tot[te] - (tid - (tile_end - tiles)[te]) * ETILE
    tvalid = jnp.where(used, jnp.where(rows > ETILE // 2, 2, 1), 0).astype(jnp.int32)
    tfirst = jnp.concatenate([jnp.ones((1,), jnp.int32), (te[1:] != te[:-1]).astype(jnp.int32)])
    eid = jnp.arange(N_EXPERTS, dtype=jnp.int32)
    later = (eid[None, :] > eid[:, None]) & (tiles[None, :] > 0)
    nxt = jnp.min(jnp.where(later, eid[None, :], N_EXPERTS), axis=1)
    nxt = jnp.where(nxt < N_EXPERTS, nxt, -1)
    return {
        "tnext": nxt[te], "wslot": (jnp.cumsum(tfirst) - 1) % 2, "tfirst": tfirst,
        "base": base.reshape(-1), "seg": seg.reshape(-1), "nch": (cpad // CH).reshape(-1),
        "ntile": jnp.sum(cpad, axis=1) // CH, "tail0": offs + tot, "tailn": (tiles * ETILE - tot) // CH,
        "segrow": jnp.pad(segrows.astype(F32), ((0, 0), (0, 0), (0, LANES - N_EXPERTS))),
        "te": te, "tvalid": tvalid, "xblk": jnp.where(used, tid, tile_end[-1] - 1),
    }


def _moe(layer, x1, h2, meta, cnt, mod, w_gu, b_gu, w_down, b_down, split_out=False):
    plan = _moe_plan(cnt)
    xs = _dispatch(plan, h2, meta)
    y = _experts(layer, plan, xs, w_gu, b_gu, w_down, b_down)
    return _combine(plan, y, x1, meta, mod, split_out)


_HK, _HV = H_C * DK_C, H_C * DV_C
_ODD_MAIN = 2 * _HK + 2 * _HV


def _odd_in_kernel(x_ref, mod_ref, nmix_ref, win_ref, wgk_ref, bgk_ref, q_ref, k_ref, v_ref, g_ref, la_ref,
                   lamin_ref):
    m = mod_ref[0]
    h = _rms(x_ref[...]) * nmix_ref[...] * (1.0 + m[1:2]) + m[0:1]
    a = _dot(h.astype(BF16), win_ref[...])
    q_ref[...] = a[:, :_HK] * (DK_C ** -0.5)
    k_ref[...] = a[:, _HK:2 * _HK]
    v_ref[...] = a[:, 2 * _HK:2 * _HK + _HV]
    g_ref[...] = a[:, 2 * _HK + _HV:_ODD_MAIN]
    z = _dot(a[:, _ODD_MAIN:].astype(BF16), wgk_ref[...]) + bgk_ref[...]
    la = (jnp.minimum(z, 0.0) - jnp.log(1.0 + jnp.exp(-jnp.abs(z)))) * (1.0 / GATE_TAU)
    la_ref[...] = la
    chunk_tot = jnp.sum(la.reshape(TM // GLA_CHUNK, GLA_CHUNK, 2 * _HK), axis=1)
    lamin_ref[0] = jnp.broadcast_to(jnp.min(chunk_tot, axis=(0, 1), keepdims=True), (SUBLANES, LANES))


def _pairwise_att(q, k, c, forward):
    tcol = lax.broadcasted_iota(jnp.int32, (GLA_CHUNK, 1), 0)
    trow = lax.broadcasted_iota(jnp.int32, (GLA_CHUNK, DK_C), 0)
    scol = lax.broadcasted_iota(jnp.int32, (GLA_CHUNK, GLA_CHUNK), 1)

    def body(s, att):
        k_s = jnp.sum(jnp.where(trow == s, k, 0.0), axis=0, keepdims=True)
        c_s = jnp.sum(jnp.where(trow == s, c, 0.0), axis=0, keepdims=True)
        allowed = (tcol >= s) if forward else (tcol <= s)
        e = jnp.exp(jnp.where(allowed, c - c_s, -jnp.inf))
        return jnp.where(scol == s, jnp.sum(q * k_s * e, axis=-1, keepdims=True), att)

    return lax.fori_loop(0, GLA_CHUNK, body, jnp.zeros((GLA_CHUNK, GLA_CHUNK), F32))


def _gla_kernel(has_init, nchunk, strong_ref, *refs):
    if has_init:
        (qf, kf, vf, laf, qb, kb, vb, lab, s0f, s0b, _, _, of_ref, ob_ref, st) = refs
    else:
        (qf, kf, vf, laf, qb, kb, vb, lab, of_ref, ob_ref, sf_ref, sb_ref, st) = refs
    j = pl.program_id(1)

    @pl.when(j == 0)
    def _():
        if has_init:
            st[0] = s0f[...]
            st[1] = s0b[...]
        else:
            st[...] = jnp.zeros_like(st)

    row = lax.broadcasted_iota(jnp.int32, (GLA_CHUNK, GLA_CHUNK), 0)
    col = lax.broadcasted_iota(jnp.int32, (GLA_CHUNK, GLA_CHUNK), 1)

    def scan_step(factored):
        for d, (q_r, k_r, v_r, la_r, o_r) in enumerate(((qf, kf, vf, laf, of_ref), (qb, kb, vb, lab, ob_ref))):
            keep = (col <= row) if d == 0 else (col >= row)
            tri = jnp.where(keep, 1.0, 0.0).astype(BF16)
            for s in range(GLA_NS):
                g = la_r[s]
                g_hi = g.astype(BF16)
                g_r = g - g_hi.astype(F32)
                g_mid = g_r.astype(BF16)
                g_lo = (g_r - g_mid.astype(F32)).astype(BF16)
                c = _dot(tri, g_hi) + _dot(tri, g_mid) + _dot(tri, g_lo)
                tot = jnp.sum(g, axis=0, keepdims=True)
                decay = jnp.exp(tot)
                q, k = q_r[s], k_r[s]
                qe = (q * jnp.exp(c)).astype(BF16)
                if factored:
                    e_neg = jnp.exp(-c)
                    kd = (k * e_neg).astype(BF16)
                    k2 = (k * (e_neg * decay)).astype(BF16)
                else:
                    k2 = (k * jnp.exp(tot - c)).astype(BF16)
                v = v_r[s].astype(BF16)
                for h in range(H_C):
                    kc = slice(h * DK_C, (h + 1) * DK_C)
                    vc = slice(h * DV_C, (h + 1) * DV_C)
                    if factored:
                        att = jnp.where(keep, _dot_nt(qe[:, kc], kd[:, kc]), 0.0)
                    else:
                        att = _pairwise_att(q[:, kc], k[:, kc], c[:, kc], d == 0)
                    s_t = st[d, s, h]
                    o_r[s, :, vc] = _dot_nt(qe[:, kc], s_t.astype(BF16)) + _dot(att.astype(BF16), v[:, vc])
                    st[d, s, h] = s_t * decay[:, kc] + _dot_tn(v[:, vc], k2[:, kc])

    pl.when(strong_ref[0] == 0)(lambda: scan_step(True))
    pl.when(strong_ref[0] != 0)(lambda: scan_step(False))

    if not has_init:
        @pl.when(j == nchunk - 1)
        def _():
            sf_ref[...] = st[0]
            sb_ref[...] = st[1]


def _gla_call(has_init, nseq, seqlen, row0, strong, q, k, v, la, prev_f=None, prev_b=None, s0f=None, s0b=None):
    nchunk = seqlen // GLA_CHUNK
    nview = T // seqlen
    g0 = row0 // seqlen // GLA_NS
    view = lambda a: a.reshape(nview, seqlen, a.shape[-1])
    fwd = lambda g, j, *_: (g0 + g, j, 0)
    bwd = lambda g, j, *_: (g0 + g, nchunk - 1 - j, 0)
    bwd_la = lambda g, j, *_: (g0 + g, nchunk - 1 - j, 1)
    blk = lambda w, m: pl.BlockSpec((GLA_NS, GLA_CHUNK, w), m)
    state_spec = pl.BlockSpec((GLA_NS, H_C, DV_C, DK_C), lambda g, j, *_: (g, 0, 0, 0))
    in_specs = [blk(_HK, fwd), blk(_HK, fwd), blk(_HV, fwd), blk(_HK, fwd),
                blk(_HK, bwd), blk(_HK, bwd), blk(_HV, bwd), blk(_HK, bwd_la)]
    args = [view(q), view(k), view(v), view(la)] * 2
    aliases = {}
    if has_init:
        in_specs += [state_spec, state_spec] + [pl.BlockSpec(memory_space=pl.ANY)] * 2
        args += [s0f, s0b, view(prev_f), view(prev_b)]
        aliases = {len(args) - 1: 0, len(args): 1}
    out_specs = [blk(_HV, fwd), blk(_HV, bwd)]
    out_shape = [jax.ShapeDtypeStruct((nview, seqlen, _HV), F32)] * 2
    if not has_init:
        out_specs += [state_spec, state_spec]
        out_shape += [jax.ShapeDtypeStruct((nseq, H_C, DV_C, DK_C), F32)] * 2
    grid_spec = pltpu.PrefetchScalarGridSpec(
        num_scalar_prefetch=1,
        grid=(nseq // GLA_NS, nchunk),
        in_specs=in_specs,
        out_specs=out_specs,
        scratch_shapes=[pltpu.VMEM((2, GLA_NS, H_C, DV_C, DK_C), F32)],
    )
    outs = pl.pallas_call(
        functools.partial(_gla_kernel, has_init, nchunk),
        grid_spec=grid_spec,
        out_shape=out_shape,
        input_output_aliases=aliases,
        compiler_params=_params(("arbitrary", "arbitrary")),
        name="gla_sample" if has_init else "gla_prompt",
    )(strong, *args)
    return [outs[0].reshape(T, _HV), outs[1].reshape(T, _HV)] + list(outs[2:])


def _rope_tables():
    half = ROPE // 2
    inv_freq = np.power(np.float32(ROPE_THETA), -np.arange(0, half, 2, dtype=np.float32) / np.float32(half))
    n = np.arange(SAMPLE_LEN)
    row = (n // GRID_W).astype(np.float32)
    col = (n % GRID_W).astype(np.float32)
    ang_r = (row[:, None] * inv_freq[None, :]).astype(np.float32)
    ang_c = (col[:, None] * inv_freq[None, :]).astype(np.float32)
    nf = half // 2
    c = np.ones((TM + SAMPLE_LEN, LANES), np.float32)
    s1 = np.zeros((TM + SAMPLE_LEN, LANES), np.float32)
    s2 = np.zeros((TM + SAMPLE_LEN, LANES), np.float32)
    for base, ang in ((NOPE, ang_r), (NOPE + half, ang_c)):
        c[TM:, base:base + nf] = np.cos(ang)
        c[TM:, base + nf:base + half] = np.cos(ang)
        s1[TM:, base:base + nf] = -np.sin(ang)
        s2[TM:, base + nf:base + half] = np.sin(ang)
    return jnp.asarray(c), jnp.asarray(s1), jnp.asarray(s2)


def _pad_heads(w, nheads, width, lo=0):
    k = w.shape[0]
    w = w.reshape(k, nheads, width)
    w = jnp.pad(w, ((0, 0), (0, 0), (lo, HP - lo - width)))
    return w.reshape(k, nheads * HP)


def _row128(v, lo=0):
    return jnp.pad(v, (lo, LANES - lo - v.shape[0])).reshape(1, LANES)


def _even_layer(xs, xp, mod, nmix, w_in, q_a_norm, w_uq, q_norm, kv_a_norm, w_ukv, k_norm, v_norm, w_s, b_s, w_out,
                cache_ckv, cache_kpe):
    s = np.cumsum([Q_LORA, KV_LORA, ROPE, W_B])
    w_q, w_ckv, w_kpe, w_u, w_v = (w_in[:, :s[0]], w_in[:, s[0]:s[1]], w_in[:, s[1]:s[2]], w_in[:, s[2]:s[3]],
                                   w_in[:, s[3]:])
    w_kpe = jnp.pad(w_kpe, ((0, 0), (NOPE, LANES - QK_DIM)))
    win = jnp.concatenate([w_q, w_ckv, w_u, w_v, w_kpe], axis=1).astype(BF16)
    wuq = _pad_heads(w_uq, H_A, QK_DIM).astype(BF16)
    ukv = w_ukv.reshape(KV_LORA, H_A, NOPE + V_A)
    wuk = _pad_heads(ukv[:, :, :NOPE].reshape(KV_LORA, H_A * NOPE), H_A, NOPE)
    wuv = _pad_heads(ukv[:, :, NOPE:].reshape(KV_LORA, H_A * V_A), H_A, V_A)
    wukv = jnp.concatenate([wuk, wuv], axis=1).astype(BF16)
    qgain = _row128(q_norm * (QK_DIM ** -0.5 * LOG2_E))
    kgain = _row128(k_norm)
    bias = b_s.reshape(G_B // 2, 2, CHUNK_B)
    bias = jnp.concatenate([jnp.broadcast_to(bias[:, 0, :, None], (G_B // 2, CHUNK_B, C_B)),
                            jnp.broadcast_to(bias[:, 1, :, None], (G_B // 2, CHUNK_B, C_B))], axis=-1)
    rc, rs1, rs2 = _rope_tables()
    rope_spec = pl.BlockSpec((TM, LANES), lambda i: (_rope_blk(i), 0))
    width = H_A * HP
    q, k, v, ob, ckv, kpe = pl.pallas_call(
        _even_in_kernel,
        grid=(NBLK,),
        in_specs=[
            _X_SAMPLE, _X_PROMPT, _MOD, _full((1, D)), _full((D, _WIN_N)), _full((1, Q_LORA)),
            _full((Q_LORA, width)),
            _full((1, LANES)), _full((1, KV_LORA)), _full((KV_LORA, 2 * width)), _full((1, LANES)),
            _full((1, W_B)), _full((G_B // 2, 2 * CHUNK_B, CHUNK_B)), _full((G_B // 2, CHUNK_B, LANES)),
            rope_spec, rope_spec, rope_spec,
        ],
        out_specs=[_TOK(width), _TOK(width), _TOK(width), _TOK(W_B), _TOK(KV_LORA), _TOK(ROPE)],
        out_shape=[
            jax.ShapeDtypeStruct((T, width), BF16), jax.ShapeDtypeStruct((T, width), BF16),
            jax.ShapeDtypeStruct((T, width), BF16), jax.ShapeDtypeStruct((T, W_B), BF16),
            jax.ShapeDtypeStruct((T, KV_LORA), F32), jax.ShapeDtypeStruct((T, ROPE), F32),
        ],
        compiler_params=_params(("arbitrary",)),
        name="even_in",
    )(xs, xp, mod, nmix.reshape(1, D), win, q_a_norm.reshape(1, Q_LORA), wuq, qgain, kv_a_norm.reshape(1, KV_LORA),
      wukv, kgain, v_norm.reshape(1, W_B), w_s.astype(BF16).reshape(G_B // 2, 2 * CHUNK_B, CHUNK_B), bias, rc, rs1,
      rs2)

    n_ctx = N_SAMPLE_SEQ * PAST_LEN
    kpe_ctx = jnp.pad(cache_kpe.reshape(n_ctx, ROPE), ((0, 0), (NOPE, LANES - QK_DIM)))
    k_ctx, v_ctx = pl.pallas_call(
        _ctx_kv_kernel,
        grid=(n_ctx // TM,),
        in_specs=[_TOK(KV_LORA), _TOK(LANES), _full((KV_LORA, 2 * width)), _full((1, LANES))],
        out_specs=[_TOK(width), _TOK(width)],
        out_shape=[jax.ShapeDtypeStruct((n_ctx, width), BF16)] * 2,
        compiler_params=_params(("arbitrary",)),
        name="ctx_kv",
    )(cache_ckv.reshape(n_ctx, KV_LORA), kpe_ctx, wukv, kgain)

    oa = _attention(q, k, v, k_ctx, v_ctx)
    woa = jnp.pad(w_out[:H_A * V_A].reshape(H_A, V_A, D), ((0, 0), (0, HP - V_A), (0, 0))).reshape(width, D)
    return oa, ob, woa.astype(BF16), w_out[H_A * V_A:].astype(BF16), ckv, kpe


def kernel(x_prompt, x_sample, cache_mla_ckv, cache_mla_kpe, state_gla_fwd, state_gla_bwd, c, c_ctx, ada_w, ada_b,
           norm_mix, norm_ffn, even_w_in, mla_q_a_norm, mla_w_uq, mla_q_norm, mla_kv_a_norm, mla_w_ukv, mla_k_norm,
           cmlp_v_norm, cmlp_w_s, cmlp_b_s, even_w_out, odd_w_in, gla_w_gk_fwd, gla_b_gk_fwd, gla_w_gk_bwd,
           gla_b_gk_bwd, gla_o_norm, odd_w_out, moe_w_router, moe_b_router, moe_w_gu, moe_b_gu, moe_w_down,
           moe_b_down):
    xs0, xp0 = x_sample.reshape(N_SAMPLE, D), x_prompt.reshape(N_PROMPT, D)
    cond8 = jnp.concatenate([c_ctx[None, :], c, jnp.zeros((SUBLANES - 1 - N_SAMPLE_SEQ, D), F32)], axis=0)
    mods = _adaln(cond8, ada_w, ada_b)
    wr = jnp.pad(moe_w_router, ((0, 0), (0, 0), (0, LANES - N_EXPERTS)))
    wr_hi = wr.astype(BF16)
    wr = jnp.concatenate([wr_hi, (wr - wr_hi.astype(F32)).astype(BF16)], axis=-1)
    br = jnp.pad(moe_b_router, ((0, 0), (0, LANES - N_EXPERTS))).reshape(2, 1, LANES)

    oa, ob, woa, wob, ckv, kpe = _even_layer(
        xs0, xp0, mods[0], norm_mix[0], even_w_in[0], mla_q_a_norm[0], mla_w_uq[0], mla_q_norm[0], mla_kv_a_norm[0],
        mla_w_ukv[0], mla_k_norm[0], cmlp_v_norm[0], cmlp_w_s[0], cmlp_b_s[0], even_w_out[0],
        cache_mla_ckv[:, 0], cache_mla_kpe[:, 0])
    width = H_A * HP
    x1, h2, meta, cnt = pl.pallas_call(
        _even_out_kernel,
        grid=(NBLK,),
        in_specs=[_TOK(width), _TOK(W_B), _X_SAMPLE, _X_PROMPT, _MOD, _full((width, D)), _full((W_B, D)),
                  _full((1, D)),
                  _full((D, 2 * LANES)), _full((1, LANES))],
        out_specs=_PROLOGUE_OUT_SPECS,
        out_shape=_PROLOGUE_OUT_SHAPE,
        compiler_params=_params(("arbitrary",)),
        name="even_out",
    )(oa, ob, xs0, xp0, mods[0], woa, wob, norm_ffn[0].reshape(1, D), wr[0], br[0])
    x2 = _moe(0, x1, h2, meta, cnt, mods[0], moe_w_gu, moe_b_gu, moe_w_down, moe_b_down)

    w_in = odd_w_in[0]
    win = jnp.concatenate([w_in, jnp.zeros((D, LANES - 2 * GATE_RANK), F32)], axis=1).astype(BF16)
    wgk = jnp.zeros((LANES, 2 * _HK), F32)
    wgk = wgk.at[:GATE_RANK, :_HK].set(gla_w_gk_fwd[0]).at[GATE_RANK:2 * GATE_RANK, _HK:].set(gla_w_gk_bwd[0])
    bgk = jnp.concatenate([gla_b_gk_fwd[0], gla_b_gk_bwd[0]]).reshape(1, 2 * _HK)
    q, k, v, g, la, lamin = pl.pallas_call(
        _odd_in_kernel,
        grid=(NBLK,),
        in_specs=[_TOK(D), _MOD, _full((1, D)), _full((D, _ODD_MAIN + LANES)), _full((LANES, 2 * _HK)),
                  _full((1, 2 * _HK))],
        out_specs=[_TOK(_HK), _TOK(_HK), _TOK(_HV), _TOK(_HV), _TOK(2 * _HK), _TILE_ROW],
        out_shape=[jax.ShapeDtypeStruct((T, _HK), F32), jax.ShapeDtypeStruct((T, _HK), F32),
                   jax.ShapeDtypeStruct((T, _HV), F32), jax.ShapeDtypeStruct((T, _HV), F32),
                   jax.ShapeDtypeStruct((T, 2 * _HK), F32), jax.ShapeDtypeStruct((NBLK, SUBLANES, LANES), F32)],
        compiler_params=_params(("arbitrary",)),
        name="odd_in",
    )(x2, mods[1], norm_mix[1].reshape(1, D), win, wgk.astype(BF16), bgk)

    strong = (jnp.min(lamin) < -GLA_SAFE_LOG_DECAY).astype(jnp.int32).reshape(1)
    of, obk, st_f, st_b = _gla_call(False, N_PROMPT_SEQ, PROMPT_LEN, N_SAMPLE, strong, q, k, v, la)
    s0f = state_gla_fwd[:, 0].transpose(0, 1, 3, 2)
    s0b = state_gla_bwd[:, 0].transpose(0, 1, 3, 2)
    of, obk = _gla_call(True, N_SAMPLE_SEQ, SAMPLE_LEN, 0, strong, q, k, v, la, of, obk, s0f, s0b)

    x3, h2, meta, cnt = pl.pallas_call(
        _odd_out_kernel,
        grid=(NBLK,),
        in_specs=[_TOK(_HV), _TOK(_HV), _TOK(_HV), _TOK(D), _MOD, _full((1, DV_C)), _full((_HV, D)),
                  _full((1, D)), _full((D, 2 * LANES)), _full((1, LANES))],
        out_specs=_PROLOGUE_OUT_SPECS,
        out_shape=_PROLOGUE_OUT_SHAPE,
        compiler_params=_params(("arbitrary",)),
        name="odd_out",
    )(of, obk, g, x2, mods[1], gla_o_norm[0].reshape(1, DV_C), odd_w_out[0].astype(BF16),
      norm_ffn[1].reshape(1, D), wr[1], br[1])
    ys, yp = _moe(1, x3, h2, meta, cnt, mods[1], moe_w_gu, moe_b_gu, moe_w_down, moe_b_down, split_out=True)

    y_sample = ys.reshape(N_SAMPLE_SEQ, SAMPLE_LEN, D)
    y_prompt = yp.reshape(N_PROMPT_SEQ, PROMPT_LEN, D)
    new_ckv = ckv[N_SAMPLE:].reshape(N_PROMPT_SEQ, 1, PROMPT_LEN, KV_LORA)
    new_kpe = kpe[N_SAMPLE:].reshape(N_PROMPT_SEQ, 1, PROMPT_LEN, ROPE)
    new_fwd = st_f.transpose(0, 1, 3, 2)[:, None]
    new_bwd = st_b.transpose(0, 1, 3, 2)[:, None]
    return (y_prompt, y_sample, new_ckv, new_kpe, new_fwd, new_bwd)
```

```python
import functools

import numpy as np
import jax
import jax.numpy as jnp
from jax import lax
from jax.experimental import pallas as pl
from jax.experimental.pallas import tpu as pltpu

F32 = jnp.float32
BF16 = jnp.bfloat16

D = 1024
N_PROMPT_SEQ, PROMPT_LEN = 16, 256
N_SAMPLE_SEQ, SAMPLE_LEN = 4, 2048
PAST_LEN = 512
N_PROMPT = N_PROMPT_SEQ * PROMPT_LEN
N_SAMPLE = N_SAMPLE_SEQ * SAMPLE_LEN
T = N_PROMPT + N_SAMPLE
EPS = 1e-6
GRID_W = 64
H_A, Q_LORA, KV_LORA, NOPE, ROPE, V_A = 8, 512, 256, 64, 32, 64
QK_DIM = NOPE + ROPE
G_B, C_B, W_B, CHUNK_B = 8, 64, 512, 128
H_C, DK_C, DV_C, GATE_RANK, GATE_TAU, GLA_CHUNK = 4, 128, 256, 16, 16.0, 64
N_EXPERTS, TOP_K, D_FF = 32, 4, 1024
SWIGLU_LIMIT, SWIGLU_ALPHA = 7.0, 1.702
ROPE_THETA = 10000.0
LOG2_E = 1.4426950408889634

LANES = 128
SUBLANES = 8
VMEM_LIMIT = 56 * 1024 * 1024

TM = 512
NBLK = T // TM
SAMPLE_BLKS = N_SAMPLE // TM
BLKS_PER_SAMPLE_SEQ = SAMPLE_LEN // TM
ATT_TQ = 256
GLA_NS = 4
GLA_SAFE_LOG_DECAY = 60.0
ETILE = 512
TMD = TM
NBD = T // TMD
NBD_SAMPLE = N_SAMPLE // TMD
CH = 2 * SUBLANES
_MAX_LOCAL = TMD * TOP_K + N_EXPERTS * (CH - 1)
_POS_BLOCK = 256
ROWS_L = -(-_MAX_LOCAL // _POS_BLOCK) * _POS_BLOCK
N_ETILES = -(-NBD * _MAX_LOCAL // ETILE) + N_EXPERTS
P_ROWS = N_ETILES * ETILE
HP = LANES


def _cond_row(i):
    return jnp.where(i < SAMPLE_BLKS, 1 + i // BLKS_PER_SAMPLE_SEQ, 0)


def _rope_blk(i):
    return jnp.where(i < SAMPLE_BLKS, 1 + i % BLKS_PER_SAMPLE_SEQ, 0)


def _rms(x):
    return x * lax.rsqrt(jnp.mean(x * x, axis=-1, keepdims=True) + EPS)


def _gelu(x):
    return 0.5 * x * (1.0 + jnp.tanh(0.7978845608028654 * (x + 0.044715 * (x * x * x))))


def _silu(x):
    return x * jax.nn.sigmoid(x)


def _dot(a, b):
    return jnp.dot(a, b, preferred_element_type=F32)


def _dot_nt(a, b):
    return lax.dot_general(a, b, (((1,), (1,)), ((), ())), preferred_element_type=F32)


def _dot_tn(a, b):
    return lax.dot_general(a, b, (((0,), (0,)), ((), ())), preferred_element_type=F32)


def _params(sem, vmem=VMEM_LIMIT):
    return pltpu.CompilerParams(dimension_semantics=sem, vmem_limit_bytes=vmem)


def _full(shape):
    nd = len(shape)
    return pl.BlockSpec(shape, lambda *_: (0,) * nd)


ADA_TN = 1536


def _adaln_kernel(c_ref, w_ref, b_ref, o_ref):
    s = _silu(c_ref[...]).astype(BF16)
    o_ref[0] = _dot(s, w_ref[0].astype(BF16)) + b_ref[0]


def _adaln(cond8, ada_w, ada_b):
    depth = ada_w.shape[0]
    n = ada_w.shape[2]
    out = pl.pallas_call(
        _adaln_kernel,
        grid=(depth, n // ADA_TN),
        in_specs=[
            pl.BlockSpec((SUBLANES, D), lambda l, j: (0, 0)),
            pl.BlockSpec((1, D, ADA_TN), lambda l, j: (l, 0, j)),
            pl.BlockSpec((1, 1, ADA_TN), lambda l, j: (l, 0, j)),
        ],
        out_specs=pl.BlockSpec((1, SUBLANES, ADA_TN), lambda l, j: (l, 0, j)),
        out_shape=jax.ShapeDtypeStruct((depth, SUBLANES, n), F32),
        compiler_params=_params(("arbitrary", "arbitrary")),
        name="adaln",
    )(cond8, ada_w, ada_b.reshape(depth, 1, n))
    return out.reshape(depth, SUBLANES, 6, D)


_QC0, _CKV0, _U0, _V0, _KPE0, _WIN_N = 0, 512, 768, 1280, 1792, 1920


def _rope(y, c, s1, s2):
    return y * c + pltpu.roll(y, LANES - 8, 1) * s1 + pltpu.roll(y, 8, 1) * s2


def _k_heads(k_raw, kpe128, kp_rot, kgain, k_ref):
    sskpe = jnp.sum(kpe128 * kpe128, axis=-1, keepdims=True)
    for h in range(H_A):
        kb = k_raw[:, h * HP:(h + 1) * HP]
        r = lax.rsqrt((jnp.sum(kb * kb, axis=-1, keepdims=True) + sskpe) * (1.0 / QK_DIM) + EPS)
        k_ref[:, h * HP:(h + 1) * HP] = ((kb * kgain + kp_rot) * r).astype(BF16)


def _input_rows(xs_ref, xp_ref):
    return jnp.where(pl.program_id(0) < SAMPLE_BLKS, xs_ref[...], xp_ref[...])


_X_SAMPLE = pl.BlockSpec((TM, D), lambda i, *_: (jnp.minimum(i, SAMPLE_BLKS - 1), 0))
_X_PROMPT = pl.BlockSpec((TM, D), lambda i, *_: (jnp.maximum(i - SAMPLE_BLKS, 0), 0))


def _even_in_kernel(xs_ref, xp_ref, mod_ref, nmix_ref, win_ref, qan_ref, wuq_ref, qgain_ref, kvan_ref, wukv_ref,
                    kgain_ref, vnorm_ref, ws_ref, bs_ref, rc_ref, rs1_ref, rs2_ref,
                    q_ref, k_ref, v_ref, ob_ref, ckv_ref, kpe_ref):
    m = mod_ref[0]
    h = _rms(_input_rows(xs_ref, xp_ref)) * nmix_ref[...] * (1.0 + m[1:2]) + m[0:1]
    a = _dot(h.astype(BF16), win_ref[...])
    qc = a[:, _QC0:_CKV0]
    ckv = a[:, _CKV0:_U0]
    u = a[:, _U0:_V0]
    vv = a[:, _V0:_KPE0]
    kpe128 = a[:, _KPE0:_WIN_N]

    ckv_n = _rms(ckv) * kvan_ref[...]
    ckv_ref[...] = ckv_n
    kpe_ref[...] = kpe128[:, NOPE:QK_DIM]

    rc, rs1, rs2 = rc_ref[...], rs1_ref[...], rs2_ref[...]
    qn = (_rms(qc) * qan_ref[...]).astype(BF16)
    qr = _dot(qn, wuq_ref[...])
    qgain = qgain_ref[...]
    for hh in range(H_A):
        blk = qr[:, hh * HP:(hh + 1) * HP]
        r = lax.rsqrt(jnp.sum(blk * blk, axis=-1, keepdims=True) * (1.0 / QK_DIM) + EPS)
        q_ref[:, hh * HP:(hh + 1) * HP] = _rope(blk * r * qgain, rc, rs1, rs2).astype(BF16)

    kv = _dot(ckv_n.astype(BF16), wukv_ref[...])
    v_ref[...] = kv[:, H_A * HP:].astype(BF16)
    kgain = kgain_ref[...]
    kp_rot = _rope(kpe128 * kgain, rc, rs1, rs2)
    _k_heads(kv[:, :H_A * HP], kpe128, kp_rot, kgain, k_ref)

    ug = _gelu(u)
    vn = (_rms(_gelu(vv)) * vnorm_ref[...]).astype(BF16)
    low = lax.broadcasted_iota(jnp.int32, (CHUNK_B, LANES), 1) < C_B
    for c in range(TM // CHUNK_B):
        rows = slice(c * CHUNK_B, (c + 1) * CHUNK_B)
        for p in range(G_B // 2):
            cols = slice(p * LANES, (p + 1) * LANES)
            blk = vn[rows, cols]
            both = _dot(ws_ref[p], blk)
            mixed = jnp.where(low, both[:CHUNK_B], both[CHUNK_B:]) + bs_ref[p]
            ob_ref[rows, cols] = (ug[rows, cols] * mixed).astype(BF16)


def _ctx_kv_kernel(ckv_ref, kpe_ref, wukv_ref, kgain_ref, k_ref, v_ref):
    kv = _dot(ckv_ref[...].astype(BF16), wukv_ref[...])
    v_ref[...] = kv[:, H_A * HP:].astype(BF16)
    kgain = kgain_ref[...]
    kpe128 = kpe_ref[...]
    _k_heads(kv[:, :H_A * HP], kpe128, kpe128 * kgain, kgain, k_ref)


def _attn_self_kernel(q_ref, k_ref, v_ref, o_ref):
    for h in range(H_A):
        cols = slice(h * HP, (h + 1) * HP)
        s = _dot_nt(q_ref[:, cols], k_ref[:, cols])
        p = jnp.exp2(s - jnp.max(s, axis=-1, keepdims=True))
        inv = 1.0 / jnp.sum(p, axis=-1, keepdims=True)
        o_ref[:, cols] = (_dot(p.astype(BF16), v_ref[:, cols]) * inv).astype(BF16)


def _attn_ctx_kernel(q_ref, k_ref, v_ref, kc_ref, vc_ref, prev_ref, o_ref):
    del prev_ref
    for h in range(H_A):
        cols = slice(h * HP, (h + 1) * HP)
        q = q_ref[:, cols]
        s1 = _dot_nt(q, k_ref[:, cols])
        s2 = _dot_nt(q, kc_ref[:, cols])
        mx = jnp.maximum(jnp.max(s1, axis=-1, keepdims=True), jnp.max(s2, axis=-1, keepdims=True))
        p1 = jnp.exp2(s1 - mx)
        p2 = jnp.exp2(s2 - mx)
        inv = 1.0 / (jnp.sum(p1, axis=-1, keepdims=True) + jnp.sum(p2, axis=-1, keepdims=True))
        o = _dot(p1.astype(BF16), v_ref[:, cols]) + _dot(p2.astype(BF16), vc_ref[:, cols])
        o_ref[:, cols] = (o * inv).astype(BF16)


def _attention(q, k, v, k_ctx, v_ctx):
    width = H_A * HP
    first = N_SAMPLE // PROMPT_LEN
    o = pl.pallas_call(
        _attn_self_kernel,
        grid=(N_PROMPT_SEQ,),
        in_specs=[pl.BlockSpec((PROMPT_LEN, width), lambda i: (first + i, 0))] * 3,
        out_specs=pl.BlockSpec((PROMPT_LEN, width), lambda i: (first + i, 0)),
        out_shape=jax.ShapeDtypeStruct((T, width), BF16),
        compiler_params=_params(("arbitrary",)),
        name="attn_prompt",
    )(q, k, v)
    qblocks = SAMPLE_LEN // ATT_TQ
    qblk = lambda b, j: (b * qblocks + j, 0)
    return pl.pallas_call(
        _attn_ctx_kernel,
        grid=(N_SAMPLE_SEQ, qblocks),
        in_specs=[
            pl.BlockSpec((ATT_TQ, width), qblk),
            pl.BlockSpec((SAMPLE_LEN, width), lambda b, j: (b, 0)),
            pl.BlockSpec((SAMPLE_LEN, width), lambda b, j: (b, 0)),
            pl.BlockSpec((PAST_LEN, width), lambda b, j: (b, 0)),
            pl.BlockSpec((PAST_LEN, width), lambda b, j: (b, 0)),
            pl.BlockSpec(memory_space=pl.ANY),
        ],
        out_specs=pl.BlockSpec((ATT_TQ, width), qblk),
        out_shape=jax.ShapeDtypeStruct((T, width), BF16),
        input_output_aliases={5: 0},
        compiler_params=_params(("arbitrary", "arbitrary")),
        name="attn_sample",
    )(q, k, v, k_ctx, v_ctx, o)


_META_IDX, _META_RANK, _META_W = 0, TOP_K, 2 * TOP_K


def _moe_prologue(x1, m, nffn_ref, wr_ref, br_ref, x1_ref, h2_ref, meta_ref, cnt_ref):
    x1_ref[...] = x1
    h2 = _rms(x1) * nffn_ref[...] * (1.0 + m[4:5]) + m[3:4]
    h2_ref[...] = h2.astype(BF16)
    lane = lax.broadcasted_iota(jnp.int32, (TM, LANES), 1)
    lanef = lane.astype(F32)
    h_hi = h2.astype(BF16)
    h_lo = (h2 - h_hi.astype(F32)).astype(BF16)
    r = _dot(h_hi, wr_ref[...])
    logits = r[:, :LANES] + r[:, LANES:] + _dot(h_lo, wr_ref[:, :LANES]) + br_ref[...]
    work = jnp.where(lane < N_EXPERTS, logits, -jnp.inf)
    hots, vals = [], []
    for _ in range(TOP_K):
        mx = jnp.max(work, axis=-1, keepdims=True)
        idx = jnp.min(jnp.where(work == mx, lanef, float(LANES)), axis=-1, keepdims=True)
        hot = lanef == idx
        work = jnp.where(hot, -jnp.inf, work)
        hots.append((hot, idx))
        vals.append(mx)
    es = [jnp.exp(v - vals[0]) for v in vals]
    inv = 1.0 / (es[0] + es[1] + es[2] + es[3])
    sel = jnp.zeros((TM, LANES), F32)
    for hot, _ in hots:
        sel = jnp.where(hot, 1.0, sel)
    row = lax.broadcasted_iota(jnp.int32, (TM, TM), 0)
    col = lax.broadcasted_iota(jnp.int32, (TM, TM), 1)
    strict = jnp.where(row > col, 1.0, 0.0).astype(BF16)
    before = _dot(strict, sel.astype(BF16))
    meta = jnp.zeros((TM, LANES), F32)
    for kk, (hot, idx) in enumerate(hots):
        rank = jnp.sum(jnp.where(hot, before, 0.0), axis=-1, keepdims=True)
        meta = jnp.where(lane == _META_IDX + kk, idx, meta)
        meta = jnp.where(lane == _META_RANK + kk, rank, meta)
        meta = jnp.where(lane == _META_W + kk, es[kk] * inv, meta)
    meta_ref[...] = meta
    cnt_ref[0] = jnp.broadcast_to(jnp.sum(sel, axis=0, keepdims=True), (SUBLANES, LANES))


def _even_out_kernel(oa_ref, ob_ref, xs_ref, xp_ref, mod_ref, woa_ref, wob_ref, nffn_ref, wr_ref, br_ref,
                     x1_ref, h2_ref, meta_ref, cnt_ref):
    m = mod_ref[0]
    out = _dot(oa_ref[...], woa_ref[...]) + _dot(ob_ref[...], wob_ref[...])
    x1 = _input_rows(xs_ref, xp_ref) + m[2:3] * out
    _moe_prologue(x1, m, nffn_ref, wr_ref, br_ref, x1_ref, h2_ref, meta_ref, cnt_ref)


def _odd_out_kernel(of_ref, ob_ref, g_ref, x_ref, mod_ref, onorm_ref, wo_ref, nffn_ref, wr_ref, br_ref,
                    x1_ref, h2_ref, meta_ref, cnt_ref):
    m = mod_ref[0]
    onorm = onorm_ref[...]
    parts = []
    for h in range(H_C):
        cols = slice(h * DV_C, (h + 1) * DV_C)
        o = of_ref[:, cols] + ob_ref[:, cols]
        parts.append((_rms(o) * onorm * _silu(g_ref[:, cols])).astype(BF16))
    out = _dot(jnp.concatenate(parts, axis=-1), wo_ref[...])
    x1 = x_ref[...] + m[2:3] * out
    _moe_prologue(x1, m, nffn_ref, wr_ref, br_ref, x1_ref, h2_ref, meta_ref, cnt_ref)


_TOK = lambda w: pl.BlockSpec((TM, w), lambda i, *_: (i, 0))
_MOD = pl.BlockSpec((1, 6, D), lambda i, *_: (_cond_row(i), 0, 0))
_TILE_ROW = pl.BlockSpec((1, SUBLANES, LANES), lambda i, *_: (i, 0, 0))

_PROLOGUE_OUT_SPECS = [_TOK(D), _TOK(D), _TOK(LANES), _TILE_ROW]
_PROLOGUE_OUT_SHAPE = [
    jax.ShapeDtypeStruct((T, D), F32),
    jax.ShapeDtypeStruct((T, D), BF16),
    jax.ShapeDtypeStruct((T, LANES), F32),
    jax.ShapeDtypeStruct((NBLK, SUBLANES, LANES), F32),
]


def _local_positions(meta, seg_rows):
    lanef = lax.broadcasted_iota(jnp.int32, (TM, LANES), 1).astype(F32)
    pos = []
    for k in range(TOP_K):
        halves = []
        for half in range(TMD // TM):
            m = meta[half * TM:(half + 1) * TM]
            hot = lanef == m[:, _META_IDX + k:_META_IDX + k + 1]
            start = jnp.sum(jnp.where(hot, seg_rows[half:half + 1], 0.0), axis=-1, keepdims=True)
            halves.append(start + m[:, _META_RANK + k:_META_RANK + k + 1])
        pos.append(jnp.concatenate(halves, axis=0))
    return pos


def _scatter_matrix(pos, vals):
    lane = lax.broadcasted_iota(jnp.int32, (TMD, _POS_BLOCK), 1).astype(F32).astype(BF16)
    blocks = []
    for j in range(ROWS_L // _POS_BLOCK):
        acc = jnp.zeros((TMD, _POS_BLOCK), BF16)
        for p, v in zip(pos, vals):
            acc = jnp.where(lane == (p - float(j * _POS_BLOCK)).astype(BF16), v, acc)
        blocks.append(acc)
    return jnp.concatenate(blocks, axis=1)


def _segment_copies(t, base_ref, seg_ref, nch_ref, make):
    def per_expert(e, carry):
        q = t * N_EXPERTS + e
        local0, global0 = seg_ref[q], base_ref[q]

        def one(j, c):
            make(pl.multiple_of(local0 + j * CH, CH), pl.multiple_of(global0 + j * CH, CH)).start()
            return c

        return lax.fori_loop(0, nch_ref[q], one, carry)

    lax.fori_loop(0, N_EXPERTS, per_expert, 0)


def _drain(count, chunk_copy):
    def one(j, c):
        chunk_copy.wait()
        return c

    lax.fori_loop(0, count, one, 0)


def _dispatch_kernel(base_ref, seg_ref, nch_ref, ntile_ref, tail0_ref, tailn_ref, h_ref, meta_ref, segrow_ref,
                     xs_hbm, buf, zbuf, sem):
    i = pl.program_id(0)
    slot = i % 2
    chunk = lambda s: pltpu.make_async_copy(buf.at[s, pl.ds(0, CH)], xs_hbm.at[pl.ds(0, CH)], sem.at[s])

    @pl.when(i == 0)
    def _():
        zbuf[...] = jnp.zeros_like(zbuf)

        def per_expert(e, carry):
            def one(j, c):
                pltpu.make_async_copy(zbuf, xs_hbm.at[pl.ds(pl.multiple_of(tail0_ref[e] + j * CH, CH), CH)],
                                      sem.at[2]).start()
                return c

            lax.fori_loop(0, tailn_ref[e], one, 0)
            return carry + tailn_ref[e]

        total = lax.fori_loop(0, N_EXPERTS, per_expert, 0)
        _drain(total, pltpu.make_async_copy(zbuf, xs_hbm.at[pl.ds(0, CH)], sem.at[2]))

    pos = _local_positions(meta_ref[...], segrow_ref[0])
    riota = lax.broadcasted_iota(jnp.int32, (TMD, ROWS_L), 1).astype(F32)
    pt = jnp.zeros((TMD, ROWS_L), F32)
    for p in pos:
        pt = jnp.where(riota == p, 1.0, pt)
    buf[slot] = _dot_tn(pt.astype(BF16), h_ref[...]).astype(BF16)

    make = lambda s, d: pltpu.make_async_copy(buf.at[slot, pl.ds(s, CH)], xs_hbm.at[pl.ds(d, CH)], sem.at[slot])
    _segment_copies(i, base_ref, seg_ref, nch_ref, make)

    @pl.when(i > 0)
    def _():
        _drain(ntile_ref[i - 1], chunk(1 - slot))

    @pl.when(i == NBD - 1)
    def _():
        _drain(ntile_ref[i], chunk(slot))


_TOKD = lambda w: pl.BlockSpec((TMD, w), lambda i, *_: (i, 0))
_SEGROWS = pl.BlockSpec((1, TMD // TM, LANES), lambda i, *_: (i, 0, 0))


def _dispatch(plan, h2, meta):
    grid_spec = pltpu.PrefetchScalarGridSpec(
        num_scalar_prefetch=6,
        grid=(NBD,),
        in_specs=[_TOKD(D), _TOKD(LANES), _SEGROWS],
        out_specs=pl.BlockSpec(memory_space=pl.ANY),
        scratch_shapes=[pltpu.VMEM((2, ROWS_L, D), BF16), pltpu.VMEM((CH, D), BF16),
                        pltpu.SemaphoreType.DMA((3,))],
    )
    return pl.pallas_call(
        _dispatch_kernel,
        grid_spec=grid_spec,
        out_shape=jax.ShapeDtypeStruct((P_ROWS, D), BF16),
        compiler_params=_params(("arbitrary",)),
        name="moe_dispatch",
    )(plan["base"], plan["seg"], plan["nch"], plan["ntile"], plan["tail0"], plan["tailn"], h2, meta, plan["segrow"])


def _expert_kernel(layer, te_ref, tfirst_ref, tvalid_ref, xblk_ref, tnext_ref, wslot_ref, x_ref, wgu_hbm, bgu_ref,
                   wd_hbm, bd_ref, y_ref, wgu_f32, wd_f32, wgu_bf, wd_bf, sem):
    del xblk_ref
    i = pl.program_id(0)

    def weight_copies(e, slot):
        return (pltpu.make_async_copy(wgu_hbm.at[layer, e], wgu_f32.at[slot], sem.at[0, slot]),
                pltpu.make_async_copy(wd_hbm.at[layer, e], wd_f32.at[slot], sem.at[1, slot]))

    @pl.when(tfirst_ref[i] == 1)
    def _():
        slot = wslot_ref[i]

        @pl.when(i == 0)
        def _():
            for cp in weight_copies(te_ref[i], slot):
                cp.start()

        for cp in weight_copies(te_ref[i], slot):
            cp.wait()
        wgu_bf[...] = wgu_f32[slot].astype(BF16)
        wd_bf[...] = wd_f32[slot].astype(BF16)

        @pl.when(tnext_ref[i] >= 0)
        def _():
            for cp in weight_copies(tnext_ref[i], 1 - slot):
                cp.start()

    def ffn(x):
        a = _dot(x, wgu_bf[...]) + bgu_ref[...]
        glu = jnp.minimum(a[:, :D_FF], SWIGLU_LIMIT)
        lin = jnp.clip(a[:, D_FF:], -SWIGLU_LIMIT, SWIGLU_LIMIT)
        act = (glu * jax.nn.sigmoid(SWIGLU_ALPHA * glu)) * (lin + 1.0)
        return (_dot(act.astype(BF16), wd_bf[...]) + bd_ref[...]).astype(BF16)

    half = ETILE // 2

    @pl.when(tvalid_ref[i] == 2)
    def _():
        y_ref[...] = ffn(x_ref[...])

    @pl.when(tvalid_ref[i] == 1)
    def _():
        y_ref[:half] = ffn(x_ref[:half])
        y_ref[half:] = jnp.zeros((ETILE - half, D), BF16)


def _experts(layer, plan, xs, w_gu, b_gu, w_down, b_down):
    depth = w_gu.shape[0]
    e_of = lambda i, te, *_: (layer, te[i], 0, 0)
    grid_spec = pltpu.PrefetchScalarGridSpec(
        num_scalar_prefetch=6,
        grid=(N_ETILES,),
        in_specs=[
            pl.BlockSpec((ETILE, D), lambda i, te, tf, tv, xb, *_: (xb[i], 0)),
            pl.BlockSpec(memory_space=pl.ANY),
            pl.BlockSpec((None, None, 1, 2 * D_FF), e_of),
            pl.BlockSpec(memory_space=pl.ANY),
            pl.BlockSpec((None, None, 1, D), e_of),
        ],
        out_specs=pl.BlockSpec((ETILE, D), lambda i, te, tf, tv, xb, *_: (xb[i], 0)),
        scratch_shapes=[pltpu.VMEM((2, D, 2 * D_FF), F32), pltpu.VMEM((2, D_FF, D), F32),
                        pltpu.VMEM((D, 2 * D_FF), BF16), pltpu.VMEM((D_FF, D), BF16),
                        pltpu.SemaphoreType.DMA((2, 2))],
    )
    return pl.pallas_call(
        functools.partial(_expert_kernel, layer),
        grid_spec=grid_spec,
        out_shape=jax.ShapeDtypeStruct((P_ROWS, D), BF16),
        compiler_params=_params(("arbitrary",)),
        name="moe_experts",
    )(plan["te"], plan["tfirst"], plan["tvalid"], plan["xblk"], plan["tnext"], plan["wslot"], xs, w_gu,
      b_gu.reshape(depth, N_EXPERTS, 1, 2 * D_FF), w_down, b_down.reshape(depth, N_EXPERTS, 1, D))


def _combine_kernel(split, base_ref, seg_ref, nch_ref, ntile_ref, y_hbm, x1_ref, meta_ref, segrow_ref, mod_ref,
                    *rest):
    *o_refs, ybuf, sem = rest
    i = pl.program_id(0)
    slot = i % 2

    def fetch(t, s):
        make = lambda loc, glob: pltpu.make_async_copy(y_hbm.at[pl.ds(glob, CH)], ybuf.at[s, pl.ds(loc, CH)],
                                                       sem.at[s])
        _segment_copies(t, base_ref, seg_ref, nch_ref, make)

    @pl.when(i == 0)
    def _():
        ybuf[...] = jnp.zeros_like(ybuf)
        fetch(i, slot)

    @pl.when(i + 1 < NBD)
    def _():
        fetch(i + 1, 1 - slot)

    _drain(ntile_ref[i], pltpu.make_async_copy(y_hbm.at[pl.ds(0, CH)], ybuf.at[slot, pl.ds(0, CH)], sem.at[slot]))

    meta = meta_ref[...]
    pos = _local_positions(meta, segrow_ref[0])
    gates = _scatter_matrix(pos, [meta[:, _META_W + k:_META_W + k + 1].astype(BF16) for k in range(TOP_K)])
    acc = _dot(gates, ybuf[slot])
    out = x1_ref[...] + mod_ref[0][5:6] * acc
    if split:
        os_ref, op_ref = o_refs

        @pl.when(i < NBD_SAMPLE)
        def _():
            os_ref[...] = out

        @pl.when(i >= NBD_SAMPLE)
        def _():
            op_ref[...] = out
    else:
        o_refs[0][...] = out


def _combine(plan, y, x1, meta, mod, split):
    if split:
        out_specs = [pl.BlockSpec((TMD, D), lambda i, *_: (jnp.minimum(i, NBD_SAMPLE - 1), 0)),
                     pl.BlockSpec((TMD, D), lambda i, *_: (jnp.maximum(i - NBD_SAMPLE, 0), 0))]
        out_shape = [jax.ShapeDtypeStruct((N_SAMPLE, D), F32), jax.ShapeDtypeStruct((N_PROMPT, D), F32)]
    else:
        out_specs, out_shape = _TOKD(D), jax.ShapeDtypeStruct((T, D), F32)
    grid_spec = pltpu.PrefetchScalarGridSpec(
        num_scalar_prefetch=4,
        grid=(NBD,),
        in_specs=[pl.BlockSpec(memory_space=pl.ANY), _TOKD(D), _TOKD(LANES), _SEGROWS,
                  pl.BlockSpec((1, 6, D), lambda i, *_: (_cond_row(i * (TMD // TM)), 0, 0))],
        out_specs=out_specs,
        scratch_shapes=[pltpu.VMEM((2, ROWS_L, D), BF16), pltpu.SemaphoreType.DMA((2,))],
    )
    return pl.pallas_call(
        functools.partial(_combine_kernel, split),
        grid_spec=grid_spec,
        out_shape=out_shape,
        compiler_params=_params(("arbitrary",)),
        name="moe_combine",
    )(plan["base"], plan["seg"], plan["nch"], plan["ntile"], y, x1, meta, plan["segrow"], mod)


def _moe_plan(cnt):
    per = TMD // TM
    cnt = cnt[:, 0, :N_EXPERTS].astype(jnp.int32).reshape(NBD, per, N_EXPERTS)
    cpad = (jnp.sum(cnt, axis=1) + CH - 1) // CH * CH
    tot = jnp.sum(cpad, axis=0)
    tiles = (tot + ETILE - 1) // ETILE
    tile_end = jnp.cumsum(tiles)
    offs = (tile_end - tiles) * ETILE
    base = offs[None, :] + jnp.cumsum(cpad, axis=0) - cpad
    seg = jnp.cumsum(cpad, axis=1) - cpad
    segrows = seg[:, None, :] + jnp.cumsum(cnt, axis=1) - cnt
    tid = jnp.arange(N_ETILES, dtype=jnp.int32)
    te = jnp.sum((tile_end[None, :] <= tid[:, None]).astype(jnp.int32), axis=1)
    used = te < N_EXPERTS
    last = jnp.max(jnp.where(tiles > 0, jnp.arange(N_EXPERTS, dtype=jnp.int32), 0))
    te = jnp.where(used, te, last)
    rows = tot[te] - (tid - (tile_end - tiles)[te]) * ETILE
    tvalid = jnp.where(used, jnp.where(rows > ETILE // 2, 2, 1), 0).astype(jnp.int32)
    tfirst = jnp.concatenate([jnp.ones((1,), jnp.int32), (te[1:] != te[:-1]).astype(jnp.int32)])
    eid = jnp.arange(N_EXPERTS, dtype=jnp.int32)
    later = (eid[None, :] > eid[:, None]) & (tiles[None, :] > 0)
    nxt = jnp.min(jnp.where(later, eid[None, :], N_EXPERTS), axis=1)
    nxt = jnp.where(nxt < N_EXPERTS, nxt, -1)
    return {
        "tnext": nxt[te], "wslot": (jnp.cumsum(tfirst) - 1) % 2, "tfirst": tfirst,
        "base": base.reshape(-1), "seg": seg.reshape(-1), "nch": (cpad // CH).reshape(-1),
        "ntile": jnp.sum(cpad, axis=1) // CH, "tail0": offs + tot, "tailn": (tiles * ETILE - tot) // CH,
        "segrow": jnp.pad(segrows.astype(F32), ((0, 0), (0, 0), (0, LANES - N_EXPERTS))),
        "te": te, "tvalid": tvalid, "xblk": jnp.where(used, tid, tile_end[-1] - 1),
    }


def _moe(layer, x1, h2, meta, cnt, mod, w_gu, b_gu, w_down, b_down, split_out=False):
    plan = _moe_plan(cnt)
    xs = _dispatch(plan, h2, meta)
    y = _experts(layer, plan, xs, w_gu, b_gu, w_down, b_down)
    return _combine(plan, y, x1, meta, mod, split_out)


_HK, _HV = H_C * DK_C, H_C * DV_C
_ODD_MAIN = 2 * _HK + 2 * _HV


def _odd_in_kernel(x_ref, mod_ref, nmix_ref, win_ref, wgk_ref, bgk_ref, q_ref, k_ref, v_ref, g_ref, la_ref,
                   lamin_ref):
    m = mod_ref[0]
    h = _rms(x_ref[...]) * nmix_ref[...] * (1.0 + m[1:2]) + m[0:1]
    a = _dot(h.astype(BF16), win_ref[...])
    q_ref[...] = a[:, :_HK] * (DK_C ** -0.5)
    k_ref[...] = a[:, _HK:2 * _HK]
    v_ref[...] = a[:, 2 * _HK:2 * _HK + _HV]
    g_ref[...] = a[:, 2 * _HK + _HV:_ODD_MAIN]
    z = _dot(a[:, _ODD_MAIN:].astype(BF16), wgk_ref[...]) + bgk_ref[...]
    la = (jnp.minimum(z, 0.0) - jnp.log(1.0 + jnp.exp(-jnp.abs(z)))) * (1.0 / GATE_TAU)
    la_ref[...] = la
    chunk_tot = jnp.sum(la.reshape(TM // GLA_CHUNK, GLA_CHUNK, 2 * _HK), axis=1)
    lamin_ref[0] = jnp.broadcast_to(jnp.min(chunk_tot, axis=(0, 1), keepdims=True), (SUBLANES, LANES))


def _pairwise_att(q, k, c, forward):
    tcol = lax.broadcasted_iota(jnp.int32, (GLA_CHUNK, 1), 0)
    trow = lax.broadcasted_iota(jnp.int32, (GLA_CHUNK, DK_C), 0)
    scol = lax.broadcasted_iota(jnp.int32, (GLA_CHUNK, GLA_CHUNK), 1)

    def body(s, att):
        k_s = jnp.sum(jnp.where(trow == s, k, 0.0), axis=0, keepdims=True)
        c_s = jnp.sum(jnp.where(trow == s, c, 0.0), axis=0, keepdims=True)
        allowed = (tcol >= s) if forward else (tcol <= s)
        e = jnp.exp(jnp.where(allowed, c - c_s, -jnp.inf))
        return jnp.where(scol == s, jnp.sum(q * k_s * e, axis=-1, keepdims=True), att)

    return lax.fori_loop(0, GLA_CHUNK, body, jnp.zeros((GLA_CHUNK, GLA_CHUNK), F32))


def _gla_kernel(has_init, nchunk, strong_ref, *refs):
    if has_init:
        (qf, kf, vf, laf, qb, kb, vb, lab, s0f, s0b, _, _, of_ref, ob_ref, st) = refs
    else:
        (qf, kf, vf, laf, qb, kb, vb, lab, of_ref, ob_ref, sf_ref, sb_ref, st) = refs
    j = pl.program_id(1)

    @pl.when(j == 0)
    def _():
        if has_init:
            st[0] = s0f[...]
            st[1] = s0b[...]
        else:
            st[...] = jnp.zeros_like(st)

    row = lax.broadcasted_iota(jnp.int32, (GLA_CHUNK, GLA_CHUNK), 0)
    col = lax.broadcasted_iota(jnp.int32, (GLA_CHUNK, GLA_CHUNK), 1)

    def scan_step(factored):
        for d, (q_r, k_r, v_r, la_r, o_r) in enumerate(((qf, kf, vf, laf, of_ref), (qb, kb, vb, lab, ob_ref))):
            keep = (col <= row) if d == 0 else (col >= row)
            tri = jnp.where(keep, 1.0, 0.0).astype(BF16)
            for s in range(GLA_NS):
                g = la_r[s]
                g_hi = g.astype(BF16)
                g_r = g - g_hi.astype(F32)
                g_mid = g_r.astype(BF16)
                g_lo = (g_r - g_mid.astype(F32)).astype(BF16)
                c = _dot(tri, g_hi) + _dot(tri, g_mid) + _dot(tri, g_lo)
                tot = jnp.sum(g, axis=0, keepdims=True)
                decay = jnp.exp(tot)
                q, k = q_r[s], k_r[s]
                qe = (q * jnp.exp(c)).astype(BF16)
                if factored:
                    e_neg = jnp.exp(-c)
                    kd = (k * e_neg).astype(BF16)
                    k2 = (k * (e_neg * decay)).astype(BF16)
                else:
                    k2 = (k * jnp.exp(tot - c)).astype(BF16)
                v = v_r[s].astype(BF16)
                for h in range(H_C):
                    kc = slice(h * DK_C, (h + 1) * DK_C)
                    vc = slice(h * DV_C, (h + 1) * DV_C)
                    if factored:
                        att = jnp.where(keep, _dot_nt(qe[:, kc], kd[:, kc]), 0.0)
                    else:
                        att = _pairwise_att(q[:, kc], k[:, kc], c[:, kc], d == 0)
                    s_t = st[d, s, h]
                    o_r[s, :, vc] = _dot_nt(qe[:, kc], s_t.astype(BF16)) + _dot(att.astype(BF16), v[:, vc])
                    st[d, s, h] = s_t * decay[:, kc] + _dot_tn(v[:, vc], k2[:, kc])

    pl.when(strong_ref[0] == 0)(lambda: scan_step(True))
    pl.when(strong_ref[0] != 0)(lambda: scan_step(False))

    if not has_init:
        @pl.when(j == nchunk - 1)
        def _():
            sf_ref[...] = st[0]
            sb_ref[...] = st[1]


def _gla_call(has_init, nseq, seqlen, row0, strong, q, k, v, la, prev_f=None, prev_b=None, s0f=None, s0b=None):
    nchunk = seqlen // GLA_CHUNK
    nview = T // seqlen
    g0 = row0 // seqlen // GLA_NS
    view = lambda a: a.reshape(nview, seqlen, a.shape[-1])
    fwd = lambda g, j, *_: (g0 + g, j, 0)
    bwd = lambda g, j, *_: (g0 + g, nchunk - 1 - j, 0)
    bwd_la = lambda g, j, *_: (g0 + g, nchunk - 1 - j, 1)
    blk = lambda w, m: pl.BlockSpec((GLA_NS, GLA_CHUNK, w), m)
    state_spec = pl.BlockSpec((GLA_NS, H_C, DV_C, DK_C), lambda g, j, *_: (g, 0, 0, 0))
    in_specs = [blk(_HK, fwd), blk(_HK, fwd), blk(_HV, fwd), blk(_HK, fwd),
                blk(_HK, bwd), blk(_HK, bwd), blk(_HV, bwd), blk(_HK, bwd_la)]
    args = [view(q), view(k), view(v), view(la)] * 2
    aliases = {}
    if has_init:
        in_specs += [state_spec, state_spec] + [pl.BlockSpec(memory_space=pl.ANY)] * 2
        args += [s0f, s0b, view(prev_f), view(prev_b)]
        aliases = {len(args) - 1: 0, len(args): 1}
    out_specs = [blk(_HV, fwd), blk(_HV, bwd)]
    out_shape = [jax.ShapeDtypeStruct((nview, seqlen, _HV), F32)] * 2
    if not has_init:
        out_specs += [state_spec, state_spec]
        out_shape += [jax.ShapeDtypeStruct((nseq, H_C, DV_C, DK_C), F32)] * 2
    grid_spec = pltpu.PrefetchScalarGridSpec(
        num_scalar_prefetch=1,
        grid=(nseq // GLA_NS, nchunk),
        in_specs=in_specs,
        out_specs=out_specs,
        scratch_shapes=[pltpu.VMEM((2, GLA_NS, H_C, DV_C, DK_C), F32)],
    )
    outs = pl.pallas_call(
        functools.partial(_gla_kernel, has_init, nchunk),
        grid_spec=grid_spec,
        out_shape=out_shape,
        input_output_aliases=aliases,
        compiler_params=_params(("arbitrary", "arbitrary")),
        name="gla_sample" if has_init else "gla_prompt",
    )(strong, *args)
    return [outs[0].reshape(T, _HV), outs[1].reshape(T, _HV)] + list(outs[2:])


def _rope_tables():
    half = ROPE // 2
    inv_freq = np.power(np.float32(ROPE_THETA), -np.arange(0, half, 2, dtype=np.float32) / np.float32(half))
    n = np.arange(SAMPLE_LEN)
    row = (n // GRID_W).astype(np.float32)
    col = (n % GRID_W).astype(np.float32)
    ang_r = (row[:, None] * inv_freq[None, :]).astype(np.float32)
    ang_c = (col[:, None] * inv_freq[None, :]).astype(np.float32)
    nf = half // 2
    c = np.ones((TM + SAMPLE_LEN, LANES), np.float32)
    s1 = np.zeros((TM + SAMPLE_LEN, LANES), np.float32)
    s2 = np.zeros((TM + SAMPLE_LEN, LANES), np.float32)
    for base, ang in ((NOPE, ang_r), (NOPE + half, ang_c)):
        c[TM:, base:base + nf] = np.cos(ang)
        c[TM:, base + nf:base + half] = np.cos(ang)
        s1[TM:, base:base + nf] = -np.sin(ang)
        s2[TM:, base + nf:base + half] = np.sin(ang)
    return jnp.asarray(c), jnp.asarray(s1), jnp.asarray(s2)


def _pad_heads(w, nheads, width, lo=0):
    k = w.shape[0]
    w = w.reshape(k, nheads, width)
    w = jnp.pad(w, ((0, 0), (0, 0), (lo, HP - lo - width)))
    return w.reshape(k, nheads * HP)


def _row128(v, lo=0):
    return jnp.pad(v, (lo, LANES - lo - v.shape[0])).reshape(1, LANES)


def _even_layer(xs, xp, mod, nmix, w_in, q_a_norm, w_uq, q_norm, kv_a_norm, w_ukv, k_norm, v_norm, w_s, b_s, w_out,
                cache_ckv, cache_kpe):
    s = np.cumsum([Q_LORA, KV_LORA, ROPE, W_B])
    w_q, w_ckv, w_kpe, w_u, w_v = (w_in[:, :s[0]], w_in[:, s[0]:s[1]], w_in[:, s[1]:s[2]], w_in[:, s[2]:s[3]],
                                   w_in[:, s[3]:])
    w_kpe = jnp.pad(w_kpe, ((0, 0), (NOPE, LANES - QK_DIM)))
    win = jnp.concatenate([w_q, w_ckv, w_u, w_v, w_kpe], axis=1).astype(BF16)
    wuq = _pad_heads(w_uq, H_A, QK_DIM).astype(BF16)
    ukv = w_ukv.reshape(KV_LORA, H_A, NOPE + V_A)
    wuk = _pad_heads(ukv[:, :, :NOPE].reshape(KV_LORA, H_A * NOPE), H_A, NOPE)
    wuv = _pad_heads(ukv[:, :, NOPE:].reshape(KV_LORA, H_A * V_A), H_A, V_A)
    wukv = jnp.concatenate([wuk, wuv], axis=1).astype(BF16)
    qgain = _row128(q_norm * (QK_DIM ** -0.5 * LOG2_E))
    kgain = _row128(k_norm)
    bias = b_s.reshape(G_B // 2, 2, CHUNK_B)
    bias = jnp.concatenate([jnp.broadcast_to(bias[:, 0, :, None], (G_B // 2, CHUNK_B, C_B)),
                            jnp.broadcast_to(bias[:, 1, :, None], (G_B // 2, CHUNK_B, C_B))], axis=-1)
    rc, rs1, rs2 = _rope_tables()
    rope_spec = pl.BlockSpec((TM, LANES), lambda i: (_rope_blk(i), 0))
    width = H_A * HP
    q, k, v, ob, ckv, kpe = pl.pallas_call(
        _even_in_kernel,
        grid=(NBLK,),
        in_specs=[
            _X_SAMPLE, _X_PROMPT, _MOD, _full((1, D)), _full((D, _WIN_N)), _full((1, Q_LORA)),
            _full((Q_LORA, width)),
            _full((1, LANES)), _full((1, KV_LORA)), _full((KV_LORA, 2 * width)), _full((1, LANES)),
            _full((1, W_B)), _full((G_B // 2, 2 * CHUNK_B, CHUNK_B)), _full((G_B // 2, CHUNK_B, LANES)),
            rope_spec, rope_spec, rope_spec,
        ],
        out_specs=[_TOK(width), _TOK(width), _TOK(width), _TOK(W_B), _TOK(KV_LORA), _TOK(ROPE)],
        out_shape=[
            jax.ShapeDtypeStruct((T, width), BF16), jax.ShapeDtypeStruct((T, width), BF16),
            jax.ShapeDtypeStruct((T, width), BF16), jax.ShapeDtypeStruct((T, W_B), BF16),
            jax.ShapeDtypeStruct((T, KV_LORA), F32), jax.ShapeDtypeStruct((T, ROPE), F32),
        ],
        compiler_params=_params(("arbitrary",)),
        name="even_in",
    )(xs, xp, mod, nmix.reshape(1, D), win, q_a_norm.reshape(1, Q_LORA), wuq, qgain, kv_a_norm.reshape(1, KV_LORA),
      wukv, kgain, v_norm.reshape(1, W_B), w_s.astype(BF16).reshape(G_B // 2, 2 * CHUNK_B, CHUNK_B), bias, rc, rs1,
      rs2)

    n_ctx = N_SAMPLE_SEQ * PAST_LEN
    kpe_ctx = jnp.pad(cache_kpe.reshape(n_ctx, ROPE), ((0, 0), (NOPE, LANES - QK_DIM)))
    k_ctx, v_ctx = pl.pallas_call(
        _ctx_kv_kernel,
        grid=(n_ctx // TM,),
        in_specs=[_TOK(KV_LORA), _TOK(LANES), _full((KV_LORA, 2 * width)), _full((1, LANES))],
        out_specs=[_TOK(width), _TOK(width)],
        out_shape=[jax.ShapeDtypeStruct((n_ctx, width), BF16)] * 2,
        compiler_params=_params(("arbitrary",)),
        name="ctx_kv",
    )(cache_ckv.reshape(n_ctx, KV_LORA), kpe_ctx, wukv, kgain)

    oa = _attention(q, k, v, k_ctx, v_ctx)
    woa = jnp.pad(w_out[:H_A * V_A].reshape(H_A, V_A, D), ((0, 0), (0, HP - V_A), (0, 0))).reshape(width, D)
    return oa, ob, woa.astype(BF16), w_out[H_A * V_A:].astype(BF16), ckv, kpe


def kernel(x_prompt, x_sample, cache_mla_ckv, cache_mla_kpe, state_gla_fwd, state_gla_bwd, c, c_ctx, ada_w, ada_b,
           norm_mix, norm_ffn, even_w_in, mla_q_a_norm, mla_w_uq, mla_q_norm, mla_kv_a_norm, mla_w_ukv, mla_k_norm,
           cmlp_v_norm, cmlp_w_s, cmlp_b_s, even_w_out, odd_w_in, gla_w_gk_fwd, gla_b_gk_fwd, gla_w_gk_bwd,
           gla_b_gk_bwd, gla_o_norm, odd_w_out, moe_w_router, moe_b_router, moe_w_gu, moe_b_gu, moe_w_down,
           moe_b_down):
    xs0, xp0 = x_sample.reshape(N_SAMPLE, D), x_prompt.reshape(N_PROMPT, D)
    cond8 = jnp.concatenate([c_ctx[None, :], c, jnp.zeros((SUBLANES - 1 - N_SAMPLE_SEQ, D), F32)], axis=0)
    mods = _adaln(cond8, ada_w, ada_b)
    wr = jnp.pad(moe_w_router, ((0, 0), (0, 0), (0, LANES - N_EXPERTS)))
    wr_hi = wr.astype(BF16)
    wr = jnp.concatenate([wr_hi, (wr - wr_hi.astype(F32)).astype(BF16)], axis=-1)
    br = jnp.pad(moe_b_router, ((0, 0), (0, LANES - N_EXPERTS))).reshape(2, 1, LANES)

    oa, ob, woa, wob, ckv, kpe = _even_layer(
        xs0, xp0, mods[0], norm_mix[0], even_w_in[0], mla_q_a_norm[0], mla_w_uq[0], mla_q_norm[0], mla_kv_a_norm[0],
        mla_w_ukv[0], mla_k_norm[0], cmlp_v_norm[0], cmlp_w_s[0], cmlp_b_s[0], even_w_out[0],
        cache_mla_ckv[:, 0], cache_mla_kpe[:, 0])
    width = H_A * HP
    x1, h2, meta, cnt = pl.pallas_call(
        _even_out_kernel,
        grid=(NBLK,),
        in_specs=[_TOK(width), _TOK(W_B), _X_SAMPLE, _X_PROMPT, _MOD, _full((width, D)), _full((W_B, D)),
                  _full((1, D)),
                  _full((D, 2 * LANES)), _full((1, LANES))],
        out_specs=_PROLOGUE_OUT_SPECS,
        out_shape=_PROLOGUE_OUT_SHAPE,
        compiler_params=_params(("arbitrary",)),
        name="even_out",
    )(oa, ob, xs0, xp0, mods[0], woa, wob, norm_ffn[0].reshape(1, D), wr[0], br[0])
    x2 = _moe(0, x1, h2, meta, cnt, mods[0], moe_w_gu, moe_b_gu, moe_w_down, moe_b_down)

    w_in = odd_w_in[0]
    win = jnp.concatenate([w_in, jnp.zeros((D, LANES - 2 * GATE_RANK), F32)], axis=1).astype(BF16)
    wgk = jnp.zeros((LANES, 2 * _HK), F32)
    wgk = wgk.at[:GATE_RANK, :_HK].set(gla_w_gk_fwd[0]).at[GATE_RANK:2 * GATE_RANK, _HK:].set(gla_w_gk_bwd[0])
    bgk = jnp.concatenate([gla_b_gk_fwd[0], gla_b_gk_bwd[0]]).reshape(1, 2 * _HK)
    q, k, v, g, la, lamin = pl.pallas_call(
        _odd_in_kernel,
        grid=(NBLK,),
        in_specs=[_TOK(D), _MOD, _full((1, D)), _full((D, _ODD_MAIN + LANES)), _full((LANES, 2 * _HK)),
                  _full((1, 2 * _HK))],
        out_specs=[_TOK(_HK), _TOK(_HK), _TOK(_HV), _TOK(_HV), _TOK(2 * _HK), _TILE_ROW],
        out_shape=[jax.ShapeDtypeStruct((T, _HK), F32), jax.ShapeDtypeStruct((T, _HK), F32),
                   jax.ShapeDtypeStruct((T, _HV), F32), jax.ShapeDtypeStruct((T, _HV), F32),
                   jax.ShapeDtypeStruct((T, 2 * _HK), F32), jax.ShapeDtypeStruct((NBLK, SUBLANES, LANES), F32)],
        compiler_params=_params(("arbitrary",)),
        name="odd_in",
    )(x2, mods[1], norm_mix[1].reshape(1, D), win, wgk.astype(BF16), bgk)

    strong = (jnp.min(lamin) < -GLA_SAFE_LOG_DECAY).astype(jnp.int32).reshape(1)
    of, obk, st_f, st_b = _gla_call(False, N_PROMPT_SEQ, PROMPT_LEN, N_SAMPLE, strong, q, k, v, la)
    s0f = state_gla_fwd[:, 0].transpose(0, 1, 3, 2)
    s0b = state_gla_bwd[:, 0].transpose(0, 1, 3, 2)
    of, obk = _gla_call(True, N_SAMPLE_SEQ, SAMPLE_LEN, 0, strong, q, k, v, la, of, obk, s0f, s0b)

    x3, h2, meta, cnt = pl.pallas_call(
        _odd_out_kernel,
        grid=(NBLK,),
        in_specs=[_TOK(_HV), _TOK(_HV), _TOK(_HV), _TOK(D), _MOD, _full((1, DV_C)), _full((_HV, D)),
                  _full((1, D)), _full((D, 2 * LANES)), _full((1, LANES))],
        out_specs=_PROLOGUE_OUT_SPECS,
        out_shape=_PROLOGUE_OUT_SHAPE,
        compiler_params=_params(("arbitrary",)),
        name="odd_out",
    )(of, obk, g, x2, mods[1], gla_o_norm[0].reshape(1, DV_C), odd_w_out[0].astype(BF16),
      norm_ffn[1].reshape(1, D), wr[1], br[1])
    ys, yp = _moe(1, x3, h2, meta, cnt, mods[1], moe_w_gu, moe_b_gu, moe_w_down, moe_b_down, split_out=True)

    y_sample = ys.reshape(N_SAMPLE_SEQ, SAMPLE_LEN, D)
    y_prompt = yp.reshape(N_PROMPT_SEQ, PROMPT_LEN, D)
    new_ckv = ckv[N_SAMPLE:].reshape(N_PROMPT_SEQ, 1, PROMPT_LEN, KV_LORA)
    new_kpe = kpe[N_SAMPLE:].reshape(N_PROMPT_SEQ, 1, PROMPT_LEN, ROPE)
    new_fwd = st_f.transpose(0, 1, 3, 2)[:, None]
    new_bwd = st_b.transpose(0, 1, 3, 2)[:, None]
    return (y_prompt, y_sample, new_ckv, new_kpe, new_fwd, new_bwd)
```

```python
import functools

import numpy as np
import jax
import jax.numpy as jnp
from jax import lax
from jax.experimental import pallas as pl
from jax.experimental.pallas import tpu as pltpu

F32 = jnp.float32
BF16 = jnp.bfloat16

D = 1024
N_PROMPT_SEQ, PROMPT_LEN = 16, 256
N_SAMPLE_SEQ, SAMPLE_LEN = 4, 2048
PAST_LEN = 512
N_PROMPT = N_PROMPT_SEQ * PROMPT_LEN
N_SAMPLE = N_SAMPLE_SEQ * SAMPLE_LEN
T = N_PROMPT + N_SAMPLE
EPS = 1e-6
GRID_W = 64
H_A, Q_LORA, KV_LORA, NOPE, ROPE, V_A = 8, 512, 256, 64, 32, 64
QK_DIM = NOPE + ROPE
G_B, C_B, W_B, CHUNK_B = 8, 64, 512, 128
H_C, DK_C, DV_C, GATE_RANK, GATE_TAU, GLA_CHUNK = 4, 128, 256, 16, 16.0, 64
N_EXPERTS, TOP_K, D_FF = 32, 4, 1024
SWIGLU_LIMIT, SWIGLU_ALPHA = 7.0, 1.702
ROPE_THETA = 10000.0
LOG2_E = 1.4426950408889634

LANES = 128
SUBLANES = 8
VMEM_LIMIT = 56 * 1024 * 1024

TM = 512
NBLK = T // TM
SAMPLE_BLKS = N_SAMPLE // TM
BLKS_PER_SAMPLE_SEQ = SAMPLE_LEN // TM
ATT_TQ = 256
GLA_NS = 4
GLA_SAFE_LOG_DECAY = 60.0
ETILE = 512
TMD = TM
NBD = T // TMD
NBD_SAMPLE = N_SAMPLE // TMD
CH = 2 * SUBLANES
_MAX_LOCAL = TMD * TOP_K + N_EXPERTS * (CH - 1)
_POS_BLOCK = 256
ROWS_L = -(-_MAX_LOCAL // _POS_BLOCK) * _POS_BLOCK
MAX_CHUNKS = ROWS_L // CH
N_ETILES = -(-NBD * _MAX_LOCAL // ETILE) + N_EXPERTS
P_ROWS = N_ETILES * ETILE
HP = LANES


def _cond_row(i):
    return jnp.where(i < SAMPLE_BLKS, 1 + i // BLKS_PER_SAMPLE_SEQ, 0)


def _rope_blk(i):
    return jnp.where(i < SAMPLE_BLKS, 1 + i % BLKS_PER_SAMPLE_SEQ, 0)


def _rms(x):
    return x * lax.rsqrt(jnp.mean(x * x, axis=-1, keepdims=True) + EPS)


def _gelu(x):
    return 0.5 * x * (1.0 + jnp.tanh(0.7978845608028654 * (x + 0.044715 * (x * x * x))))


def _silu(x):
    return x * jax.nn.sigmoid(x)


def _dot(a, b):
    return jnp.dot(a, b, preferred_element_type=F32)


def _dot_nt(a, b):
    return lax.dot_general(a, b, (((1,), (1,)), ((), ())), preferred_element_type=F32)


def _dot_tn(a, b):
    return lax.dot_general(a, b, (((0,), (0,)), ((), ())), preferred_element_type=F32)


def _params(sem, vmem=VMEM_LIMIT):
    return pltpu.CompilerParams(dimension_semantics=sem, vmem_limit_bytes=vmem)


def _full(shape):
    nd = len(shape)
    return pl.BlockSpec(shape, lambda *_: (0,) * nd)


ADA_TN = 1536


def _adaln_kernel(c_ref, w_ref, b_ref, o_ref):
    s = _silu(c_ref[...]).astype(BF16)
    o_ref[0] = _dot(s, w_ref[0].astype(BF16)) + b_ref[0]


def _adaln(cond8, ada_w, ada_b):
    depth = ada_w.shape[0]
    n = ada_w.shape[2]
    out = pl.pallas_call(
        _adaln_kernel,
        grid=(depth, n // ADA_TN),
        in_specs=[
            pl.BlockSpec((SUBLANES, D), lambda l, j: (0, 0)),
            pl.BlockSpec((1, D, ADA_TN), lambda l, j: (l, 0, j)),
            pl.BlockSpec((1, 1, ADA_TN), lambda l, j: (l, 0, j)),
        ],
        out_specs=pl.BlockSpec((1, SUBLANES, ADA_TN), lambda l, j: (l, 0, j)),
        out_shape=jax.ShapeDtypeStruct((depth, SUBLANES, n), F32),
        compiler_params=_params(("arbitrary", "arbitrary")),
        name="adaln",
    )(cond8, ada_w, ada_b.reshape(depth, 1, n))
    return out.reshape(depth, SUBLANES, 6, D)


_QC0, _CKV0, _U0, _V0, _KPE0, _WIN_N = 0, 512, 768, 1280, 1792, 1920


def _rope(y, c, s1, s2):
    return y * c + pltpu.roll(y, LANES - 8, 1) * s1 + pltpu.roll(y, 8, 1) * s2


def _k_heads(k_raw, kpe128, kp_rot, kgain, k_ref):
    sskpe = jnp.sum(kpe128 * kpe128, axis=-1, keepdims=True)
    for h in range(H_A):
        kb = k_raw[:, h * HP:(h + 1) * HP]
        r = lax.rsqrt((jnp.sum(kb * kb, axis=-1, keepdims=True) + sskpe) * (1.0 / QK_DIM) + EPS)
        k_ref[:, h * HP:(h + 1) * HP] = ((kb * kgain + kp_rot) * r).astype(BF16)


def _input_rows(xs_ref, xp_ref):
    return jnp.where(pl.program_id(0) < SAMPLE_BLKS, xs_ref[...], xp_ref[...])


_X_SAMPLE = pl.BlockSpec((TM, D), lambda i, *_: (jnp.minimum(i, SAMPLE_BLKS - 1), 0))
_X_PROMPT = pl.BlockSpec((TM, D), lambda i, *_: (jnp.maximum(i - SAMPLE_BLKS, 0), 0))


def _even_in_kernel(xs_ref, xp_ref, mod_ref, nmix_ref, win_ref, qan_ref, wuq_ref, qgain_ref, kvan_ref, wukv_ref,
                    kgain_ref, vnorm_ref, ws_ref, bs_ref, rc_ref, rs1_ref, rs2_ref,
                    q_ref, k_ref, v_ref, ob_ref, ckv_ref, kpe_ref):
    m = mod_ref[0]
    h = _rms(_input_rows(xs_ref, xp_ref)) * nmix_ref[...] * (1.0 + m[1:2]) + m[0:1]
    a = _dot(h.astype(BF16), win_ref[...])
    qc = a[:, _QC0:_CKV0]
    ckv = a[:, _CKV0:_U0]
    u = a[:, _U0:_V0]
    vv = a[:, _V0:_KPE0]
    kpe128 = a[:, _KPE0:_WIN_N]

    ckv_n = _rms(ckv) * kvan_ref[...]
    ckv_ref[...] = ckv_n
    kpe_ref[...] = kpe128[:, NOPE:QK_DIM]

    rc, rs1, rs2 = rc_ref[...], rs1_ref[...], rs2_ref[...]
    qn = (_rms(qc) * qan_ref[...]).astype(BF16)
    qr = _dot(qn, wuq_ref[...])
    qgain = qgain_ref[...]
    for hh in range(H_A):
        blk = qr[:, hh * HP:(hh + 1) * HP]
        r = lax.rsqrt(jnp.sum(blk * blk, axis=-1, keepdims=True) * (1.0 / QK_DIM) + EPS)
        q_ref[:, hh * HP:(hh + 1) * HP] = _rope(blk * r * qgain, rc, rs1, rs2).astype(BF16)

    kv = _dot(ckv_n.astype(BF16), wukv_ref[...])
    v_ref[...] = kv[:, H_A * HP:].astype(BF16)
    kgain = kgain_ref[...]
    kp_rot = _rope(kpe128 * kgain, rc, rs1, rs2)
    _k_heads(kv[:, :H_A * HP], kpe128, kp_rot, kgain, k_ref)

    ug = _gelu(u)
    vn = (_rms(_gelu(vv)) * vnorm_ref[...]).astype(BF16)
    low = lax.broadcasted_iota(jnp.int32, (CHUNK_B, LANES), 1) < C_B
    for c in range(TM // CHUNK_B):
        rows = slice(c * CHUNK_B, (c + 1) * CHUNK_B)
        for p in range(G_B // 2):
            cols = slice(p * LANES, (p + 1) * LANES)
            blk = vn[rows, cols]
            both = _dot(ws_ref[p], blk)
            mixed = jnp.where(low, both[:CHUNK_B], both[CHUNK_B:]) + bs_ref[p]
            ob_ref[rows, cols] = (ug[rows, cols] * mixed).astype(BF16)


def _ctx_kv_kernel(ckv_ref, kpe_ref, wukv_ref, kgain_ref, k_ref, v_ref):
    kv = _dot(ckv_ref[...].astype(BF16), wukv_ref[...])
    v_ref[...] = kv[:, H_A * HP:].astype(BF16)
    kgain = kgain_ref[...]
    kpe128 = kpe_ref[...]
    _k_heads(kv[:, :H_A * HP], kpe128, kpe128 * kgain, kgain, k_ref)


def _attn_self_kernel(q_ref, k_ref, v_ref, o_ref):
    for h in range(H_A):
        cols = slice(h * HP, (h + 1) * HP)
        s = _dot_nt(q_ref[:, cols], k_ref[:, cols])
        p = jnp.exp2(s - jnp.max(s, axis=-1, keepdims=True))
        inv = 1.0 / jnp.sum(p, axis=-1, keepdims=True)
        o_ref[:, cols] = (_dot(p.astype(BF16), v_ref[:, cols]) * inv).astype(BF16)


def _attn_ctx_kernel(q_ref, k_ref, v_ref, kc_ref, vc_ref, prev_ref, o_ref):
    del prev_ref
    for h in range(H_A):
        cols = slice(h * HP, (h + 1) * HP)
        q = q_ref[:, cols]
        s1 = _dot_nt(q, k_ref[:, cols])
        s2 = _dot_nt(q, kc_ref[:, cols])
        mx = jnp.maximum(jnp.max(s1, axis=-1, keepdims=True), jnp.max(s2, axis=-1, keepdims=True))
        p1 = jnp.exp2(s1 - mx)
        p2 = jnp.exp2(s2 - mx)
        inv = 1.0 / (jnp.sum(p1, axis=-1, keepdims=True) + jnp.sum(p2, axis=-1, keepdims=True))
        o = _dot(p1.astype(BF16), v_ref[:, cols]) + _dot(p2.astype(BF16), vc_ref[:, cols])
        o_ref[:, cols] = (o * inv).astype(BF16)


def _attention(q, k, v, k_ctx, v_ctx):
    width = H_A * HP
    first = N_SAMPLE // PROMPT_LEN
    o = pl.pallas_call(
        _attn_self_kernel,
        grid=(N_PROMPT_SEQ,),
        in_specs=[pl.BlockSpec((PROMPT_LEN, width), lambda i: (first + i, 0))] * 3,
        out_specs=pl.BlockSpec((PROMPT_LEN, width), lambda i: (first + i, 0)),
        out_shape=jax.ShapeDtypeStruct((T, width), BF16),
        compiler_params=_params(("arbitrary",)),
        name="attn_prompt",
    )(q, k, v)
    qblocks = SAMPLE_LEN // ATT_TQ
    qblk = lambda b, j: (b * qblocks + j, 0)
    return pl.pallas_call(
        _attn_ctx_kernel,
        grid=(N_SAMPLE_SEQ, qblocks),
        in_specs=[
            pl.BlockSpec((ATT_TQ, width), qblk),
            pl.BlockSpec((SAMPLE_LEN, width), lambda b, j: (b, 0)),
            pl.BlockSpec((SAMPLE_LEN, width), lambda b, j: (b, 0)),
            pl.BlockSpec((PAST_LEN, width), lambda b, j: (b, 0)),
            pl.BlockSpec((PAST_LEN, width), lambda b, j: (b, 0)),
            pl.BlockSpec(memory_space=pl.ANY),
        ],
        out_specs=pl.BlockSpec((ATT_TQ, width), qblk),
        out_shape=jax.ShapeDtypeStruct((T, width), BF16),
        input_output_aliases={5: 0},
        compiler_params=_params(("arbitrary", "arbitrary")),
        name="attn_sample",
    )(q, k, v, k_ctx, v_ctx, o)


_META_IDX, _META_RANK, _META_W = 0, TOP_K, 2 * TOP_K


def _moe_prologue(x1, m, nffn_ref, wr_ref, br_ref, x1_ref, h2_ref, meta_ref, cnt_ref):
    x1_ref[...] = x1
    h2 = _rms(x1) * nffn_ref[...] * (1.0 + m[4:5]) + m[3:4]
    h2_ref[...] = h2.astype(BF16)
    lane = lax.broadcasted_iota(jnp.int32, (TM, LANES), 1)
    lanef = lane.astype(F32)
    h_hi = h2.astype(BF16)
    h_lo = (h2 - h_hi.astype(F32)).astype(BF16)
    r = _dot(h_hi, wr_ref[...])
    logits = r[:, :LANES] + r[:, LANES:] + _dot(h_lo, wr_ref[:, :LANES]) + br_ref[...]
    work = jnp.where(lane < N_EXPERTS, logits, -jnp.inf)
    hots, vals = [], []
    for _ in range(TOP_K):
        mx = jnp.max(work, axis=-1, keepdims=True)
        idx = jnp.min(jnp.where(work == mx, lanef, float(LANES)), axis=-1, keepdims=True)
        hot = lanef == idx
        work = jnp.where(hot, -jnp.inf, work)
        hots.append((hot, idx))
        vals.append(mx)
    es = [jnp.exp(v - vals[0]) for v in vals]
    inv = 1.0 / (es[0] + es[1] + es[2] + es[3])
    sel = jnp.zeros((TM, LANES), F32)
    for hot, _ in hots:
        sel = jnp.where(hot, 1.0, sel)
    row = lax.broadcasted_iota(jnp.int32, (TM, TM), 0)
    col = lax.broadcasted_iota(jnp.int32, (TM, TM), 1)
    strict = jnp.where(row > col, 1.0, 0.0).astype(BF16)
    before = _dot(strict, sel.astype(BF16))
    meta = jnp.zeros((TM, LANES), F32)
    for kk, (hot, idx) in enumerate(hots):
        rank = jnp.sum(jnp.where(hot, before, 0.0), axis=-1, keepdims=True)
        meta = jnp.where(lane == _META_IDX + kk, idx, meta)
        meta = jnp.where(lane == _META_RANK + kk, rank, meta)
        meta = jnp.where(lane == _META_W + kk, es[kk] * inv, meta)
    meta_ref[...] = meta
    cnt_ref[0] = jnp.broadcast_to(jnp.sum(sel, axis=0, keepdims=True), (SUBLANES, LANES))


def _even_out_kernel(oa_ref, ob_ref, xs_ref, xp_ref, mod_ref, woa_ref, wob_ref, nffn_ref, wr_ref, br_ref,
                     x1_ref, h2_ref, meta_ref, cnt_ref):
    m = mod_ref[0]
    out = _dot(oa_ref[...], woa_ref[...]) + _dot(ob_ref[...], wob_ref[...])
    x1 = _input_rows(xs_ref, xp_ref) + m[2:3] * out
    _moe_prologue(x1, m, nffn_ref, wr_ref, br_ref, x1_ref, h2_ref, meta_ref, cnt_ref)


def _odd_out_kernel(of_ref, ob_ref, g_ref, x_ref, mod_ref, onorm_ref, wo_ref, nffn_ref, wr_ref, br_ref,
                    x1_ref, h2_ref, meta_ref, cnt_ref):
    m = mod_ref[0]
    onorm = onorm_ref[...]
    parts = []
    for h in range(H_C):
        cols = slice(h * DV_C, (h + 1) * DV_C)
        o = of_ref[:, cols] + ob_ref[:, cols]
        parts.append((_rms(o) * onorm * _silu(g_ref[:, cols])).astype(BF16))
    out = _dot(jnp.concatenate(parts, axis=-1), wo_ref[...])
    x1 = x_ref[...] + m[2:3] * out
    _moe_prologue(x1, m, nffn_ref, wr_ref, br_ref, x1_ref, h2_ref, meta_ref, cnt_ref)


_TOK = lambda w: pl.BlockSpec((TM, w), lambda i, *_: (i, 0))
_MOD = pl.BlockSpec((1, 6, D), lambda i, *_: (_cond_row(i), 0, 0))
_TILE_ROW = pl.BlockSpec((1, SUBLANES, LANES), lambda i, *_: (i, 0, 0))

_PROLOGUE_OUT_SPECS = [_TOK(D), _TOK(D), _TOK(LANES), _TILE_ROW]
_PROLOGUE_OUT_SHAPE = [
    jax.ShapeDtypeStruct((T, D), F32),
    jax.ShapeDtypeStruct((T, D), BF16),
    jax.ShapeDtypeStruct((T, LANES), F32),
    jax.ShapeDtypeStruct((NBLK, SUBLANES, LANES), F32),
]


def _local_positions(meta, seg_rows):
    lanef = lax.broadcasted_iota(jnp.int32, (TM, LANES), 1).astype(F32)
    pos = []
    for k in range(TOP_K):
        halves = []
        for half in range(TMD // TM):
            m = meta[half * TM:(half + 1) * TM]
            hot = lanef == m[:, _META_IDX + k:_META_IDX + k + 1]
            start = jnp.sum(jnp.where(hot, seg_rows[half:half + 1], 0.0), axis=-1, keepdims=True)
            halves.append(start + m[:, _META_RANK + k:_META_RANK + k + 1])
        pos.append(jnp.concatenate(halves, axis=0))
    return pos


def _scatter_matrix(pos, vals):
    lane = lax.broadcasted_iota(jnp.int32, (TMD, _POS_BLOCK), 1).astype(F32).astype(BF16)
    blocks = []
    for j in range(ROWS_L // _POS_BLOCK):
        acc = jnp.zeros((TMD, _POS_BLOCK), BF16)
        for p, v in zip(pos, vals):
            acc = jnp.where(lane == (p - float(j * _POS_BLOCK)).astype(BF16), v, acc)
        blocks.append(acc)
    return jnp.concatenate(blocks, axis=1)


def _chunk_copies(t, count, csrc_ref, cdst_ref, make):
    def one(j, c):
        q = t * MAX_CHUNKS + j
        make(pl.multiple_of(csrc_ref[q], CH), pl.multiple_of(cdst_ref[q], CH)).start()
        return c

    lax.fori_loop(0, count, one, 0)


def _drain(count, chunk_copy):
    def one(j, c):
        chunk_copy.wait()
        return c

    lax.fori_loop(0, count, one, 0)


def _dispatch_kernel(csrc_ref, cdst_ref, ntile_ref, tail0_ref, tailn_ref, h_ref, meta_ref, segrow_ref,
                     xs_hbm, buf, zbuf, sem):
    i = pl.program_id(0)
    slot = i % 2
    chunk = lambda s: pltpu.make_async_copy(buf.at[s, pl.ds(0, CH)], xs_hbm.at[pl.ds(0, CH)], sem.at[s])

    @pl.when(i == 0)
    def _():
        zbuf[...] = jnp.zeros_like(zbuf)

        def per_expert(e, carry):
            def one(j, c):
                pltpu.make_async_copy(zbuf, xs_hbm.at[pl.ds(pl.multiple_of(tail0_ref[e] + j * CH, CH), CH)],
                                      sem.at[2]).start()
                return c

            lax.fori_loop(0, tailn_ref[e], one, 0)
            return carry + tailn_ref[e]

        total = lax.fori_loop(0, N_EXPERTS, per_expert, 0)
        _drain(total, pltpu.make_async_copy(zbuf, xs_hbm.at[pl.ds(0, CH)], sem.at[2]))

    pos = _local_positions(meta_ref[...], segrow_ref[0])
    riota = lax.broadcasted_iota(jnp.int32, (TMD, ROWS_L), 1).astype(F32)
    pt = jnp.zeros((TMD, ROWS_L), F32)
    for p in pos:
        pt = jnp.where(riota == p, 1.0, pt)
    buf[slot] = _dot_tn(pt.astype(BF16), h_ref[...]).astype(BF16)

    make = lambda s, d: pltpu.make_async_copy(buf.at[slot, pl.ds(s, CH)], xs_hbm.at[pl.ds(d, CH)], sem.at[slot])
    _chunk_copies(i, ntile_ref[i], csrc_ref, cdst_ref, make)

    @pl.when(i > 0)
    def _():
        _drain(ntile_ref[i - 1], chunk(1 - slot))

    @pl.when(i == NBD - 1)
    def _():
        _drain(ntile_ref[i], chunk(slot))


_TOKD = lambda w: pl.BlockSpec((TMD, w), lambda i, *_: (i, 0))
_SEGROWS = pl.BlockSpec((1, TMD // TM, LANES), lambda i, *_: (i, 0, 0))


def _dispatch(plan, h2, meta):
    grid_spec = pltpu.PrefetchScalarGridSpec(
        num_scalar_prefetch=5,
        grid=(NBD,),
        in_specs=[_TOKD(D), _TOKD(LANES), _SEGROWS],
        out_specs=pl.BlockSpec(memory_space=pl.ANY),
        scratch_shapes=[pltpu.VMEM((2, ROWS_L, D), BF16), pltpu.VMEM((CH, D), BF16),
                        pltpu.SemaphoreType.DMA((3,))],
    )
    return pl.pallas_call(
        _dispatch_kernel,
        grid_spec=grid_spec,
        out_shape=jax.ShapeDtypeStruct((P_ROWS, D), BF16),
        compiler_params=_params(("arbitrary",)),
        name="moe_dispatch",
    )(plan["csrc"], plan["cdst"], plan["ntile"], plan["tail0"], plan["tailn"], h2, meta, plan["segrow"])


def _expert_kernel(layer, te_ref, tfirst_ref, tvalid_ref, xblk_ref, tnext_ref, wslot_ref, x_ref, wgu_hbm, bgu_ref,
                   wd_hbm, bd_ref, y_ref, wgu_f32, wd_f32, wgu_bf, wd_bf, sem):
    del xblk_ref
    i = pl.program_id(0)

    def weight_copies(e, slot):
        return (pltpu.make_async_copy(wgu_hbm.at[layer, e], wgu_f32.at[slot], sem.at[0, slot]),
                pltpu.make_async_copy(wd_hbm.at[layer, e], wd_f32.at[slot], sem.at[1, slot]))

    @pl.when(tfirst_ref[i] == 1)
    def _():
        slot = wslot_ref[i]

        @pl.when(i == 0)
        def _():
            for cp in weight_copies(te_ref[i], slot):
                cp.start()

        for cp in weight_copies(te_ref[i], slot):
            cp.wait()
        wgu_bf[...] = wgu_f32[slot].astype(BF16)
        wd_bf[...] = wd_f32[slot].astype(BF16)

        @pl.when(tnext_ref[i] >= 0)
        def _():
            for cp in weight_copies(tnext_ref[i], 1 - slot):
                cp.start()

    def ffn(x):
        a = _dot(x, wgu_bf[...]) + bgu_ref[...]
        glu = jnp.minimum(a[:, :D_FF], SWIGLU_LIMIT)
        lin = jnp.clip(a[:, D_FF:], -SWIGLU_LIMIT, SWIGLU_LIMIT)
        act = (glu * jax.nn.sigmoid(SWIGLU_ALPHA * glu)) * (lin + 1.0)
        return (_dot(act.astype(BF16), wd_bf[...]) + bd_ref[...]).astype(BF16)

    half = ETILE // 2

    @pl.when(tvalid_ref[i] == 2)
    def _():
        y_ref[...] = ffn(x_ref[...])

    @pl.when(tvalid_ref[i] == 1)
    def _():
        y_ref[:half] = ffn(x_ref[:half])
        y_ref[half:] = jnp.zeros((ETILE - half, D), BF16)


def _experts(layer, plan, xs, w_gu, b_gu, w_down, b_down):
    depth = w_gu.shape[0]
    e_of = lambda i, te, *_: (layer, te[i], 0, 0)
    grid_spec = pltpu.PrefetchScalarGridSpec(
        num_scalar_prefetch=6,
        grid=(N_ETILES,),
        in_specs=[
            pl.BlockSpec((ETILE, D), lambda i, te, tf, tv, xb, *_: (xb[i], 0)),
            pl.BlockSpec(memory_space=pl.ANY),
            pl.BlockSpec((None, None, 1, 2 * D_FF), e_of),
            pl.BlockSpec(memory_space=pl.ANY),
            pl.BlockSpec((None, None, 1, D), e_of),
        ],
        out_specs=pl.BlockSpec((ETILE, D), lambda i, te, tf, tv, xb, *_: (xb[i], 0)),
        scratch_shapes=[pltpu.VMEM((2, D, 2 * D_FF), F32), pltpu.VMEM((2, D_FF, D), F32),
                        pltpu.VMEM((D, 2 * D_FF), BF16), pltpu.VMEM((D_FF, D), BF16),
                        pltpu.SemaphoreType.DMA((2, 2))],
    )
    return pl.pallas_call(
        functools.partial(_expert_kernel, layer),
        grid_spec=grid_spec,
        out_shape=jax.ShapeDtypeStruct((P_ROWS, D), BF16),
        compiler_params=_params(("arbitrary",)),
        name="moe_experts",
    )(plan["te"], plan["tfirst"], plan["tvalid"], plan["xblk"], plan["tnext"], plan["wslot"], xs, w_gu,
      b_gu.reshape(depth, N_EXPERTS, 1, 2 * D_FF), w_down, b_down.reshape(depth, N_EXPERTS, 1, D))


def _combine_kernel(split, csrc_ref, cdst_ref, ntile_ref, y_hbm, x1_ref, meta_ref, segrow_ref, mod_ref, *rest):
    *o_refs, ybuf, sem = rest
    i = pl.program_id(0)
    slot = i % 2

    def fetch(t, s):
        make = lambda loc, glob: pltpu.make_async_copy(y_hbm.at[pl.ds(glob, CH)], ybuf.at[s, pl.ds(loc, CH)],
                                                       sem.at[s])
        _chunk_copies(t, ntile_ref[t], csrc_ref, cdst_ref, make)

    @pl.when(i == 0)
    def _():
        ybuf[...] = jnp.zeros_like(ybuf)
        fetch(i, slot)

    @pl.when(i + 1 < NBD)
    def _():
        fetch(i + 1, 1 - slot)

    _drain(ntile_ref[i], pltpu.make_async_copy(y_hbm.at[pl.ds(0, CH)], ybuf.at[slot, pl.ds(0, CH)], sem.at[slot]))

    meta = meta_ref[...]
    pos = _local_positions(meta, segrow_ref[0])
    gates = _scatter_matrix(pos, [meta[:, _META_W + k:_META_W + k + 1].astype(BF16) for k in range(TOP_K)])
    acc = _dot(gates, ybuf[slot])
    out = x1_ref[...] + mod_ref[0][5:6] * acc
    if split:
        os_ref, op_ref = o_refs

        @pl.when(i < NBD_SAMPLE)
        def _():
            os_ref[...] = out

        @pl.when(i >= NBD_SAMPLE)
        def _():
            op_ref[...] = out
    else:
        o_refs[0][...] = out


def _combine(plan, y, x1, meta, mod, split):
    if split:
        out_specs = [pl.BlockSpec((TMD, D), lambda i, *_: (jnp.minimum(i, NBD_SAMPLE - 1), 0)),
                     pl.BlockSpec((TMD, D), lambda i, *_: (jnp.maximum(i - NBD_SAMPLE, 0), 0))]
        out_shape = [jax.ShapeDtypeStruct((N_SAMPLE, D), F32), jax.ShapeDtypeStruct((N_PROMPT, D), F32)]
    else:
        out_specs, out_shape = _TOKD(D), jax.ShapeDtypeStruct((T, D), F32)
    grid_spec = pltpu.PrefetchScalarGridSpec(
        num_scalar_prefetch=3,
        grid=(NBD,),
        in_specs=[pl.BlockSpec(memory_space=pl.ANY), _TOKD(D), _TOKD(LANES), _SEGROWS,
                  pl.BlockSpec((1, 6, D), lambda i, *_: (_cond_row(i * (TMD // TM)), 0, 0))],
        out_specs=out_specs,
        scratch_shapes=[pltpu.VMEM((2, ROWS_L, D), BF16), pltpu.SemaphoreType.DMA((2,))],
    )
    return pl.pallas_call(
        functools.partial(_combine_kernel, split),
        grid_spec=grid_spec,
        out_shape=out_shape,
        compiler_params=_params(("arbitrary",)),
        name="moe_combine",
    )(plan["csrc"], plan["cdst"], plan["ntile"], y, x1, meta, plan["segrow"], mod)


def _moe_plan(cnt):
    per = TMD // TM
    cnt = cnt[:, 0, :N_EXPERTS].astype(jnp.int32).reshape(NBD, per, N_EXPERTS)
    cpad = (jnp.sum(cnt, axis=1) + CH - 1) // CH * CH
    tot = jnp.sum(cpad, axis=0)
    tiles = (tot + ETILE - 1) // ETILE
    tile_end = jnp.cumsum(tiles)
    offs = (tile_end - tiles) * ETILE
    base = offs[None, :] + jnp.cumsum(cpad, axis=0) - cpad
    seg = jnp.cumsum(cpad, axis=1) - cpad
    segrows = seg[:, None, :] + jnp.cumsum(cnt, axis=1) - cnt
    tid = jnp.arange(N_ETILES, dtype=jnp.int32)
    te = jnp.sum((tile_end[None, :] <= tid[:, None]).astype(jnp.int32), axis=1)
    used = te < N_EXPERTS
    last = jnp.max(jnp.where(tiles > 0, jnp.arange(N_EXPERTS, dtype=jnp.int32), 0))
    te = jnp.where(used, te, last)
    rows = tot[te] - (tid - (tile_end - tiles)[te]) * ETILE
    tvalid = jnp.where(used, jnp.where(rows > ETILE // 2, 2, 1), 0).astype(jnp.int32)
    tfirst = jnp.concatenate([jnp.ones((1,), jnp.int32), (te[1:] != te[:-1]).astype(jnp.int32)])
    eid = jnp.arange(N_EXPERTS, dtype=jnp.int32)
    later = (eid[None, :] > eid[:, None]) & (tiles[None, :] > 0)
    nxt = jnp.min(jnp.where(later, eid[None, :], N_EXPERTS), axis=1)
    nxt = jnp.where(nxt < N_EXPERTS, nxt, -1)
    nch = cpad // CH
    cend = jnp.cumsum(nch, axis=1)
    cid = jnp.arange(MAX_CHUNKS, dtype=jnp.int32)
    cstart = cend - nch
    owns = ((cstart[:, None, :] <= cid[None, :, None]) & (cid[None, :, None] < cend[:, None, :])).astype(jnp.int32)
    pick = lambda a: jnp.sum(owns * a[:, None, :], axis=2)
    within = (cid[None, :] - pick(cstart)) * CH
    live = jnp.sum(owns, axis=2)
    csrc = (pick(seg) + within) * live
    cdst = (pick(base) + within) * live
    return {
        "tnext": nxt[te], "wslot": (jnp.cumsum(tfirst) - 1) % 2, "tfirst": tfirst,
        "csrc": csrc.reshape(-1), "cdst": cdst.reshape(-1),
        "ntile": jnp.sum(nch, axis=1), "tail0": offs + tot, "tailn": (tiles * ETILE - tot) // CH,
        "segrow": jnp.pad(segrows.astype(F32), ((0, 0), (0, 0), (0, LANES - N_EXPERTS))),
        "te": te, "tvalid": tvalid, "xblk": jnp.where(used, tid, tile_end[-1] - 1),
    }


def _moe(layer, x1, h2, meta, cnt, mod, w_gu, b_gu, w_down, b_down, split_out=False):
    plan = _moe_plan(cnt)
    xs = _dispatch(plan, h2, meta)
    y = _experts(layer, plan, xs, w_gu, b_gu, w_down, b_down)
    return _combine(plan, y, x1, meta, mod, split_out)


_HK, _HV = H_C * DK_C, H_C * DV_C
_ODD_MAIN = 2 * _HK + 2 * _HV


def _odd_in_kernel(x_ref, mod_ref, nmix_ref, win_ref, wgk_ref, bgk_ref, q_ref, k_ref, v_ref, g_ref, la_ref,
                   lamin_ref):
    m = mod_ref[0]
    h = _rms(x_ref[...]) * nmix_ref[...] * (1.0 + m[1:2]) + m[0:1]
    a = _dot(h.astype(BF16), win_ref[...])
    q_ref[...] = a[:, :_HK] * (DK_C ** -0.5)
    k_ref[...] = a[:, _HK:2 * _HK]
    v_ref[...] = a[:, 2 * _HK:2 * _HK + _HV]
    g_ref[...] = a[:, 2 * _HK + _HV:_ODD_MAIN]
    z = _dot(a[:, _ODD_MAIN:].astype(BF16), wgk_ref[...]) + bgk_ref[...]
    la = (jnp.minimum(z, 0.0) - jnp.log(1.0 + jnp.exp(-jnp.abs(z)))) * (1.0 / GATE_TAU)
    la_ref[...] = la
    chunk_tot = jnp.sum(la.reshape(TM // GLA_CHUNK, GLA_CHUNK, 2 * _HK), axis=1)
    lamin_ref[0] = jnp.broadcast_to(jnp.min(chunk_tot, axis=(0, 1), keepdims=True), (SUBLANES, LANES))


def _pairwise_att(q, k, c, forward):
    tcol = lax.broadcasted_iota(jnp.int32, (GLA_CHUNK, 1), 0)
    trow = lax.broadcasted_iota(jnp.int32, (GLA_CHUNK, DK_C), 0)
    scol = lax.broadcasted_iota(jnp.int32, (GLA_CHUNK, GLA_CHUNK), 1)

    def body(s, att):
        k_s = jnp.sum(jnp.where(trow == s, k, 0.0), axis=0, keepdims=True)
        c_s = jnp.sum(jnp.where(trow == s, c, 0.0), axis=0, keepdims=True)
        allowed = (tcol >= s) if forward else (tcol <= s)
        e = jnp.exp(jnp.where(allowed, c - c_s, -jnp.inf))
        return jnp.where(scol == s, jnp.sum(q * k_s * e, axis=-1, keepdims=True), att)

    return lax.fori_loop(0, GLA_CHUNK, body, jnp.zeros((GLA_CHUNK, GLA_CHUNK), F32))


def _gla_kernel(has_init, nchunk, strong_ref, *refs):
    if has_init:
        (qf, kf, vf, laf, qb, kb, vb, lab, s0f, s0b, _, _, of_ref, ob_ref, st) = refs
    else:
        (qf, kf, vf, laf, qb, kb, vb, lab, of_ref, ob_ref, sf_ref, sb_ref, st) = refs
    j = pl.program_id(1)

    @pl.when(j == 0)
    def _():
        if has_init:
            st[0] = s0f[...]
            st[1] = s0b[...]
        else:
            st[...] = jnp.zeros_like(st)

    row = lax.broadcasted_iota(jnp.int32, (GLA_CHUNK, GLA_CHUNK), 0)
    col = lax.broadcasted_iota(jnp.int32, (GLA_CHUNK, GLA_CHUNK), 1)

    def scan_step(factored):
        for d, (q_r, k_r, v_r, la_r, o_r) in enumerate(((qf, kf, vf, laf, of_ref), (qb, kb, vb, lab, ob_ref))):
            keep = (col <= row) if d == 0 else (col >= row)
            tri = jnp.where(keep, 1.0, 0.0).astype(BF16)
            for s in range(GLA_NS):
                g = la_r[s]
                g_hi = g.astype(BF16)
                g_r = g - g_hi.astype(F32)
                g_mid = g_r.astype(BF16)
                g_lo = (g_r - g_mid.astype(F32)).astype(BF16)
                c = _dot(tri, g_hi) + _dot(tri, g_mid) + _dot(tri, g_lo)
                tot = jnp.sum(g, axis=0, keepdims=True)
                decay = jnp.exp(tot)
                q, k = q_r[s], k_r[s]
                qe = (q * jnp.exp(c)).astype(BF16)
                if factored:
                    e_neg = jnp.exp(-c)
                    kd = (k * e_neg).astype(BF16)
                    k2 = (k * (e_neg * decay)).astype(BF16)
                else:
                    k2 = (k * jnp.exp(tot - c)).astype(BF16)
                v = v_r[s].astype(BF16)
                for h in range(H_C):
                    kc = slice(h * DK_C, (h + 1) * DK_C)
                    vc = slice(h * DV_C, (h + 1) * DV_C)
                    if factored:
                        att = jnp.where(keep, _dot_nt(qe[:, kc], kd[:, kc]), 0.0)
                    else:
                        att = _pairwise_att(q[:, kc], k[:, kc], c[:, kc], d == 0)
                    s_t = st[d, s, h]
                    o_r[s, :, vc] = _dot_nt(qe[:, kc], s_t.astype(BF16)) + _dot(att.astype(BF16), v[:, vc])
                    st[d, s, h] = s_t * decay[:, kc] + _dot_tn(v[:, vc], k2[:, kc])

    pl.when(strong_ref[0] == 0)(lambda: scan_step(True))
    pl.when(strong_ref[0] != 0)(lambda: scan_step(False))

    if not has_init:
        @pl.when(j == nchunk - 1)
        def _():
            sf_ref[...] = st[0]
            sb_ref[...] = st[1]


def _gla_call(has_init, nseq, seqlen, row0, strong, q, k, v, la, prev_f=None, prev_b=None, s0f=None, s0b=None):
    nchunk = seqlen // GLA_CHUNK
    nview = T // seqlen
    g0 = row0 // seqlen // GLA_NS
    view = lambda a: a.reshape(nview, seqlen, a.shape[-1])
    fwd = lambda g, j, *_: (g0 + g, j, 0)
    bwd = lambda g, j, *_: (g0 + g, nchunk - 1 - j, 0)
    bwd_la = lambda g, j, *_: (g0 + g, nchunk - 1 - j, 1)
    blk = lambda w, m: pl.BlockSpec((GLA_NS, GLA_CHUNK, w), m)
    state_spec = pl.BlockSpec((GLA_NS, H_C, DV_C, DK_C), lambda g, j, *_: (g, 0, 0, 0))
    in_specs = [blk(_HK, fwd), blk(_HK, fwd), blk(_HV, fwd), blk(_HK, fwd),
                blk(_HK, bwd), blk(_HK, bwd), blk(_HV, bwd), blk(_HK, bwd_la)]
    args = [view(q), view(k), view(v), view(la)] * 2
    aliases = {}
    if has_init:
        in_specs += [state_spec, state_spec] + [pl.BlockSpec(memory_space=pl.ANY)] * 2
        args += [s0f, s0b, view(prev_f), view(prev_b)]
        aliases = {len(args) - 1: 0, len(args): 1}
    out_specs = [blk(_HV, fwd), blk(_HV, bwd)]
    out_shape = [jax.ShapeDtypeStruct((nview, seqlen, _HV), F32)] * 2
    if not has_init:
        out_specs += [state_spec, state_spec]
        out_shape += [jax.ShapeDtypeStruct((nseq, H_C, DV_C, DK_C), F32)] * 2
    grid_spec = pltpu.PrefetchScalarGridSpec(
        num_scalar_prefetch=1,
        grid=(nseq // GLA_NS, nchunk),
        in_specs=in_specs,
        out_specs=out_specs,
        scratch_shapes=[pltpu.VMEM((2, GLA_NS, H_C, DV_C, DK_C), F32)],
    )
    outs = pl.pallas_call(
        functools.partial(_gla_kernel, has_init, nchunk),
        grid_spec=grid_spec,
        out_shape=out_shape,
        input_output_aliases=aliases,
        compiler_params=_params(("arbitrary", "arbitrary")),
        name="gla_sample" if has_init else "gla_prompt",
    )(strong, *args)
    return [outs[0].reshape(T, _HV), outs[1].reshape(T, _HV)] + list(outs[2:])


def _rope_tables():
    half = ROPE // 2
    inv_freq = np.power(np.float32(ROPE_THETA), -np.arange(0, half, 2, dtype=np.float32) / np.float32(half))
    n = np.arange(SAMPLE_LEN)
    row = (n // GRID_W).astype(np.float32)
    col = (n % GRID_W).astype(np.float32)
    ang_r = (row[:, None] * inv_freq[None, :]).astype(np.float32)
    ang_c = (col[:, None] * inv_freq[None, :]).astype(np.float32)
    nf = half // 2
    c = np.ones((TM + SAMPLE_LEN, LANES), np.float32)
    s1 = np.zeros((TM + SAMPLE_LEN, LANES), np.float32)
    s2 = np.zeros((TM + SAMPLE_LEN, LANES), np.float32)
    for base, ang in ((NOPE, ang_r), (NOPE + half, ang_c)):
        c[TM:, base:base + nf] = np.cos(ang)
        c[TM:, base + nf:base + half] = np.cos(ang)
        s1[TM:, base:base + nf] = -np.sin(ang)
        s2[TM:, base + nf:base + half] = np.sin(ang)
    return jnp.asarray(c), jnp.asarray(s1), jnp.asarray(s2)


def _pad_heads(w, nheads, width, lo=0):
    k = w.shape[0]
    w = w.reshape(k, nheads, width)
    w = jnp.pad(w, ((0, 0), (0, 0), (lo, HP - lo - width)))
    return w.reshape(k, nheads * HP)


def _row128(v, lo=0):
    return jnp.pad(v, (lo, LANES - lo - v.shape[0])).reshape(1, LANES)


def _even_layer(xs, xp, mod, nmix, w_in, q_a_norm, w_uq, q_norm, kv_a_norm, w_ukv, k_norm, v_norm, w_s, b_s, w_out,
                cache_ckv, cache_kpe):
    s = np.cumsum([Q_LORA, KV_LORA, ROPE, W_B])
    w_q, w_ckv, w_kpe, w_u, w_v = (w_in[:, :s[0]], w_in[:, s[0]:s[1]], w_in[:, s[1]:s[2]], w_in[:, s[2]:s[3]],
                                   w_in[:, s[3]:])
    w_kpe = jnp.pad(w_kpe, ((0, 0), (NOPE, LANES - QK_DIM)))
    win = jnp.concatenate([w_q, w_ckv, w_u, w_v, w_kpe], axis=1).astype(BF16)
    wuq = _pad_heads(w_uq, H_A, QK_DIM).astype(BF16)
    ukv = w_ukv.reshape(KV_LORA, H_A, NOPE + V_A)
    wuk = _pad_heads(ukv[:, :, :NOPE].reshape(KV_LORA, H_A * NOPE), H_A, NOPE)
    wuv = _pad_heads(ukv[:, :, NOPE:].reshape(KV_LORA, H_A * V_A), H_A, V_A)
    wukv = jnp.concatenate([wuk, wuv], axis=1).astype(BF16)
    qgain = _row128(q_norm * (QK_DIM ** -0.5 * LOG2_E))
    kgain = _row128(k_norm)
    bias = b_s.reshape(G_B // 2, 2, CHUNK_B)
    bias = jnp.concatenate([jnp.broadcast_to(bias[:, 0, :, None], (G_B // 2, CHUNK_B, C_B)),
                            jnp.broadcast_to(bias[:, 1, :, None], (G_B // 2, CHUNK_B, C_B))], axis=-1)
    rc, rs1, rs2 = _rope_tables()
    rope_spec = pl.BlockSpec((TM, LANES), lambda i: (_rope_blk(i), 0))
    width = H_A * HP
    q, k, v, ob, ckv, kpe = pl.pallas_call(
        _even_in_kernel,
        grid=(NBLK,),
        in_specs=[
            _X_SAMPLE, _X_PROMPT, _MOD, _full((1, D)), _full((D, _WIN_N)), _full((1, Q_LORA)),
            _full((Q_LORA, width)),
            _full((1, LANES)), _full((1, KV_LORA)), _full((KV_LORA, 2 * width)), _full((1, LANES)),
            _full((1, W_B)), _full((G_B // 2, 2 * CHUNK_B, CHUNK_B)), _full((G_B // 2, CHUNK_B, LANES)),
            rope_spec, rope_spec, rope_spec,
        ],
        out_specs=[_TOK(width), _TOK(width), _TOK(width), _TOK(W_B), _TOK(KV_LORA), _TOK(ROPE)],
        out_shape=[
            jax.ShapeDtypeStruct((T, width), BF16), jax.ShapeDtypeStruct((T, width), BF16),
            jax.ShapeDtypeStruct((T, width), BF16), jax.ShapeDtypeStruct((T, W_B), BF16),
            jax.ShapeDtypeStruct((T, KV_LORA), F32), jax.ShapeDtypeStruct((T, ROPE), F32),
        ],
        compiler_params=_params(("arbitrary",)),
        name="even_in",
    )(xs, xp, mod, nmix.reshape(1, D), win, q_a_norm.reshape(1, Q_LORA), wuq, qgain, kv_a_norm.reshape(1, KV_LORA),
      wukv, kgain, v_norm.reshape(1, W_B), w_s.astype(BF16).reshape(G_B // 2, 2 * CHUNK_B, CHUNK_B), bias, rc, rs1,
      rs2)

    n_ctx = N_SAMPLE_SEQ * PAST_LEN
    kpe_ctx = jnp.pad(cache_kpe.reshape(n_ctx, ROPE), ((0, 0), (NOPE, LANES - QK_DIM)))
    k_ctx, v_ctx = pl.pallas_call(
        _ctx_kv_kernel,
        grid=(n_ctx // TM,),
        in_specs=[_TOK(KV_LORA), _TOK(LANES), _full((KV_LORA, 2 * width)), _full((1, LANES))],
        out_specs=[_TOK(width), _TOK(width)],
        out_shape=[jax.ShapeDtypeStruct((n_ctx, width), BF16)] * 2,
        compiler_params=_params(("arbitrary",)),
        name="ctx_kv",
    )(cache_ckv.reshape(n_ctx, KV_LORA), kpe_ctx, wukv, kgain)

    oa = _attention(q, k, v, k_ctx, v_ctx)
    woa = jnp.pad(w_out[:H_A * V_A].reshape(H_A, V_A, D), ((0, 0), (0, HP - V_A), (0, 0))).reshape(width, D)
    return oa, ob, woa.astype(BF16), w_out[H_A * V_A:].astype(BF16), ckv, kpe


def kernel(x_prompt, x_sample, cache_mla_ckv, cache_mla_kpe, state_gla_fwd, state_gla_bwd, c, c_ctx, ada_w, ada_b,
           norm_mix, norm_ffn, even_w_in, mla_q_a_norm, mla_w_uq, mla_q_norm, mla_kv_a_norm, mla_w_ukv, mla_k_norm,
           cmlp_v_norm, cmlp_w_s, cmlp_b_s, even_w_out, odd_w_in, gla_w_gk_fwd, gla_b_gk_fwd, gla_w_gk_bwd,
           gla_b_gk_bwd, gla_o_norm, odd_w_out, moe_w_router, moe_b_router, moe_w_gu, moe_b_gu, moe_w_down,
           moe_b_down):
    xs0, xp0 = x_sample.reshape(N_SAMPLE, D), x_prompt.reshape(N_PROMPT, D)
    cond8 = jnp.concatenate([c_ctx[None, :], c, jnp.zeros((SUBLANES - 1 - N_SAMPLE_SEQ, D), F32)], axis=0)
    mods = _adaln(cond8, ada_w, ada_b)
    wr = jnp.pad(moe_w_router, ((0, 0), (0, 0), (0, LANES - N_EXPERTS)))
    wr_hi = wr.astype(BF16)
    wr = jnp.concatenate([wr_hi, (wr - wr_hi.astype(F32)).astype(BF16)], axis=-1)
    br = jnp.pad(moe_b_router, ((0, 0), (0, LANES - N_EXPERTS))).reshape(2, 1, LANES)

    oa, ob, woa, wob, ckv, kpe = _even_layer(
        xs0, xp0, mods[0], norm_mix[0], even_w_in[0], mla_q_a_norm[0], mla_w_uq[0], mla_q_norm[0], mla_kv_a_norm[0],
        mla_w_ukv[0], mla_k_norm[0], cmlp_v_norm[0], cmlp_w_s[0], cmlp_b_s[0], even_w_out[0],
        cache_mla_ckv[:, 0], cache_mla_kpe[:, 0])
    width = H_A * HP
    x1, h2, meta, cnt = pl.pallas_call(
        _even_out_kernel,
        grid=(NBLK,),
        in_specs=[_TOK(width), _TOK(W_B), _X_SAMPLE, _X_PROMPT, _MOD, _full((width, D)), _full((W_B, D)),
                  _full((1, D)),
                  _full((D, 2 * LANES)), _full((1, LANES))],
        out_specs=_PROLOGUE_OUT_SPECS,
        out_shape=_PROLOGUE_OUT_SHAPE,
        compiler_params=_params(("arbitrary",)),
        name="even_out",
    )(oa, ob, xs0, xp0, mods[0], woa, wob, norm_ffn[0].reshape(1, D), wr[0], br[0])
    x2 = _moe(0, x1, h2, meta, cnt, mods[0], moe_w_gu, moe_b_gu, moe_w_down, moe_b_down)

    w_in = odd_w_in[0]
    win = jnp.concatenate([w_in, jnp.zeros((D, LANES - 2 * GATE_RANK), F32)], axis=1).astype(BF16)
    wgk = jnp.zeros((LANES, 2 * _HK), F32)
    wgk = wgk.at[:GATE_RANK, :_HK].set(gla_w_gk_fwd[0]).at[GATE_RANK:2 * GATE_RANK, _HK:].set(gla_w_gk_bwd[0])
    bgk = jnp.concatenate([gla_b_gk_fwd[0], gla_b_gk_bwd[0]]).reshape(1, 2 * _HK)
    q, k, v, g, la, lamin = pl.pallas_call(
        _odd_in_kernel,
        grid=(NBLK,),
        in_specs=[_TOK(D), _MOD, _full((1, D)), _full((D, _ODD_MAIN + LANES)), _full((LANES, 2 * _HK)),
                  _full((1, 2 * _HK))],
        out_specs=[_TOK(_HK), _TOK(_HK), _TOK(_HV), _TOK(_HV), _TOK(2 * _HK), _TILE_ROW],
        out_shape=[jax.ShapeDtypeStruct((T, _HK), F32), jax.ShapeDtypeStruct((T, _HK), F32),
                   jax.ShapeDtypeStruct((T, _HV), F32), jax.ShapeDtypeStruct((T, _HV), F32),
                   jax.ShapeDtypeStruct((T, 2 * _HK), F32), jax.ShapeDtypeStruct((NBLK, SUBLANES, LANES), F32)],
        compiler_params=_params(("arbitrary",)),
        name="odd_in",
    )(x2, mods[1], norm_mix[1].reshape(1, D), win, wgk.astype(BF16), bgk)

    strong = (jnp.min(lamin) < -GLA_SAFE_LOG_DECAY).astype(jnp.int32).reshape(1)
    of, obk, st_f, st_b = _gla_call(False, N_PROMPT_SEQ, PROMPT_LEN, N_SAMPLE, strong, q, k, v, la)
    s0f = state_gla_fwd[:, 0].transpose(0, 1, 3, 2)
    s0b = state_gla_bwd[:, 0].transpose(0, 1, 3, 2)
    of, obk = _gla_call(True, N_SAMPLE_SEQ, SAMPLE_LEN, 0, strong, q, k, v, la, of, obk, s0f, s0b)

    x3, h2, meta, cnt = pl.pallas_call(
        _odd_out_kernel,
        grid=(NBLK,),
        in_specs=[_TOK(_HV), _TOK(_HV), _TOK(_HV), _TOK(D), _MOD, _full((1, DV_C)), _full((_HV, D)),
                  _full((1, D)), _full((D, 2 * LANES)), _full((1, LANES))],
        out_specs=_PROLOGUE_OUT_SPECS,
        out_shape=_PROLOGUE_OUT_SHAPE,
        compiler_params=_params(("arbitrary",)),
        name="odd_out",
    )(of, obk, g, x2, mods[1], gla_o_norm[0].reshape(1, DV_C), odd_w_out[0].astype(BF16),
      norm_ffn[1].reshape(1, D), wr[1], br[1])
    ys, yp = _moe(1, x3, h2, meta, cnt, mods[1], moe_w_gu, moe_b_gu, moe_w_down, moe_b_down, split_out=True)

    y_sample = ys.reshape(N_SAMPLE_SEQ, SAMPLE_LEN, D)
    y_prompt = yp.reshape(N_PROMPT_SEQ, PROMPT_LEN, D)
    new_ckv = ckv[N_SAMPLE:].reshape(N_PROMPT_SEQ, 1, PROMPT_LEN, KV_LORA)
    new_kpe = kpe[N_SAMPLE:].reshape(N_PROMPT_SEQ, 1, PROMPT_LEN, ROPE)
    new_fwd = st_f.transpose(0, 1, 3, 2)[:, None]
    new_bwd = st_b.transpose(0, 1, 3, 2)[:, None]
    return (y_prompt, y_sample, new_ckv, new_kpe, new_fwd, new_bwd)
```

```python
import functools

import numpy as np
import jax
import jax.numpy as jnp
from jax import lax
from jax.experimental import pallas as pl
from jax.experimental.pallas import tpu as pltpu

F32 = jnp.float32
BF16 = jnp.bfloat16

D = 1024
N_PROMPT_SEQ, PROMPT_LEN = 16, 256
N_SAMPLE_SEQ, SAMPLE_LEN = 4, 2048
PAST_LEN = 512
N_PROMPT = N_PROMPT_SEQ * PROMPT_LEN
N_SAMPLE = N_SAMPLE_SEQ * SAMPLE_LEN
T = N_PROMPT + N_SAMPLE
EPS = 1e-6
GRID_W = 64
H_A, Q_LORA, KV_LORA, NOPE, ROPE, V_A = 8, 512, 256, 64, 32, 64
QK_DIM = NOPE + ROPE
G_B, C_B, W_B, CHUNK_B = 8, 64, 512, 128
H_C, DK_C, DV_C, GATE_RANK, GATE_TAU, GLA_CHUNK = 4, 128, 256, 16, 16.0, 64
N_EXPERTS, TOP_K, D_FF = 32, 4, 1024
SWIGLU_LIMIT, SWIGLU_ALPHA = 7.0, 1.702
ROPE_THETA = 10000.0
LOG2_E = 1.4426950408889634

LANES = 128
SUBLANES = 8
VMEM_LIMIT = 56 * 1024 * 1024

TM = 512
NBLK = T // TM
SAMPLE_BLKS = N_SAMPLE // TM
BLKS_PER_SAMPLE_SEQ = SAMPLE_LEN // TM
ATT_TQ = 512
ATT_VMEM_LIMIT = 60 * 1024 * 1024
GLA_NS = 4
GLA_SAFE_LOG_DECAY = 60.0
ETILE = 512
TMD = TM
NBD = T // TMD
NBD_SAMPLE = N_SAMPLE // TMD
CH = 2 * SUBLANES
_MAX_LOCAL = TMD * TOP_K + N_EXPERTS * (CH - 1)
_POS_BLOCK = 256
ROWS_L = -(-_MAX_LOCAL // _POS_BLOCK) * _POS_BLOCK
MAX_CHUNKS = ROWS_L // CH
N_ETILES = -(-NBD * _MAX_LOCAL // ETILE) + N_EXPERTS
P_ROWS = N_ETILES * ETILE
HP = LANES


def _cond_row(i):
    return jnp.where(i < SAMPLE_BLKS, 1 + i // BLKS_PER_SAMPLE_SEQ, 0)


def _rope_blk(i):
    return jnp.where(i < SAMPLE_BLKS, 1 + i % BLKS_PER_SAMPLE_SEQ, 0)


def _rms(x):
    return x * lax.rsqrt(jnp.mean(x * x, axis=-1, keepdims=True) + EPS)


def _gelu(x):
    return 0.5 * x * (1.0 + jnp.tanh(0.7978845608028654 * (x + 0.044715 * (x * x * x))))


def _silu(x):
    return x * jax.nn.sigmoid(x)


def _dot(a, b):
    return jnp.dot(a, b, preferred_element_type=F32)


def _dot_nt(a, b):
    return lax.dot_general(a, b, (((1,), (1,)), ((), ())), preferred_element_type=F32)


def _dot_tn(a, b):
    return lax.dot_general(a, b, (((0,), (0,)), ((), ())), preferred_element_type=F32)


def _params(sem, vmem=VMEM_LIMIT):
    return pltpu.CompilerParams(dimension_semantics=sem, vmem_limit_bytes=vmem)


def _full(shape):
    nd = len(shape)
    return pl.BlockSpec(shape, lambda *_: (0,) * nd)


ADA_TN = 1536


def _adaln_kernel(c_ref, w_ref, b_ref, o_ref):
    s = _silu(c_ref[...]).astype(BF16)
    o_ref[0] = _dot(s, w_ref[0].astype(BF16)) + b_ref[0]


def _adaln(cond8, ada_w, ada_b):
    depth = ada_w.shape[0]
    n = ada_w.shape[2]
    out = pl.pallas_call(
        _adaln_kernel,
        grid=(depth, n // ADA_TN),
        in_specs=[
            pl.BlockSpec((SUBLANES, D), lambda l, j: (0, 0)),
            pl.BlockSpec((1, D, ADA_TN), lambda l, j: (l, 0, j)),
            pl.BlockSpec((1, 1, ADA_TN), lambda l, j: (l, 0, j)),
        ],
        out_specs=pl.BlockSpec((1, SUBLANES, ADA_TN), lambda l, j: (l, 0, j)),
        out_shape=jax.ShapeDtypeStruct((depth, SUBLANES, n), F32),
        compiler_params=_params(("arbitrary", "arbitrary")),
        name="adaln",
    )(cond8, ada_w, ada_b.reshape(depth, 1, n))
    return out.reshape(depth, SUBLANES, 6, D)


_QC0, _CKV0, _U0, _V0, _KPE0, _WIN_N = 0, 512, 768, 1280, 1792, 1920


def _rope(y, c, s1, s2):
    return y * c + pltpu.roll(y, LANES - 8, 1) * s1 + pltpu.roll(y, 8, 1) * s2


def _k_heads(k_raw, kpe128, kp_rot, kgain, k_ref):
    sskpe = jnp.sum(kpe128 * kpe128, axis=-1, keepdims=True)
    for h in range(H_A):
        kb = k_raw[:, h * HP:(h + 1) * HP]
        r = lax.rsqrt((jnp.sum(kb * kb, axis=-1, keepdims=True) + sskpe) * (1.0 / QK_DIM) + EPS)
        k_ref[:, h * HP:(h + 1) * HP] = ((kb * kgain + kp_rot) * r).astype(BF16)


def _input_rows(xs_ref, xp_ref):
    return jnp.where(pl.program_id(0) < SAMPLE_BLKS, xs_ref[...], xp_ref[...])


_X_SAMPLE = pl.BlockSpec((TM, D), lambda i, *_: (jnp.minimum(i, SAMPLE_BLKS - 1), 0))
_X_PROMPT = pl.BlockSpec((TM, D), lambda i, *_: (jnp.maximum(i - SAMPLE_BLKS, 0), 0))


def _even_in_kernel(xs_ref, xp_ref, mod_ref, nmix_ref, win_ref, qan_ref, wuq_ref, qgain_ref, kvan_ref, wukv_ref,
                    kgain_ref, vnorm_ref, ws_ref, bs_ref, rc_ref, rs1_ref, rs2_ref,
                    q_ref, k_ref, v_ref, ob_ref, ckv_ref, kpe_ref):
    m = mod_ref[0]
    h = _rms(_input_rows(xs_ref, xp_ref)) * nmix_ref[...] * (1.0 + m[1:2]) + m[0:1]
    a = _dot(h.astype(BF16), win_ref[...])
    qc = a[:, _QC0:_CKV0]
    ckv = a[:, _CKV0:_U0]
    u = a[:, _U0:_V0]
    vv = a[:, _V0:_KPE0]
    kpe128 = a[:, _KPE0:_WIN_N]

    ckv_n = _rms(ckv) * kvan_ref[...]
    ckv_ref[...] = ckv_n
    kpe_ref[...] = kpe128[:, NOPE:QK_DIM]

    rc, rs1, rs2 = rc_ref[...], rs1_ref[...], rs2_ref[...]
    qn = (_rms(qc) * qan_ref[...]).astype(BF16)
    qr = _dot(qn, wuq_ref[...])
    qgain = qgain_ref[...]
    for hh in range(H_A):
        blk = qr[:, hh * HP:(hh + 1) * HP]
        r = lax.rsqrt(jnp.sum(blk * blk, axis=-1, keepdims=True) * (1.0 / QK_DIM) + EPS)
        q_ref[:, hh * HP:(hh + 1) * HP] = _rope(blk * r * qgain, rc, rs1, rs2).astype(BF16)

    kv = _dot(ckv_n.astype(BF16), wukv_ref[...])
    v_ref[...] = kv[:, H_A * HP:].astype(BF16)
    kgain = kgain_ref[...]
    kp_rot = _rope(kpe128 * kgain, rc, rs1, rs2)
    _k_heads(kv[:, :H_A * HP], kpe128, kp_rot, kgain, k_ref)

    ug = _gelu(u)
    vn = (_rms(_gelu(vv)) * vnorm_ref[...]).astype(BF16)
    low = lax.broadcasted_iota(jnp.int32, (CHUNK_B, LANES), 1) < C_B
    for c in range(TM // CHUNK_B):
        rows = slice(c * CHUNK_B, (c + 1) * CHUNK_B)
        for p in range(G_B // 2):
            cols = slice(p * LANES, (p + 1) * LANES)
            blk = vn[rows, cols]
            both = _dot(ws_ref[p], blk)
            mixed = jnp.where(low, both[:CHUNK_B], both[CHUNK_B:]) + bs_ref[p]
            ob_ref[rows, cols] = (ug[rows, cols] * mixed).astype(BF16)


def _ctx_kv_kernel(ckv_ref, kpe_ref, wukv_ref, kgain_ref, k_ref, v_ref):
    kv = _dot(ckv_ref[...].astype(BF16), wukv_ref[...])
    v_ref[...] = kv[:, H_A * HP:].astype(BF16)
    kgain = kgain_ref[...]
    kpe128 = kpe_ref[...]
    _k_heads(kv[:, :H_A * HP], kpe128, kpe128 * kgain, kgain, k_ref)


def _attn_self_kernel(q_ref, k_ref, v_ref, o_ref):
    for h in range(H_A):
        cols = slice(h * HP, (h + 1) * HP)
        s = _dot_nt(q_ref[:, cols], k_ref[:, cols])
        p = jnp.exp2(s - jnp.max(s, axis=-1, keepdims=True))
        inv = 1.0 / jnp.sum(p, axis=-1, keepdims=True)
        o_ref[:, cols] = (_dot(p.astype(BF16), v_ref[:, cols]) * inv).astype(BF16)


def _attn_ctx_kernel(q_ref, k_ref, v_ref, kc_ref, vc_ref, prev_ref, o_ref):
    del prev_ref
    for h in range(H_A):
        cols = slice(h * HP, (h + 1) * HP)
        q = q_ref[:, cols]
        s1 = _dot_nt(q, k_ref[:, cols])
        s2 = _dot_nt(q, kc_ref[:, cols])
        mx = jnp.maximum(jnp.max(s1, axis=-1, keepdims=True), jnp.max(s2, axis=-1, keepdims=True))
        p1 = jnp.exp2(s1 - mx)
        p2 = jnp.exp2(s2 - mx)
        inv = 1.0 / (jnp.sum(p1, axis=-1, keepdims=True) + jnp.sum(p2, axis=-1, keepdims=True))
        o = _dot(p1.astype(BF16), v_ref[:, cols]) + _dot(p2.astype(BF16), vc_ref[:, cols])
        o_ref[:, cols] = (o * inv).astype(BF16)


def _attention(q, k, v, k_ctx, v_ctx):
    width = H_A * HP
    first = N_SAMPLE // PROMPT_LEN
    o = pl.pallas_call(
        _attn_self_kernel,
        grid=(N_PROMPT_SEQ,),
        in_specs=[pl.BlockSpec((PROMPT_LEN, width), lambda i: (first + i, 0))] * 3,
        out_specs=pl.BlockSpec((PROMPT_LEN, width), lambda i: (first + i, 0)),
        out_shape=jax.ShapeDtypeStruct((T, width), BF16),
        compiler_params=_params(("arbitrary",)),
        name="attn_prompt",
    )(q, k, v)
    qblocks = SAMPLE_LEN // ATT_TQ
    qblk = lambda b, j: (b * qblocks + j, 0)
    return pl.pallas_call(
        _attn_ctx_kernel,
        grid=(N_SAMPLE_SEQ, qblocks),
        in_specs=[
            pl.BlockSpec((ATT_TQ, width), qblk),
            pl.BlockSpec((SAMPLE_LEN, width), lambda b, j: (b, 0)),
            pl.BlockSpec((SAMPLE_LEN, width), lambda b, j: (b, 0)),
            pl.BlockSpec((PAST_LEN, width), lambda b, j: (b, 0)),
            pl.BlockSpec((PAST_LEN, width), lambda b, j: (b, 0)),
            pl.BlockSpec(memory_space=pl.ANY),
        ],
        out_specs=pl.BlockSpec((ATT_TQ, width), qblk),
        out_shape=jax.ShapeDtypeStruct((T, width), BF16),
        input_output_aliases={5: 0},
        compiler_params=_params(("arbitrary", "arbitrary"), ATT_VMEM_LIMIT),
        name="attn_sample",
    )(q, k, v, k_ctx, v_ctx, o)


_META_IDX, _META_RANK, _META_W = 0, TOP_K, 2 * TOP_K


def _moe_prologue(x1, m, nffn_ref, wr_ref, br_ref, x1_ref, h2_ref, meta_ref, cnt_ref):
    x1_ref[...] = x1
    h2 = _rms(x1) * nffn_ref[...] * (1.0 + m[4:5]) + m[3:4]
    h2_ref[...] = h2.astype(BF16)
    lane = lax.broadcasted_iota(jnp.int32, (TM, LANES), 1)
    lanef = lane.astype(F32)
    h_hi = h2.astype(BF16)
    h_lo = (h2 - h_hi.astype(F32)).astype(BF16)
    r = _dot(h_hi, wr_ref[...])
    logits = r[:, :LANES] + r[:, LANES:] + _dot(h_lo, wr_ref[:, :LANES]) + br_ref[...]
    work = jnp.where(lane < N_EXPERTS, logits, -jnp.inf)
    hots, vals = [], []
    for _ in range(TOP_K):
        mx = jnp.max(work, axis=-1, keepdims=True)
        idx = jnp.min(jnp.where(work == mx, lanef, float(LANES)), axis=-1, keepdims=True)
        hot = lanef == idx
        work = jnp.where(hot, -jnp.inf, work)
        hots.append((hot, idx))
        vals.append(mx)
    es = [jnp.exp(v - vals[0]) for v in vals]
    inv = 1.0 / (es[0] + es[1] + es[2] + es[3])
    sel = jnp.zeros((TM, LANES), F32)
    for hot, _ in hots:
        sel = jnp.where(hot, 1.0, sel)
    row = lax.broadcasted_iota(jnp.int32, (TM, TM), 0)
    col = lax.broadcasted_iota(jnp.int32, (TM, TM), 1)
    strict = jnp.where(row > col, 1.0, 0.0).astype(BF16)
    before = _dot(strict, sel.astype(BF16))
    meta = jnp.zeros((TM, LANES), F32)
    for kk, (hot, idx) in enumerate(hots):
        rank = jnp.sum(jnp.where(hot, before, 0.0), axis=-1, keepdims=True)
        meta = jnp.where(lane == _META_IDX + kk, idx, meta)
        meta = jnp.where(lane == _META_RANK + kk, rank, meta)
        meta = jnp.where(lane == _META_W + kk, es[kk] * inv, meta)
    meta_ref[...] = meta
    cnt_ref[0] = jnp.broadcast_to(jnp.sum(sel, axis=0, keepdims=True), (SUBLANES, LANES))


def _even_out_kernel(oa_ref, ob_ref, xs_ref, xp_ref, mod_ref, woa_ref, wob_ref, nffn_ref, wr_ref, br_ref,
                     x1_ref, h2_ref, meta_ref, cnt_ref):
    m = mod_ref[0]
    out = _dot(oa_ref[...], woa_ref[...]) + _dot(ob_ref[...], wob_ref[...])
    x1 = _input_rows(xs_ref, xp_ref) + m[2:3] * out
    _moe_prologue(x1, m, nffn_ref, wr_ref, br_ref, x1_ref, h2_ref, meta_ref, cnt_ref)


def _odd_out_kernel(of_ref, ob_ref, g_ref, x_ref, mod_ref, onorm_ref, wo_ref, nffn_ref, wr_ref, br_ref,
                    x1_ref, h2_ref, meta_ref, cnt_ref):
    m = mod_ref[0]
    onorm = onorm_ref[...]
    parts = []
    for h in range(H_C):
        cols = slice(h * DV_C, (h + 1) * DV_C)
        o = of_ref[:, cols] + ob_ref[:, cols]
        parts.append((_rms(o) * onorm * _silu(g_ref[:, cols])).astype(BF16))
    out = _dot(jnp.concatenate(parts, axis=-1), wo_ref[...])
    x1 = x_ref[...] + m[2:3] * out
    _moe_prologue(x1, m, nffn_ref, wr_ref, br_ref, x1_ref, h2_ref, meta_ref, cnt_ref)


_TOK = lambda w: pl.BlockSpec((TM, w), lambda i, *_: (i, 0))
_MOD = pl.BlockSpec((1, 6, D), lambda i, *_: (_cond_row(i), 0, 0))
_TILE_ROW = pl.BlockSpec((1, SUBLANES, LANES), lambda i, *_: (i, 0, 0))

_PROLOGUE_OUT_SPECS = [_TOK(D), _TOK(D), _TOK(LANES), _TILE_ROW]
_PROLOGUE_OUT_SHAPE = [
    jax.ShapeDtypeStruct((T, D), F32),
    jax.ShapeDtypeStruct((T, D), BF16),
    jax.ShapeDtypeStruct((T, LANES), F32),
    jax.ShapeDtypeStruct((NBLK, SUBLANES, LANES), F32),
]


def _local_positions(meta, seg_rows):
    lanef = lax.broadcasted_iota(jnp.int32, (TM, LANES), 1).astype(F32)
    pos = []
    for k in range(TOP_K):
        halves = []
        for half in range(TMD // TM):
            m = meta[half * TM:(half + 1) * TM]
            hot = lanef == m[:, _META_IDX + k:_META_IDX + k + 1]
            start = jnp.sum(jnp.where(hot, seg_rows[half:half + 1], 0.0), axis=-1, keepdims=True)
            halves.append(start + m[:, _META_RANK + k:_META_RANK + k + 1])
        pos.append(jnp.concatenate(halves, axis=0))
    return pos


def _scatter_matrix(pos, vals):
    lane = lax.broadcasted_iota(jnp.int32, (TMD, _POS_BLOCK), 1).astype(F32).astype(BF16)
    blocks = []
    for j in range(ROWS_L // _POS_BLOCK):
        acc = jnp.zeros((TMD, _POS_BLOCK), BF16)
        for p, v in zip(pos, vals):
            acc = jnp.where(lane == (p - float(j * _POS_BLOCK)).astype(BF16), v, acc)
        blocks.append(acc)
    return jnp.concatenate(blocks, axis=1)


def _chunk_copies(t, count, csrc_ref, cdst_ref, make):
    def one(j, c):
        q = t * MAX_CHUNKS + j
        make(pl.multiple_of(csrc_ref[q], CH), pl.multiple_of(cdst_ref[q], CH)).start()
        return c

    lax.fori_loop(0, count, one, 0)


def _drain(count, chunk_copy):
    def one(j, c):
        chunk_copy.wait()
        return c

    lax.fori_loop(0, count, one, 0)


def _dispatch_kernel(csrc_ref, cdst_ref, ntile_ref, tail0_ref, tailn_ref, h_ref, meta_ref, segrow_ref,
                     xs_hbm, buf, zbuf, sem):
    i = pl.program_id(0)
    slot = i % 2
    chunk = lambda s: pltpu.make_async_copy(buf.at[s, pl.ds(0, CH)], xs_hbm.at[pl.ds(0, CH)], sem.at[s])

    @pl.when(i == 0)
    def _():
        zbuf[...] = jnp.zeros_like(zbuf)

        def per_expert(e, carry):
            def one(j, c):
                pltpu.make_async_copy(zbuf, xs_hbm.at[pl.ds(pl.multiple_of(tail0_ref[e] + j * CH, CH), CH)],
                                      sem.at[2]).start()
                return c

            lax.fori_loop(0, tailn_ref[e], one, 0)
            return carry + tailn_ref[e]

        total = lax.fori_loop(0, N_EXPERTS, per_expert, 0)
        _drain(total, pltpu.make_async_copy(zbuf, xs_hbm.at[pl.ds(0, CH)], sem.at[2]))

    pos = _local_positions(meta_ref[...], segrow_ref[0])
    riota = lax.broadcasted_iota(jnp.int32, (TMD, ROWS_L), 1).astype(F32)
    pt = jnp.zeros((TMD, ROWS_L), F32)
    for p in pos:
        pt = jnp.where(riota == p, 1.0, pt)
    buf[slot] = _dot_tn(pt.astype(BF16), h_ref[...]).astype(BF16)

    make = lambda s, d: pltpu.make_async_copy(buf.at[slot, pl.ds(s, CH)], xs_hbm.at[pl.ds(d, CH)], sem.at[slot])
    _chunk_copies(i, ntile_ref[i], csrc_ref, cdst_ref, make)

    @pl.when(i > 0)
    def _():
        _drain(ntile_ref[i - 1], chunk(1 - slot))

    @pl.when(i == NBD - 1)
    def _():
        _drain(ntile_ref[i], chunk(slot))


_TOKD = lambda w: pl.BlockSpec((TMD, w), lambda i, *_: (i, 0))
_SEGROWS = pl.BlockSpec((1, TMD // TM, LANES), lambda i, *_: (i, 0, 0))


def _dispatch(plan, h2, meta):
    grid_spec = pltpu.PrefetchScalarGridSpec(
        num_scalar_prefetch=5,
        grid=(NBD,),
        in_specs=[_TOKD(D), _TOKD(LANES), _SEGROWS],
        out_specs=pl.BlockSpec(memory_space=pl.ANY),
        scratch_shapes=[pltpu.VMEM((2, ROWS_L, D), BF16), pltpu.VMEM((CH, D), BF16),
                        pltpu.SemaphoreType.DMA((3,))],
    )
    return pl.pallas_call(
        _dispatch_kernel,
        grid_spec=grid_spec,
        out_shape=jax.ShapeDtypeStruct((P_ROWS, D), BF16),
        compiler_params=_params(("arbitrary",)),
        name="moe_dispatch",
    )(plan["csrc"], plan["cdst"], plan["ntile"], plan["tail0"], plan["tailn"], h2, meta, plan["segrow"])


def _expert_kernel(layer, te_ref, tfirst_ref, tvalid_ref, xblk_ref, tnext_ref, wslot_ref, x_ref, wgu_hbm, bgu_ref,
                   wd_hbm, bd_ref, y_ref, wgu_f32, wd_f32, wgu_bf, wd_bf, sem):
    del xblk_ref
    i = pl.program_id(0)

    def weight_copies(e, slot):
        return (pltpu.make_async_copy(wgu_hbm.at[layer, e], wgu_f32.at[slot], sem.at[0, slot]),
                pltpu.make_async_copy(wd_hbm.at[layer, e], wd_f32.at[slot], sem.at[1, slot]))

    @pl.when(tfirst_ref[i] == 1)
    def _():
        slot = wslot_ref[i]

        @pl.when(i == 0)
        def _():
            for cp in weight_copies(te_ref[i], slot):
                cp.start()

        for cp in weight_copies(te_ref[i], slot):
            cp.wait()
        wgu_bf[...] = wgu_f32[slot].astype(BF16)
        wd_bf[...] = wd_f32[slot].astype(BF16)

        @pl.when(tnext_ref[i] >= 0)
        def _():
            for cp in weight_copies(tnext_ref[i], 1 - slot):
                cp.start()

    def ffn(x):
        a = _dot(x, wgu_bf[...]) + bgu_ref[...]
        glu = jnp.minimum(a[:, :D_FF], SWIGLU_LIMIT)
        lin = jnp.clip(a[:, D_FF:], -SWIGLU_LIMIT, SWIGLU_LIMIT)
        act = (glu * jax.nn.sigmoid(SWIGLU_ALPHA * glu)) * (lin + 1.0)
        return (_dot(act.astype(BF16), wd_bf[...]) + bd_ref[...]).astype(BF16)

    half = ETILE // 2

    @pl.when(tvalid_ref[i] == 2)
    def _():
        y_ref[...] = ffn(x_ref[...])

    @pl.when(tvalid_ref[i] == 1)
    def _():
        y_ref[:half] = ffn(x_ref[:half])
        y_ref[half:] = jnp.zeros((ETILE - half, D), BF16)


def _experts(layer, plan, xs, w_gu, b_gu, w_down, b_down):
    depth = w_gu.shape[0]
    e_of = lambda i, te, *_: (layer, te[i], 0, 0)
    grid_spec = pltpu.PrefetchScalarGridSpec(
        num_scalar_prefetch=6,
        grid=(N_ETILES,),
        in_specs=[
            pl.BlockSpec((ETILE, D), lambda i, te, tf, tv, xb, *_: (xb[i], 0)),
            pl.BlockSpec(memory_space=pl.ANY),
            pl.BlockSpec((None, None, 1, 2 * D_FF), e_of),
            pl.BlockSpec(memory_space=pl.ANY),
            pl.BlockSpec((None, None, 1, D), e_of),
        ],
        out_specs=pl.BlockSpec((ETILE, D), lambda i, te, tf, tv, xb, *_: (xb[i], 0)),
        scratch_shapes=[pltpu.VMEM((2, D, 2 * D_FF), F32), pltpu.VMEM((2, D_FF, D), F32),
                        pltpu.VMEM((D, 2 * D_FF), BF16), pltpu.VMEM((D_FF, D), BF16),
                        pltpu.SemaphoreType.DMA((2, 2))],
    )
    return pl.pallas_call(
        functools.partial(_expert_kernel, layer),
        grid_spec=grid_spec,
        out_shape=jax.ShapeDtypeStruct((P_ROWS, D), BF16),
        compiler_params=_params(("arbitrary",)),
        name="moe_experts",
    )(plan["te"], plan["tfirst"], plan["tvalid"], plan["xblk"], plan["tnext"], plan["wslot"], xs, w_gu,
      b_gu.reshape(depth, N_EXPERTS, 1, 2 * D_FF), w_down, b_down.reshape(depth, N_EXPERTS, 1, D))


def _combine_kernel(split, csrc_ref, cdst_ref, ntile_ref, y_hbm, x1_ref, meta_ref, segrow_ref, mod_ref, *rest):
    *o_refs, ybuf, sem = rest
    i = pl.program_id(0)
    slot = i % 2

    def fetch(t, s):
        make = lambda loc, glob: pltpu.make_async_copy(y_hbm.at[pl.ds(glob, CH)], ybuf.at[s, pl.ds(loc, CH)],
                                                       sem.at[s])
        _chunk_copies(t, ntile_ref[t], csrc_ref, cdst_ref, make)

    @pl.when(i == 0)
    def _():
        ybuf[...] = jnp.zeros_like(ybuf)
        fetch(i, slot)

    @pl.when(i + 1 < NBD)
    def _():
        fetch(i + 1, 1 - slot)

    _drain(ntile_ref[i], pltpu.make_async_copy(y_hbm.at[pl.ds(0, CH)], ybuf.at[slot, pl.ds(0, CH)], sem.at[slot]))

    meta = meta_ref[...]
    pos = _local_positions(meta, segrow_ref[0])
    gates = _scatter_matrix(pos, [meta[:, _META_W + k:_META_W + k + 1].astype(BF16) for k in range(TOP_K)])
    acc = _dot(gates, ybuf[slot])
    out = x1_ref[...] + mod_ref[0][5:6] * acc
    if split:
        os_ref, op_ref = o_refs

        @pl.when(i < NBD_SAMPLE)
        def _():
            os_ref[...] = out

        @pl.when(i >= NBD_SAMPLE)
        def _():
            op_ref[...] = out
    else:
        o_refs[0][...] = out


def _combine(plan, y, x1, meta, mod, split):
    if split:
        out_specs = [pl.BlockSpec((TMD, D), lambda i, *_: (jnp.minimum(i, NBD_SAMPLE - 1), 0)),
                     pl.BlockSpec((TMD, D), lambda i, *_: (jnp.maximum(i - NBD_SAMPLE, 0), 0))]
        out_shape = [jax.ShapeDtypeStruct((N_SAMPLE, D), F32), jax.ShapeDtypeStruct((N_PROMPT, D), F32)]
    else:
        out_specs, out_shape = _TOKD(D), jax.ShapeDtypeStruct((T, D), F32)
    grid_spec = pltpu.PrefetchScalarGridSpec(
        num_scalar_prefetch=3,
        grid=(NBD,),
        in_specs=[pl.BlockSpec(memory_space=pl.ANY), _TOKD(D), _TOKD(LANES), _SEGROWS,
                  pl.BlockSpec((1, 6, D), lambda i, *_: (_cond_row(i * (TMD // TM)), 0, 0))],
        out_specs=out_specs,
        scratch_shapes=[pltpu.VMEM((2, ROWS_L, D), BF16), pltpu.SemaphoreType.DMA((2,))],
    )
    return pl.pallas_call(
        functools.partial(_combine_kernel, split),
        grid_spec=grid_spec,
        out_shape=out_shape,
        compiler_params=_params(("arbitrary",)),
        name="moe_combine",
    )(plan["csrc"], plan["cdst"], plan["ntile"], y, x1, meta, plan["segrow"], mod)


def _moe_plan(cnt):
    per = TMD // TM
    cnt = cnt[:, 0, :N_EXPERTS].astype(jnp.int32).reshape(NBD, per, N_EXPERTS)
    cpad = (jnp.sum(cnt, axis=1) + CH - 1) // CH * CH
    tot = jnp.sum(cpad, axis=0)
    tiles = (tot + ETILE - 1) // ETILE
    tile_end = jnp.cumsum(tiles)
    offs = (tile_end - tiles) * ETILE
    base = offs[None, :] + jnp.cumsum(cpad, axis=0) - cpad
    seg = jnp.cumsum(cpad, axis=1) - cpad
    segrows = seg[:, None, :] + jnp.cumsum(cnt, axis=1) - cnt
    tid = jnp.arange(N_ETILES, dtype=jnp.int32)
    te = jnp.sum((tile_end[None, :] <= tid[:, None]).astype(jnp.int32), axis=1)
    used = te < N_EXPERTS
    last = jnp.max(jnp.where(tiles > 0, jnp.arange(N_EXPERTS, dtype=jnp.int32), 0))
    te = jnp.where(used, te, last)
    rows = tot[te] - (tid - (tile_end - tiles)[te]) * ETILE
    tvalid = jnp.where(used, jnp.where(rows > ETILE // 2, 2, 1), 0).astype(jnp.int32)
    tfirst = jnp.concatenate([jnp.ones((1,), jnp.int32), (te[1:] != te[:-1]).astype(jnp.int32)])
    eid = jnp.arange(N_EXPERTS, dtype=jnp.int32)
    later = (eid[None, :] > eid[:, None]) & (tiles[None, :] > 0)
    nxt = jnp.min(jnp.where(later, eid[None, :], N_EXPERTS), axis=1)
    nxt = jnp.where(nxt < N_EXPERTS, nxt, -1)
    nch = cpad // CH
    cend = jnp.cumsum(nch, axis=1)
    cid = jnp.arange(MAX_CHUNKS, dtype=jnp.int32)
    cstart = cend - nch
    owns = ((cstart[:, None, :] <= cid[None, :, None]) & (cid[None, :, None] < cend[:, None, :])).astype(jnp.int32)
    pick = lambda a: jnp.sum(owns * a[:, None, :], axis=2)
    within = (cid[None, :] - pick(cstart)) * CH
    live = jnp.sum(owns, axis=2)
    csrc = (pick(seg) + within) * live
    cdst = (pick(base) + within) * live
    return {
        "tnext": nxt[te], "wslot": (jnp.cumsum(tfirst) - 1) % 2, "tfirst": tfirst,
        "csrc": csrc.reshape(-1), "cdst": cdst.reshape(-1),
        "ntile": jnp.sum(nch, axis=1), "tail0": offs + tot, "tailn": (tiles * ETILE - tot) // CH,
        "segrow": jnp.pad(segrows.astype(F32), ((0, 0), (0, 0), (0, LANES - N_EXPERTS))),
        "te": te, "tvalid": tvalid, "xblk": jnp.where(used, tid, tile_end[-1] - 1),
    }


def _moe(layer, x1, h2, meta, cnt, mod, w_gu, b_gu, w_down, b_down, split_out=False):
    plan = _moe_plan(cnt)
    xs = _dispatch(plan, h2, meta)
    y = _experts(layer, plan, xs, w_gu, b_gu, w_down, b_down)
    return _combine(plan, y, x1, meta, mod, split_out)


_HK, _HV = H_C * DK_C, H_C * DV_C
_ODD_MAIN = 2 * _HK + 2 * _HV


def _odd_in_kernel(x_ref, mod_ref, nmix_ref, win_ref, wgk_ref, bgk_ref, q_ref, k_ref, v_ref, g_ref, la_ref,
                   lamin_ref):
    m = mod_ref[0]
    h = _rms(x_ref[...]) * nmix_ref[...] * (1.0 + m[1:2]) + m[0:1]
    a = _dot(h.astype(BF16), win_ref[...])
    q_ref[...] = a[:, :_HK] * (DK_C ** -0.5)
    k_ref[...] = a[:, _HK:2 * _HK]
    v_ref[...] = a[:, 2 * _HK:2 * _HK + _HV]
    g_ref[...] = a[:, 2 * _HK + _HV:_ODD_MAIN]
    z = _dot(a[:, _ODD_MAIN:].astype(BF16), wgk_ref[...]) + bgk_ref[...]
    la = (jnp.minimum(z, 0.0) - jnp.log(1.0 + jnp.exp(-jnp.abs(z)))) * (1.0 / GATE_TAU)
    la_ref[...] = la
    chunk_tot = jnp.sum(la.reshape(TM // GLA_CHUNK, GLA_CHUNK, 2 * _HK), axis=1)
    lamin_ref[0] = jnp.broadcast_to(jnp.min(chunk_tot, axis=(0, 1), keepdims=True), (SUBLANES, LANES))


def _pairwise_att(q, k, c, forward):
    tcol = lax.broadcasted_iota(jnp.int32, (GLA_CHUNK, 1), 0)
    trow = lax.broadcasted_iota(jnp.int32, (GLA_CHUNK, DK_C), 0)
    scol = lax.broadcasted_iota(jnp.int32, (GLA_CHUNK, GLA_CHUNK), 1)

    def body(s, att):
        k_s = jnp.sum(jnp.where(trow == s, k, 0.0), axis=0, keepdims=True)
        c_s = jnp.sum(jnp.where(trow == s, c, 0.0), axis=0, keepdims=True)
        allowed = (tcol >= s) if forward else (tcol <= s)
        e = jnp.exp(jnp.where(allowed, c - c_s, -jnp.inf))
        return jnp.where(scol == s, jnp.sum(q * k_s * e, axis=-1, keepdims=True), att)

    return lax.fori_loop(0, GLA_CHUNK, body, jnp.zeros((GLA_CHUNK, GLA_CHUNK), F32))


def _gla_kernel(has_init, nchunk, strong_ref, *refs):
    if has_init:
        (qf, kf, vf, laf, qb, kb, vb, lab, s0f, s0b, _, _, of_ref, ob_ref, st) = refs
    else:
        (qf, kf, vf, laf, qb, kb, vb, lab, of_ref, ob_ref, sf_ref, sb_ref, st) = refs
    j = pl.program_id(1)

    @pl.when(j == 0)
    def _():
        if has_init:
            st[0] = s0f[...]
            st[1] = s0b[...]
        else:
            st[...] = jnp.zeros_like(st)

    row = lax.broadcasted_iota(jnp.int32, (GLA_CHUNK, GLA_CHUNK), 0)
    col = lax.broadcasted_iota(jnp.int32, (GLA_CHUNK, GLA_CHUNK), 1)

    def scan_step(factored):
        for d, (q_r, k_r, v_r, la_r, o_r) in enumerate(((qf, kf, vf, laf, of_ref), (qb, kb, vb, lab, ob_ref))):
            keep = (col <= row) if d == 0 else (col >= row)
            tri = jnp.where(keep, 1.0, 0.0).astype(BF16)
            for s in range(GLA_NS):
                g = la_r[s]
                g_hi = g.astype(BF16)
                g_r = g - g_hi.astype(F32)
                g_mid = g_r.astype(BF16)
                g_lo = (g_r - g_mid.astype(F32)).astype(BF16)
                c = _dot(tri, g_hi) + _dot(tri, g_mid) + _dot(tri, g_lo)
                tot = jnp.sum(g, axis=0, keepdims=True)
                decay = jnp.exp(tot)
                q, k = q_r[s], k_r[s]
                qe = (q * jnp.exp(c)).astype(BF16)
                if factored:
                    e_neg = jnp.exp(-c)
                    kd = (k * e_neg).astype(BF16)
                    k2 = (k * (e_neg * decay)).astype(BF16)
                else:
                    k2 = (k * jnp.exp(tot - c)).astype(BF16)
                v = v_r[s].astype(BF16)
                for h in range(H_C):
                    kc = slice(h * DK_C, (h + 1) * DK_C)
                    vc = slice(h * DV_C, (h + 1) * DV_C)
                    if factored:
                        att = jnp.where(keep, _dot_nt(qe[:, kc], kd[:, kc]), 0.0)
                    else:
                        att = _pairwise_att(q[:, kc], k[:, kc], c[:, kc], d == 0)
                    s_t = st[d, s, h]
                    o_r[s, :, vc] = _dot_nt(qe[:, kc], s_t.astype(BF16)) + _dot(att.astype(BF16), v[:, vc])
                    st[d, s, h] = s_t * decay[:, kc] + _dot_tn(v[:, vc], k2[:, kc])

    pl.when(strong_ref[0] == 0)(lambda: scan_step(True))
    pl.when(strong_ref[0] != 0)(lambda: scan_step(False))

    if not has_init:
        @pl.when(j == nchunk - 1)
        def _():
            sf_ref[...] = st[0]
            sb_ref[...] = st[1]


def _gla_call(has_init, nseq, seqlen, row0, strong, q, k, v, la, prev_f=None, prev_b=None, s0f=None, s0b=None):
    nchunk = seqlen // GLA_CHUNK
    nview = T // seqlen
    g0 = row0 // seqlen // GLA_NS
    view = lambda a: a.reshape(nview, seqlen, a.shape[-1])
    fwd = lambda g, j, *_: (g0 + g, j, 0)
    bwd = lambda g, j, *_: (g0 + g, nchunk - 1 - j, 0)
    bwd_la = lambda g, j, *_: (g0 + g, nchunk - 1 - j, 1)
    blk = lambda w, m: pl.BlockSpec((GLA_NS, GLA_CHUNK, w), m)
    state_spec = pl.BlockSpec((GLA_NS, H_C, DV_C, DK_C), lambda g, j, *_: (g, 0, 0, 0))
    in_specs = [blk(_HK, fwd), blk(_HK, fwd), blk(_HV, fwd), blk(_HK, fwd),
                blk(_HK, bwd), blk(_HK, bwd), blk(_HV, bwd), blk(_HK, bwd_la)]
    args = [view(q), view(k), view(v), view(la)] * 2
    aliases = {}
    if has_init:
        in_specs += [state_spec, state_spec] + [pl.BlockSpec(memory_space=pl.ANY)] * 2
        args += [s0f, s0b, view(prev_f), view(prev_b)]
        aliases = {len(args) - 1: 0, len(args): 1}
    out_specs = [blk(_HV, fwd), blk(_HV, bwd)]
    out_shape = [jax.ShapeDtypeStruct((nview, seqlen, _HV), F32)] * 2
    if not has_init:
        out_specs += [state_spec, state_spec]
        out_shape += [jax.ShapeDtypeStruct((nseq, H_C, DV_C, DK_C), F32)] * 2
    grid_spec = pltpu.PrefetchScalarGridSpec(
        num_scalar_prefetch=1,
        grid=(nseq // GLA_NS, nchunk),
        in_specs=in_specs,
        out_specs=out_specs,
        scratch_shapes=[pltpu.VMEM((2, GLA_NS, H_C, DV_C, DK_C), F32)],
    )
    outs = pl.pallas_call(
        functools.partial(_gla_kernel, has_init, nchunk),
        grid_spec=grid_spec,
        out_shape=out_shape,
        input_output_aliases=aliases,
        compiler_params=_params(("arbitrary", "arbitrary")),
        name="gla_sample" if has_init else "gla_prompt",
    )(strong, *args)
    return [outs[0].reshape(T, _HV), outs[1].reshape(T, _HV)] + list(outs[2:])


def _rope_tables():
    half = ROPE // 2
    inv_freq = np.power(np.float32(ROPE_THETA), -np.arange(0, half, 2, dtype=np.float32) / np.float32(half))
    n = np.arange(SAMPLE_LEN)
    row = (n // GRID_W).astype(np.float32)
    col = (n % GRID_W).astype(np.float32)
    ang_r = (row[:, None] * inv_freq[None, :]).astype(np.float32)
    ang_c = (col[:, None] * inv_freq[None, :]).astype(np.float32)
    nf = half // 2
    c = np.ones((TM + SAMPLE_LEN, LANES), np.float32)
    s1 = np.zeros((TM + SAMPLE_LEN, LANES), np.float32)
    s2 = np.zeros((TM + SAMPLE_LEN, LANES), np.float32)
    for base, ang in ((NOPE, ang_r), (NOPE + half, ang_c)):
        c[TM:, base:base + nf] = np.cos(ang)
        c[TM:, base + nf:base + half] = np.cos(ang)
        s1[TM:, base:base + nf] = -np.sin(ang)
        s2[TM:, base + nf:base + half] = np.sin(ang)
    return jnp.asarray(c), jnp.asarray(s1), jnp.asarray(s2)


def _pad_heads(w, nheads, width, lo=0):
    k = w.shape[0]
    w = w.reshape(k, nheads, width)
    w = jnp.pad(w, ((0, 0), (0, 0), (lo, HP - lo - width)))
    return w.reshape(k, nheads * HP)


def _row128(v, lo=0):
    return jnp.pad(v, (lo, LANES - lo - v.shape[0])).reshape(1, LANES)


def _even_layer(xs, xp, mod, nmix, w_in, q_a_norm, w_uq, q_norm, kv_a_norm, w_ukv, k_norm, v_norm, w_s, b_s, w_out,
                cache_ckv, cache_kpe):
    s = np.cumsum([Q_LORA, KV_LORA, ROPE, W_B])
    w_q, w_ckv, w_kpe, w_u, w_v = (w_in[:, :s[0]], w_in[:, s[0]:s[1]], w_in[:, s[1]:s[2]], w_in[:, s[2]:s[3]],
                                   w_in[:, s[3]:])
    w_kpe = jnp.pad(w_kpe, ((0, 0), (NOPE, LANES - QK_DIM)))
    win = jnp.concatenate([w_q, w_ckv, w_u, w_v, w_kpe], axis=1).astype(BF16)
    wuq = _pad_heads(w_uq, H_A, QK_DIM).astype(BF16)
    ukv = w_ukv.reshape(KV_LORA, H_A, NOPE + V_A)
    wuk = _pad_heads(ukv[:, :, :NOPE].reshape(KV_LORA, H_A * NOPE), H_A, NOPE)
    wuv = _pad_heads(ukv[:, :, NOPE:].reshape(KV_LORA, H_A * V_A), H_A, V_A)
    wukv = jnp.concatenate([wuk, wuv], axis=1).astype(BF16)
    qgain = _row128(q_norm * (QK_DIM ** -0.5 * LOG2_E))
    kgain = _row128(k_norm)
    bias = b_s.reshape(G_B // 2, 2, CHUNK_B)
    bias = jnp.concatenate([jnp.broadcast_to(bias[:, 0, :, None], (G_B // 2, CHUNK_B, C_B)),
                            jnp.broadcast_to(bias[:, 1, :, None], (G_B // 2, CHUNK_B, C_B))], axis=-1)
    rc, rs1, rs2 = _rope_tables()
    rope_spec = pl.BlockSpec((TM, LANES), lambda i: (_rope_blk(i), 0))
    width = H_A * HP
    q, k, v, ob, ckv, kpe = pl.pallas_call(
        _even_in_kernel,
        grid=(NBLK,),
        in_specs=[
            _X_SAMPLE, _X_PROMPT, _MOD, _full((1, D)), _full((D, _WIN_N)), _full((1, Q_LORA)),
            _full((Q_LORA, width)),
            _full((1, LANES)), _full((1, KV_LORA)), _full((KV_LORA, 2 * width)), _full((1, LANES)),
            _full((1, W_B)), _full((G_B // 2, 2 * CHUNK_B, CHUNK_B)), _full((G_B // 2, CHUNK_B, LANES)),
            rope_spec, rope_spec, rope_spec,
        ],
        out_specs=[_TOK(width), _TOK(width), _TOK(width), _TOK(W_B), _TOK(KV_LORA), _TOK(ROPE)],
        out_shape=[
            jax.ShapeDtypeStruct((T, width), BF16), jax.ShapeDtypeStruct((T, width), BF16),
            jax.ShapeDtypeStruct((T, width), BF16), jax.ShapeDtypeStruct((T, W_B), BF16),
            jax.ShapeDtypeStruct((T, KV_LORA), F32), jax.ShapeDtypeStruct((T, ROPE), F32),
        ],
        compiler_params=_params(("arbitrary",)),
        name="even_in",
    )(xs, xp, mod, nmix.reshape(1, D), win, q_a_norm.reshape(1, Q_LORA), wuq, qgain, kv_a_norm.reshape(1, KV_LORA),
      wukv, kgain, v_norm.reshape(1, W_B), w_s.astype(BF16).reshape(G_B // 2, 2 * CHUNK_B, CHUNK_B), bias, rc, rs1,
      rs2)

    n_ctx = N_SAMPLE_SEQ * PAST_LEN
    kpe_ctx = jnp.pad(cache_kpe.reshape(n_ctx, ROPE), ((0, 0), (NOPE, LANES - QK_DIM)))
    k_ctx, v_ctx = pl.pallas_call(
        _ctx_kv_kernel,
        grid=(n_ctx // TM,),
        in_specs=[_TOK(KV_LORA), _TOK(LANES), _full((KV_LORA, 2 * width)), _full((1, LANES))],
        out_specs=[_TOK(width), _TOK(width)],
        out_shape=[jax.ShapeDtypeStruct((n_ctx, width), BF16)] * 2,
        compiler_params=_params(("arbitrary",)),
        name="ctx_kv",
    )(cache_ckv.reshape(n_ctx, KV_LORA), kpe_ctx, wukv, kgain)

    oa = _attention(q, k, v, k_ctx, v_ctx)
    woa = jnp.pad(w_out[:H_A * V_A].reshape(H_A, V_A, D), ((0, 0), (0, HP - V_A), (0, 0))).reshape(width, D)
    return oa, ob, woa.astype(BF16), w_out[H_A * V_A:].astype(BF16), ckv, kpe


def kernel(x_prompt, x_sample, cache_mla_ckv, cache_mla_kpe, state_gla_fwd, state_gla_bwd, c, c_ctx, ada_w, ada_b,
           norm_mix, norm_ffn, even_w_in, mla_q_a_norm, mla_w_uq, mla_q_norm, mla_kv_a_norm, mla_w_ukv, mla_k_norm,
           cmlp_v_norm, cmlp_w_s, cmlp_b_s, even_w_out, odd_w_in, gla_w_gk_fwd, gla_b_gk_fwd, gla_w_gk_bwd,
           gla_b_gk_bwd, gla_o_norm, odd_w_out, moe_w_router, moe_b_router, moe_w_gu, moe_b_gu, moe_w_down,
           moe_b_down):
    xs0, xp0 = x_sample.reshape(N_SAMPLE, D), x_prompt.reshape(N_PROMPT, D)
    cond8 = jnp.concatenate([c_ctx[None, :], c, jnp.zeros((SUBLANES - 1 - N_SAMPLE_SEQ, D), F32)], axis=0)
    mods = _adaln(cond8, ada_w, ada_b)
    wr = jnp.pad(moe_w_router, ((0, 0), (0, 0), (0, LANES - N_EXPERTS)))
    wr_hi = wr.astype(BF16)
    wr = jnp.concatenate([wr_hi, (wr - wr_hi.astype(F32)).astype(BF16)], axis=-1)
    br = jnp.pad(moe_b_router, ((0, 0), (0, LANES - N_EXPERTS))).reshape(2, 1, LANES)

    oa, ob, woa, wob, ckv, kpe = _even_layer(
        xs0, xp0, mods[0], norm_mix[0], even_w_in[0], mla_q_a_norm[0], mla_w_uq[0], mla_q_norm[0], mla_kv_a_norm[0],
        mla_w_ukv[0], mla_k_norm[0], cmlp_v_norm[0], cmlp_w_s[0], cmlp_b_s[0], even_w_out[0],
        cache_mla_ckv[:, 0], cache_mla_kpe[:, 0])
    width = H_A * HP
    x1, h2, meta, cnt = pl.pallas_call(
        _even_out_kernel,
        grid=(NBLK,),
        in_specs=[_TOK(width), _TOK(W_B), _X_SAMPLE, _X_PROMPT, _MOD, _full((width, D)), _full((W_B, D)),
                  _full((1, D)),
                  _full((D, 2 * LANES)), _full((1, LANES))],
        out_specs=_PROLOGUE_OUT_SPECS,
        out_shape=_PROLOGUE_OUT_SHAPE,
        compiler_params=_params(("arbitrary",)),
        name="even_out",
    )(oa, ob, xs0, xp0, mods[0], woa, wob, norm_ffn[0].reshape(1, D), wr[0], br[0])
    x2 = _moe(0, x1, h2, meta, cnt, mods[0], moe_w_gu, moe_b_gu, moe_w_down, moe_b_down)

    w_in = odd_w_in[0]
    win = jnp.concatenate([w_in, jnp.zeros((D, LANES - 2 * GATE_RANK), F32)], axis=1).astype(BF16)
    wgk = jnp.zeros((LANES, 2 * _HK), F32)
    wgk = wgk.at[:GATE_RANK, :_HK].set(gla_w_gk_fwd[0]).at[GATE_RANK:2 * GATE_RANK, _HK:].set(gla_w_gk_bwd[0])
    bgk = jnp.concatenate([gla_b_gk_fwd[0], gla_b_gk_bwd[0]]).reshape(1, 2 * _HK)
    q, k, v, g, la, lamin = pl.pallas_call(
        _odd_in_kernel,
        grid=(NBLK,),
        in_specs=[_TOK(D), _MOD, _full((1, D)), _full((D, _ODD_MAIN + LANES)), _full((LANES, 2 * _HK)),
                  _full((1, 2 * _HK))],
        out_specs=[_TOK(_HK), _TOK(_HK), _TOK(_HV), _TOK(_HV), _TOK(2 * _HK), _TILE_ROW],
        out_shape=[jax.ShapeDtypeStruct((T, _HK), F32), jax.ShapeDtypeStruct((T, _HK), F32),
                   jax.ShapeDtypeStruct((T, _HV), F32), jax.ShapeDtypeStruct((T, _HV), F32),
                   jax.ShapeDtypeStruct((T, 2 * _HK), F32), jax.ShapeDtypeStruct((NBLK, SUBLANES, LANES), F32)],
        compiler_params=_params(("arbitrary",)),
        name="odd_in",
    )(x2, mods[1], norm_mix[1].reshape(1, D), win, wgk.astype(BF16), bgk)

    strong = (jnp.min(lamin) < -GLA_SAFE_LOG_DECAY).astype(jnp.int32).reshape(1)
    of, obk, st_f, st_b = _gla_call(False, N_PROMPT_SEQ, PROMPT_LEN, N_SAMPLE, strong, q, k, v, la)
    s0f = state_gla_fwd[:, 0].transpose(0, 1, 3, 2)
    s0b = state_gla_bwd[:, 0].transpose(0, 1, 3, 2)
    of, obk = _gla_call(True, N_SAMPLE_SEQ, SAMPLE_LEN, 0, strong, q, k, v, la, of, obk, s0f, s0b)

    x3, h2, meta, cnt = pl.pallas_call(
        _odd_out_kernel,
        grid=(NBLK,),
        in_specs=[_TOK(_HV), _TOK(_HV), _TOK(_HV), _TOK(D), _MOD, _full((1, DV_C)), _full((_HV, D)),
                  _full((1, D)), _full((D, 2 * LANES)), _full((1, LANES))],
        out_specs=_PROLOGUE_OUT_SPECS,
        out_shape=_PROLOGUE_OUT_SHAPE,
        compiler_params=_params(("arbitrary",)),
        name="odd_out",
    )(of, obk, g, x2, mods[1], gla_o_norm[0].reshape(1, DV_C), odd_w_out[0].astype(BF16),
      norm_ffn[1].reshape(1, D), wr[1], br[1])
    ys, yp = _moe(1, x3, h2, meta, cnt, mods[1], moe_w_gu, moe_b_gu, moe_w_down, moe_b_down, split_out=True)

    y_sample = ys.reshape(N_SAMPLE_SEQ, SAMPLE_LEN, D)
    y_prompt = yp.reshape(N_PROMPT_SEQ, PROMPT_LEN, D)
    new_ckv = ckv[N_SAMPLE:].reshape(N_PROMPT_SEQ, 1, PROMPT_LEN, KV_LORA)
    new_kpe = kpe[N_SAMPLE:].reshape(N_PROMPT_SEQ, 1, PROMPT_LEN, ROPE)
    new_fwd = st_f.transpose(0, 1, 3, 2)[:, None]
    new_bwd = st_b.transpose(0, 1, 3, 2)[:, None]
    return (y_prompt, y_sample, new_ckv, new_kpe, new_fwd, new_bwd)
```

```python
import functools

import numpy as np
import jax
import jax.numpy as jnp
from jax import lax
from jax.experimental import pallas as pl
from jax.experimental.pallas import tpu as pltpu

F32 = jnp.float32
BF16 = jnp.bfloat16

D = 1024
N_PROMPT_SEQ, PROMPT_LEN = 16, 256
N_SAMPLE_SEQ, SAMPLE_LEN = 4, 2048
PAST_LEN = 512
N_PROMPT = N_PROMPT_SEQ * PROMPT_LEN
N_SAMPLE = N_SAMPLE_SEQ * SAMPLE_LEN
T = N_PROMPT + N_SAMPLE
EPS = 1e-6
GRID_W = 64
H_A, Q_LORA, KV_LORA, NOPE, ROPE, V_A = 8, 512, 256, 64, 32, 64
QK_DIM = NOPE + ROPE
G_B, C_B, W_B, CHUNK_B = 8, 64, 512, 128
H_C, DK_C, DV_C, GATE_RANK, GATE_TAU, GLA_CHUNK = 4, 128, 256, 16, 16.0, 64
N_EXPERTS, TOP_K, D_FF = 32, 4, 1024
SWIGLU_LIMIT, SWIGLU_ALPHA = 7.0, 1.702
ROPE_THETA = 10000.0
LOG2_E = 1.4426950408889634

LANES = 128
SUBLANES = 8
VMEM_LIMIT = 56 * 1024 * 1024

TM = 512
NBLK = T // TM
SAMPLE_BLKS = N_SAMPLE // TM
BLKS_PER_SAMPLE_SEQ = SAMPLE_LEN // TM
ATT_TQ = 512
ATT_VMEM_LIMIT = 60 * 1024 * 1024
GLA_NS = 4
GLA_SAFE_LOG_DECAY = 60.0
ETILE = 512
TMD = TM
NBD = T // TMD
NBD_SAMPLE = N_SAMPLE // TMD
CH = 2 * SUBLANES
_MAX_LOCAL = TMD * TOP_K + N_EXPERTS * (CH - 1)
_POS_BLOCK = 256
ROWS_L = -(-_MAX_LOCAL // _POS_BLOCK) * _POS_BLOCK
BIG_ROWS = 4 * CH
MAX_BIG = ROWS_L // BIG_ROWS
MAX_SMALL = N_EXPERTS * (BIG_ROWS // CH - 1)
N_ETILES = -(-NBD * _MAX_LOCAL // ETILE) + N_EXPERTS
P_ROWS = N_ETILES * ETILE
HP = LANES


def _cond_row(i):
    return jnp.where(i < SAMPLE_BLKS, 1 + i // BLKS_PER_SAMPLE_SEQ, 0)


def _rope_blk(i):
    return jnp.where(i < SAMPLE_BLKS, 1 + i % BLKS_PER_SAMPLE_SEQ, 0)


def _rms(x):
    return x * lax.rsqrt(jnp.mean(x * x, axis=-1, keepdims=True) + EPS)


def _gelu(x):
    return 0.5 * x * (1.0 + jnp.tanh(0.7978845608028654 * (x + 0.044715 * (x * x * x))))


def _silu(x):
    return x * jax.nn.sigmoid(x)


def _dot(a, b):
    return jnp.dot(a, b, preferred_element_type=F32)


def _dot_nt(a, b):
    return lax.dot_general(a, b, (((1,), (1,)), ((), ())), preferred_element_type=F32)


def _dot_tn(a, b):
    return lax.dot_general(a, b, (((0,), (0,)), ((), ())), preferred_element_type=F32)


def _params(sem, vmem=VMEM_LIMIT):
    return pltpu.CompilerParams(dimension_semantics=sem, vmem_limit_bytes=vmem)


def _full(shape):
    nd = len(shape)
    return pl.BlockSpec(shape, lambda *_: (0,) * nd)


ADA_TN = 1536


def _adaln_kernel(c_ref, w_ref, b_ref, o_ref):
    s = _silu(c_ref[...]).astype(BF16)
    o_ref[0] = _dot(s, w_ref[0].astype(BF16)) + b_ref[0]


def _adaln(cond8, ada_w, ada_b):
    depth = ada_w.shape[0]
    n = ada_w.shape[2]
    out = pl.pallas_call(
        _adaln_kernel,
        grid=(depth, n // ADA_TN),
        in_specs=[
            pl.BlockSpec((SUBLANES, D), lambda l, j: (0, 0)),
            pl.BlockSpec((1, D, ADA_TN), lambda l, j: (l, 0, j)),
            pl.BlockSpec((1, 1, ADA_TN), lambda l, j: (l, 0, j)),
        ],
        out_specs=pl.BlockSpec((1, SUBLANES, ADA_TN), lambda l, j: (l, 0, j)),
        out_shape=jax.ShapeDtypeStruct((depth, SUBLANES, n), F32),
        compiler_params=_params(("arbitrary", "arbitrary")),
        name="adaln",
    )(cond8, ada_w, ada_b.reshape(depth, 1, n))
    return out.reshape(depth, SUBLANES, 6, D)


_QC0, _CKV0, _U0, _V0, _KPE0, _WIN_N = 0, 512, 768, 1280, 1792, 1920


def _rope(y, c, s1, s2):
    return y * c + pltpu.roll(y, LANES - 8, 1) * s1 + pltpu.roll(y, 8, 1) * s2


def _k_heads(k_raw, kpe128, kp_rot, kgain, k_ref):
    sskpe = jnp.sum(kpe128 * kpe128, axis=-1, keepdims=True)
    for h in range(H_A):
        kb = k_raw[:, h * HP:(h + 1) * HP]
        r = lax.rsqrt((jnp.sum(kb * kb, axis=-1, keepdims=True) + sskpe) * (1.0 / QK_DIM) + EPS)
        k_ref[:, h * HP:(h + 1) * HP] = ((kb * kgain + kp_rot) * r).astype(BF16)


def _input_rows(xs_ref, xp_ref):
    return jnp.where(pl.program_id(0) < SAMPLE_BLKS, xs_ref[...], xp_ref[...])


_X_SAMPLE = pl.BlockSpec((TM, D), lambda i, *_: (jnp.minimum(i, SAMPLE_BLKS - 1), 0))
_X_PROMPT = pl.BlockSpec((TM, D), lambda i, *_: (jnp.maximum(i - SAMPLE_BLKS, 0), 0))


def _even_in_kernel(xs_ref, xp_ref, mod_ref, nmix_ref, win_ref, qan_ref, wuq_ref, qgain_ref, kvan_ref, wukv_ref,
                    kgain_ref, vnorm_ref, ws_ref, bs_ref, rc_ref, rs1_ref, rs2_ref,
                    q_ref, k_ref, v_ref, ob_ref, ckv_ref, kpe_ref):
    m = mod_ref[0]
    h = _rms(_input_rows(xs_ref, xp_ref)) * nmix_ref[...] * (1.0 + m[1:2]) + m[0:1]
    a = _dot(h.astype(BF16), win_ref[...])
    qc = a[:, _QC0:_CKV0]
    ckv = a[:, _CKV0:_U0]
    u = a[:, _U0:_V0]
    vv = a[:, _V0:_KPE0]
    kpe128 = a[:, _KPE0:_WIN_N]

    ckv_n = _rms(ckv) * kvan_ref[...]
    ckv_ref[...] = ckv_n
    kpe_ref[...] = kpe128[:, NOPE:QK_DIM]

    rc, rs1, rs2 = rc_ref[...], rs1_ref[...], rs2_ref[...]
    qn = (_rms(qc) * qan_ref[...]).astype(BF16)
    qr = _dot(qn, wuq_ref[...])
    qgain = qgain_ref[...]
    for hh in range(H_A):
        blk = qr[:, hh * HP:(hh + 1) * HP]
        r = lax.rsqrt(jnp.sum(blk * blk, axis=-1, keepdims=True) * (1.0 / QK_DIM) + EPS)
        q_ref[:, hh * HP:(hh + 1) * HP] = _rope(blk * r * qgain, rc, rs1, rs2).astype(BF16)

    kv = _dot(ckv_n.astype(BF16), wukv_ref[...])
    v_ref[...] = kv[:, H_A * HP:].astype(BF16)
    kgain = kgain_ref[...]
    kp_rot = _rope(kpe128 * kgain, rc, rs1, rs2)
    _k_heads(kv[:, :H_A * HP], kpe128, kp_rot, kgain, k_ref)

    ug = _gelu(u)
    vn = (_rms(_gelu(vv)) * vnorm_ref[...]).astype(BF16)
    low = lax.broadcasted_iota(jnp.int32, (CHUNK_B, LANES), 1) < C_B
    for c in range(TM // CHUNK_B):
        rows = slice(c * CHUNK_B, (c + 1) * CHUNK_B)
        for p in range(G_B // 2):
            cols = slice(p * LANES, (p + 1) * LANES)
            blk = vn[rows, cols]
            both = _dot(ws_ref[p], blk)
            mixed = jnp.where(low, both[:CHUNK_B], both[CHUNK_B:]) + bs_ref[p]
            ob_ref[rows, cols] = (ug[rows, cols] * mixed).astype(BF16)


def _ctx_kv_kernel(ckv_ref, kpe_ref, wukv_ref, kgain_ref, k_ref, v_ref):
    kv = _dot(ckv_ref[...].astype(BF16), wukv_ref[...])
    v_ref[...] = kv[:, H_A * HP:].astype(BF16)
    kgain = kgain_ref[...]
    kpe128 = kpe_ref[...]
    _k_heads(kv[:, :H_A * HP], kpe128, kpe128 * kgain, kgain, k_ref)


def _attn_self_kernel(q_ref, k_ref, v_ref, o_ref):
    for h in range(H_A):
        cols = slice(h * HP, (h + 1) * HP)
        s = _dot_nt(q_ref[:, cols], k_ref[:, cols])
        p = jnp.exp2(s - jnp.max(s, axis=-1, keepdims=True))
        inv = 1.0 / jnp.sum(p, axis=-1, keepdims=True)
        o_ref[:, cols] = (_dot(p.astype(BF16), v_ref[:, cols]) * inv).astype(BF16)


def _attn_ctx_kernel(q_ref, k_ref, v_ref, kc_ref, vc_ref, prev_ref, o_ref):
    del prev_ref
    for h in range(H_A):
        cols = slice(h * HP, (h + 1) * HP)
        q = q_ref[:, cols]
        s1 = _dot_nt(q, k_ref[:, cols])
        s2 = _dot_nt(q, kc_ref[:, cols])
        mx = jnp.maximum(jnp.max(s1, axis=-1, keepdims=True), jnp.max(s2, axis=-1, keepdims=True))
        p1 = jnp.exp2(s1 - mx)
        p2 = jnp.exp2(s2 - mx)
        inv = 1.0 / (jnp.sum(p1, axis=-1, keepdims=True) + jnp.sum(p2, axis=-1, keepdims=True))
        o = _dot(p1.astype(BF16), v_ref[:, cols]) + _dot(p2.astype(BF16), vc_ref[:, cols])
        o_ref[:, cols] = (o * inv).astype(BF16)


def _attention(q, k, v, k_ctx, v_ctx):
    width = H_A * HP
    first = N_SAMPLE // PROMPT_LEN
    o = pl.pallas_call(
        _attn_self_kernel,
        grid=(N_PROMPT_SEQ,),
        in_specs=[pl.BlockSpec((PROMPT_LEN, width), lambda i: (first + i, 0))] * 3,
        out_specs=pl.BlockSpec((PROMPT_LEN, width), lambda i: (first + i, 0)),
        out_shape=jax.ShapeDtypeStruct((T, width), BF16),
        compiler_params=_params(("arbitrary",)),
        name="attn_prompt",
    )(q, k, v)
    qblocks = SAMPLE_LEN // ATT_TQ
    qblk = lambda b, j: (b * qblocks + j, 0)
    return pl.pallas_call(
        _attn_ctx_kernel,
        grid=(N_SAMPLE_SEQ, qblocks),
        in_specs=[
            pl.BlockSpec((ATT_TQ, width), qblk),
            pl.BlockSpec((SAMPLE_LEN, width), lambda b, j: (b, 0)),
            pl.BlockSpec((SAMPLE_LEN, width), lambda b, j: (b, 0)),
            pl.BlockSpec((PAST_LEN, width), lambda b, j: (b, 0)),
            pl.BlockSpec((PAST_LEN, width), lambda b, j: (b, 0)),
            pl.BlockSpec(memory_space=pl.ANY),
        ],
        out_specs=pl.BlockSpec((ATT_TQ, width), qblk),
        out_shape=jax.ShapeDtypeStruct((T, width), BF16),
        input_output_aliases={5: 0},
        compiler_params=_params(("arbitrary", "arbitrary"), ATT_VMEM_LIMIT),
        name="attn_sample",
    )(q, k, v, k_ctx, v_ctx, o)


_META_IDX, _META_RANK, _META_W = 0, TOP_K, 2 * TOP_K


def _moe_prologue(x1, m, nffn_ref, wr_ref, br_ref, x1_ref, h2_ref, meta_ref, cnt_ref):
    x1_ref[...] = x1
    h2 = _rms(x1) * nffn_ref[...] * (1.0 + m[4:5]) + m[3:4]
    h2_ref[...] = h2.astype(BF16)
    lane = lax.broadcasted_iota(jnp.int32, (TM, LANES), 1)
    lanef = lane.astype(F32)
    h_hi = h2.astype(BF16)
    h_lo = (h2 - h_hi.astype(F32)).astype(BF16)
    r = _dot(h_hi, wr_ref[...])
    logits = r[:, :LANES] + r[:, LANES:] + _dot(h_lo, wr_ref[:, :LANES]) + br_ref[...]
    work = jnp.where(lane < N_EXPERTS, logits, -jnp.inf)
    hots, vals = [], []
    for _ in range(TOP_K):
        mx = jnp.max(work, axis=-1, keepdims=True)
        idx = jnp.min(jnp.where(work == mx, lanef, float(LANES)), axis=-1, keepdims=True)
        hot = lanef == idx
        work = jnp.where(hot, -jnp.inf, work)
        hots.append((hot, idx))
        vals.append(mx)
    es = [jnp.exp(v - vals[0]) for v in vals]
    inv = 1.0 / (es[0] + es[1] + es[2] + es[3])
    sel = jnp.zeros((TM, LANES), F32)
    for hot, _ in hots:
        sel = jnp.where(hot, 1.0, sel)
    row = lax.broadcasted_iota(jnp.int32, (TM, TM), 0)
    col = lax.broadcasted_iota(jnp.int32, (TM, TM), 1)
    strict = jnp.where(row > col, 1.0, 0.0).astype(BF16)
    before = _dot(strict, sel.astype(BF16))
    meta = jnp.zeros((TM, LANES), F32)
    for kk, (hot, idx) in enumerate(hots):
        rank = jnp.sum(jnp.where(hot, before, 0.0), axis=-1, keepdims=True)
        meta = jnp.where(lane == _META_IDX + kk, idx, meta)
        meta = jnp.where(lane == _META_RANK + kk, rank, meta)
        meta = jnp.where(lane == _META_W + kk, es[kk] * inv, meta)
    meta_ref[...] = meta
    cnt_ref[0] = jnp.broadcast_to(jnp.sum(sel, axis=0, keepdims=True), (SUBLANES, LANES))


def _even_out_kernel(oa_ref, ob_ref, xs_ref, xp_ref, mod_ref, woa_ref, wob_ref, nffn_ref, wr_ref, br_ref,
                     x1_ref, h2_ref, meta_ref, cnt_ref):
    m = mod_ref[0]
    out = _dot(oa_ref[...], woa_ref[...]) + _dot(ob_ref[...], wob_ref[...])
    x1 = _input_rows(xs_ref, xp_ref) + m[2:3] * out
    _moe_prologue(x1, m, nffn_ref, wr_ref, br_ref, x1_ref, h2_ref, meta_ref, cnt_ref)


def _odd_out_kernel(of_ref, ob_ref, g_ref, x_ref, mod_ref, onorm_ref, wo_ref, nffn_ref, wr_ref, br_ref,
                    x1_ref, h2_ref, meta_ref, cnt_ref):
    m = mod_ref[0]
    onorm = onorm_ref[...]
    parts = []
    for h in range(H_C):
        cols = slice(h * DV_C, (h + 1) * DV_C)
        o = of_ref[:, cols] + ob_ref[:, cols]
        parts.append((_rms(o) * onorm * _silu(g_ref[:, cols])).astype(BF16))
    out = _dot(jnp.concatenate(parts, axis=-1), wo_ref[...])
    x1 = x_ref[...] + m[2:3] * out
    _moe_prologue(x1, m, nffn_ref, wr_ref, br_ref, x1_ref, h2_ref, meta_ref, cnt_ref)


_TOK = lambda w: pl.BlockSpec((TM, w), lambda i, *_: (i, 0))
_MOD = pl.BlockSpec((1, 6, D), lambda i, *_: (_cond_row(i), 0, 0))
_TILE_ROW = pl.BlockSpec((1, SUBLANES, LANES), lambda i, *_: (i, 0, 0))

_PROLOGUE_OUT_SPECS = [_TOK(D), _TOK(D), _TOK(LANES), _TILE_ROW]
_PROLOGUE_OUT_SHAPE = [
    jax.ShapeDtypeStruct((T, D), F32),
    jax.ShapeDtypeStruct((T, D), BF16),
    jax.ShapeDtypeStruct((T, LANES), F32),
    jax.ShapeDtypeStruct((NBLK, SUBLANES, LANES), F32),
]


def _local_positions(meta, seg_rows):
    lanef = lax.broadcasted_iota(jnp.int32, (TM, LANES), 1).astype(F32)
    pos = []
    for k in range(TOP_K):
        halves = []
        for half in range(TMD // TM):
            m = meta[half * TM:(half + 1) * TM]
            hot = lanef == m[:, _META_IDX + k:_META_IDX + k + 1]
            start = jnp.sum(jnp.where(hot, seg_rows[half:half + 1], 0.0), axis=-1, keepdims=True)
            halves.append(start + m[:, _META_RANK + k:_META_RANK + k + 1])
        pos.append(jnp.concatenate(halves, axis=0))
    return pos


def _scatter_matrix(pos, vals):
    lane = lax.broadcasted_iota(jnp.int32, (TMD, _POS_BLOCK), 1).astype(F32).astype(BF16)
    blocks = []
    for j in range(ROWS_L // _POS_BLOCK):
        acc = jnp.zeros((TMD, _POS_BLOCK), BF16)
        for p, v in zip(pos, vals):
            acc = jnp.where(lane == (p - float(j * _POS_BLOCK)).astype(BF16), v, acc)
        blocks.append(acc)
    return jnp.concatenate(blocks, axis=1)


def _chunk_copies(t, lists, make):
    for rows, (src_ref, dst_ref, n_ref, stride) in zip((BIG_ROWS, CH), lists):
        def one(j, c, rows=rows, src_ref=src_ref, dst_ref=dst_ref, stride=stride):
            q = t * stride + j
            make(pl.multiple_of(src_ref[q], CH), pl.multiple_of(dst_ref[q], CH), rows).start()
            return c

        lax.fori_loop(0, n_ref[t], one, 0)


def _drain_tile(t, lists, wait_copy):
    for rows, (_, _, n_ref, _) in zip((BIG_ROWS, CH), lists):
        _drain(n_ref[t], wait_copy(rows))


def _drain(count, chunk_copy):
    def one(j, c):
        chunk_copy.wait()
        return c

    lax.fori_loop(0, count, one, 0)


def _piece_sem(rows, slot):
    return slot if rows == BIG_ROWS else 2 + slot


def _dispatch_kernel(bsrc, bdst, nbig, ssrc, sdst, nsml, tail0_ref, tailn_ref, h_ref, meta_ref, segrow_ref,
                     xs_hbm, buf, zbuf, sem):
    i = pl.program_id(0)
    slot = i % 2
    lists = ((bsrc, bdst, nbig, MAX_BIG), (ssrc, sdst, nsml, MAX_SMALL))
    wait_copy = lambda s: lambda rows: pltpu.make_async_copy(
        buf.at[s, pl.ds(0, rows)], xs_hbm.at[pl.ds(0, rows)], sem.at[_piece_sem(rows, s)])

    @pl.when(i == 0)
    def _():
        zbuf[...] = jnp.zeros_like(zbuf)

        def per_expert(e, carry):
            def one(j, c):
                pltpu.make_async_copy(zbuf, xs_hbm.at[pl.ds(pl.multiple_of(tail0_ref[e] + j * CH, CH), CH)],
                                      sem.at[4]).start()
                return c

            lax.fori_loop(0, tailn_ref[e], one, 0)
            return carry + tailn_ref[e]

        total = lax.fori_loop(0, N_EXPERTS, per_expert, 0)
        _drain(total, pltpu.make_async_copy(zbuf, xs_hbm.at[pl.ds(0, CH)], sem.at[4]))

    pos = _local_positions(meta_ref[...], segrow_ref[0])
    riota = lax.broadcasted_iota(jnp.int32, (TMD, ROWS_L), 1).astype(F32)
    pt = jnp.zeros((TMD, ROWS_L), F32)
    for p in pos:
        pt = jnp.where(riota == p, 1.0, pt)
    buf[slot] = _dot_tn(pt.astype(BF16), h_ref[...]).astype(BF16)

    make = lambda s, d, rows: pltpu.make_async_copy(buf.at[slot, pl.ds(s, rows)], xs_hbm.at[pl.ds(d, rows)],
                                                    sem.at[_piece_sem(rows, slot)])
    _chunk_copies(i, lists, make)

    @pl.when(i > 0)
    def _():
        _drain_tile(i - 1, lists, wait_copy(1 - slot))

    @pl.when(i == NBD - 1)
    def _():
        _drain_tile(i, lists, wait_copy(slot))


_TOKD = lambda w: pl.BlockSpec((TMD, w), lambda i, *_: (i, 0))
_SEGROWS = pl.BlockSpec((1, TMD // TM, LANES), lambda i, *_: (i, 0, 0))


def _dispatch(plan, h2, meta):
    grid_spec = pltpu.PrefetchScalarGridSpec(
        num_scalar_prefetch=8,
        grid=(NBD,),
        in_specs=[_TOKD(D), _TOKD(LANES), _SEGROWS],
        out_specs=pl.BlockSpec(memory_space=pl.ANY),
        scratch_shapes=[pltpu.VMEM((2, ROWS_L, D), BF16), pltpu.VMEM((CH, D), BF16),
                        pltpu.SemaphoreType.DMA((5,))],
    )
    return pl.pallas_call(
        _dispatch_kernel,
        grid_spec=grid_spec,
        out_shape=jax.ShapeDtypeStruct((P_ROWS, D), BF16),
        compiler_params=_params(("arbitrary",)),
        name="moe_dispatch",
    )(*plan["pieces"], plan["tail0"], plan["tailn"], h2, meta, plan["segrow"])


def _expert_kernel(layer, te_ref, tfirst_ref, tvalid_ref, xblk_ref, tnext_ref, wslot_ref, x_ref, wgu_hbm, bgu_ref,
                   wd_hbm, bd_ref, y_ref, wgu_f32, wd_f32, wgu_bf, wd_bf, sem):
    del xblk_ref
    i = pl.program_id(0)

    def weight_copies(e, slot):
        return (pltpu.make_async_copy(wgu_hbm.at[layer, e], wgu_f32.at[slot], sem.at[0, slot]),
                pltpu.make_async_copy(wd_hbm.at[layer, e], wd_f32.at[slot], sem.at[1, slot]))

    @pl.when(tfirst_ref[i] == 1)
    def _():
        slot = wslot_ref[i]

        @pl.when(i == 0)
        def _():
            for cp in weight_copies(te_ref[i], slot):
                cp.start()

        for cp in weight_copies(te_ref[i], slot):
            cp.wait()
        wgu_bf[...] = wgu_f32[slot].astype(BF16)
        wd_bf[...] = wd_f32[slot].astype(BF16)

        @pl.when(tnext_ref[i] >= 0)
        def _():
            for cp in weight_copies(tnext_ref[i], 1 - slot):
                cp.start()

    def ffn(x):
        a = _dot(x, wgu_bf[...]) + bgu_ref[...]
        glu = jnp.minimum(a[:, :D_FF], SWIGLU_LIMIT)
        lin = jnp.clip(a[:, D_FF:], -SWIGLU_LIMIT, SWIGLU_LIMIT)
        act = (glu * jax.nn.sigmoid(SWIGLU_ALPHA * glu)) * (lin + 1.0)
        return (_dot(act.astype(BF16), wd_bf[...]) + bd_ref[...]).astype(BF16)

    half = ETILE // 2

    @pl.when(tvalid_ref[i] == 2)
    def _():
        y_ref[...] = ffn(x_ref[...])

    @pl.when(tvalid_ref[i] == 1)
    def _():
        y_ref[:half] = ffn(x_ref[:half])
        y_ref[half:] = jnp.zeros((ETILE - half, D), BF16)


def _experts(layer, plan, xs, w_gu, b_gu, w_down, b_down):
    depth = w_gu.shape[0]
    e_of = lambda i, te, *_: (layer, te[i], 0, 0)
    grid_spec = pltpu.PrefetchScalarGridSpec(
        num_scalar_prefetch=6,
        grid=(N_ETILES,),
        in_specs=[
            pl.BlockSpec((ETILE, D), lambda i, te, tf, tv, xb, *_: (xb[i], 0)),
            pl.BlockSpec(memory_space=pl.ANY),
            pl.BlockSpec((None, None, 1, 2 * D_FF), e_of),
            pl.BlockSpec(memory_space=pl.ANY),
            pl.BlockSpec((None, None, 1, D), e_of),
        ],
        out_specs=pl.BlockSpec((ETILE, D), lambda i, te, tf, tv, xb, *_: (xb[i], 0)),
        scratch_shapes=[pltpu.VMEM((2, D, 2 * D_FF), F32), pltpu.VMEM((2, D_FF, D), F32),
                        pltpu.VMEM((D, 2 * D_FF), BF16), pltpu.VMEM((D_FF, D), BF16),
                        pltpu.SemaphoreType.DMA((2, 2))],
    )
    return pl.pallas_call(
        functools.partial(_expert_kernel, layer),
        grid_spec=grid_spec,
        out_shape=jax.ShapeDtypeStruct((P_ROWS, D), BF16),
        compiler_params=_params(("arbitrary",)),
        name="moe_experts",
    )(plan["te"], plan["tfirst"], plan["tvalid"], plan["xblk"], plan["tnext"], plan["wslot"], xs, w_gu,
      b_gu.reshape(depth, N_EXPERTS, 1, 2 * D_FF), w_down, b_down.reshape(depth, N_EXPERTS, 1, D))


def _combine_kernel(split, bsrc, bdst, nbig, ssrc, sdst, nsml, y_hbm, x1_ref, meta_ref, segrow_ref, mod_ref,
                    *rest):
    *o_refs, ybuf, sem = rest
    i = pl.program_id(0)
    slot = i % 2
    lists = ((bsrc, bdst, nbig, MAX_BIG), (ssrc, sdst, nsml, MAX_SMALL))

    def fetch(t, s):
        make = lambda loc, glob, rows: pltpu.make_async_copy(
            y_hbm.at[pl.ds(glob, rows)], ybuf.at[s, pl.ds(loc, rows)], sem.at[_piece_sem(rows, s)])
        _chunk_copies(t, lists, make)

    @pl.when(i == 0)
    def _():
        ybuf[...] = jnp.zeros_like(ybuf)
        fetch(i, slot)

    @pl.when(i + 1 < NBD)
    def _():
        fetch(i + 1, 1 - slot)

    _drain_tile(i, lists, lambda rows: pltpu.make_async_copy(
        y_hbm.at[pl.ds(0, rows)], ybuf.at[slot, pl.ds(0, rows)], sem.at[_piece_sem(rows, slot)]))

    meta = meta_ref[...]
    pos = _local_positions(meta, segrow_ref[0])
    gates = _scatter_matrix(pos, [meta[:, _META_W + k:_META_W + k + 1].astype(BF16) for k in range(TOP_K)])
    acc = _dot(gates, ybuf[slot])
    out = x1_ref[...] + mod_ref[0][5:6] * acc
    if split:
        os_ref, op_ref = o_refs

        @pl.when(i < NBD_SAMPLE)
        def _():
            os_ref[...] = out

        @pl.when(i >= NBD_SAMPLE)
        def _():
            op_ref[...] = out
    else:
        o_refs[0][...] = out


def _combine(plan, y, x1, meta, mod, split):
    if split:
        out_specs = [pl.BlockSpec((TMD, D), lambda i, *_: (jnp.minimum(i, NBD_SAMPLE - 1), 0)),
                     pl.BlockSpec((TMD, D), lambda i, *_: (jnp.maximum(i - NBD_SAMPLE, 0), 0))]
        out_shape = [jax.ShapeDtypeStruct((N_SAMPLE, D), F32), jax.ShapeDtypeStruct((N_PROMPT, D), F32)]
    else:
        out_specs, out_shape = _TOKD(D), jax.ShapeDtypeStruct((T, D), F32)
    grid_spec = pltpu.PrefetchScalarGridSpec(
        num_scalar_prefetch=6,
        grid=(NBD,),
        in_specs=[pl.BlockSpec(memory_space=pl.ANY), _TOKD(D), _TOKD(LANES), _SEGROWS,
                  pl.BlockSpec((1, 6, D), lambda i, *_: (_cond_row(i * (TMD // TM)), 0, 0))],
        out_specs=out_specs,
        scratch_shapes=[pltpu.VMEM((2, ROWS_L, D), BF16), pltpu.SemaphoreType.DMA((4,))],
    )
    return pl.pallas_call(
        functools.partial(_combine_kernel, split),
        grid_spec=grid_spec,
        out_shape=out_shape,
        compiler_params=_params(("arbitrary",)),
        name="moe_combine",
    )(*plan["pieces"], y, x1, meta, plan["segrow"], mod)


def _moe_plan(cnt):
    per = TMD // TM
    cnt = cnt[:, 0, :N_EXPERTS].astype(jnp.int32).reshape(NBD, per, N_EXPERTS)
    cpad = (jnp.sum(cnt, axis=1) + CH - 1) // CH * CH
    tot = jnp.sum(cpad, axis=0)
    tiles = (tot + ETILE - 1) // ETILE
    tile_end = jnp.cumsum(tiles)
    offs = (tile_end - tiles) * ETILE
    base = offs[None, :] + jnp.cumsum(cpad, axis=0) - cpad
    seg = jnp.cumsum(cpad, axis=1) - cpad
    segrows = seg[:, None, :] + jnp.cumsum(cnt, axis=1) - cnt
    tid = jnp.arange(N_ETILES, dtype=jnp.int32)
    te = jnp.sum((tile_end[None, :] <= tid[:, None]).astype(jnp.int32), axis=1)
    used = te < N_EXPERTS
    last = jnp.max(jnp.where(tiles > 0, jnp.arange(N_EXPERTS, dtype=jnp.int32), 0))
    te = jnp.where(used, te, last)
    rows = tot[te] - (tid - (tile_end - tiles)[te]) * ETILE
    tvalid = jnp.where(used, jnp.where(rows > ETILE // 2, 2, 1), 0).astype(jnp.int32)
    tfirst = jnp.concatenate([jnp.ones((1,), jnp.int32), (te[1:] != te[:-1]).astype(jnp.int32)])
    eid = jnp.arange(N_EXPERTS, dtype=jnp.int32)
    later = (eid[None, :] > eid[:, None]) & (tiles[None, :] > 0)
    nxt = jnp.min(jnp.where(later, eid[None, :], N_EXPERTS), axis=1)
    nxt = jnp.where(nxt < N_EXPERTS, nxt, -1)
    def piece_lists(counts, local0, global0, rows, max_pieces):
        cend = jnp.cumsum(counts, axis=1)
        cstart = cend - counts
        cid = jnp.arange(max_pieces, dtype=jnp.int32)
        owns = ((cstart[:, None, :] <= cid[None, :, None]) & (cid[None, :, None] < cend[:, None, :]))
        owns = owns.astype(jnp.int32)
        pick = lambda a: jnp.sum(owns * a[:, None, :], axis=2)
        within = (cid[None, :] - pick(cstart)) * rows
        live = jnp.sum(owns, axis=2)
        return [((pick(local0) + within) * live).reshape(-1), ((pick(global0) + within) * live).reshape(-1),
                jnp.sum(counts, axis=1)]

    nbig = cpad // BIG_ROWS
    nsml = (cpad - nbig * BIG_ROWS) // CH
    pieces = (piece_lists(nbig, seg, base, BIG_ROWS, MAX_BIG)
              + piece_lists(nsml, seg + nbig * BIG_ROWS, base + nbig * BIG_ROWS, CH, MAX_SMALL))
    return {
        "tnext": nxt[te], "wslot": (jnp.cumsum(tfirst) - 1) % 2, "tfirst": tfirst, "pieces": pieces,
        "tail0": offs + tot, "tailn": (tiles * ETILE - tot) // CH,
        "segrow": jnp.pad(segrows.astype(F32), ((0, 0), (0, 0), (0, LANES - N_EXPERTS))),
        "te": te, "tvalid": tvalid, "xblk": jnp.where(used, tid, tile_end[-1] - 1),
    }


def _moe(layer, x1, h2, meta, cnt, mod, w_gu, b_gu, w_down, b_down, split_out=False):
    plan = _moe_plan(cnt)
    xs = _dispatch(plan, h2, meta)
    y = _experts(layer, plan, xs, w_gu, b_gu, w_down, b_down)
    return _combine(plan, y, x1, meta, mod, split_out)


_HK, _HV = H_C * DK_C, H_C * DV_C
_ODD_MAIN = 2 * _HK + 2 * _HV


def _odd_in_kernel(x_ref, mod_ref, nmix_ref, win_ref, wgk_ref, bgk_ref, q_ref, k_ref, v_ref, g_ref, la_ref,
                   lamin_ref):
    m = mod_ref[0]
    h = _rms(x_ref[...]) * nmix_ref[...] * (1.0 + m[1:2]) + m[0:1]
    a = _dot(h.astype(BF16), win_ref[...])
    q_ref[...] = a[:, :_HK] * (DK_C ** -0.5)
    k_ref[...] = a[:, _HK:2 * _HK]
    v_ref[...] = a[:, 2 * _HK:2 * _HK + _HV]
    g_ref[...] = a[:, 2 * _HK + _HV:_ODD_MAIN]
    z = _dot(a[:, _ODD_MAIN:].astype(BF16), wgk_ref[...]) + bgk_ref[...]
    la = (jnp.minimum(z, 0.0) - jnp.log(1.0 + jnp.exp(-jnp.abs(z)))) * (1.0 / GATE_TAU)
    la_ref[...] = la
    chunk_tot = jnp.sum(la.reshape(TM // GLA_CHUNK, GLA_CHUNK, 2 * _HK), axis=1)
    lamin_ref[0] = jnp.broadcast_to(jnp.min(chunk_tot, axis=(0, 1), keepdims=True), (SUBLANES, LANES))


def _pairwise_att(q, k, c, forward):
    tcol = lax.broadcasted_iota(jnp.int32, (GLA_CHUNK, 1), 0)
    trow = lax.broadcasted_iota(jnp.int32, (GLA_CHUNK, DK_C), 0)
    scol = lax.broadcasted_iota(jnp.int32, (GLA_CHUNK, GLA_CHUNK), 1)

    def body(s, att):
        k_s = jnp.sum(jnp.where(trow == s, k, 0.0), axis=0, keepdims=True)
        c_s = jnp.sum(jnp.where(trow == s, c, 0.0), axis=0, keepdims=True)
        allowed = (tcol >= s) if forward else (tcol <= s)
        e = jnp.exp(jnp.where(allowed, c - c_s, -jnp.inf))
        return jnp.where(scol == s, jnp.sum(q * k_s * e, axis=-1, keepdims=True), att)

    return lax.fori_loop(0, GLA_CHUNK, body, jnp.zeros((GLA_CHUNK, GLA_CHUNK), F32))


def _gla_kernel(has_init, nchunk, strong_ref, *refs):
    if has_init:
        (qf, kf, vf, laf, qb, kb, vb, lab, s0f, s0b, _, _, of_ref, ob_ref, st) = refs
    else:
        (qf, kf, vf, laf, qb, kb, vb, lab, of_ref, ob_ref, sf_ref, sb_ref, st) = refs
    j = pl.program_id(1)

    @pl.when(j == 0)
    def _():
        if has_init:
            st[0] = s0f[...]
            st[1] = s0b[...]
        else:
            st[...] = jnp.zeros_like(st)

    row = lax.broadcasted_iota(jnp.int32, (GLA_CHUNK, GLA_CHUNK), 0)
    col = lax.broadcasted_iota(jnp.int32, (GLA_CHUNK, GLA_CHUNK), 1)

    def scan_step(factored):
        for d, (q_r, k_r, v_r, la_r, o_r) in enumerate(((qf, kf, vf, laf, of_ref), (qb, kb, vb, lab, ob_ref))):
            keep = (col <= row) if d == 0 else (col >= row)
            tri = jnp.where(keep, 1.0, 0.0).astype(BF16)
            for s in range(GLA_NS):
                g = la_r[s]
                g_hi = g.astype(BF16)
                g_r = g - g_hi.astype(F32)
                g_mid = g_r.astype(BF16)
                g_lo = (g_r - g_mid.astype(F32)).astype(BF16)
                c = _dot(tri, g_hi) + _dot(tri, g_mid) + _dot(tri, g_lo)
                tot = jnp.sum(g, axis=0, keepdims=True)
                decay = jnp.exp(tot)
                q, k = q_r[s], k_r[s]
                qe = (q * jnp.exp(c)).astype(BF16)
                if factored:
                    e_neg = jnp.exp(-c)
                    kd = (k * e_neg).astype(BF16)
                    k2 = (k * (e_neg * decay)).astype(BF16)
                else:
                    k2 = (k * jnp.exp(tot - c)).astype(BF16)
                v = v_r[s].astype(BF16)
                for h in range(H_C):
                    kc = slice(h * DK_C, (h + 1) * DK_C)
                    vc = slice(h * DV_C, (h + 1) * DV_C)
                    if factored:
                        att = jnp.where(keep, _dot_nt(qe[:, kc], kd[:, kc]), 0.0)
                    else:
                        att = _pairwise_att(q[:, kc], k[:, kc], c[:, kc], d == 0)
                    s_t = st[d, s, h]
                    o_r[s, :, vc] = _dot_nt(qe[:, kc], s_t.astype(BF16)) + _dot(att.astype(BF16), v[:, vc])
                    st[d, s, h] = s_t * decay[:, kc] + _dot_tn(v[:, vc], k2[:, kc])

    pl.when(strong_ref[0] == 0)(lambda: scan_step(True))
    pl.when(strong_ref[0] != 0)(lambda: scan_step(False))

    if not has_init:
        @pl.when(j == nchunk - 1)
        def _():
            sf_ref[...] = st[0]
            sb_ref[...] = st[1]


def _gla_call(has_init, nseq, seqlen, row0, strong, q, k, v, la, prev_f=None, prev_b=None, s0f=None, s0b=None):
    nchunk = seqlen // GLA_CHUNK
    nview = T // seqlen
    g0 = row0 // seqlen // GLA_NS
    view = lambda a: a.reshape(nview, seqlen, a.shape[-1])
    fwd = lambda g, j, *_: (g0 + g, j, 0)
    bwd = lambda g, j, *_: (g0 + g, nchunk - 1 - j, 0)
    bwd_la = lambda g, j, *_: (g0 + g, nchunk - 1 - j, 1)
    blk = lambda w, m: pl.BlockSpec((GLA_NS, GLA_CHUNK, w), m)
    state_spec = pl.BlockSpec((GLA_NS, H_C, DV_C, DK_C), lambda g, j, *_: (g, 0, 0, 0))
    in_specs = [blk(_HK, fwd), blk(_HK, fwd), blk(_HV, fwd), blk(_HK, fwd),
                blk(_HK, bwd), blk(_HK, bwd), blk(_HV, bwd), blk(_HK, bwd_la)]
    args = [view(q), view(k), view(v), view(la)] * 2
    aliases = {}
    if has_init:
        in_specs += [state_spec, state_spec] + [pl.BlockSpec(memory_space=pl.ANY)] * 2
        args += [s0f, s0b, view(prev_f), view(prev_b)]
        aliases = {len(args) - 1: 0, len(args): 1}
    out_specs = [blk(_HV, fwd), blk(_HV, bwd)]
    out_shape = [jax.ShapeDtypeStruct((nview, seqlen, _HV), F32)] * 2
    if not has_init:
        out_specs += [state_spec, state_spec]
        out_shape += [jax.ShapeDtypeStruct((nseq, H_C, DV_C, DK_C), F32)] * 2
    grid_spec = pltpu.PrefetchScalarGridSpec(
        num_scalar_prefetch=1,
        grid=(nseq // GLA_NS, nchunk),
        in_specs=in_specs,
        out_specs=out_specs,
        scratch_shapes=[pltpu.VMEM((2, GLA_NS, H_C, DV_C, DK_C), F32)],
    )
    outs = pl.pallas_call(
        functools.partial(_gla_kernel, has_init, nchunk),
        grid_spec=grid_spec,
        out_shape=out_shape,
        input_output_aliases=aliases,
        compiler_params=_params(("arbitrary", "arbitrary")),
        name="gla_sample" if has_init else "gla_prompt",
    )(strong, *args)
    return [outs[0].reshape(T, _HV), outs[1].reshape(T, _HV)] + list(outs[2:])


def _rope_tables():
    half = ROPE // 2
    inv_freq = np.power(np.float32(ROPE_THETA), -np.arange(0, half, 2, dtype=np.float32) / np.float32(half))
    n = np.arange(SAMPLE_LEN)
    row = (n // GRID_W).astype(np.float32)
    col = (n % GRID_W).astype(np.float32)
    ang_r = (row[:, None] * inv_freq[None, :]).astype(np.float32)
    ang_c = (col[:, None] * inv_freq[None, :]).astype(np.float32)
    nf = half // 2
    c = np.ones((TM + SAMPLE_LEN, LANES), np.float32)
    s1 = np.zeros((TM + SAMPLE_LEN, LANES), np.float32)
    s2 = np.zeros((TM + SAMPLE_LEN, LANES), np.float32)
    for base, ang in ((NOPE, ang_r), (NOPE + half, ang_c)):
        c[TM:, base:base + nf] = np.cos(ang)
        c[TM:, base + nf:base + half] = np.cos(ang)
        s1[TM:, base:base + nf] = -np.sin(ang)
        s2[TM:, base + nf:base + half] = np.sin(ang)
    return jnp.asarray(c), jnp.asarray(s1), jnp.asarray(s2)


def _pad_heads(w, nheads, width, lo=0):
    k = w.shape[0]
    w = w.reshape(k, nheads, width)
    w = jnp.pad(w, ((0, 0), (0, 0), (lo, HP - lo - width)))
    return w.reshape(k, nheads * HP)


def _row128(v, lo=0):
    return jnp.pad(v, (lo, LANES - lo - v.shape[0])).reshape(1, LANES)


def _even_layer(xs, xp, mod, nmix, w_in, q_a_norm, w_uq, q_norm, kv_a_norm, w_ukv, k_norm, v_norm, w_s, b_s, w_out,
                cache_ckv, cache_kpe):
    s = np.cumsum([Q_LORA, KV_LORA, ROPE, W_B])
    w_q, w_ckv, w_kpe, w_u, w_v = (w_in[:, :s[0]], w_in[:, s[0]:s[1]], w_in[:, s[1]:s[2]], w_in[:, s[2]:s[3]],
                                   w_in[:, s[3]:])
    w_kpe = jnp.pad(w_kpe, ((0, 0), (NOPE, LANES - QK_DIM)))
    win = jnp.concatenate([w_q, w_ckv, w_u, w_v, w_kpe], axis=1).astype(BF16)
    wuq = _pad_heads(w_uq, H_A, QK_DIM).astype(BF16)
    ukv = w_ukv.reshape(KV_LORA, H_A, NOPE + V_A)
    wuk = _pad_heads(ukv[:, :, :NOPE].reshape(KV_LORA, H_A * NOPE), H_A, NOPE)
    wuv = _pad_heads(ukv[:, :, NOPE:].reshape(KV_LORA, H_A * V_A), H_A, V_A)
    wukv = jnp.concatenate([wuk, wuv], axis=1).astype(BF16)
    qgain = _row128(q_norm * (QK_DIM ** -0.5 * LOG2_E))
    kgain = _row128(k_norm)
    bias = b_s.reshape(G_B // 2, 2, CHUNK_B)
    bias = jnp.concatenate([jnp.broadcast_to(bias[:, 0, :, None], (G_B // 2, CHUNK_B, C_B)),
                            jnp.broadcast_to(bias[:, 1, :, None], (G_B // 2, CHUNK_B, C_B))], axis=-1)
    rc, rs1, rs2 = _rope_tables()
    rope_spec = pl.BlockSpec((TM, LANES), lambda i: (_rope_blk(i), 0))
    width = H_A * HP
    q, k, v, ob, ckv, kpe = pl.pallas_call(
        _even_in_kernel,
        grid=(NBLK,),
        in_specs=[
            _X_SAMPLE, _X_PROMPT, _MOD, _full((1, D)), _full((D, _WIN_N)), _full((1, Q_LORA)),
            _full((Q_LORA, width)),
            _full((1, LANES)), _full((1, KV_LORA)), _full((KV_LORA, 2 * width)), _full((1, LANES)),
            _full((1, W_B)), _full((G_B // 2, 2 * CHUNK_B, CHUNK_B)), _full((G_B // 2, CHUNK_B, LANES)),
            rope_spec, rope_spec, rope_spec,
        ],
        out_specs=[_TOK(width), _TOK(width), _TOK(width), _TOK(W_B), _TOK(KV_LORA), _TOK(ROPE)],
        out_shape=[
            jax.ShapeDtypeStruct((T, width), BF16), jax.ShapeDtypeStruct((T, width), BF16),
            jax.ShapeDtypeStruct((T, width), BF16), jax.ShapeDtypeStruct((T, W_B), BF16),
            jax.ShapeDtypeStruct((T, KV_LORA), F32), jax.ShapeDtypeStruct((T, ROPE), F32),
        ],
        compiler_params=_params(("arbitrary",)),
        name="even_in",
    )(xs, xp, mod, nmix.reshape(1, D), win, q_a_norm.reshape(1, Q_LORA), wuq, qgain, kv_a_norm.reshape(1, KV_LORA),
      wukv, kgain, v_norm.reshape(1, W_B), w_s.astype(BF16).reshape(G_B // 2, 2 * CHUNK_B, CHUNK_B), bias, rc, rs1,
      rs2)

    n_ctx = N_SAMPLE_SEQ * PAST_LEN
    kpe_ctx = jnp.pad(cache_kpe.reshape(n_ctx, ROPE), ((0, 0), (NOPE, LANES - QK_DIM)))
    k_ctx, v_ctx = pl.pallas_call(
        _ctx_kv_kernel,
        grid=(n_ctx // TM,),
        in_specs=[_TOK(KV_LORA), _TOK(LANES), _full((KV_LORA, 2 * width)), _full((1, LANES))],
        out_specs=[_TOK(width), _TOK(width)],
        out_shape=[jax.ShapeDtypeStruct((n_ctx, width), BF16)] * 2,
        compiler_params=_params(("arbitrary",)),
        name="ctx_kv",
    )(cache_ckv.reshape(n_ctx, KV_LORA), kpe_ctx, wukv, kgain)

    oa = _attention(q, k, v, k_ctx, v_ctx)
    woa = jnp.pad(w_out[:H_A * V_A].reshape(H_A, V_A, D), ((0, 0), (0, HP - V_A), (0, 0))).reshape(width, D)
    return oa, ob, woa.astype(BF16), w_out[H_A * V_A:].astype(BF16), ckv, kpe


def kernel(x_prompt, x_sample, cache_mla_ckv, cache_mla_kpe, state_gla_fwd, state_gla_bwd, c, c_ctx, ada_w, ada_b,
           norm_mix, norm_ffn, even_w_in, mla_q_a_norm, mla_w_uq, mla_q_norm, mla_kv_a_norm, mla_w_ukv, mla_k_norm,
           cmlp_v_norm, cmlp_w_s, cmlp_b_s, even_w_out, odd_w_in, gla_w_gk_fwd, gla_b_gk_fwd, gla_w_gk_bwd,
           gla_b_gk_bwd, gla_o_norm, odd_w_out, moe_w_router, moe_b_router, moe_w_gu, moe_b_gu, moe_w_down,
           moe_b_down):
    xs0, xp0 = x_sample.reshape(N_SAMPLE, D), x_prompt.reshape(N_PROMPT, D)
    cond8 = jnp.concatenate([c_ctx[None, :], c, jnp.zeros((SUBLANES - 1 - N_SAMPLE_SEQ, D), F32)], axis=0)
    mods = _adaln(cond8, ada_w, ada_b)
    wr = jnp.pad(moe_w_router, ((0, 0), (0, 0), (0, LANES - N_EXPERTS)))
    wr_hi = wr.astype(BF16)
    wr = jnp.concatenate([wr_hi, (wr - wr_hi.astype(F32)).astype(BF16)], axis=-1)
    br = jnp.pad(moe_b_router, ((0, 0), (0, LANES - N_EXPERTS))).reshape(2, 1, LANES)

    oa, ob, woa, wob, ckv, kpe = _even_layer(
        xs0, xp0, mods[0], norm_mix[0], even_w_in[0], mla_q_a_norm[0], mla_w_uq[0], mla_q_norm[0], mla_kv_a_norm[0],
        mla_w_ukv[0], mla_k_norm[0], cmlp_v_norm[0], cmlp_w_s[0], cmlp_b_s[0], even_w_out[0],
        cache_mla_ckv[:, 0], cache_mla_kpe[:, 0])
    width = H_A * HP
    x1, h2, meta, cnt = pl.pallas_call(
        _even_out_kernel,
        grid=(NBLK,),
        in_specs=[_TOK(width), _TOK(W_B), _X_SAMPLE, _X_PROMPT, _MOD, _full((width, D)), _full((W_B, D)),
                  _full((1, D)),
                  _full((D, 2 * LANES)), _full((1, LANES))],
        out_specs=_PROLOGUE_OUT_SPECS,
        out_shape=_PROLOGUE_OUT_SHAPE,
        compiler_params=_params(("arbitrary",)),
        name="even_out",
    )(oa, ob, xs0, xp0, mods[0], woa, wob, norm_ffn[0].reshape(1, D), wr[0], br[0])
    x2 = _moe(0, x1, h2, meta, cnt, mods[0], moe_w_gu, moe_b_gu, moe_w_down, moe_b_down)

    w_in = odd_w_in[0]
    win = jnp.concatenate([w_in, jnp.zeros((D, LANES - 2 * GATE_RANK), F32)], axis=1).astype(BF16)
    wgk = jnp.zeros((LANES, 2 * _HK), F32)
    wgk = wgk.at[:GATE_RANK, :_HK].set(gla_w_gk_fwd[0]).at[GATE_RANK:2 * GATE_RANK, _HK:].set(gla_w_gk_bwd[0])
    bgk = jnp.concatenate([gla_b_gk_fwd[0], gla_b_gk_bwd[0]]).reshape(1, 2 * _HK)
    q, k, v, g, la, lamin = pl.pallas_call(
        _odd_in_kernel,
        grid=(NBLK,),
        in_specs=[_TOK(D), _MOD, _full((1, D)), _full((D, _ODD_MAIN + LANES)), _full((LANES, 2 * _HK)),
                  _full((1, 2 * _HK))],
        out_specs=[_TOK(_HK), _TOK(_HK), _TOK(_HV), _TOK(_HV), _TOK(2 * _HK), _TILE_ROW],
        out_shape=[jax.ShapeDtypeStruct((T, _HK), F32), jax.ShapeDtypeStruct((T, _HK), F32),
                   jax.ShapeDtypeStruct((T, _HV), F32), jax.ShapeDtypeStruct((T, _HV), F32),
                   jax.ShapeDtypeStruct((T, 2 * _HK), F32), jax.ShapeDtypeStruct((NBLK, SUBLANES, LANES), F32)],
        compiler_params=_params(("arbitrary",)),
        name="odd_in",
    )(x2, mods[1], norm_mix[1].reshape(1, D), win, wgk.astype(BF16), bgk)

    strong = (jnp.min(lamin) < -GLA_SAFE_LOG_DECAY).astype(jnp.int32).reshape(1)
    of, obk, st_f, st_b = _gla_call(False, N_PROMPT_SEQ, PROMPT_LEN, N_SAMPLE, strong, q, k, v, la)
    s0f = state_gla_fwd[:, 0].transpose(0, 1, 3, 2)
    s0b = state_gla_bwd[:, 0].transpose(0, 1, 3, 2)
    of, obk = _gla_call(True, N_SAMPLE_SEQ, SAMPLE_LEN, 0, strong, q, k, v, la, of, obk, s0f, s0b)

    x3, h2, meta, cnt = pl.pallas_call(
        _odd_out_kernel,
        grid=(NBLK,),
        in_specs=[_TOK(_HV), _TOK(_HV), _TOK(_HV), _TOK(D), _MOD, _full((1, DV_C)), _full((_HV, D)),
                  _full((1, D)), _full((D, 2 * LANES)), _full((1, LANES))],
        out_specs=_PROLOGUE_OUT_SPECS,
        out_shape=_PROLOGUE_OUT_SHAPE,
        compiler_params=_params(("arbitrary",)),
        name="odd_out",
    )(of, obk, g, x2, mods[1], gla_o_norm[0].reshape(1, DV_C), odd_w_out[0].astype(BF16),
      norm_ffn[1].reshape(1, D), wr[1], br[1])
    ys, yp = _moe(1, x3, h2, meta, cnt, mods[1], moe_w_gu, moe_b_gu, moe_w_down, moe_b_down, split_out=True)

    y_sample = ys.reshape(N_SAMPLE_SEQ, SAMPLE_LEN, D)
    y_prompt = yp.reshape(N_PROMPT_SEQ, PROMPT_LEN, D)
    new_ckv = ckv[N_SAMPLE:].reshape(N_PROMPT_SEQ, 1, PROMPT_LEN, KV_LORA)
    new_kpe = kpe[N_SAMPLE:].reshape(N_PROMPT_SEQ, 1, PROMPT_LEN, ROPE)
    new_fwd = st_f.transpose(0, 1, 3, 2)[:, None]
    new_bwd = st_b.transpose(0, 1, 3, 2)[:, None]
    return (y_prompt, y_sample, new_ckv, new_kpe, new_fwd, new_bwd)
```

```python
import functools

import numpy as np
import jax
import jax.numpy as jnp
from jax import lax
from jax.experimental import pallas as pl
from jax.experimental.pallas import tpu as pltpu

F32 = jnp.float32
BF16 = jnp.bfloat16

D = 1024
N_PROMPT_SEQ, PROMPT_LEN = 16, 256
N_SAMPLE_SEQ, SAMPLE_LEN = 4, 2048
PAST_LEN = 512
N_PROMPT = N_PROMPT_SEQ * PROMPT_LEN
N_SAMPLE = N_SAMPLE_SEQ * SAMPLE_LEN
T = N_PROMPT + N_SAMPLE
EPS = 1e-6
GRID_W = 64
H_A, Q_LORA, KV_LORA, NOPE, ROPE, V_A = 8, 512, 256, 64, 32, 64
QK_DIM = NOPE + ROPE
G_B, C_B, W_B, CHUNK_B = 8, 64, 512, 128
H_C, DK_C, DV_C, GATE_RANK, GATE_TAU = 4, 128, 256, 16, 16.0
GLA_CHUNK = 128
N_EXPERTS, TOP_K, D_FF = 32, 4, 1024
SWIGLU_LIMIT, SWIGLU_ALPHA = 7.0, 1.702
ROPE_THETA = 10000.0
LOG2_E = 1.4426950408889634

LANES = 128
SUBLANES = 8
VMEM_LIMIT = 56 * 1024 * 1024

TM = 512
NBLK = T // TM
SAMPLE_BLKS = N_SAMPLE // TM
BLKS_PER_SAMPLE_SEQ = SAMPLE_LEN // TM
ATT_TQ = 512
ATT_VMEM_LIMIT = 60 * 1024 * 1024
GLA_NS = 4
GLA_SAFE_LOG_DECAY = 60.0
ETILE = 512
TMD = TM
NBD = T // TMD
NBD_SAMPLE = N_SAMPLE // TMD
CH = 2 * SUBLANES
_MAX_LOCAL = TMD * TOP_K + N_EXPERTS * (CH - 1)
_POS_BLOCK = 256
ROWS_L = -(-_MAX_LOCAL // _POS_BLOCK) * _POS_BLOCK
BIG_ROWS = 4 * CH
MAX_BIG = ROWS_L // BIG_ROWS
MAX_SMALL = N_EXPERTS * (BIG_ROWS // CH - 1)
N_ETILES = -(-NBD * _MAX_LOCAL // ETILE) + N_EXPERTS
P_ROWS = N_ETILES * ETILE
HP = LANES


def _cond_row(i):
    return jnp.where(i < SAMPLE_BLKS, 1 + i // BLKS_PER_SAMPLE_SEQ, 0)


def _rope_blk(i):
    return jnp.where(i < SAMPLE_BLKS, 1 + i % BLKS_PER_SAMPLE_SEQ, 0)


def _rms(x):
    return x * lax.rsqrt(jnp.mean(x * x, axis=-1, keepdims=True) + EPS)


def _gelu(x):
    return 0.5 * x * (1.0 + jnp.tanh(0.7978845608028654 * (x + 0.044715 * (x * x * x))))


def _silu(x):
    return x * jax.nn.sigmoid(x)


def _dot(a, b):
    return jnp.dot(a, b, preferred_element_type=F32)


def _dot_nt(a, b):
    return lax.dot_general(a, b, (((1,), (1,)), ((), ())), preferred_element_type=F32)


def _dot_tn(a, b):
    return lax.dot_general(a, b, (((0,), (0,)), ((), ())), preferred_element_type=F32)


def _params(sem, vmem=VMEM_LIMIT):
    return pltpu.CompilerParams(dimension_semantics=sem, vmem_limit_bytes=vmem)


def _full(shape):
    nd = len(shape)
    return pl.BlockSpec(shape, lambda *_: (0,) * nd)


ADA_TN = 1536


def _adaln_kernel(c_ref, w_ref, b_ref, o_ref):
    s = _silu(c_ref[...]).astype(BF16)
    o_ref[0] = _dot(s, w_ref[0].astype(BF16)) + b_ref[0]


def _adaln(cond8, ada_w, ada_b):
    depth = ada_w.shape[0]
    n = ada_w.shape[2]
    out = pl.pallas_call(
        _adaln_kernel,
        grid=(depth, n // ADA_TN),
        in_specs=[
            pl.BlockSpec((SUBLANES, D), lambda l, j: (0, 0)),
            pl.BlockSpec((1, D, ADA_TN), lambda l, j: (l, 0, j)),
            pl.BlockSpec((1, 1, ADA_TN), lambda l, j: (l, 0, j)),
        ],
        out_specs=pl.BlockSpec((1, SUBLANES, ADA_TN), lambda l, j: (l, 0, j)),
        out_shape=jax.ShapeDtypeStruct((depth, SUBLANES, n), F32),
        compiler_params=_params(("arbitrary", "arbitrary")),
        name="adaln",
    )(cond8, ada_w, ada_b.reshape(depth, 1, n))
    return out.reshape(depth, SUBLANES, 6, D)


_QC0, _CKV0, _U0, _V0, _KPE0, _WIN_N = 0, 512, 768, 1280, 1792, 1920


def _rope(y, c, s1, s2):
    return y * c + pltpu.roll(y, LANES - 8, 1) * s1 + pltpu.roll(y, 8, 1) * s2


def _k_heads(k_raw, kpe128, kp_rot, kgain, k_ref):
    sskpe = jnp.sum(kpe128 * kpe128, axis=-1, keepdims=True)
    for h in range(H_A):
        kb = k_raw[:, h * HP:(h + 1) * HP]
        r = lax.rsqrt((jnp.sum(kb * kb, axis=-1, keepdims=True) + sskpe) * (1.0 / QK_DIM) + EPS)
        k_ref[:, h * HP:(h + 1) * HP] = ((kb * kgain + kp_rot) * r).astype(BF16)


def _input_rows(xs_ref, xp_ref):
    return jnp.where(pl.program_id(0) < SAMPLE_BLKS, xs_ref[...], xp_ref[...])


_X_SAMPLE = pl.BlockSpec((TM, D), lambda i, *_: (jnp.minimum(i, SAMPLE_BLKS - 1), 0))
_X_PROMPT = pl.BlockSpec((TM, D), lambda i, *_: (jnp.maximum(i - SAMPLE_BLKS, 0), 0))


def _even_in_kernel(xs_ref, xp_ref, mod_ref, nmix_ref, win_ref, qan_ref, wuq_ref, qgain_ref, kvan_ref, wukv_ref,
                    kgain_ref, vnorm_ref, ws_ref, bs_ref, rc_ref, rs1_ref, rs2_ref,
                    q_ref, k_ref, v_ref, ob_ref, ckv_ref, kpe_ref):
    m = mod_ref[0]
    h = _rms(_input_rows(xs_ref, xp_ref)) * nmix_ref[...] * (1.0 + m[1:2]) + m[0:1]
    a = _dot(h.astype(BF16), win_ref[...])
    qc = a[:, _QC0:_CKV0]
    ckv = a[:, _CKV0:_U0]
    u = a[:, _U0:_V0]
    vv = a[:, _V0:_KPE0]
    kpe128 = a[:, _KPE0:_WIN_N]

    ckv_n = _rms(ckv) * kvan_ref[...]
    ckv_ref[...] = ckv_n
    kpe_ref[...] = kpe128[:, NOPE:QK_DIM]

    rc, rs1, rs2 = rc_ref[...], rs1_ref[...], rs2_ref[...]
    qn = (_rms(qc) * qan_ref[...]).astype(BF16)
    qr = _dot(qn, wuq_ref[...])
    qgain = qgain_ref[...]
    for hh in range(H_A):
        blk = qr[:, hh * HP:(hh + 1) * HP]
        r = lax.rsqrt(jnp.sum(blk * blk, axis=-1, keepdims=True) * (1.0 / QK_DIM) + EPS)
        q_ref[:, hh * HP:(hh + 1) * HP] = _rope(blk * r * qgain, rc, rs1, rs2).astype(BF16)

    kv = _dot(ckv_n.astype(BF16), wukv_ref[...])
    v_ref[...] = kv[:, H_A * HP:].astype(BF16)
    kgain = kgain_ref[...]
    kp_rot = _rope(kpe128 * kgain, rc, rs1, rs2)
    _k_heads(kv[:, :H_A * HP], kpe128, kp_rot, kgain, k_ref)

    ug = _gelu(u)
    vn = (_rms(_gelu(vv)) * vnorm_ref[...]).astype(BF16)
    low = lax.broadcasted_iota(jnp.int32, (CHUNK_B, LANES), 1) < C_B
    for c in range(TM // CHUNK_B):
        rows = slice(c * CHUNK_B, (c + 1) * CHUNK_B)
        for p in range(G_B // 2):
            cols = slice(p * LANES, (p + 1) * LANES)
            blk = vn[rows, cols]
            both = _dot(ws_ref[p], blk)
            mixed = jnp.where(low, both[:CHUNK_B], both[CHUNK_B:]) + bs_ref[p]
            ob_ref[rows, cols] = (ug[rows, cols] * mixed).astype(BF16)


def _ctx_kv_kernel(ckv_ref, kpe_ref, wukv_ref, kgain_ref, k_ref, v_ref):
    kv = _dot(ckv_ref[...].astype(BF16), wukv_ref[...])
    v_ref[...] = kv[:, H_A * HP:].astype(BF16)
    kgain = kgain_ref[...]
    kpe128 = kpe_ref[...]
    _k_heads(kv[:, :H_A * HP], kpe128, kpe128 * kgain, kgain, k_ref)


def _attn_self_kernel(q_ref, k_ref, v_ref, o_ref):
    for h in range(H_A):
        cols = slice(h * HP, (h + 1) * HP)
        s = _dot_nt(q_ref[:, cols], k_ref[:, cols])
        p = jnp.exp2(s - jnp.max(s, axis=-1, keepdims=True))
        inv = 1.0 / jnp.sum(p, axis=-1, keepdims=True)
        o_ref[:, cols] = (_dot(p.astype(BF16), v_ref[:, cols]) * inv).astype(BF16)


def _attn_ctx_kernel(q_ref, k_ref, v_ref, kc_ref, vc_ref, prev_ref, o_ref):
    del prev_ref
    for h in range(H_A):
        cols = slice(h * HP, (h + 1) * HP)
        q = q_ref[:, cols]
        s1 = _dot_nt(q, k_ref[:, cols])
        s2 = _dot_nt(q, kc_ref[:, cols])
        mx = jnp.maximum(jnp.max(s1, axis=-1, keepdims=True), jnp.max(s2, axis=-1, keepdims=True))
        p1 = jnp.exp2(s1 - mx)
        p2 = jnp.exp2(s2 - mx)
        inv = 1.0 / (jnp.sum(p1, axis=-1, keepdims=True) + jnp.sum(p2, axis=-1, keepdims=True))
        o = _dot(p1.astype(BF16), v_ref[:, cols]) + _dot(p2.astype(BF16), vc_ref[:, cols])
        o_ref[:, cols] = (o * inv).astype(BF16)


def _attention(q, k, v, k_ctx, v_ctx):
    width = H_A * HP
    first = N_SAMPLE // PROMPT_LEN
    o = pl.pallas_call(
        _attn_self_kernel,
        grid=(N_PROMPT_SEQ,),
        in_specs=[pl.BlockSpec((PROMPT_LEN, width), lambda i: (first + i, 0))] * 3,
        out_specs=pl.BlockSpec((PROMPT_LEN, width), lambda i: (first + i, 0)),
        out_shape=jax.ShapeDtypeStruct((T, width), BF16),
        compiler_params=_params(("arbitrary",)),
        name="attn_prompt",
    )(q, k, v)
    qblocks = SAMPLE_LEN // ATT_TQ
    qblk = lambda b, j: (b * qblocks + j, 0)
    return pl.pallas_call(
        _attn_ctx_kernel,
        grid=(N_SAMPLE_SEQ, qblocks),
        in_specs=[
            pl.BlockSpec((ATT_TQ, width), qblk),
            pl.BlockSpec((SAMPLE_LEN, width), lambda b, j: (b, 0)),
            pl.BlockSpec((SAMPLE_LEN, width), lambda b, j: (b, 0)),
            pl.BlockSpec((PAST_LEN, width), lambda b, j: (b, 0)),
            pl.BlockSpec((PAST_LEN, width), lambda b, j: (b, 0)),
            pl.BlockSpec(memory_space=pl.ANY),
        ],
        out_specs=pl.BlockSpec((ATT_TQ, width), qblk),
        out_shape=jax.ShapeDtypeStruct((T, width), BF16),
        input_output_aliases={5: 0},
        compiler_params=_params(("arbitrary", "arbitrary"), ATT_VMEM_LIMIT),
        name="attn_sample",
    )(q, k, v, k_ctx, v_ctx, o)


_META_IDX, _META_RANK, _META_W = 0, TOP_K, 2 * TOP_K


def _moe_prologue(x1, m, nffn_ref, wr_ref, br_ref, x1_ref, h2_ref, meta_ref, cnt_ref):
    x1_ref[...] = x1
    h2 = _rms(x1) * nffn_ref[...] * (1.0 + m[4:5]) + m[3:4]
    h2_ref[...] = h2.astype(BF16)
    lane = lax.broadcasted_iota(jnp.int32, (TM, LANES), 1)
    lanef = lane.astype(F32)
    h_hi = h2.astype(BF16)
    h_lo = (h2 - h_hi.astype(F32)).astype(BF16)
    r = _dot(h_hi, wr_ref[...])
    logits = r[:, :LANES] + r[:, LANES:] + _dot(h_lo, wr_ref[:, :LANES]) + br_ref[...]
    work = jnp.where(lane < N_EXPERTS, logits, -jnp.inf)
    hots, vals = [], []
    for _ in range(TOP_K):
        mx = jnp.max(work, axis=-1, keepdims=True)
        idx = jnp.min(jnp.where(work == mx, lanef, float(LANES)), axis=-1, keepdims=True)
        hot = lanef == idx
        work = jnp.where(hot, -jnp.inf, work)
        hots.append((hot, idx))
        vals.append(mx)
    es = [jnp.exp(v - vals[0]) for v in vals]
    inv = 1.0 / (es[0] + es[1] + es[2] + es[3])
    sel = jnp.zeros((TM, LANES), F32)
    for hot, _ in hots:
        sel = jnp.where(hot, 1.0, sel)
    row = lax.broadcasted_iota(jnp.int32, (TM, TM), 0)
    col = lax.broadcasted_iota(jnp.int32, (TM, TM), 1)
    strict = jnp.where(row > col, 1.0, 0.0).astype(BF16)
    before = _dot(strict, sel.astype(BF16))
    meta = jnp.zeros((TM, LANES), F32)
    for kk, (hot, idx) in enumerate(hots):
        rank = jnp.sum(jnp.where(hot, before, 0.0), axis=-1, keepdims=True)
        meta = jnp.where(lane == _META_IDX + kk, idx, meta)
        meta = jnp.where(lane == _META_RANK + kk, rank, meta)
        meta = jnp.where(lane == _META_W + kk, es[kk] * inv, meta)
    meta_ref[...] = meta
    cnt_ref[0] = jnp.broadcast_to(jnp.sum(sel, axis=0, keepdims=True), (SUBLANES, LANES))


def _even_out_kernel(oa_ref, ob_ref, xs_ref, xp_ref, mod_ref, woa_ref, wob_ref, nffn_ref, wr_ref, br_ref,
                     x1_ref, h2_ref, meta_ref, cnt_ref):
    m = mod_ref[0]
    out = _dot(oa_ref[...], woa_ref[...]) + _dot(ob_ref[...], wob_ref[...])
    x1 = _input_rows(xs_ref, xp_ref) + m[2:3] * out
    _moe_prologue(x1, m, nffn_ref, wr_ref, br_ref, x1_ref, h2_ref, meta_ref, cnt_ref)


def _odd_out_kernel(of_ref, ob_ref, g_ref, x_ref, mod_ref, onorm_ref, wo_ref, nffn_ref, wr_ref, br_ref,
                    x1_ref, h2_ref, meta_ref, cnt_ref):
    m = mod_ref[0]
    onorm = onorm_ref[...]
    parts = []
    for h in range(H_C):
        cols = slice(h * DV_C, (h + 1) * DV_C)
        o = of_ref[:, cols] + ob_ref[:, cols]
        parts.append((_rms(o) * onorm * _silu(g_ref[:, cols])).astype(BF16))
    out = _dot(jnp.concatenate(parts, axis=-1), wo_ref[...])
    x1 = x_ref[...] + m[2:3] * out
    _moe_prologue(x1, m, nffn_ref, wr_ref, br_ref, x1_ref, h2_ref, meta_ref, cnt_ref)


_TOK = lambda w: pl.BlockSpec((TM, w), lambda i, *_: (i, 0))
_MOD = pl.BlockSpec((1, 6, D), lambda i, *_: (_cond_row(i), 0, 0))
_TILE_ROW = pl.BlockSpec((1, SUBLANES, LANES), lambda i, *_: (i, 0, 0))

_PROLOGUE_OUT_SPECS = [_TOK(D), _TOK(D), _TOK(LANES), _TILE_ROW]
_PROLOGUE_OUT_SHAPE = [
    jax.ShapeDtypeStruct((T, D), F32),
    jax.ShapeDtypeStruct((T, D), BF16),
    jax.ShapeDtypeStruct((T, LANES), F32),
    jax.ShapeDtypeStruct((NBLK, SUBLANES, LANES), F32),
]


def _local_positions(meta, seg_rows):
    lanef = lax.broadcasted_iota(jnp.int32, (TM, LANES), 1).astype(F32)
    pos = []
    for k in range(TOP_K):
        halves = []
        for half in range(TMD // TM):
            m = meta[half * TM:(half + 1) * TM]
            hot = lanef == m[:, _META_IDX + k:_META_IDX + k + 1]
            start = jnp.sum(jnp.where(hot, seg_rows[half:half + 1], 0.0), axis=-1, keepdims=True)
            halves.append(start + m[:, _META_RANK + k:_META_RANK + k + 1])
        pos.append(jnp.concatenate(halves, axis=0))
    return pos


def _scatter_matrix(pos, vals):
    lane = lax.broadcasted_iota(jnp.int32, (TMD, _POS_BLOCK), 1).astype(F32).astype(BF16)
    blocks = []
    for j in range(ROWS_L // _POS_BLOCK):
        acc = jnp.zeros((TMD, _POS_BLOCK), BF16)
        for p, v in zip(pos, vals):
            acc = jnp.where(lane == (p - float(j * _POS_BLOCK)).astype(BF16), v, acc)
        blocks.append(acc)
    return jnp.concatenate(blocks, axis=1)


def _chunk_copies(t, lists, make):
    for rows, (src_ref, dst_ref, n_ref, stride) in zip((BIG_ROWS, CH), lists):
        def one(j, c, rows=rows, src_ref=src_ref, dst_ref=dst_ref, stride=stride):
            q = t * stride + j
            make(pl.multiple_of(src_ref[q], CH), pl.multiple_of(dst_ref[q], CH), rows).start()
            return c

        lax.fori_loop(0, n_ref[t], one, 0)


def _drain_tile(t, lists, wait_copy):
    for rows, (_, _, n_ref, _) in zip((BIG_ROWS, CH), lists):
        _drain(n_ref[t], wait_copy(rows))


def _drain(count, chunk_copy):
    def one(j, c):
        chunk_copy.wait()
        return c

    lax.fori_loop(0, count, one, 0)


def _piece_sem(rows, slot):
    return slot if rows == BIG_ROWS else 2 + slot


def _dispatch_kernel(bsrc, bdst, nbig, ssrc, sdst, nsml, tail0_ref, tailn_ref, h_ref, meta_ref, segrow_ref,
                     xs_hbm, buf, zbuf, sem):
    i = pl.program_id(0)
    slot = i % 2
    lists = ((bsrc, bdst, nbig, MAX_BIG), (ssrc, sdst, nsml, MAX_SMALL))
    wait_copy = lambda s: lambda rows: pltpu.make_async_copy(
        buf.at[s, pl.ds(0, rows)], xs_hbm.at[pl.ds(0, rows)], sem.at[_piece_sem(rows, s)])

    @pl.when(i == 0)
    def _():
        zbuf[...] = jnp.zeros_like(zbuf)

        def per_expert(e, carry):
            def one(j, c):
                pltpu.make_async_copy(zbuf, xs_hbm.at[pl.ds(pl.multiple_of(tail0_ref[e] + j * CH, CH), CH)],
                                      sem.at[4]).start()
                return c

            lax.fori_loop(0, tailn_ref[e], one, 0)
            return carry + tailn_ref[e]

        total = lax.fori_loop(0, N_EXPERTS, per_expert, 0)
        _drain(total, pltpu.make_async_copy(zbuf, xs_hbm.at[pl.ds(0, CH)], sem.at[4]))

    pos = _local_positions(meta_ref[...], segrow_ref[0])
    riota = lax.broadcasted_iota(jnp.int32, (TMD, ROWS_L), 1).astype(F32)
    pt = jnp.zeros((TMD, ROWS_L), F32)
    for p in pos:
        pt = jnp.where(riota == p, 1.0, pt)
    buf[slot] = _dot_tn(pt.astype(BF16), h_ref[...]).astype(BF16)

    make = lambda s, d, rows: pltpu.make_async_copy(buf.at[slot, pl.ds(s, rows)], xs_hbm.at[pl.ds(d, rows)],
                                                    sem.at[_piece_sem(rows, slot)])
    _chunk_copies(i, lists, make)

    @pl.when(i > 0)
    def _():
        _drain_tile(i - 1, lists, wait_copy(1 - slot))

    @pl.when(i == NBD - 1)
    def _():
        _drain_tile(i, lists, wait_copy(slot))


_TOKD = lambda w: pl.BlockSpec((TMD, w), lambda i, *_: (i, 0))
_SEGROWS = pl.BlockSpec((1, TMD // TM, LANES), lambda i, *_: (i, 0, 0))


def _dispatch(plan, h2, meta):
    grid_spec = pltpu.PrefetchScalarGridSpec(
        num_scalar_prefetch=8,
        grid=(NBD,),
        in_specs=[_TOKD(D), _TOKD(LANES), _SEGROWS],
        out_specs=pl.BlockSpec(memory_space=pl.ANY),
        scratch_shapes=[pltpu.VMEM((2, ROWS_L, D), BF16), pltpu.VMEM((CH, D), BF16),
                        pltpu.SemaphoreType.DMA((5,))],
    )
    return pl.pallas_call(
        _dispatch_kernel,
        grid_spec=grid_spec,
        out_shape=jax.ShapeDtypeStruct((P_ROWS, D), BF16),
        compiler_params=_params(("arbitrary",)),
        name="moe_dispatch",
    )(*plan["pieces"], plan["tail0"], plan["tailn"], h2, meta, plan["segrow"])


def _expert_kernel(layer, te_ref, tfirst_ref, tvalid_ref, xblk_ref, tnext_ref, wslot_ref, x_ref, wgu_hbm, bgu_ref,
                   wd_hbm, bd_ref, y_ref, wgu_f32, wd_f32, wgu_bf, wd_bf, sem):
    del xblk_ref
    i = pl.program_id(0)

    def weight_copies(e, slot):
        return (pltpu.make_async_copy(wgu_hbm.at[layer, e], wgu_f32.at[slot], sem.at[0, slot]),
                pltpu.make_async_copy(wd_hbm.at[layer, e], wd_f32.at[slot], sem.at[1, slot]))

    @pl.when(tfirst_ref[i] == 1)
    def _():
        slot = wslot_ref[i]

        @pl.when(i == 0)
        def _():
            for cp in weight_copies(te_ref[i], slot):
                cp.start()

        for cp in weight_copies(te_ref[i], slot):
            cp.wait()
        wgu_bf[...] = wgu_f32[slot].astype(BF16)
        wd_bf[...] = wd_f32[slot].astype(BF16)

        @pl.when(tnext_ref[i] >= 0)
        def _():
            for cp in weight_copies(tnext_ref[i], 1 - slot):
                cp.start()

    def ffn(x):
        a = _dot(x, wgu_bf[...]) + bgu_ref[...]
        glu = jnp.minimum(a[:, :D_FF], SWIGLU_LIMIT)
        lin = jnp.clip(a[:, D_FF:], -SWIGLU_LIMIT, SWIGLU_LIMIT)
        act = (glu * jax.nn.sigmoid(SWIGLU_ALPHA * glu)) * (lin + 1.0)
        return (_dot(act.astype(BF16), wd_bf[...]) + bd_ref[...]).astype(BF16)

    half = ETILE // 2

    @pl.when(tvalid_ref[i] == 2)
    def _():
        y_ref[...] = ffn(x_ref[...])

    @pl.when(tvalid_ref[i] == 1)
    def _():
        y_ref[:half] = ffn(x_ref[:half])
        y_ref[half:] = jnp.zeros((ETILE - half, D), BF16)


def _experts(layer, plan, xs, w_gu, b_gu, w_down, b_down):
    depth = w_gu.shape[0]
    e_of = lambda i, te, *_: (layer, te[i], 0, 0)
    grid_spec = pltpu.PrefetchScalarGridSpec(
        num_scalar_prefetch=6,
        grid=(N_ETILES,),
        in_specs=[
            pl.BlockSpec((ETILE, D), lambda i, te, tf, tv, xb, *_: (xb[i], 0)),
            pl.BlockSpec(memory_space=pl.ANY),
            pl.BlockSpec((None, None, 1, 2 * D_FF), e_of),
            pl.BlockSpec(memory_space=pl.ANY),
            pl.BlockSpec((None, None, 1, D), e_of),
        ],
        out_specs=pl.BlockSpec((ETILE, D), lambda i, te, tf, tv, xb, *_: (xb[i], 0)),
        scratch_shapes=[pltpu.VMEM((2, D, 2 * D_FF), F32), pltpu.VMEM((2, D_FF, D), F32),
                        pltpu.VMEM((D, 2 * D_FF), BF16), pltpu.VMEM((D_FF, D), BF16),
                        pltpu.SemaphoreType.DMA((2, 2))],
    )
    return pl.pallas_call(
        functools.partial(_expert_kernel, layer),
        grid_spec=grid_spec,
        out_shape=jax.ShapeDtypeStruct((P_ROWS, D), BF16),
        compiler_params=_params(("arbitrary",)),
        name="moe_experts",
    )(plan["te"], plan["tfirst"], plan["tvalid"], plan["xblk"], plan["tnext"], plan["wslot"], xs, w_gu,
      b_gu.reshape(depth, N_EXPERTS, 1, 2 * D_FF), w_down, b_down.reshape(depth, N_EXPERTS, 1, D))


def _combine_kernel(split, bsrc, bdst, nbig, ssrc, sdst, nsml, y_hbm, x1_ref, meta_ref, segrow_ref, mod_ref,
                    *rest):
    *o_refs, ybuf, sem = rest
    i = pl.program_id(0)
    slot = i % 2
    lists = ((bsrc, bdst, nbig, MAX_BIG), (ssrc, sdst, nsml, MAX_SMALL))

    def fetch(t, s):
        make = lambda loc, glob, rows: pltpu.make_async_copy(
            y_hbm.at[pl.ds(glob, rows)], ybuf.at[s, pl.ds(loc, rows)], sem.at[_piece_sem(rows, s)])
        _chunk_copies(t, lists, make)

    @pl.when(i == 0)
    def _():
        ybuf[...] = jnp.zeros_like(ybuf)
        fetch(i, slot)

    @pl.when(i + 1 < NBD)
    def _():
        fetch(i + 1, 1 - slot)

    _drain_tile(i, lists, lambda rows: pltpu.make_async_copy(
        y_hbm.at[pl.ds(0, rows)], ybuf.at[slot, pl.ds(0, rows)], sem.at[_piece_sem(rows, slot)]))

    meta = meta_ref[...]
    pos = _local_positions(meta, segrow_ref[0])
    gates = _scatter_matrix(pos, [meta[:, _META_W + k:_META_W + k + 1].astype(BF16) for k in range(TOP_K)])
    acc = _dot(gates, ybuf[slot])
    out = x1_ref[...] + mod_ref[0][5:6] * acc
    if split:
        os_ref, op_ref = o_refs

        @pl.when(i < NBD_SAMPLE)
        def _():
            os_ref[...] = out

        @pl.when(i >= NBD_SAMPLE)
        def _():
            op_ref[...] = out
    else:
        o_refs[0][...] = out


def _combine(plan, y, x1, meta, mod, split):
    if split:
        out_specs = [pl.BlockSpec((TMD, D), lambda i, *_: (jnp.minimum(i, NBD_SAMPLE - 1), 0)),
                     pl.BlockSpec((TMD, D), lambda i, *_: (jnp.maximum(i - NBD_SAMPLE, 0), 0))]
        out_shape = [jax.ShapeDtypeStruct((N_SAMPLE, D), F32), jax.ShapeDtypeStruct((N_PROMPT, D), F32)]
    else:
        out_specs, out_shape = _TOKD(D), jax.ShapeDtypeStruct((T, D), F32)
    grid_spec = pltpu.PrefetchScalarGridSpec(
        num_scalar_prefetch=6,
        grid=(NBD,),
        in_specs=[pl.BlockSpec(memory_space=pl.ANY), _TOKD(D), _TOKD(LANES), _SEGROWS,
                  pl.BlockSpec((1, 6, D), lambda i, *_: (_cond_row(i * (TMD // TM)), 0, 0))],
        out_specs=out_specs,
        scratch_shapes=[pltpu.VMEM((2, ROWS_L, D), BF16), pltpu.SemaphoreType.DMA((4,))],
    )
    return pl.pallas_call(
        functools.partial(_combine_kernel, split),
        grid_spec=grid_spec,
        out_shape=out_shape,
        compiler_params=_params(("arbitrary",)),
        name="moe_combine",
    )(*plan["pieces"], y, x1, meta, plan["segrow"], mod)


def _moe_plan(cnt):
    per = TMD // TM
    cnt = cnt[:, 0, :N_EXPERTS].astype(jnp.int32).reshape(NBD, per, N_EXPERTS)
    cpad = (jnp.sum(cnt, axis=1) + CH - 1) // CH * CH
    tot = jnp.sum(cpad, axis=0)
    tiles = (tot + ETILE - 1) // ETILE
    tile_end = jnp.cumsum(tiles)
    offs = (tile_end - tiles) * ETILE
    base = offs[None, :] + jnp.cumsum(cpad, axis=0) - cpad
    seg = jnp.cumsum(cpad, axis=1) - cpad
    segrows = seg[:, None, :] + jnp.cumsum(cnt, axis=1) - cnt
    tid = jnp.arange(N_ETILES, dtype=jnp.int32)
    te = jnp.sum((tile_end[None, :] <= tid[:, None]).astype(jnp.int32), axis=1)
    used = te < N_EXPERTS
    last = jnp.max(jnp.where(tiles > 0, jnp.arange(N_EXPERTS, dtype=jnp.int32), 0))
    te = jnp.where(used, te, last)
    rows = tot[te] - (tid - (tile_end - tiles)[te]) * ETILE
    tvalid = jnp.where(used, jnp.where(rows > ETILE // 2, 2, 1), 0).astype(jnp.int32)
    tfirst = jnp.concatenate([jnp.ones((1,), jnp.int32), (te[1:] != te[:-1]).astype(jnp.int32)])
    eid = jnp.arange(N_EXPERTS, dtype=jnp.int32)
    later = (eid[None, :] > eid[:, None]) & (tiles[None, :] > 0)
    nxt = jnp.min(jnp.where(later, eid[None, :], N_EXPERTS), axis=1)
    nxt = jnp.where(nxt < N_EXPERTS, nxt, -1)
    def piece_lists(counts, local0, global0, rows, max_pieces):
        cend = jnp.cumsum(counts, axis=1)
        cstart = cend - counts
        cid = jnp.arange(max_pieces, dtype=jnp.int32)
        owns = ((cstart[:, None, :] <= cid[None, :, None]) & (cid[None, :, None] < cend[:, None, :]))
        owns = owns.astype(jnp.int32)
        pick = lambda a: jnp.sum(owns * a[:, None, :], axis=2)
        within = (cid[None, :] - pick(cstart)) * rows
        live = jnp.sum(owns, axis=2)
        return [((pick(local0) + within) * live).reshape(-1), ((pick(global0) + within) * live).reshape(-1),
                jnp.sum(counts, axis=1)]

    nbig = cpad // BIG_ROWS
    nsml = (cpad - nbig * BIG_ROWS) // CH
    pieces = (piece_lists(nbig, seg, base, BIG_ROWS, MAX_BIG)
              + piece_lists(nsml, seg + nbig * BIG_ROWS, base + nbig * BIG_ROWS, CH, MAX_SMALL))
    return {
        "tnext": nxt[te], "wslot": (jnp.cumsum(tfirst) - 1) % 2, "tfirst": tfirst, "pieces": pieces,
        "tail0": offs + tot, "tailn": (tiles * ETILE - tot) // CH,
        "segrow": jnp.pad(segrows.astype(F32), ((0, 0), (0, 0), (0, LANES - N_EXPERTS))),
        "te": te, "tvalid": tvalid, "xblk": jnp.where(used, tid, tile_end[-1] - 1),
    }


def _moe(layer, x1, h2, meta, cnt, mod, w_gu, b_gu, w_down, b_down, split_out=False):
    plan = _moe_plan(cnt)
    xs = _dispatch(plan, h2, meta)
    y = _experts(layer, plan, xs, w_gu, b_gu, w_down, b_down)
    return _combine(plan, y, x1, meta, mod, split_out)


_HK, _HV = H_C * DK_C, H_C * DV_C
_ODD_MAIN = 2 * _HK + 2 * _HV


def _odd_in_kernel(x_ref, mod_ref, nmix_ref, win_ref, wgk_ref, bgk_ref, q_ref, k_ref, v_ref, g_ref, la_ref,
                   lamin_ref):
    m = mod_ref[0]
    h = _rms(x_ref[...]) * nmix_ref[...] * (1.0 + m[1:2]) + m[0:1]
    a = _dot(h.astype(BF16), win_ref[...])
    q_ref[...] = a[:, :_HK] * (DK_C ** -0.5)
    k_ref[...] = a[:, _HK:2 * _HK]
    v_ref[...] = a[:, 2 * _HK:2 * _HK + _HV]
    g_ref[...] = a[:, 2 * _HK + _HV:_ODD_MAIN]
    z = _dot(a[:, _ODD_MAIN:].astype(BF16), wgk_ref[...]) + bgk_ref[...]
    la = (jnp.minimum(z, 0.0) - jnp.log(1.0 + jnp.exp(-jnp.abs(z)))) * (1.0 / GATE_TAU)
    la_ref[...] = la
    chunk_tot = jnp.sum(la.reshape(TM // GLA_CHUNK, GLA_CHUNK, 2 * _HK), axis=1)
    lamin_ref[0] = jnp.broadcast_to(jnp.min(chunk_tot, axis=(0, 1), keepdims=True), (SUBLANES, LANES))


def _pairwise_att(q, k, c, forward):
    tcol = lax.broadcasted_iota(jnp.int32, (GLA_CHUNK, 1), 0)
    trow = lax.broadcasted_iota(jnp.int32, (GLA_CHUNK, DK_C), 0)
    scol = lax.broadcasted_iota(jnp.int32, (GLA_CHUNK, GLA_CHUNK), 1)

    def body(s, att):
        k_s = jnp.sum(jnp.where(trow == s, k, 0.0), axis=0, keepdims=True)
        c_s = jnp.sum(jnp.where(trow == s, c, 0.0), axis=0, keepdims=True)
        allowed = (tcol >= s) if forward else (tcol <= s)
        e = jnp.exp(jnp.where(allowed, c - c_s, -jnp.inf))
        return jnp.where(scol == s, jnp.sum(q * k_s * e, axis=-1, keepdims=True), att)

    return lax.fori_loop(0, GLA_CHUNK, body, jnp.zeros((GLA_CHUNK, GLA_CHUNK), F32))


def _gla_kernel(has_init, nchunk, strong_ref, *refs):
    if has_init:
        (qf, kf, vf, laf, qb, kb, vb, lab, s0f, s0b, _, _, of_ref, ob_ref, st) = refs
    else:
        (qf, kf, vf, laf, qb, kb, vb, lab, of_ref, ob_ref, sf_ref, sb_ref, st) = refs
    j = pl.program_id(1)

    @pl.when(j == 0)
    def _():
        if has_init:
            st[0] = s0f[...]
            st[1] = s0b[...]
        else:
            st[...] = jnp.zeros_like(st)

    row = lax.broadcasted_iota(jnp.int32, (GLA_CHUNK, GLA_CHUNK), 0)
    col = lax.broadcasted_iota(jnp.int32, (GLA_CHUNK, GLA_CHUNK), 1)

    def scan_step(factored):
        for d, (q_r, k_r, v_r, la_r, o_r) in enumerate(((qf, kf, vf, laf, of_ref), (qb, kb, vb, lab, ob_ref))):
            keep = (col <= row) if d == 0 else (col >= row)
            tri = jnp.where(keep, 1.0, 0.0).astype(BF16)
            for s in range(GLA_NS):
                g = la_r[s]
                g_hi = g.astype(BF16)
                g_r = g - g_hi.astype(F32)
                g_mid = g_r.astype(BF16)
                g_lo = (g_r - g_mid.astype(F32)).astype(BF16)
                c = _dot(tri, g_hi) + _dot(tri, g_mid) + _dot(tri, g_lo)
                tot = jnp.sum(g, axis=0, keepdims=True)
                decay = jnp.exp(tot)
                q, k = q_r[s], k_r[s]
                qe = (q * jnp.exp(c)).astype(BF16)
                if factored:
                    e_neg = jnp.exp(-c)
                    kd = (k * e_neg).astype(BF16)
                    k2 = (k * (e_neg * decay)).astype(BF16)
                else:
                    k2 = (k * jnp.exp(tot - c)).astype(BF16)
                v = v_r[s].astype(BF16)
                for h in range(H_C):
                    kc = slice(h * DK_C, (h + 1) * DK_C)
                    vc = slice(h * DV_C, (h + 1) * DV_C)
                    if factored:
                        att = jnp.where(keep, _dot_nt(qe[:, kc], kd[:, kc]), 0.0)
                    else:
                        att = _pairwise_att(q[:, kc], k[:, kc], c[:, kc], d == 0)
                    s_t = st[d, s, h]
                    o_r[s, :, vc] = _dot_nt(qe[:, kc], s_t.astype(BF16)) + _dot(att.astype(BF16), v[:, vc])
                    st[d, s, h] = s_t * decay[:, kc] + _dot_tn(v[:, vc], k2[:, kc])

    pl.when(strong_ref[0] == 0)(lambda: scan_step(True))
    pl.when(strong_ref[0] != 0)(lambda: scan_step(False))

    if not has_init:
        @pl.when(j == nchunk - 1)
        def _():
            sf_ref[...] = st[0]
            sb_ref[...] = st[1]


def _gla_call(has_init, nseq, seqlen, row0, strong, q, k, v, la, prev_f=None, prev_b=None, s0f=None, s0b=None):
    nchunk = seqlen // GLA_CHUNK
    nview = T // seqlen
    g0 = row0 // seqlen // GLA_NS
    view = lambda a: a.reshape(nview, seqlen, a.shape[-1])
    fwd = lambda g, j, *_: (g0 + g, j, 0)
    bwd = lambda g, j, *_: (g0 + g, nchunk - 1 - j, 0)
    bwd_la = lambda g, j, *_: (g0 + g, nchunk - 1 - j, 1)
    blk = lambda w, m: pl.BlockSpec((GLA_NS, GLA_CHUNK, w), m)
    state_spec = pl.BlockSpec((GLA_NS, H_C, DV_C, DK_C), lambda g, j, *_: (g, 0, 0, 0))
    in_specs = [blk(_HK, fwd), blk(_HK, fwd), blk(_HV, fwd), blk(_HK, fwd),
                blk(_HK, bwd), blk(_HK, bwd), blk(_HV, bwd), blk(_HK, bwd_la)]
    args = [view(q), view(k), view(v), view(la)] * 2
    aliases = {}
    if has_init:
        in_specs += [state_spec, state_spec] + [pl.BlockSpec(memory_space=pl.ANY)] * 2
        args += [s0f, s0b, view(prev_f), view(prev_b)]
        aliases = {len(args) - 1: 0, len(args): 1}
    out_specs = [blk(_HV, fwd), blk(_HV, bwd)]
    out_shape = [jax.ShapeDtypeStruct((nview, seqlen, _HV), F32)] * 2
    if not has_init:
        out_specs += [state_spec, state_spec]
        out_shape += [jax.ShapeDtypeStruct((nseq, H_C, DV_C, DK_C), F32)] * 2
    grid_spec = pltpu.PrefetchScalarGridSpec(
        num_scalar_prefetch=1,
        grid=(nseq // GLA_NS, nchunk),
        in_specs=in_specs,
        out_specs=out_specs,
        scratch_shapes=[pltpu.VMEM((2, GLA_NS, H_C, DV_C, DK_C), F32)],
    )
    outs = pl.pallas_call(
        functools.partial(_gla_kernel, has_init, nchunk),
        grid_spec=grid_spec,
        out_shape=out_shape,
        input_output_aliases=aliases,
        compiler_params=_params(("arbitrary", "arbitrary")),
        name="gla_sample" if has_init else "gla_prompt",
    )(strong, *args)
    return [outs[0].reshape(T, _HV), outs[1].reshape(T, _HV)] + list(outs[2:])


def _rope_tables():
    half = ROPE // 2
    inv_freq = np.power(np.float32(ROPE_THETA), -np.arange(0, half, 2, dtype=np.float32) / np.float32(half))
    n = np.arange(SAMPLE_LEN)
    row = (n // GRID_W).astype(np.float32)
    col = (n % GRID_W).astype(np.float32)
    ang_r = (row[:, None] * inv_freq[None, :]).astype(np.float32)
    ang_c = (col[:, None] * inv_freq[None, :]).astype(np.float32)
    nf = half // 2
    c = np.ones((TM + SAMPLE_LEN, LANES), np.float32)
    s1 = np.zeros((TM + SAMPLE_LEN, LANES), np.float32)
    s2 = np.zeros((TM + SAMPLE_LEN, LANES), np.float32)
    for base, ang in ((NOPE, ang_r), (NOPE + half, ang_c)):
        c[TM:, base:base + nf] = np.cos(ang)
        c[TM:, base + nf:base + half] = np.cos(ang)
        s1[TM:, base:base + nf] = -np.sin(ang)
        s2[TM:, base + nf:base + half] = np.sin(ang)
    return jnp.asarray(c), jnp.asarray(s1), jnp.asarray(s2)


def _pad_heads(w, nheads, width, lo=0):
    k = w.shape[0]
    w = w.reshape(k, nheads, width)
    w = jnp.pad(w, ((0, 0), (0, 0), (lo, HP - lo - width)))
    return w.reshape(k, nheads * HP)


def _row128(v, lo=0):
    return jnp.pad(v, (lo, LANES - lo - v.shape[0])).reshape(1, LANES)


def _even_layer(xs, xp, mod, nmix, w_in, q_a_norm, w_uq, q_norm, kv_a_norm, w_ukv, k_norm, v_norm, w_s, b_s, w_out,
                cache_ckv, cache_kpe):
    s = np.cumsum([Q_LORA, KV_LORA, ROPE, W_B])
    w_q, w_ckv, w_kpe, w_u, w_v = (w_in[:, :s[0]], w_in[:, s[0]:s[1]], w_in[:, s[1]:s[2]], w_in[:, s[2]:s[3]],
                                   w_in[:, s[3]:])
    w_kpe = jnp.pad(w_kpe, ((0, 0), (NOPE, LANES - QK_DIM)))
    win = jnp.concatenate([w_q, w_ckv, w_u, w_v, w_kpe], axis=1).astype(BF16)
    wuq = _pad_heads(w_uq, H_A, QK_DIM).astype(BF16)
    ukv = w_ukv.reshape(KV_LORA, H_A, NOPE + V_A)
    wuk = _pad_heads(ukv[:, :, :NOPE].reshape(KV_LORA, H_A * NOPE), H_A, NOPE)
    wuv = _pad_heads(ukv[:, :, NOPE:].reshape(KV_LORA, H_A * V_A), H_A, V_A)
    wukv = jnp.concatenate([wuk, wuv], axis=1).astype(BF16)
    qgain = _row128(q_norm * (QK_DIM ** -0.5 * LOG2_E))
    kgain = _row128(k_norm)
    bias = b_s.reshape(G_B // 2, 2, CHUNK_B)
    bias = jnp.concatenate([jnp.broadcast_to(bias[:, 0, :, None], (G_B // 2, CHUNK_B, C_B)),
                            jnp.broadcast_to(bias[:, 1, :, None], (G_B // 2, CHUNK_B, C_B))], axis=-1)
    rc, rs1, rs2 = _rope_tables()
    rope_spec = pl.BlockSpec((TM, LANES), lambda i: (_rope_blk(i), 0))
    width = H_A * HP
    q, k, v, ob, ckv, kpe = pl.pallas_call(
        _even_in_kernel,
        grid=(NBLK,),
        in_specs=[
            _X_SAMPLE, _X_PROMPT, _MOD, _full((1, D)), _full((D, _WIN_N)), _full((1, Q_LORA)),
            _full((Q_LORA, width)),
            _full((1, LANES)), _full((1, KV_LORA)), _full((KV_LORA, 2 * width)), _full((1, LANES)),
            _full((1, W_B)), _full((G_B // 2, 2 * CHUNK_B, CHUNK_B)), _full((G_B // 2, CHUNK_B, LANES)),
            rope_spec, rope_spec, rope_spec,
        ],
        out_specs=[_TOK(width), _TOK(width), _TOK(width), _TOK(W_B), _TOK(KV_LORA), _TOK(ROPE)],
        out_shape=[
            jax.ShapeDtypeStruct((T, width), BF16), jax.ShapeDtypeStruct((T, width), BF16),
            jax.ShapeDtypeStruct((T, width), BF16), jax.ShapeDtypeStruct((T, W_B), BF16),
            jax.ShapeDtypeStruct((T, KV_LORA), F32), jax.ShapeDtypeStruct((T, ROPE), F32),
        ],
        compiler_params=_params(("arbitrary",)),
        name="even_in",
    )(xs, xp, mod, nmix.reshape(1, D), win, q_a_norm.reshape(1, Q_LORA), wuq, qgain, kv_a_norm.reshape(1, KV_LORA),
      wukv, kgain, v_norm.reshape(1, W_B), w_s.astype(BF16).reshape(G_B // 2, 2 * CHUNK_B, CHUNK_B), bias, rc, rs1,
      rs2)

    n_ctx = N_SAMPLE_SEQ * PAST_LEN
    kpe_ctx = jnp.pad(cache_kpe.reshape(n_ctx, ROPE), ((0, 0), (NOPE, LANES - QK_DIM)))
    k_ctx, v_ctx = pl.pallas_call(
        _ctx_kv_kernel,
        grid=(n_ctx // TM,),
        in_specs=[_TOK(KV_LORA), _TOK(LANES), _full((KV_LORA, 2 * width)), _full((1, LANES))],
        out_specs=[_TOK(width), _TOK(width)],
        out_shape=[jax.ShapeDtypeStruct((n_ctx, width), BF16)] * 2,
        compiler_params=_params(("arbitrary",)),
        name="ctx_kv",
    )(cache_ckv.reshape(n_ctx, KV_LORA), kpe_ctx, wukv, kgain)

    oa = _attention(q, k, v, k_ctx, v_ctx)
    woa = jnp.pad(w_out[:H_A * V_A].reshape(H_A, V_A, D), ((0, 0), (0, HP - V_A), (0, 0))).reshape(width, D)
    return oa, ob, woa.astype(BF16), w_out[H_A * V_A:].astype(BF16), ckv, kpe


def kernel(x_prompt, x_sample, cache_mla_ckv, cache_mla_kpe, state_gla_fwd, state_gla_bwd, c, c_ctx, ada_w, ada_b,
           norm_mix, norm_ffn, even_w_in, mla_q_a_norm, mla_w_uq, mla_q_norm, mla_kv_a_norm, mla_w_ukv, mla_k_norm,
           cmlp_v_norm, cmlp_w_s, cmlp_b_s, even_w_out, odd_w_in, gla_w_gk_fwd, gla_b_gk_fwd, gla_w_gk_bwd,
           gla_b_gk_bwd, gla_o_norm, odd_w_out, moe_w_router, moe_b_router, moe_w_gu, moe_b_gu, moe_w_down,
           moe_b_down):
    xs0, xp0 = x_sample.reshape(N_SAMPLE, D), x_prompt.reshape(N_PROMPT, D)
    cond8 = jnp.concatenate([c_ctx[None, :], c, jnp.zeros((SUBLANES - 1 - N_SAMPLE_SEQ, D), F32)], axis=0)
    mods = _adaln(cond8, ada_w, ada_b)
    wr = jnp.pad(moe_w_router, ((0, 0), (0, 0), (0, LANES - N_EXPERTS)))
    wr_hi = wr.astype(BF16)
    wr = jnp.concatenate([wr_hi, (wr - wr_hi.astype(F32)).astype(BF16)], axis=-1)
    br = jnp.pad(moe_b_router, ((0, 0), (0, LANES - N_EXPERTS))).reshape(2, 1, LANES)

    oa, ob, woa, wob, ckv, kpe = _even_layer(
        xs0, xp0, mods[0], norm_mix[0], even_w_in[0], mla_q_a_norm[0], mla_w_uq[0], mla_q_norm[0], mla_kv_a_norm[0],
        mla_w_ukv[0], mla_k_norm[0], cmlp_v_norm[0], cmlp_w_s[0], cmlp_b_s[0], even_w_out[0],
        cache_mla_ckv[:, 0], cache_mla_kpe[:, 0])
    width = H_A * HP
    x1, h2, meta, cnt = pl.pallas_call(
        _even_out_kernel,
        grid=(NBLK,),
        in_specs=[_TOK(width), _TOK(W_B), _X_SAMPLE, _X_PROMPT, _MOD, _full((width, D)), _full((W_B, D)),
                  _full((1, D)),
                  _full((D, 2 * LANES)), _full((1, LANES))],
        out_specs=_PROLOGUE_OUT_SPECS,
        out_shape=_PROLOGUE_OUT_SHAPE,
        compiler_params=_params(("arbitrary",)),
        name="even_out",
    )(oa, ob, xs0, xp0, mods[0], woa, wob, norm_ffn[0].reshape(1, D), wr[0], br[0])
    x2 = _moe(0, x1, h2, meta, cnt, mods[0], moe_w_gu, moe_b_gu, moe_w_down, moe_b_down)

    w_in = odd_w_in[0]
    win = jnp.concatenate([w_in, jnp.zeros((D, LANES - 2 * GATE_RANK), F32)], axis=1).astype(BF16)
    wgk = jnp.zeros((LANES, 2 * _HK), F32)
    wgk = wgk.at[:GATE_RANK, :_HK].set(gla_w_gk_fwd[0]).at[GATE_RANK:2 * GATE_RANK, _HK:].set(gla_w_gk_bwd[0])
    bgk = jnp.concatenate([gla_b_gk_fwd[0], gla_b_gk_bwd[0]]).reshape(1, 2 * _HK)
    q, k, v, g, la, lamin = pl.pallas_call(
        _odd_in_kernel,
        grid=(NBLK,),
        in_specs=[_TOK(D), _MOD, _full((1, D)), _full((D, _ODD_MAIN + LANES)), _full((LANES, 2 * _HK)),
                  _full((1, 2 * _HK))],
        out_specs=[_TOK(_HK), _TOK(_HK), _TOK(_HV), _TOK(_HV), _TOK(2 * _HK), _TILE_ROW],
        out_shape=[jax.ShapeDtypeStruct((T, _HK), F32), jax.ShapeDtypeStruct((T, _HK), F32),
                   jax.ShapeDtypeStruct((T, _HV), F32), jax.ShapeDtypeStruct((T, _HV), F32),
                   jax.ShapeDtypeStruct((T, 2 * _HK), F32), jax.ShapeDtypeStruct((NBLK, SUBLANES, LANES), F32)],
        compiler_params=_params(("arbitrary",)),
        name="odd_in",
    )(x2, mods[1], norm_mix[1].reshape(1, D), win, wgk.astype(BF16), bgk)

    strong = (jnp.min(lamin) < -GLA_SAFE_LOG_DECAY).astype(jnp.int32).reshape(1)
    of, obk, st_f, st_b = _gla_call(False, N_PROMPT_SEQ, PROMPT_LEN, N_SAMPLE, strong, q, k, v, la)
    s0f = state_gla_fwd[:, 0].transpose(0, 1, 3, 2)
    s0b = state_gla_bwd[:, 0].transpose(0, 1, 3, 2)
    of, obk = _gla_call(True, N_SAMPLE_SEQ, SAMPLE_LEN, 0, strong, q, k, v, la, of, obk, s0f, s0b)

    x3, h2, meta, cnt = pl.pallas_call(
        _odd_out_kernel,
        grid=(NBLK,),
        in_specs=[_TOK(_HV), _TOK(_HV), _TOK(_HV), _TOK(D), _MOD, _full((1, DV_C)), _full((_HV, D)),
                  _full((1, D)), _full((D, 2 * LANES)), _full((1, LANES))],
        out_specs=_PROLOGUE_OUT_SPECS,
        out_shape=_PROLOGUE_OUT_SHAPE,
        compiler_params=_params(("arbitrary",)),
        name="odd_out",
    )(of, obk, g, x2, mods[1], gla_o_norm[0].reshape(1, DV_C), odd_w_out[0].astype(BF16),
      norm_ffn[1].reshape(1, D), wr[1], br[1])
    ys, yp = _moe(1, x3, h2, meta, cnt, mods[1], moe_w_gu, moe_b_gu, moe_w_down, moe_b_down, split_out=True)

    y_sample = ys.reshape(N_SAMPLE_SEQ, SAMPLE_LEN, D)
    y_prompt = yp.reshape(N_PROMPT_SEQ, PROMPT_LEN, D)
    new_ckv = ckv[N_SAMPLE:].reshape(N_PROMPT_SEQ, 1, PROMPT_LEN, KV_LORA)
    new_kpe = kpe[N_SAMPLE:].reshape(N_PROMPT_SEQ, 1, PROMPT_LEN, ROPE)
    new_fwd = st_f.transpose(0, 1, 3, 2)[:, None]
    new_bwd = st_b.transpose(0, 1, 3, 2)[:, None]
    return (y_prompt, y_sample, new_ckv, new_kpe, new_fwd, new_bwd)
```

```python
import functools

import numpy as np
import jax
import jax.numpy as jnp
from jax import lax
from jax.experimental import pallas as pl
from jax.experimental.pallas import tpu as pltpu

F32 = jnp.float32
BF16 = jnp.bfloat16

D = 1024
N_PROMPT_SEQ, PROMPT_LEN = 16, 256
N_SAMPLE_SEQ, SAMPLE_LEN = 4, 2048
PAST_LEN = 512
N_PROMPT = N_PROMPT_SEQ * PROMPT_LEN
N_SAMPLE = N_SAMPLE_SEQ * SAMPLE_LEN
T = N_PROMPT + N_SAMPLE
EPS = 1e-6
GRID_W = 64
H_A, Q_LORA, KV_LORA, NOPE, ROPE, V_A = 8, 512, 256, 64, 32, 64
QK_DIM = NOPE + ROPE
G_B, C_B, W_B, CHUNK_B = 8, 64, 512, 128
H_C, DK_C, DV_C, GATE_RANK, GATE_TAU = 4, 128, 256, 16, 16.0
GLA_CHUNK = 128
N_EXPERTS, TOP_K, D_FF = 32, 4, 1024
SWIGLU_LIMIT, SWIGLU_ALPHA = 7.0, 1.702
ROPE_THETA = 10000.0
LOG2_E = 1.4426950408889634

LANES = 128
SUBLANES = 8
VMEM_LIMIT = 56 * 1024 * 1024

TM = 512
NBLK = T // TM
SAMPLE_BLKS = N_SAMPLE // TM
BLKS_PER_SAMPLE_SEQ = SAMPLE_LEN // TM
ATT_TQ = 512
ATT_VMEM_LIMIT = 60 * 1024 * 1024
GLA_NS = 4
GLA_SAFE_LOG_DECAY = 60.0
ETILE = 512
TMD = TM
NBD = T // TMD
NBD_SAMPLE = N_SAMPLE // TMD
CH = 2 * SUBLANES
_MAX_LOCAL = TMD * TOP_K + N_EXPERTS * (CH - 1)
_POS_BLOCK = 256
ROWS_L = -(-_MAX_LOCAL // _POS_BLOCK) * _POS_BLOCK
BIG_ROWS = 4 * CH
MAX_BIG = ROWS_L // BIG_ROWS
MAX_SMALL = N_EXPERTS * (BIG_ROWS // CH - 1)
N_ETILES = -(-NBD * _MAX_LOCAL // ETILE) + N_EXPERTS
P_ROWS = N_ETILES * ETILE
HP = LANES


def _cond_row(i):
    return jnp.where(i < SAMPLE_BLKS, 1 + i // BLKS_PER_SAMPLE_SEQ, 0)


def _rope_blk(i):
    return jnp.where(i < SAMPLE_BLKS, 1 + i % BLKS_PER_SAMPLE_SEQ, 0)


def _rms(x):
    return x * lax.rsqrt(jnp.mean(x * x, axis=-1, keepdims=True) + EPS)


def _gelu(x):
    return 0.5 * x * (1.0 + jnp.tanh(0.7978845608028654 * (x + 0.044715 * (x * x * x))))


def _silu(x):
    return x * jax.nn.sigmoid(x)


def _dot(a, b):
    return jnp.dot(a, b, preferred_element_type=F32)


def _dot_nt(a, b):
    return lax.dot_general(a, b, (((1,), (1,)), ((), ())), preferred_element_type=F32)


def _dot_tn(a, b):
    return lax.dot_general(a, b, (((0,), (0,)), ((), ())), preferred_element_type=F32)


def _params(sem, vmem=VMEM_LIMIT):
    return pltpu.CompilerParams(dimension_semantics=sem, vmem_limit_bytes=vmem)


def _full(shape):
    nd = len(shape)
    return pl.BlockSpec(shape, lambda *_: (0,) * nd)


ADA_TN = 1536


def _adaln_kernel(c_ref, w_ref, b_ref, o_ref):
    s = _silu(c_ref[...]).astype(BF16)
    o_ref[0] = _dot(s, w_ref[0].astype(BF16)) + b_ref[0]


def _adaln(cond8, ada_w, ada_b):
    depth = ada_w.shape[0]
    n = ada_w.shape[2]
    out = pl.pallas_call(
        _adaln_kernel,
        grid=(depth, n // ADA_TN),
        in_specs=[
            pl.BlockSpec((SUBLANES, D), lambda l, j: (0, 0)),
            pl.BlockSpec((1, D, ADA_TN), lambda l, j: (l, 0, j)),
            pl.BlockSpec((1, 1, ADA_TN), lambda l, j: (l, 0, j)),
        ],
        out_specs=pl.BlockSpec((1, SUBLANES, ADA_TN), lambda l, j: (l, 0, j)),
        out_shape=jax.ShapeDtypeStruct((depth, SUBLANES, n), F32),
        compiler_params=_params(("arbitrary", "arbitrary")),
        name="adaln",
    )(cond8, ada_w, ada_b.reshape(depth, 1, n))
    return out.reshape(depth, SUBLANES, 6, D)


_QC0, _CKV0, _U0, _V0, _KPE0, _WIN_N = 0, 512, 768, 1280, 1792, 1920


def _rope(y, c, s1, s2):
    return y * c + pltpu.roll(y, LANES - 8, 1) * s1 + pltpu.roll(y, 8, 1) * s2


def _k_heads(k_raw, kpe128, kp_rot, kgain, k_ref):
    sskpe = jnp.sum(kpe128 * kpe128, axis=-1, keepdims=True)
    for h in range(H_A):
        kb = k_raw[:, h * HP:(h + 1) * HP]
        r = lax.rsqrt((jnp.sum(kb * kb, axis=-1, keepdims=True) + sskpe) * (1.0 / QK_DIM) + EPS)
        k_ref[:, h * HP:(h + 1) * HP] = ((kb * kgain + kp_rot) * r).astype(BF16)


def _input_rows(xs_ref, xp_ref):
    return jnp.where(pl.program_id(0) < SAMPLE_BLKS, xs_ref[...], xp_ref[...])


_X_SAMPLE = pl.BlockSpec((TM, D), lambda i, *_: (jnp.minimum(i, SAMPLE_BLKS - 1), 0))
_X_PROMPT = pl.BlockSpec((TM, D), lambda i, *_: (jnp.maximum(i - SAMPLE_BLKS, 0), 0))


def _even_in_kernel(xs_ref, xp_ref, mod_ref, nmix_ref, win_ref, qan_ref, wuq_ref, qgain_ref, kvan_ref, wukv_ref,
                    kgain_ref, vnorm_ref, ws_ref, bs_ref, rc_ref, rs1_ref, rs2_ref,
                    q_ref, k_ref, v_ref, ob_ref, ckv_ref, kpe_ref):
    m = mod_ref[0]
    h = _rms(_input_rows(xs_ref, xp_ref)) * nmix_ref[...] * (1.0 + m[1:2]) + m[0:1]
    a = _dot(h.astype(BF16), win_ref[...])
    qc = a[:, _QC0:_CKV0]
    ckv = a[:, _CKV0:_U0]
    u = a[:, _U0:_V0]
    vv = a[:, _V0:_KPE0]
    kpe128 = a[:, _KPE0:_WIN_N]

    ckv_n = _rms(ckv) * kvan_ref[...]
    ckv_ref[...] = ckv_n
    kpe_ref[...] = kpe128[:, NOPE:QK_DIM]

    rc, rs1, rs2 = rc_ref[...], rs1_ref[...], rs2_ref[...]
    qn = (_rms(qc) * qan_ref[...]).astype(BF16)
    qr = _dot(qn, wuq_ref[...])
    qgain = qgain_ref[...]
    for hh in range(H_A):
        blk = qr[:, hh * HP:(hh + 1) * HP]
        r = lax.rsqrt(jnp.sum(blk * blk, axis=-1, keepdims=True) * (1.0 / QK_DIM) + EPS)
        q_ref[:, hh * HP:(hh + 1) * HP] = _rope(blk * r * qgain, rc, rs1, rs2).astype(BF16)

    kv = _dot(ckv_n.astype(BF16), wukv_ref[...])
    v_ref[...] = kv[:, H_A * HP:].astype(BF16)
    kgain = kgain_ref[...]
    kp_rot = _rope(kpe128 * kgain, rc, rs1, rs2)
    _k_heads(kv[:, :H_A * HP], kpe128, kp_rot, kgain, k_ref)

    ug = _gelu(u)
    vn = (_rms(_gelu(vv)) * vnorm_ref[...]).astype(BF16)
    low = lax.broadcasted_iota(jnp.int32, (CHUNK_B, LANES), 1) < C_B
    for c in range(TM // CHUNK_B):
        rows = slice(c * CHUNK_B, (c + 1) * CHUNK_B)
        for p in range(G_B // 2):
            cols = slice(p * LANES, (p + 1) * LANES)
            blk = vn[rows, cols]
            both = _dot(ws_ref[p], blk)
            mixed = jnp.where(low, both[:CHUNK_B], both[CHUNK_B:]) + bs_ref[p]
            ob_ref[rows, cols] = (ug[rows, cols] * mixed).astype(BF16)


def _ctx_kv_kernel(ckv_ref, kpe_ref, wukv_ref, kgain_ref, k_ref, v_ref):
    kv = _dot(ckv_ref[...].astype(BF16), wukv_ref[...])
    v_ref[...] = kv[:, H_A * HP:].astype(BF16)
    kgain = kgain_ref[...]
    kpe128 = kpe_ref[...]
    _k_heads(kv[:, :H_A * HP], kpe128, kpe128 * kgain, kgain, k_ref)


def _attn_self_kernel(q_ref, k_ref, v_ref, o_ref):
    for h in range(H_A):
        cols = slice(h * HP, (h + 1) * HP)
        s = _dot_nt(q_ref[:, cols], k_ref[:, cols])
        p = jnp.exp2(s - jnp.max(s, axis=-1, keepdims=True))
        inv = 1.0 / jnp.sum(p, axis=-1, keepdims=True)
        o_ref[:, cols] = (_dot(p.astype(BF16), v_ref[:, cols]) * inv).astype(BF16)


def _attn_ctx_kernel(q_ref, k_ref, v_ref, kc_ref, vc_ref, prev_ref, o_ref):
    del prev_ref
    for h in range(H_A):
        cols = slice(h * HP, (h + 1) * HP)
        q = q_ref[:, cols]
        s1 = _dot_nt(q, k_ref[:, cols])
        s2 = _dot_nt(q, kc_ref[:, cols])
        mx = jnp.maximum(jnp.max(s1, axis=-1, keepdims=True), jnp.max(s2, axis=-1, keepdims=True))
        p1 = jnp.exp2(s1 - mx)
        p2 = jnp.exp2(s2 - mx)
        inv = 1.0 / (jnp.sum(p1, axis=-1, keepdims=True) + jnp.sum(p2, axis=-1, keepdims=True))
        o = _dot(p1.astype(BF16), v_ref[:, cols]) + _dot(p2.astype(BF16), vc_ref[:, cols])
        o_ref[:, cols] = (o * inv).astype(BF16)


def _attention(q, k, v, k_ctx, v_ctx):
    width = H_A * HP
    first = N_SAMPLE // PROMPT_LEN
    o = pl.pallas_call(
        _attn_self_kernel,
        grid=(N_PROMPT_SEQ,),
        in_specs=[pl.BlockSpec((PROMPT_LEN, width), lambda i: (first + i, 0))] * 3,
        out_specs=pl.BlockSpec((PROMPT_LEN, width), lambda i: (first + i, 0)),
        out_shape=jax.ShapeDtypeStruct((T, width), BF16),
        compiler_params=_params(("arbitrary",)),
        name="attn_prompt",
    )(q, k, v)
    qblocks = SAMPLE_LEN // ATT_TQ
    qblk = lambda b, j: (b * qblocks + j, 0)
    return pl.pallas_call(
        _attn_ctx_kernel,
        grid=(N_SAMPLE_SEQ, qblocks),
        in_specs=[
            pl.BlockSpec((ATT_TQ, width), qblk),
            pl.BlockSpec((SAMPLE_LEN, width), lambda b, j: (b, 0)),
            pl.BlockSpec((SAMPLE_LEN, width), lambda b, j: (b, 0)),
            pl.BlockSpec((PAST_LEN, width), lambda b, j: (b, 0)),
            pl.BlockSpec((PAST_LEN, width), lambda b, j: (b, 0)),
            pl.BlockSpec(memory_space=pl.ANY),
        ],
        out_specs=pl.BlockSpec((ATT_TQ, width), qblk),
        out_shape=jax.ShapeDtypeStruct((T, width), BF16),
        input_output_aliases={5: 0},
        compiler_params=_params(("arbitrary", "arbitrary"), ATT_VMEM_LIMIT),
        name="attn_sample",
    )(q, k, v, k_ctx, v_ctx, o)


_META_IDX, _META_RANK, _META_W = 0, TOP_K, 2 * TOP_K


def _moe_prologue(x1, m, nffn_ref, wr_ref, br_ref, x1_ref, h2_ref, meta_ref, cnt_ref):
    x1_ref[...] = x1
    h2 = _rms(x1) * nffn_ref[...] * (1.0 + m[4:5]) + m[3:4]
    h2_ref[...] = h2.astype(BF16)
    lane = lax.broadcasted_iota(jnp.int32, (TM, LANES), 1)
    lanef = lane.astype(F32)
    h_hi = h2.astype(BF16)
    h_lo = (h2 - h_hi.astype(F32)).astype(BF16)
    r = _dot(h_hi, wr_ref[...])
    logits = r[:, :LANES] + r[:, LANES:] + _dot(h_lo, wr_ref[:, :LANES]) + br_ref[...]
    work = jnp.where(lane < N_EXPERTS, logits, -jnp.inf)
    hots, vals = [], []
    for _ in range(TOP_K):
        mx = jnp.max(work, axis=-1, keepdims=True)
        idx = jnp.min(jnp.where(work == mx, lanef, float(LANES)), axis=-1, keepdims=True)
        hot = lanef == idx
        work = jnp.where(hot, -jnp.inf, work)
        hots.append((hot, idx))
        vals.append(mx)
    es = [jnp.exp(v - vals[0]) for v in vals]
    inv = 1.0 / (es[0] + es[1] + es[2] + es[3])
    sel = jnp.zeros((TM, LANES), F32)
    for hot, _ in hots:
        sel = jnp.where(hot, 1.0, sel)
    row = lax.broadcasted_iota(jnp.int32, (TM, TM), 0)
    col = lax.broadcasted_iota(jnp.int32, (TM, TM), 1)
    strict = jnp.where(row > col, 1.0, 0.0).astype(BF16)
    before = _dot(strict, sel.astype(BF16))
    meta = jnp.zeros((TM, LANES), F32)
    for kk, (hot, idx) in enumerate(hots):
        rank = jnp.sum(jnp.where(hot, before, 0.0), axis=-1, keepdims=True)
        meta = jnp.where(lane == _META_IDX + kk, idx, meta)
        meta = jnp.where(lane == _META_RANK + kk, rank, meta)
        meta = jnp.where(lane == _META_W + kk, es[kk] * inv, meta)
    meta_ref[...] = meta
    cnt_ref[0] = jnp.broadcast_to(jnp.sum(sel, axis=0, keepdims=True), (SUBLANES, LANES))


def _even_out_kernel(oa_ref, ob_ref, xs_ref, xp_ref, mod_ref, woa_ref, wob_ref, nffn_ref, wr_ref, br_ref,
                     x1_ref, h2_ref, meta_ref, cnt_ref):
    m = mod_ref[0]
    out = _dot(oa_ref[...], woa_ref[...]) + _dot(ob_ref[...], wob_ref[...])
    x1 = _input_rows(xs_ref, xp_ref) + m[2:3] * out
    _moe_prologue(x1, m, nffn_ref, wr_ref, br_ref, x1_ref, h2_ref, meta_ref, cnt_ref)


def _odd_out_kernel(of_ref, ob_ref, g_ref, x_ref, mod_ref, onorm_ref, wo_ref, nffn_ref, wr_ref, br_ref,
                    x1_ref, h2_ref, meta_ref, cnt_ref):
    m = mod_ref[0]
    onorm = onorm_ref[...]
    parts = []
    for h in range(H_C):
        cols = slice(h * DV_C, (h + 1) * DV_C)
        o = of_ref[:, cols] + ob_ref[:, cols]
        parts.append((_rms(o) * onorm * _silu(g_ref[:, cols])).astype(BF16))
    out = _dot(jnp.concatenate(parts, axis=-1), wo_ref[...])
    x1 = x_ref[...] + m[2:3] * out
    _moe_prologue(x1, m, nffn_ref, wr_ref, br_ref, x1_ref, h2_ref, meta_ref, cnt_ref)


_TOK = lambda w: pl.BlockSpec((TM, w), lambda i, *_: (i, 0))
_MOD = pl.BlockSpec((1, 6, D), lambda i, *_: (_cond_row(i), 0, 0))
_TILE_ROW = pl.BlockSpec((1, SUBLANES, LANES), lambda i, *_: (i, 0, 0))

_PROLOGUE_OUT_SPECS = [_TOK(D), _TOK(D), _TOK(LANES), _TILE_ROW]
_PROLOGUE_OUT_SHAPE = [
    jax.ShapeDtypeStruct((T, D), F32),
    jax.ShapeDtypeStruct((T, D), BF16),
    jax.ShapeDtypeStruct((T, LANES), F32),
    jax.ShapeDtypeStruct((NBLK, SUBLANES, LANES), F32),
]


def _local_positions(meta, seg_rows):
    lanef = lax.broadcasted_iota(jnp.int32, (TM, LANES), 1).astype(F32)
    pos = []
    for k in range(TOP_K):
        halves = []
        for half in range(TMD // TM):
            m = meta[half * TM:(half + 1) * TM]
            hot = lanef == m[:, _META_IDX + k:_META_IDX + k + 1]
            start = jnp.sum(jnp.where(hot, seg_rows[half:half + 1], 0.0), axis=-1, keepdims=True)
            halves.append(start + m[:, _META_RANK + k:_META_RANK + k + 1])
        pos.append(jnp.concatenate(halves, axis=0))
    return pos


def _scatter_matrix(pos, vals):
    lane = lax.broadcasted_iota(jnp.int32, (TMD, _POS_BLOCK), 1).astype(F32).astype(BF16)
    blocks = []
    for j in range(ROWS_L // _POS_BLOCK):
        acc = jnp.zeros((TMD, _POS_BLOCK), BF16)
        for p, v in zip(pos, vals):
            acc = jnp.where(lane == (p - float(j * _POS_BLOCK)).astype(BF16), v, acc)
        blocks.append(acc)
    return jnp.concatenate(blocks, axis=1)


def _chunk_copies(t, lists, make, two_queues=False):
    for kind, (rows, (src_ref, dst_ref, n_ref, stride)) in enumerate(zip((BIG_ROWS, CH), lists)):
        def one(j, c, rows=rows, src_ref=src_ref, dst_ref=dst_ref, stride=stride, prio=kind if two_queues else 0):
            q = t * stride + j
            make(pl.multiple_of(src_ref[q], CH), pl.multiple_of(dst_ref[q], CH), rows).start(priority=prio)
            return c

        lax.fori_loop(0, n_ref[t], one, 0)


def _drain_tile(t, lists, wait_copy):
    for rows, (_, _, n_ref, _) in zip((BIG_ROWS, CH), lists):
        _drain(n_ref[t], wait_copy(rows))


def _drain(count, chunk_copy):
    def one(j, c):
        chunk_copy.wait()
        return c

    lax.fori_loop(0, count, one, 0)


def _piece_sem(rows, slot):
    return slot if rows == BIG_ROWS else 2 + slot


def _dispatch_kernel(bsrc, bdst, nbig, ssrc, sdst, nsml, tail0_ref, tailn_ref, h_ref, meta_ref, segrow_ref,
                     xs_hbm, buf, zbuf, sem):
    i = pl.program_id(0)
    slot = i % 2
    lists = ((bsrc, bdst, nbig, MAX_BIG), (ssrc, sdst, nsml, MAX_SMALL))
    wait_copy = lambda s: lambda rows: pltpu.make_async_copy(
        buf.at[s, pl.ds(0, rows)], xs_hbm.at[pl.ds(0, rows)], sem.at[_piece_sem(rows, s)])

    @pl.when(i == 0)
    def _():
        zbuf[...] = jnp.zeros_like(zbuf)

        def per_expert(e, carry):
            def one(j, c):
                pltpu.make_async_copy(zbuf, xs_hbm.at[pl.ds(pl.multiple_of(tail0_ref[e] + j * CH, CH), CH)],
                                      sem.at[4]).start()
                return c

            lax.fori_loop(0, tailn_ref[e], one, 0)
            return carry + tailn_ref[e]

        total = lax.fori_loop(0, N_EXPERTS, per_expert, 0)
        _drain(total, pltpu.make_async_copy(zbuf, xs_hbm.at[pl.ds(0, CH)], sem.at[4]))

    pos = _local_positions(meta_ref[...], segrow_ref[0])
    riota = lax.broadcasted_iota(jnp.int32, (TMD, ROWS_L), 1).astype(F32)
    pt = jnp.zeros((TMD, ROWS_L), F32)
    for p in pos:
        pt = jnp.where(riota == p, 1.0, pt)
    buf[slot] = _dot_tn(pt.astype(BF16), h_ref[...]).astype(BF16)

    make = lambda s, d, rows: pltpu.make_async_copy(buf.at[slot, pl.ds(s, rows)], xs_hbm.at[pl.ds(d, rows)],
                                                    sem.at[_piece_sem(rows, slot)])
    _chunk_copies(i, lists, make, two_queues=True)

    @pl.when(i > 0)
    def _():
        _drain_tile(i - 1, lists, wait_copy(1 - slot))

    @pl.when(i == NBD - 1)
    def _():
        _drain_tile(i, lists, wait_copy(slot))


_TOKD = lambda w: pl.BlockSpec((TMD, w), lambda i, *_: (i, 0))
_SEGROWS = pl.BlockSpec((1, TMD // TM, LANES), lambda i, *_: (i, 0, 0))


def _dispatch(plan, h2, meta):
    grid_spec = pltpu.PrefetchScalarGridSpec(
        num_scalar_prefetch=8,
        grid=(NBD,),
        in_specs=[_TOKD(D), _TOKD(LANES), _SEGROWS],
        out_specs=pl.BlockSpec(memory_space=pl.ANY),
        scratch_shapes=[pltpu.VMEM((2, ROWS_L, D), BF16), pltpu.VMEM((CH, D), BF16),
                        pltpu.SemaphoreType.DMA((5,))],
    )
    return pl.pallas_call(
        _dispatch_kernel,
        grid_spec=grid_spec,
        out_shape=jax.ShapeDtypeStruct((P_ROWS, D), BF16),
        compiler_params=_params(("arbitrary",)),
        name="moe_dispatch",
    )(*plan["pieces"], plan["tail0"], plan["tailn"], h2, meta, plan["segrow"])


def _expert_kernel(layer, te_ref, tfirst_ref, tvalid_ref, xblk_ref, tnext_ref, wslot_ref, x_ref, wgu_hbm, bgu_ref,
                   wd_hbm, bd_ref, y_ref, wgu_f32, wd_f32, wgu_bf, wd_bf, sem):
    del xblk_ref
    i = pl.program_id(0)

    def weight_copies(e, slot):
        return (pltpu.make_async_copy(wgu_hbm.at[layer, e], wgu_f32.at[slot], sem.at[0, slot]),
                pltpu.make_async_copy(wd_hbm.at[layer, e], wd_f32.at[slot], sem.at[1, slot]))

    @pl.when(tfirst_ref[i] == 1)
    def _():
        slot = wslot_ref[i]

        @pl.when(i == 0)
        def _():
            for cp in weight_copies(te_ref[i], slot):
                cp.start()

        for cp in weight_copies(te_ref[i], slot):
            cp.wait()
        wgu_bf[...] = wgu_f32[slot].astype(BF16)
        wd_bf[...] = wd_f32[slot].astype(BF16)

        @pl.when(tnext_ref[i] >= 0)
        def _():
            for cp in weight_copies(tnext_ref[i], 1 - slot):
                cp.start()

    def ffn(x):
        a = _dot(x, wgu_bf[...]) + bgu_ref[...]
        glu = jnp.minimum(a[:, :D_FF], SWIGLU_LIMIT)
        lin = jnp.clip(a[:, D_FF:], -SWIGLU_LIMIT, SWIGLU_LIMIT)
        act = (glu * jax.nn.sigmoid(SWIGLU_ALPHA * glu)) * (lin + 1.0)
        return (_dot(act.astype(BF16), wd_bf[...]) + bd_ref[...]).astype(BF16)

    half = ETILE // 2

    @pl.when(tvalid_ref[i] == 2)
    def _():
        y_ref[...] = ffn(x_ref[...])

    @pl.when(tvalid_ref[i] == 1)
    def _():
        y_ref[:half] = ffn(x_ref[:half])
        y_ref[half:] = jnp.zeros((ETILE - half, D), BF16)


def _experts(layer, plan, xs, w_gu, b_gu, w_down, b_down):
    depth = w_gu.shape[0]
    e_of = lambda i, te, *_: (layer, te[i], 0, 0)
    grid_spec = pltpu.PrefetchScalarGridSpec(
        num_scalar_prefetch=6,
        grid=(N_ETILES,),
        in_specs=[
            pl.BlockSpec((ETILE, D), lambda i, te, tf, tv, xb, *_: (xb[i], 0)),
            pl.BlockSpec(memory_space=pl.ANY),
            pl.BlockSpec((None, None, 1, 2 * D_FF), e_of),
            pl.BlockSpec(memory_space=pl.ANY),
            pl.BlockSpec((None, None, 1, D), e_of),
        ],
        out_specs=pl.BlockSpec((ETILE, D), lambda i, te, tf, tv, xb, *_: (xb[i], 0)),
        scratch_shapes=[pltpu.VMEM((2, D, 2 * D_FF), F32), pltpu.VMEM((2, D_FF, D), F32),
                        pltpu.VMEM((D, 2 * D_FF), BF16), pltpu.VMEM((D_FF, D), BF16),
                        pltpu.SemaphoreType.DMA((2, 2))],
    )
    return pl.pallas_call(
        functools.partial(_expert_kernel, layer),
        grid_spec=grid_spec,
        out_shape=jax.ShapeDtypeStruct((P_ROWS, D), BF16),
        compiler_params=_params(("arbitrary",)),
        name="moe_experts",
    )(plan["te"], plan["tfirst"], plan["tvalid"], plan["xblk"], plan["tnext"], plan["wslot"], xs, w_gu,
      b_gu.reshape(depth, N_EXPERTS, 1, 2 * D_FF), w_down, b_down.reshape(depth, N_EXPERTS, 1, D))


def _combine_kernel(split, bsrc, bdst, nbig, ssrc, sdst, nsml, y_hbm, x1_ref, meta_ref, segrow_ref, mod_ref,
                    *rest):
    *o_refs, ybuf, sem = rest
    i = pl.program_id(0)
    slot = i % 2
    lists = ((bsrc, bdst, nbig, MAX_BIG), (ssrc, sdst, nsml, MAX_SMALL))

    def fetch(t, s):
        make = lambda loc, glob, rows: pltpu.make_async_copy(
            y_hbm.at[pl.ds(glob, rows)], ybuf.at[s, pl.ds(loc, rows)], sem.at[_piece_sem(rows, s)])
        _chunk_copies(t, lists, make)

    @pl.when(i == 0)
    def _():
        ybuf[...] = jnp.zeros_like(ybuf)
        fetch(i, slot)

    @pl.when(i + 1 < NBD)
    def _():
        fetch(i + 1, 1 - slot)

    _drain_tile(i, lists, lambda rows: pltpu.make_async_copy(
        y_hbm.at[pl.ds(0, rows)], ybuf.at[slot, pl.ds(0, rows)], sem.at[_piece_sem(rows, slot)]))

    meta = meta_ref[...]
    pos = _local_positions(meta, segrow_ref[0])
    gates = _scatter_matrix(pos, [meta[:, _META_W + k:_META_W + k + 1].astype(BF16) for k in range(TOP_K)])
    acc = _dot(gates, ybuf[slot])
    out = x1_ref[...] + mod_ref[0][5:6] * acc
    if split:
        os_ref, op_ref = o_refs

        @pl.when(i < NBD_SAMPLE)
        def _():
            os_ref[...] = out

        @pl.when(i >= NBD_SAMPLE)
        def _():
            op_ref[...] = out
    else:
        o_refs[0][...] = out


def _combine(plan, y, x1, meta, mod, split):
    if split:
        out_specs = [pl.BlockSpec((TMD, D), lambda i, *_: (jnp.minimum(i, NBD_SAMPLE - 1), 0)),
                     pl.BlockSpec((TMD, D), lambda i, *_: (jnp.maximum(i - NBD_SAMPLE, 0), 0))]
        out_shape = [jax.ShapeDtypeStruct((N_SAMPLE, D), F32), jax.ShapeDtypeStruct((N_PROMPT, D), F32)]
    else:
        out_specs, out_shape = _TOKD(D), jax.ShapeDtypeStruct((T, D), F32)
    grid_spec = pltpu.PrefetchScalarGridSpec(
        num_scalar_prefetch=6,
        grid=(NBD,),
        in_specs=[pl.BlockSpec(memory_space=pl.ANY), _TOKD(D), _TOKD(LANES), _SEGROWS,
                  pl.BlockSpec((1, 6, D), lambda i, *_: (_cond_row(i * (TMD // TM)), 0, 0))],
        out_specs=out_specs,
        scratch_shapes=[pltpu.VMEM((2, ROWS_L, D), BF16), pltpu.SemaphoreType.DMA((4,))],
    )
    return pl.pallas_call(
        functools.partial(_combine_kernel, split),
        grid_spec=grid_spec,
        out_shape=out_shape,
        compiler_params=_params(("arbitrary",)),
        name="moe_combine",
    )(*plan["pieces"], y, x1, meta, plan["segrow"], mod)


def _moe_plan(cnt):
    per = TMD // TM
    cnt = cnt[:, 0, :N_EXPERTS].astype(jnp.int32).reshape(NBD, per, N_EXPERTS)
    cpad = (jnp.sum(cnt, axis=1) + CH - 1) // CH * CH
    tot = jnp.sum(cpad, axis=0)
    tiles = (tot + ETILE - 1) // ETILE
    tile_end = jnp.cumsum(tiles)
    offs = (tile_end - tiles) * ETILE
    base = offs[None, :] + jnp.cumsum(cpad, axis=0) - cpad
    seg = jnp.cumsum(cpad, axis=1) - cpad
    segrows = seg[:, None, :] + jnp.cumsum(cnt, axis=1) - cnt
    tid = jnp.arange(N_ETILES, dtype=jnp.int32)
    te = jnp.sum((tile_end[None, :] <= tid[:, None]).astype(jnp.int32), axis=1)
    used = te < N_EXPERTS
    last = jnp.max(jnp.where(tiles > 0, jnp.arange(N_EXPERTS, dtype=jnp.int32), 0))
    te = jnp.where(used, te, last)
    rows = tot[te] - (tid - (tile_end - tiles)[te]) * ETILE
    tvalid = jnp.where(used, jnp.where(rows > ETILE // 2, 2, 1), 0).astype(jnp.int32)
    tfirst = jnp.concatenate([jnp.ones((1,), jnp.int32), (te[1:] != te[:-1]).astype(jnp.int32)])
    eid = jnp.arange(N_EXPERTS, dtype=jnp.int32)
    later = (eid[None, :] > eid[:, None]) & (tiles[None, :] > 0)
    nxt = jnp.min(jnp.where(later, eid[None, :], N_EXPERTS), axis=1)
    nxt = jnp.where(nxt < N_EXPERTS, nxt, -1)
    def piece_lists(counts, local0, global0, rows, max_pieces):
        cend = jnp.cumsum(counts, axis=1)
        cstart = cend - counts
        cid = jnp.arange(max_pieces, dtype=jnp.int32)
        owns = ((cstart[:, None, :] <= cid[None, :, None]) & (cid[None, :, None] < cend[:, None, :]))
        owns = owns.astype(jnp.int32)
        pick = lambda a: jnp.sum(owns * a[:, None, :], axis=2)
        within = (cid[None, :] - pick(cstart)) * rows
        live = jnp.sum(owns, axis=2)
        return [((pick(local0) + within) * live).reshape(-1), ((pick(global0) + within) * live).reshape(-1),
                jnp.sum(counts, axis=1)]

    nbig = cpad // BIG_ROWS
    nsml = (cpad - nbig * BIG_ROWS) // CH
    pieces = (piece_lists(nbig, seg, base, BIG_ROWS, MAX_BIG)
              + piece_lists(nsml, seg + nbig * BIG_ROWS, base + nbig * BIG_ROWS, CH, MAX_SMALL))
    return {
        "tnext": nxt[te], "wslot": (jnp.cumsum(tfirst) - 1) % 2, "tfirst": tfirst, "pieces": pieces,
        "tail0": offs + tot, "tailn": (tiles * ETILE - tot) // CH,
        "segrow": jnp.pad(segrows.astype(F32), ((0, 0), (0, 0), (0, LANES - N_EXPERTS))),
        "te": te, "tvalid": tvalid, "xblk": jnp.where(used, tid, tile_end[-1] - 1),
    }


def _moe(layer, x1, h2, meta, cnt, mod, w_gu, b_gu, w_down, b_down, split_out=False):
    plan = _moe_plan(cnt)
    xs = _dispatch(plan, h2, meta)
    y = _experts(layer, plan, xs, w_gu, b_gu, w_down, b_down)
    return _combine(plan, y, x1, meta, mod, split_out)


_HK, _HV = H_C * DK_C, H_C * DV_C
_ODD_MAIN = 2 * _HK + 2 * _HV


def _odd_in_kernel(x_ref, mod_ref, nmix_ref, win_ref, wgk_ref, bgk_ref, q_ref, k_ref, v_ref, g_ref, la_ref,
                   lamin_ref):
    m = mod_ref[0]
    h = _rms(x_ref[...]) * nmix_ref[...] * (1.0 + m[1:2]) + m[0:1]
    a = _dot(h.astype(BF16), win_ref[...])
    q_ref[...] = a[:, :_HK] * (DK_C ** -0.5)
    k_ref[...] = a[:, _HK:2 * _HK]
    v_ref[...] = a[:, 2 * _HK:2 * _HK + _HV]
    g_ref[...] = a[:, 2 * _HK + _HV:_ODD_MAIN]
    z = _dot(a[:, _ODD_MAIN:].astype(BF16), wgk_ref[...]) + bgk_ref[...]
    la = (jnp.minimum(z, 0.0) - jnp.log(1.0 + jnp.exp(-jnp.abs(z)))) * (1.0 / GATE_TAU)
    la_ref[...] = la
    chunk_tot = jnp.sum(la.reshape(TM // GLA_CHUNK, GLA_CHUNK, 2 * _HK), axis=1)
    lamin_ref[0] = jnp.broadcast_to(jnp.min(chunk_tot, axis=(0, 1), keepdims=True), (SUBLANES, LANES))


def _pairwise_att(q, k, c, forward):
    tcol = lax.broadcasted_iota(jnp.int32, (GLA_CHUNK, 1), 0)
    trow = lax.broadcasted_iota(jnp.int32, (GLA_CHUNK, DK_C), 0)
    scol = lax.broadcasted_iota(jnp.int32, (GLA_CHUNK, GLA_CHUNK), 1)

    def body(s, att):
        k_s = jnp.sum(jnp.where(trow == s, k, 0.0), axis=0, keepdims=True)
        c_s = jnp.sum(jnp.where(trow == s, c, 0.0), axis=0, keepdims=True)
        allowed = (tcol >= s) if forward else (tcol <= s)
        e = jnp.exp(jnp.where(allowed, c - c_s, -jnp.inf))
        return jnp.where(scol == s, jnp.sum(q * k_s * e, axis=-1, keepdims=True), att)

    return lax.fori_loop(0, GLA_CHUNK, body, jnp.zeros((GLA_CHUNK, GLA_CHUNK), F32))


def _gla_kernel(has_init, nchunk, strong_ref, *refs):
    if has_init:
        (qf, kf, vf, laf, qb, kb, vb, lab, s0f, s0b, _, _, of_ref, ob_ref, st) = refs
    else:
        (qf, kf, vf, laf, qb, kb, vb, lab, of_ref, ob_ref, sf_ref, sb_ref, st) = refs
    j = pl.program_id(1)

    @pl.when(j == 0)
    def _():
        if has_init:
            st[0] = s0f[...]
            st[1] = s0b[...]
        else:
            st[...] = jnp.zeros_like(st)

    row = lax.broadcasted_iota(jnp.int32, (GLA_CHUNK, GLA_CHUNK), 0)
    col = lax.broadcasted_iota(jnp.int32, (GLA_CHUNK, GLA_CHUNK), 1)

    def scan_step(factored):
        for d, (q_r, k_r, v_r, la_r, o_r) in enumerate(((qf, kf, vf, laf, of_ref), (qb, kb, vb, lab, ob_ref))):
            keep = (col <= row) if d == 0 else (col >= row)
            tri = jnp.where(keep, 1.0, 0.0).astype(BF16)
            for s in range(GLA_NS):
                g = la_r[s]
                g_hi = g.astype(BF16)
                g_r = g - g_hi.astype(F32)
                g_mid = g_r.astype(BF16)
                g_lo = (g_r - g_mid.astype(F32)).astype(BF16)
                c = _dot(tri, g_hi) + _dot(tri, g_mid) + _dot(tri, g_lo)
                tot = jnp.sum(g, axis=0, keepdims=True)
                decay = jnp.exp(tot)
                q, k = q_r[s], k_r[s]
                qe = (q * jnp.exp(c)).astype(BF16)
                if factored:
                    e_neg = jnp.exp(-c)
                    kd = (k * e_neg).astype(BF16)
                    k2 = (k * (e_neg * decay)).astype(BF16)
                else:
                    k2 = (k * jnp.exp(tot - c)).astype(BF16)
                v = v_r[s].astype(BF16)
                for h in range(H_C):
                    kc = slice(h * DK_C, (h + 1) * DK_C)
                    vc = slice(h * DV_C, (h + 1) * DV_C)
                    if factored:
                        att = jnp.where(keep, _dot_nt(qe[:, kc], kd[:, kc]), 0.0)
                    else:
                        att = _pairwise_att(q[:, kc], k[:, kc], c[:, kc], d == 0)
                    s_t = st[d, s, h]
                    o_r[s, :, vc] = _dot_nt(qe[:, kc], s_t.astype(BF16)) + _dot(att.astype(BF16), v[:, vc])
                    st[d, s, h] = s_t * decay[:, kc] + _dot_tn(v[:, vc], k2[:, kc])

    pl.when(strong_ref[0] == 0)(lambda: scan_step(True))
    pl.when(strong_ref[0] != 0)(lambda: scan_step(False))

    if not has_init:
        @pl.when(j == nchunk - 1)
        def _():
            sf_ref[...] = st[0]
            sb_ref[...] = st[1]


def _gla_call(has_init, nseq, seqlen, row0, strong, q, k, v, la, prev_f=None, prev_b=None, s0f=None, s0b=None):
    nchunk = seqlen // GLA_CHUNK
    nview = T // seqlen
    g0 = row0 // seqlen // GLA_NS
    view = lambda a: a.reshape(nview, seqlen, a.shape[-1])
    fwd = lambda g, j, *_: (g0 + g, j, 0)
    bwd = lambda g, j, *_: (g0 + g, nchunk - 1 - j, 0)
    bwd_la = lambda g, j, *_: (g0 + g, nchunk - 1 - j, 1)
    blk = lambda w, m: pl.BlockSpec((GLA_NS, GLA_CHUNK, w), m)
    state_spec = pl.BlockSpec((GLA_NS, H_C, DV_C, DK_C), lambda g, j, *_: (g, 0, 0, 0))
    in_specs = [blk(_HK, fwd), blk(_HK, fwd), blk(_HV, fwd), blk(_HK, fwd),
                blk(_HK, bwd), blk(_HK, bwd), blk(_HV, bwd), blk(_HK, bwd_la)]
    args = [view(q), view(k), view(v), view(la)] * 2
    aliases = {}
    if has_init:
        in_specs += [state_spec, state_spec] + [pl.BlockSpec(memory_space=pl.ANY)] * 2
        args += [s0f, s0b, view(prev_f), view(prev_b)]
        aliases = {len(args) - 1: 0, len(args): 1}
    out_specs = [blk(_HV, fwd), blk(_HV, bwd)]
    out_shape = [jax.ShapeDtypeStruct((nview, seqlen, _HV), F32)] * 2
    if not has_init:
        out_specs += [state_spec, state_spec]
        out_shape += [jax.ShapeDtypeStruct((nseq, H_C, DV_C, DK_C), F32)] * 2
    grid_spec = pltpu.PrefetchScalarGridSpec(
        num_scalar_prefetch=1,
        grid=(nseq // GLA_NS, nchunk),
        in_specs=in_specs,
        out_specs=out_specs,
        scratch_shapes=[pltpu.VMEM((2, GLA_NS, H_C, DV_C, DK_C), F32)],
    )
    outs = pl.pallas_call(
        functools.partial(_gla_kernel, has_init, nchunk),
        grid_spec=grid_spec,
        out_shape=out_shape,
        input_output_aliases=aliases,
        compiler_params=_params(("arbitrary", "arbitrary")),
        name="gla_sample" if has_init else "gla_prompt",
    )(strong, *args)
    return [outs[0].reshape(T, _HV), outs[1].reshape(T, _HV)] + list(outs[2:])


def _rope_tables():
    half = ROPE // 2
    inv_freq = np.power(np.float32(ROPE_THETA), -np.arange(0, half, 2, dtype=np.float32) / np.float32(half))
    n = np.arange(SAMPLE_LEN)
    row = (n // GRID_W).astype(np.float32)
    col = (n % GRID_W).astype(np.float32)
    ang_r = (row[:, None] * inv_freq[None, :]).astype(np.float32)
    ang_c = (col[:, None] * inv_freq[None, :]).astype(np.float32)
    nf = half // 2
    c = np.ones((TM + SAMPLE_LEN, LANES), np.float32)
    s1 = np.zeros((TM + SAMPLE_LEN, LANES), np.float32)
    s2 = np.zeros((TM + SAMPLE_LEN, LANES), np.float32)
    for base, ang in ((NOPE, ang_r), (NOPE + half, ang_c)):
        c[TM:, base:base + nf] = np.cos(ang)
        c[TM:, base + nf:base + half] = np.cos(ang)
        s1[TM:, base:base + nf] = -np.sin(ang)
        s2[TM:, base + nf:base + half] = np.sin(ang)
    return jnp.asarray(c), jnp.asarray(s1), jnp.asarray(s2)


def _pad_heads(w, nheads, width, lo=0):
    k = w.shape[0]
    w = w.reshape(k, nheads, width)
    w = jnp.pad(w, ((0, 0), (0, 0), (lo, HP - lo - width)))
    return w.reshape(k, nheads * HP)


def _row128(v, lo=0):
    return jnp.pad(v, (lo, LANES - lo - v.shape[0])).reshape(1, LANES)


def _even_layer(xs, xp, mod, nmix, w_in, q_a_norm, w_uq, q_norm, kv_a_norm, w_ukv, k_norm, v_norm, w_s, b_s, w_out,
                cache_ckv, cache_kpe):
    s = np.cumsum([Q_LORA, KV_LORA, ROPE, W_B])
    w_q, w_ckv, w_kpe, w_u, w_v = (w_in[:, :s[0]], w_in[:, s[0]:s[1]], w_in[:, s[1]:s[2]], w_in[:, s[2]:s[3]],
                                   w_in[:, s[3]:])
    w_kpe = jnp.pad(w_kpe, ((0, 0), (NOPE, LANES - QK_DIM)))
    win = jnp.concatenate([w_q, w_ckv, w_u, w_v, w_kpe], axis=1).astype(BF16)
    wuq = _pad_heads(w_uq, H_A, QK_DIM).astype(BF16)
    ukv = w_ukv.reshape(KV_LORA, H_A, NOPE + V_A)
    wuk = _pad_heads(ukv[:, :, :NOPE].reshape(KV_LORA, H_A * NOPE), H_A, NOPE)
    wuv = _pad_heads(ukv[:, :, NOPE:].reshape(KV_LORA, H_A * V_A), H_A, V_A)
    wukv = jnp.concatenate([wuk, wuv], axis=1).astype(BF16)
    qgain = _row128(q_norm * (QK_DIM ** -0.5 * LOG2_E))
    kgain = _row128(k_norm)
    bias = b_s.reshape(G_B // 2, 2, CHUNK_B)
    bias = jnp.concatenate([jnp.broadcast_to(bias[:, 0, :, None], (G_B // 2, CHUNK_B, C_B)),
                            jnp.broadcast_to(bias[:, 1, :, None], (G_B // 2, CHUNK_B, C_B))], axis=-1)
    rc, rs1, rs2 = _rope_tables()
    rope_spec = pl.BlockSpec((TM, LANES), lambda i: (_rope_blk(i), 0))
    width = H_A * HP
    q, k, v, ob, ckv, kpe = pl.pallas_call(
        _even_in_kernel,
        grid=(NBLK,),
        in_specs=[
            _X_SAMPLE, _X_PROMPT, _MOD, _full((1, D)), _full((D, _WIN_N)), _full((1, Q_LORA)),
            _full((Q_LORA, width)),
            _full((1, LANES)), _full((1, KV_LORA)), _full((KV_LORA, 2 * width)), _full((1, LANES)),
            _full((1, W_B)), _full((G_B // 2, 2 * CHUNK_B, CHUNK_B)), _full((G_B // 2, CHUNK_B, LANES)),
            rope_spec, rope_spec, rope_spec,
        ],
        out_specs=[_TOK(width), _TOK(width), _TOK(width), _TOK(W_B), _TOK(KV_LORA), _TOK(ROPE)],
        out_shape=[
            jax.ShapeDtypeStruct((T, width), BF16), jax.ShapeDtypeStruct((T, width), BF16),
            jax.ShapeDtypeStruct((T, width), BF16), jax.ShapeDtypeStruct((T, W_B), BF16),
            jax.ShapeDtypeStruct((T, KV_LORA), F32), jax.ShapeDtypeStruct((T, ROPE), F32),
        ],
        compiler_params=_params(("arbitrary",)),
        name="even_in",
    )(xs, xp, mod, nmix.reshape(1, D), win, q_a_norm.reshape(1, Q_LORA), wuq, qgain, kv_a_norm.reshape(1, KV_LORA),
      wukv, kgain, v_norm.reshape(1, W_B), w_s.astype(BF16).reshape(G_B // 2, 2 * CHUNK_B, CHUNK_B), bias, rc, rs1,
      rs2)

    n_ctx = N_SAMPLE_SEQ * PAST_LEN
    kpe_ctx = jnp.pad(cache_kpe.reshape(n_ctx, ROPE), ((0, 0), (NOPE, LANES - QK_DIM)))
    k_ctx, v_ctx = pl.pallas_call(
        _ctx_kv_kernel,
        grid=(n_ctx // TM,),
        in_specs=[_TOK(KV_LORA), _TOK(LANES), _full((KV_LORA, 2 * width)), _full((1, LANES))],
        out_specs=[_TOK(width), _TOK(width)],
        out_shape=[jax.ShapeDtypeStruct((n_ctx, width), BF16)] * 2,
        compiler_params=_params(("arbitrary",)),
        name="ctx_kv",
    )(cache_ckv.reshape(n_ctx, KV_LORA), kpe_ctx, wukv, kgain)

    oa = _attention(q, k, v, k_ctx, v_ctx)
    woa = jnp.pad(w_out[:H_A * V_A].reshape(H_A, V_A, D), ((0, 0), (0, HP - V_A), (0, 0))).reshape(width, D)
    return oa, ob, woa.astype(BF16), w_out[H_A * V_A:].astype(BF16), ckv, kpe


def kernel(x_prompt, x_sample, cache_mla_ckv, cache_mla_kpe, state_gla_fwd, state_gla_bwd, c, c_ctx, ada_w, ada_b,
           norm_mix, norm_ffn, even_w_in, mla_q_a_norm, mla_w_uq, mla_q_norm, mla_kv_a_norm, mla_w_ukv, mla_k_norm,
           cmlp_v_norm, cmlp_w_s, cmlp_b_s, even_w_out, odd_w_in, gla_w_gk_fwd, gla_b_gk_fwd, gla_w_gk_bwd,
           gla_b_gk_bwd, gla_o_norm, odd_w_out, moe_w_router, moe_b_router, moe_w_gu, moe_b_gu, moe_w_down,
           moe_b_down):
    xs0, xp0 = x_sample.reshape(N_SAMPLE, D), x_prompt.reshape(N_PROMPT, D)
    cond8 = jnp.concatenate([c_ctx[None, :], c, jnp.zeros((SUBLANES - 1 - N_SAMPLE_SEQ, D), F32)], axis=0)
    mods = _adaln(cond8, ada_w, ada_b)
    wr = jnp.pad(moe_w_router, ((0, 0), (0, 0), (0, LANES - N_EXPERTS)))
    wr_hi = wr.astype(BF16)
    wr = jnp.concatenate([wr_hi, (wr - wr_hi.astype(F32)).astype(BF16)], axis=-1)
    br = jnp.pad(moe_b_router, ((0, 0), (0, LANES - N_EXPERTS))).reshape(2, 1, LANES)

    oa, ob, woa, wob, ckv, kpe = _even_layer(
        xs0, xp0, mods[0], norm_mix[0], even_w_in[0], mla_q_a_norm[0], mla_w_uq[0], mla_q_norm[0], mla_kv_a_norm[0],
        mla_w_ukv[0], mla_k_norm[0], cmlp_v_norm[0], cmlp_w_s[0], cmlp_b_s[0], even_w_out[0],
        cache_mla_ckv[:, 0], cache_mla_kpe[:, 0])
    width = H_A * HP
    x1, h2, meta, cnt = pl.pallas_call(
        _even_out_kernel,
        grid=(NBLK,),
        in_specs=[_TOK(width), _TOK(W_B), _X_SAMPLE, _X_PROMPT, _MOD, _full((width, D)), _full((W_B, D)),
                  _full((1, D)),
                  _full((D, 2 * LANES)), _full((1, LANES))],
        out_specs=_PROLOGUE_OUT_SPECS,
        out_shape=_PROLOGUE_OUT_SHAPE,
        compiler_params=_params(("arbitrary",)),
        name="even_out",
    )(oa, ob, xs0, xp0, mods[0], woa, wob, norm_ffn[0].reshape(1, D), wr[0], br[0])
    x2 = _moe(0, x1, h2, meta, cnt, mods[0], moe_w_gu, moe_b_gu, moe_w_down, moe_b_down)

    w_in = odd_w_in[0]
    win = jnp.concatenate([w_in, jnp.zeros((D, LANES - 2 * GATE_RANK), F32)], axis=1).astype(BF16)
    wgk = jnp.zeros((LANES, 2 * _HK), F32)
    wgk = wgk.at[:GATE_RANK, :_HK].set(gla_w_gk_fwd[0]).at[GATE_RANK:2 * GATE_RANK, _HK:].set(gla_w_gk_bwd[0])
    bgk = jnp.concatenate([gla_b_gk_fwd[0], gla_b_gk_bwd[0]]).reshape(1, 2 * _HK)
    q, k, v, g, la, lamin = pl.pallas_call(
        _odd_in_kernel,
        grid=(NBLK,),
        in_specs=[_TOK(D), _MOD, _full((1, D)), _full((D, _ODD_MAIN + LANES)), _full((LANES, 2 * _HK)),
                  _full((1, 2 * _HK))],
        out_specs=[_TOK(_HK), _TOK(_HK), _TOK(_HV), _TOK(_HV), _TOK(2 * _HK), _TILE_ROW],
        out_shape=[jax.ShapeDtypeStruct((T, _HK), F32), jax.ShapeDtypeStruct((T, _HK), F32),
                   jax.ShapeDtypeStruct((T, _HV), F32), jax.ShapeDtypeStruct((T, _HV), F32),
                   jax.ShapeDtypeStruct((T, 2 * _HK), F32), jax.ShapeDtypeStruct((NBLK, SUBLANES, LANES), F32)],
        compiler_params=_params(("arbitrary",)),
        name="odd_in",
    )(x2, mods[1], norm_mix[1].reshape(1, D), win, wgk.astype(BF16), bgk)

    strong = (jnp.min(lamin) < -GLA_SAFE_LOG_DECAY).astype(jnp.int32).reshape(1)
    of, obk, st_f, st_b = _gla_call(False, N_PROMPT_SEQ, PROMPT_LEN, N_SAMPLE, strong, q, k, v, la)
    s0f = state_gla_fwd[:, 0].transpose(0, 1, 3, 2)
    s0b = state_gla_bwd[:, 0].transpose(0, 1, 3, 2)
    of, obk = _gla_call(True, N_SAMPLE_SEQ, SAMPLE_LEN, 0, strong, q, k, v, la, of, obk, s0f, s0b)

    x3, h2, meta, cnt = pl.pallas_call(
        _odd_out_kernel,
        grid=(NBLK,),
        in_specs=[_TOK(_HV), _TOK(_HV), _TOK(_HV), _TOK(D), _MOD, _full((1, DV_C)), _full((_HV, D)),
                  _full((1, D)), _full((D, 2 * LANES)), _full((1, LANES))],
        out_specs=_PROLOGUE_OUT_SPECS,
        out_shape=_PROLOGUE_OUT_SHAPE,
        compiler_params=_params(("arbitrary",)),
        name="odd_out",
    )(of, obk, g, x2, mods[1], gla_o_norm[0].reshape(1, DV_C), odd_w_out[0].astype(BF16),
      norm_ffn[1].reshape(1, D), wr[1], br[1])
    ys, yp = _moe(1, x3, h2, meta, cnt, mods[1], moe_w_gu, moe_b_gu, moe_w_down, moe_b_down, split_out=True)

    y_sample = ys.reshape(N_SAMPLE_SEQ, SAMPLE_LEN, D)
    y_prompt = yp.reshape(N_PROMPT_SEQ, PROMPT_LEN, D)
    new_ckv = ckv[N_SAMPLE:].reshape(N_PROMPT_SEQ, 1, PROMPT_LEN, KV_LORA)
    new_kpe = kpe[N_SAMPLE:].reshape(N_PROMPT_SEQ, 1, PROMPT_LEN, ROPE)
    new_fwd = st_f.transpose(0, 1, 3, 2)[:, None]
    new_bwd = st_b.transpose(0, 1, 3, 2)[:, None]
    return (y_prompt, y_sample, new_ckv, new_kpe, new_fwd, new_bwd)
```
